```python
import math
import jax, jax.numpy as jnp
from jax import lax
import numpy as np

D_MODEL = 2048
BATCH = 4
SEQ = 2048
DEPTH = 4

D_CONV = 1024
CONV_WIDTH = 31
SB_HEADS = 8
SB_HEAD_DIM = 128
SB_WIDTH = SB_HEADS * SB_HEAD_DIM
NSA_HEADS = 8
NSA_KV_HEADS = 2
NSA_HEAD_DIM = 128
NSA_WIDTH = NSA_HEADS * NSA_HEAD_DIM
NSA_KV_WIDTH = NSA_KV_HEADS * NSA_HEAD_DIM
CMP_LEN = 32
CMP_STRIDE = 16
SLC_LEN = 64
SLC_TOP_N = 16
N_LOCAL_BLOCKS = 2
WINDOW = 512
FORCE_SCORE = 1e4
REL_BUCKETS = 32
REL_MAX_DIST = 128
D_FF = 5632
N_EXPERTS = 8
TOP_K = 2
D_FF_EXPERT = 2816
Q_BLOCK = 128
SLC_Q_BLOCK = 32
EPS = 1e-6
NEG_INF = -1e30
TINY = 1e-20
IN_SIZES = (2 * D_CONV, SB_WIDTH, SB_WIDTH, SB_WIDTH, NSA_WIDTH, NSA_KV_WIDTH, NSA_KV_WIDTH, NSA_KV_WIDTH, NSA_KV_WIDTH, NSA_KV_WIDTH, NSA_KV_WIDTH, 3 * NSA_HEADS, 3 * D_MODEL)

kernel_name = "hybrid_conv_stickbreak_nsa_moe_block"


def rms_norm(x, g):
    xf = x.astype(jnp.float32)
    y = xf * lax.rsqrt(jnp.mean(xf * xf, axis=-1, keepdims=True) + EPS)
    return (y * g.astype(jnp.float32)).astype(x.dtype)


def layer_norm(x, g, b):
    xf = x.astype(jnp.float32)
    mu = jnp.mean(xf, axis=-1, keepdims=True)
    var = jnp.mean(jnp.square(xf - mu), axis=-1, keepdims=True)
    return ((xf - mu) * lax.rsqrt(var + EPS) * g.astype(jnp.float32) + b.astype(jnp.float32)).astype(x.dtype)


def masked_softmax(logits, mask):
    logits = jnp.where(mask, logits.astype(jnp.float32), NEG_INF)
    m = jnp.max(logits, axis=-1, keepdims=True)
    p = jnp.where(mask, jnp.exp(logits - m), 0.0)
    return p / jnp.maximum(jnp.sum(p, axis=-1, keepdims=True), TINY)


def rel_bucket(dist):
    n = jnp.maximum(dist, 0)
    max_exact = REL_BUCKETS // 2
    nf = jnp.maximum(n, 1).astype(jnp.float32)
    large = max_exact + (jnp.log(nf / max_exact) / math.log(REL_MAX_DIST / max_exact) * (REL_BUCKETS - max_exact)).astype(jnp.int32)
    large = jnp.minimum(large, REL_BUCKETS - 1)
    return jnp.where(n < max_exact, n, large)


def to_heads(t, n):
    b, s, _ = t.shape
    return t.reshape(b, s, n, -1).transpose(0, 2, 1, 3)


def merge_heads(t):
    b, h, s, d = t.shape
    return t.transpose(0, 2, 1, 3).reshape(b, s, h * d)


def split_points():
    return np.cumsum(np.array(IN_SIZES))[:-1].tolist()


def conv_module(glu_in, conv_w, conv_b, ln_g, ln_b, w_out):
    a, gate = jnp.split(glu_in, 2, axis=-1)
    u = a * jax.nn.sigmoid(gate)
    u = lax.conv_general_dilated(u, conv_w[:, None, :], window_strides=(1,), padding=[(CONV_WIDTH - 1, 0)],
                                 dimension_numbers=('NWC', 'WIO', 'NWC'), feature_group_count=D_CONV) + conv_b
    u = jax.nn.silu(layer_norm(u, ln_g, ln_b))
    return u @ w_out


def stick_breaking_attention(q, k, v):
    b, h, s, d = q.shape
    nblk = s // Q_BLOCK
    scale = d ** -0.5
    kpos = jnp.arange(s)
    q_blocks = q.reshape(b, h, nblk, Q_BLOCK, d).transpose(2, 0, 1, 3, 4)

    def block(args):
        qb, i = args
        tpos = i * Q_BLOCK + jnp.arange(Q_BLOCK)
        before = kpos[None, :] < tpos[:, None]
        z = jnp.einsum('bhqd,bhkd->bhqk', qb, k).astype(jnp.float32) * scale
        log_beta = jax.nn.log_sigmoid(z)
        log_keep = jnp.where(before, jax.nn.log_sigmoid(-z), 0.0)
        log_survive = lax.cumsum(log_keep, axis=3, reverse=True) - log_keep
        a = jnp.where(before, jnp.exp(log_beta + log_survive), 0.0)
        return jnp.einsum('bhqk,bhkd->bhqd', a.astype(v.dtype), v)

    out = lax.map(block, (q_blocks, jnp.arange(nblk)))
    return out.transpose(1, 2, 0, 3, 4).reshape(b, h, s, d)


def compress_blocks(kv, pos, w):
    s = kv.shape[2]
    nc = (s - CMP_LEN) // CMP_STRIDE + 1
    idx = np.arange(nc)[:, None] * CMP_STRIDE + np.arange(CMP_LEN)[None, :]
    blocks = kv[:, :, idx, :] + pos
    return blocks.reshape(blocks.shape[0], blocks.shape[1], nc, -1) @ w


def nsa_attention(q, kc, vc, ks, vs, kw, vw, gate_logits, cmp_pos_k, cmp_pos_v, cmp_wk, cmp_wv, q_g, kc_g, ks_g, kw_g, rel_bias):
    b, s, _ = q.shape
    g_n, r_n, d = NSA_KV_HEADS, NSA_HEADS // NSA_KV_HEADS, NSA_HEAD_DIM
    scale = d ** -0.5
    qh = rms_norm(to_heads(q, NSA_HEADS), q_g).reshape(b, g_n, r_n, s, d)
    tpos = jnp.arange(s)
    table = rel_bias.T.reshape(g_n, r_n, REL_BUCKETS)
    table_g = table.transpose(0, 2, 1)

    kcb = rms_norm(compress_blocks(to_heads(kc, g_n), cmp_pos_k, cmp_wk), kc_g)
    vcb = compress_blocks(to_heads(vc, g_n), cmp_pos_v, cmp_wv)
    nc = kcb.shape[2]
    cend = jnp.arange(nc) * CMP_STRIDE + CMP_LEN - 1
    dist_c = tpos[:, None] - cend[None, :]
    logit_c = jnp.einsum('bgrsd,bgcd->bgrsc', qh, kcb).astype(jnp.float32) * scale + table[:, :, rel_bucket(dist_c)]
    p_c = masked_softmax(logit_c, dist_c >= 0)
    o_c = jnp.einsum('bgrsc,bgcd->bgrsd', p_c.astype(vcb.dtype), vcb)

    nsel = s // SLC_LEN
    ci = np.arange(nc)[:, None]
    sj = np.arange(nsel)[None, :]
    overlap = (ci * CMP_STRIDE <= sj * SLC_LEN + SLC_LEN - 1) & (ci * CMP_STRIDE + CMP_LEN - 1 >= sj * SLC_LEN)
    imp = jnp.einsum('bgrsc,cn->bgsn', p_c, jnp.asarray(overlap, jnp.float32))
    tb = tpos // SLC_LEN
    jb = jnp.arange(nsel)
    causal_blk = jb[None, :] <= tb[:, None]
    forced = (jb[None, :] == 0) | (causal_blk & (jb[None, :] > tb[:, None] - N_LOCAL_BLOCKS))
    score = jnp.where(forced, FORCE_SCORE, jnp.where(causal_blk, imp, -FORCE_SCORE))
    n_top = min(SLC_TOP_N, nsel)
    top_score, top_idx = lax.top_k(score, n_top)
    top_ok = top_score > -0.5 * FORCE_SCORE

    ks_blk = rms_norm(to_heads(ks, g_n), ks_g).reshape(b, g_n, nsel, SLC_LEN, d)
    vs_blk = to_heads(vs, g_n).reshape(b, g_n, nsel, SLC_LEN, d)
    nq = s // SLC_Q_BLOCK
    bi = jnp.arange(b)[:, None, None, None]
    gi = jnp.arange(g_n)[None, :, None, None]

    def sel_chunk(args):
        qc, idx, ok, i = args
        t = i * SLC_Q_BLOCK + jnp.arange(SLC_Q_BLOCK)
        kg = ks_blk[bi, gi, idx]
        vg = vs_blk[bi, gi, idx]
        tok = idx[..., None] * SLC_LEN + jnp.arange(SLC_LEN)
        dist = t[None, None, :, None, None] - tok
        mask = (dist >= 0) & ok[..., None]
        bias = jnp.moveaxis(table_g[gi[..., None], rel_bucket(dist)], -1, 2)
        logits = jnp.einsum('bgrqd,bgqnld->bgrqnl', qc, kg).astype(jnp.float32) * scale + bias
        p = masked_softmax(logits.reshape(b, g_n, r_n, SLC_Q_BLOCK, -1), mask.reshape(b, g_n, 1, SLC_Q_BLOCK, -1))
        return jnp.einsum('bgrqm,bgqmd->bgrqd', p.astype(vg.dtype), vg.reshape(b, g_n, SLC_Q_BLOCK, -1, d))

    q_chunks = qh.reshape(b, g_n, r_n, nq, SLC_Q_BLOCK, d).transpose(3, 0, 1, 2, 4, 5)
    idx_chunks = top_idx.reshape(b, g_n, nq, SLC_Q_BLOCK, n_top).transpose(2, 0, 1, 3, 4)
    ok_chunks = top_ok.reshape(b, g_n, nq, SLC_Q_BLOCK, n_top).transpose(2, 0, 1, 3, 4)
    o_s = lax.map(sel_chunk, (q_chunks, idx_chunks, ok_chunks, jnp.arange(nq)))
    o_s = o_s.transpose(1, 2, 3, 0, 4, 5).reshape(b, g_n, r_n, s, d)

    pad = ((0, 0), (0, 0), (WINDOW, 0), (0, 0))
    kw_p = jnp.pad(rms_norm(to_heads(kw, g_n), kw_g), pad)
    vw_p = jnp.pad(to_heads(vw, g_n), pad)
    span = WINDOW + Q_BLOCK
    nb = s // Q_BLOCK

    def win_block(args):
        qb, i = args
        start = i * Q_BLOCK
        kb = lax.dynamic_slice_in_dim(kw_p, start, span, axis=2)
        vb = lax.dynamic_slice_in_dim(vw_p, start, span, axis=2)
        t = start + jnp.arange(Q_BLOCK)
        spos = start - WINDOW + jnp.arange(span)
        dist = t[:, None] - spos[None, :]
        mask = (dist >= 0) & (dist < WINDOW) & (spos[None, :] >= 0)
        logits = jnp.einsum('bgrqd,bgkd->bgrqk', qb, kb).astype(jnp.float32) * scale + table[:, :, rel_bucket(dist)]
        p = masked_softmax(logits, mask)
        return jnp.einsum('bgrqk,bgkd->bgrqd', p.astype(vb.dtype), vb)

    q_blocks = qh.reshape(b, g_n, r_n, nb, Q_BLOCK, d).transpose(3, 0, 1, 2, 4, 5)
    o_w = lax.map(win_block, (q_blocks, jnp.arange(nb)))
    o_w = o_w.transpose(1, 2, 3, 0, 4, 5).reshape(b, g_n, r_n, s, d)

    gates = jax.nn.sigmoid(gate_logits.astype(jnp.float32)).reshape(b, s, 3, g_n, r_n).transpose(2, 0, 3, 4, 1)[..., None].astype(q.dtype)
    o = gates[0] * o_c + gates[1] * o_s + gates[2] * o_w
    return o.transpose(0, 3, 1, 2, 4).reshape(b, s, NSA_WIDTH)


def hybrid_mixer(h, w_in, conv_w, conv_b, conv_ln_g, conv_ln_b, w_conv_out, w_sb_out, cmp_pos_k, cmp_pos_v, cmp_wk, cmp_wv,
                 q_g, kc_g, ks_g, kw_g, w_nsa_out, w_o, rel_bias):
    (glu_in, sb_q, sb_k, sb_v, nq, nkc, nvc, nks, nvs, nkw, nvw, nsa_gate, merge_gate) = jnp.split(h @ w_in, split_points(), axis=-1)
    y_conv = conv_module(glu_in, conv_w, conv_b, conv_ln_g, conv_ln_b, w_conv_out)
    y_sb = merge_heads(stick_breaking_attention(to_heads(sb_q, SB_HEADS), to_heads(sb_k, SB_HEADS), to_heads(sb_v, SB_HEADS))) @ w_sb_out
    y_nsa = nsa_attention(nq, nkc, nvc, nks, nvs, nkw, nvw, nsa_gate, cmp_pos_k, cmp_pos_v, cmp_wk, cmp_wv,
                          q_g, kc_g, ks_g, kw_g, rel_bias) @ w_nsa_out
    g_conv, g_sb, g_nsa = jnp.split(jax.nn.sigmoid(merge_gate), 3, axis=-1)
    return (g_conv * y_conv + g_sb * y_sb + g_nsa * y_nsa) @ w_o


def swiglu(t, w1, w3, w2):
    return (jax.nn.silu(t @ w1) * (t @ w3)) @ w2


def moe_swiglu(h, w_router, b_router, w1, w3, w2):
    b, s, dm = h.shape
    t = h.reshape(-1, dm)
    logits = (t @ w_router).astype(jnp.float32) + b_router.astype(jnp.float32)
    top_vals, top_idx = lax.top_k(logits, TOP_K)
    top_w = jax.nn.softmax(top_vals, axis=-1)
    combine = jnp.sum(jax.nn.one_hot(top_idx, N_EXPERTS, dtype=jnp.float32) * top_w[..., None], axis=1).astype(h.dtype)
    out = jnp.zeros_like(t)
    for e in range(N_EXPERTS):
        out = out + combine[:, e:e + 1] * swiglu(t, w1[e], w3[e], w2[e])
    return out.reshape(b, s, dm)


def setup_inputs(seed: int = 0) -> dict:
    key = jax.random.key(seed)
    k = jax.random.split(key, 32)
    n_dense = (DEPTH + 1) // 2
    n_moe = DEPTH // 2
    n_in = int(sum(IN_SIZES))
    d = NSA_HEAD_DIM

    def nrm(i, shape, scale):
        return jax.random.normal(k[i], shape, jnp.float32) * scale

    def gain(i, shape):
        return 1.0 + nrm(i, shape, 0.02)

    return {
        "x": nrm(0, (BATCH, SEQ, D_MODEL), 1.0),
        "c": nrm(1, (BATCH, D_MODEL), 1.0),
        "w_ada": nrm(2, (DEPTH, D_MODEL, 6 * D_MODEL), 0.5 * D_MODEL ** -0.5),
        "b_ada": nrm(3, (DEPTH, 6 * D_MODEL), 0.01),
        "g_mix": gain(4, (DEPTH, D_MODEL)),
        "g_ffn": gain(5, (DEPTH, D_MODEL)),
        "w_in": nrm(6, (DEPTH, D_MODEL, n_in), D_MODEL ** -0.5),
        "conv_w": nrm(7, (DEPTH, CONV_WIDTH, D_CONV), CONV_WIDTH ** -0.5),
        "conv_b": nrm(8, (DEPTH, D_CONV), 0.01),
        "conv_ln_g": gain(9, (DEPTH, D_CONV)),
        "conv_ln_b": nrm(10, (DEPTH, D_CONV), 0.01),
        "w_conv_out": nrm(11, (DEPTH, D_CONV, D_MODEL), D_CONV ** -0.5),
        "w_sb_out": nrm(12, (DEPTH, SB_WIDTH, D_MODEL), SB_WIDTH ** -0.5),
        "nsa_cmp_pos_k": nrm(13, (DEPTH, CMP_LEN, d), 0.02),
        "nsa_cmp_pos_v": nrm(14, (DEPTH, CMP_LEN, d), 0.02),
        "nsa_cmp_wk": nrm(15, (DEPTH, CMP_LEN * d, d), (CMP_LEN * d) ** -0.5),
        "nsa_cmp_wv": nrm(16, (DEPTH, CMP_LEN * d, d), (CMP_LEN * d) ** -0.5),
        "nsa_q_g": gain(17, (DEPTH, d)),
        "nsa_kc_g": gain(18, (DEPTH, d)),
        "nsa_ks_g": gain(19, (DEPTH, d)),
        "nsa_kw_g": gain(20, (DEPTH, d)),
        "w_nsa_out": nrm(21, (DEPTH, NSA_WIDTH, D_MODEL), NSA_WIDTH ** -0.5),
        "w_o": nrm(22, (DEPTH, D_MODEL, D_MODEL), D_MODEL ** -0.5),
        "rel_bias": nrm(23, (REL_BUCKETS, NSA_HEADS), 0.1),
        "ffn_w1": nrm(24, (n_dense, D_MODEL, D_FF), D_MODEL ** -0.5),
        "ffn_w3": nrm(25, (n_dense, D_MODEL, D_FF), D_MODEL ** -0.5),
        "ffn_w2": nrm(26, (n_dense, D_FF, D_MODEL), D_FF ** -0.5),
        "moe_router": nrm(27, (n_moe, D_MODEL, N_EXPERTS), D_MODEL ** -0.5),
        "moe_router_b": nrm(28, (n_moe, N_EXPERTS), 0.01),
        "moe_w1": nrm(29, (n_moe, N_EXPERTS, D_MODEL, D_FF_EXPERT), D_MODEL ** -0.5),
        "moe_w3": nrm(30, (n_moe, N_EXPERTS, D_MODEL, D_FF_EXPERT), D_MODEL ** -0.5),
        "moe_w2": nrm(31, (n_moe, N_EXPERTS, D_FF_EXPERT, D_MODEL), D_FF_EXPERT ** -0.5),
    }


def reference(x, c, w_ada, b_ada, g_mix, g_ffn, w_in, conv_w, conv_b, conv_ln_g, conv_ln_b, w_conv_out, w_sb_out,
              nsa_cmp_pos_k, nsa_cmp_pos_v, nsa_cmp_wk, nsa_cmp_wv, nsa_q_g, nsa_kc_g, nsa_ks_g, nsa_kw_g, w_nsa_out, w_o,
              rel_bias, ffn_w1, ffn_w3, ffn_w2, moe_router, moe_router_b, moe_w1, moe_w3, moe_w2):
    c_act = jax.nn.silu(c)
    for layer in range(DEPTH):
        mod = (c_act @ w_ada[layer] + b_ada[layer])[:, None, :]
        sh_mix, sc_mix, ga_mix, sh_ffn, sc_ffn, ga_ffn = jnp.split(mod, 6, axis=-1)
        h = rms_norm(x, g_mix[layer]) * (1 + sc_mix) + sh_mix
        y = hybrid_mixer(h, w_in[layer], conv_w[layer], conv_b[layer], conv_ln_g[layer], conv_ln_b[layer], w_conv_out[layer],
                         w_sb_out[layer], nsa_cmp_pos_k[layer], nsa_cmp_pos_v[layer], nsa_cmp_wk[layer], nsa_cmp_wv[layer],
                         nsa_q_g[layer], nsa_kc_g[layer], nsa_ks_g[layer], nsa_kw_g[layer], w_nsa_out[layer], w_o[layer], rel_bias)
        x = x + ga_mix * y
        h = rms_norm(x, g_ffn[layer]) * (1 + sc_ffn) + sh_ffn
        i = layer // 2
        if layer % 2 == 0:
            y = swiglu(h, ffn_w1[i], ffn_w3[i], ffn_w2[i])
        else:
            y = moe_swiglu(h, moe_router[i], moe_router_b[i], moe_w1[i], moe_w3[i], moe_w2[i])
        x = x + ga_ffn * y
    return x
```

```python
import functools
import math

import numpy as np
import jax
import jax.numpy as jnp
from jax import lax
from jax.experimental import pallas as pl
from jax.experimental.pallas import tpu as pltpu

F32 = jnp.float32
BF16 = jnp.bfloat16

D_MODEL = 2048
DEPTH = 4
D_CONV = 1024
CONV_WIDTH = 31
SB_HEADS = 8
HEAD_DIM = 128
NSA_HEADS = 8
NSA_KV_HEADS = 2
NSA_REP = NSA_HEADS // NSA_KV_HEADS
CMP_LEN = 32
CMP_STRIDE = 16
SLC_LEN = 64
SLC_TOP_N = 16
N_LOCAL_BLOCKS = 2
WINDOW = 512
FORCE_SCORE = 1e4
REL_BUCKETS = 32
REL_MAX_DIST = 128
D_FF = 5632
N_EXPERTS = 8
D_FF_EXPERT = 2816
EPS = 1e-6
NEG_INF = -1e30
TINY = 1e-20

LANES = 128
V7X_VMEM_LIMIT_BYTES = 56 * 1024 * 1024

NP_BLOCKS = 112
NP = NP_BLOCKS * LANES
CB_GLU_A, CB_GLU_G = 0, 8
CB_SB_Q, CB_SB_K, CB_SB_V = 16, 24, 32
CB_NQ = 40
CB_KC, CB_VC, CB_KS, CB_VS, CB_KW, CB_VW = 48, 50, 52, 54, 56, 58
CB_NGATE = 60
CB_MERGE = 64
RAW_GATE_COL = 7680
RAW_MERGE_COL = 7704
ROW_SH_MIX, ROW_SC_MIX, ROW_GA_MIX, ROW_SH_FFN, ROW_SC_FFN, ROW_GA_FFN = range(6)

SCALE = HEAD_DIM ** -0.5
TQ = 128
CONV_HALO = 32


def _cparams(n_axes):
    return pltpu.CompilerParams(
        dimension_semantics=("arbitrary",) * n_axes,
        vmem_limit_bytes=V7X_VMEM_LIMIT_BYTES,
    )


def _dot(a, b):
    return jnp.dot(a, b, preferred_element_type=F32)


def _dot_t(a, b):
    return lax.dot_general(a, b, (((1,), (1,)), ((), ())), preferred_element_type=F32)


def _split_dot(x, w01):
    hi = x.astype(BF16)
    lo = (x - hi.astype(F32)).astype(BF16)
    return _dot(hi, w01) + _dot(lo, w01)


def _norm_mod(x, g, sh, sc):
    ms = jnp.mean(x * x, axis=-1, keepdims=True)
    return (x * lax.rsqrt(ms + EPS) * g) * (1.0 + sc) + sh


def _rms(x, g):
    ms = jnp.mean(x * x, axis=-1, keepdims=True)
    return x * lax.rsqrt(ms + EPS) * g


def _ada_kernel(c_ref, w_ref, b_ref, o_ref):
    c = c_ref[...]
    ca = (c * jax.nn.sigmoid(c)).astype(BF16)
    o_ref[...] = _dot(ca, w_ref[...].astype(BF16)) + b_ref[...]


def _ada_all(c, w_ada, b_ada):
    depth, d, n = w_ada.shape
    nb = c.shape[0]
    b = 16
    c = jnp.zeros((b, d), c.dtype).at[:nb].set(c)
    tn = 1024
    out = pl.pallas_call(
        _ada_kernel,
        grid=(depth, n // tn),
        in_specs=[
            pl.BlockSpec((b, d), lambda l, j: (0, 0)),
            pl.BlockSpec((None, d, tn), lambda l, j: (l, 0, j)),
            pl.BlockSpec((None, 1, tn), lambda l, j: (l, 0, j)),
        ],
        out_specs=pl.BlockSpec((None, b, tn), lambda l, j: (l, 0, j)),
        out_shape=jax.ShapeDtypeStruct((depth, b, n), F32),
        compiler_params=_cparams(2),
        name="ada",
    )(c, w_ada, b_ada.reshape(depth, 1, n))
    return out[:, :nb]


def _bias_kernel(tab_ref, bk_ref, o_ref):
    h = pl.program_id(0)
    bk = bk_ref[...]
    acc = jnp.zeros(bk.shape, F32)
    for b in range(REL_BUCKETS):
        acc = jnp.where(bk == b, tab_ref[b, h], acc)
    o_ref[...] = acc


def _bias_expand(rel_bias, buckets):
    r = buckets.shape[0]
    tr = TQ
    return pl.pallas_call(
        _bias_kernel,
        grid=(NSA_HEADS, r // tr),
        in_specs=[
            pl.BlockSpec(memory_space=pltpu.SMEM),
            pl.BlockSpec((tr, LANES), lambda h, i: (i, 0)),
        ],
        out_specs=pl.BlockSpec((None, tr, LANES), lambda h, i: (h, i, 0)),
        out_shape=jax.ShapeDtypeStruct((NSA_HEADS, r, LANES), F32),
        compiler_params=_cparams(2),
        name="bias_expand",
    )(rel_bias, buckets)


def _rel_bucket(dist):
    n = jnp.maximum(dist, 0)
    max_exact = REL_BUCKETS // 2
    nf = jnp.maximum(n, 1).astype(F32)
    large = max_exact + (jnp.log(nf / max_exact) / math.log(REL_MAX_DIST / max_exact) * (REL_BUCKETS - max_exact)).astype(jnp.int32)
    large = jnp.minimum(large, REL_BUCKETS - 1)
    return jnp.where(n < max_exact, n, large)


def _bias_tables(rel_bias, s):
    t = jnp.arange(s, dtype=jnp.int32)[:, None]
    cend = jnp.arange(LANES, dtype=jnp.int32)[None, :] * CMP_STRIDE + (CMP_LEN - 1)
    bk_c = _rel_bucket(t - cend)
    q = jnp.arange(TQ, dtype=jnp.int32)[:, None]
    k = jnp.arange(TQ, dtype=jnp.int32)[None, :]
    bk_t = jnp.concatenate([_rel_bucket(q - k), _rel_bucket(TQ + q - k), _rel_bucket(2 * TQ + q - k)], axis=0)
    out = _bias_expand(rel_bias, jnp.concatenate([bk_c, bk_t], axis=0))
    bias_c = out[:, :s].reshape(NSA_KV_HEADS, NSA_REP, s, LANES)
    bias_t = out[:, s:].reshape(NSA_KV_HEADS, NSA_REP, 3, TQ, TQ).transpose(0, 2, 1, 3, 4)
    return bias_c, bias_t.reshape(NSA_KV_HEADS, 3, NSA_REP * TQ, TQ)


def _norm_matmul_kernel(x_ref, mod_ref, g_ref, w_ref, o_ref, h_ref, *, row_sh, row_sc):
    @pl.when(pl.program_id(1) == 0)
    def _():
        h = _norm_mod(x_ref[...], g_ref[...], mod_ref[row_sh:row_sh + 1, :], mod_ref[row_sc:row_sc + 1, :])
        h_ref[...] = h.astype(BF16)

    o_ref[...] = _dot(h_ref[...], w_ref[...]).astype(o_ref.dtype)


def _norm_matmul(x, mod, g, w, row_sh, row_sc, seq, tm=1024, tn=1024):
    m, d = x.shape
    n = w.shape[1]
    tpb = seq // tm
    kern = functools.partial(_norm_matmul_kernel, row_sh=row_sh, row_sc=row_sc)
    return pl.pallas_call(
        kern,
        grid=(m // tm, n // tn),
        in_specs=[
            pl.BlockSpec((tm, d), lambda i, j: (i, 0)),
            pl.BlockSpec((None, 6, d), lambda i, j: (i // tpb, 0, 0)),
            pl.BlockSpec((1, d), lambda i, j: (0, 0)),
            pl.BlockSpec((d, tn), lambda i, j: (0, j)),
        ],
        out_specs=pl.BlockSpec((tm, tn), lambda i, j: (i, j)),
        out_shape=jax.ShapeDtypeStruct((m, n), BF16),
        scratch_shapes=[pltpu.VMEM((tm, d), BF16)],
        compiler_params=_cparams(2),
        name="norm_matmul",
    )(x, mod, g.reshape(1, d), w)


def _conv_kernel(a_ref, g_ref, ap_ref, gp_ref, cw_ref, cb_ref, lng_ref, lnb_ref, o_ref, ubuf, vbuf, *, ts):
    i = pl.program_id(1)
    ubuf[CONV_HALO:, :] = a_ref[...].astype(F32) * jax.nn.sigmoid(g_ref[...].astype(F32))
    up = ap_ref[...].astype(F32) * jax.nn.sigmoid(gp_ref[...].astype(F32))
    ubuf[:CONV_HALO, :] = jnp.where(i > 0, up, 0.0)

    def chunk(c, carry):
        c0 = pl.multiple_of(c * LANES, LANES)
        acc = jnp.zeros((ts, LANES), F32) + cb_ref[:, pl.ds(c0, LANES)]
        for k in range(CONV_WIDTH):
            acc = acc + cw_ref[k:k + 1, pl.ds(c0, LANES)] * ubuf[pl.ds(CONV_HALO - (CONV_WIDTH - 1) + k, ts), pl.ds(c0, LANES)]
        vbuf[:, pl.ds(c0, LANES)] = acc
        return carry

    lax.fori_loop(0, D_CONV // LANES, chunk, 0)
    v = vbuf[...]
    mu = jnp.mean(v, axis=-1, keepdims=True)
    vc = v - mu
    var = jnp.mean(vc * vc, axis=-1, keepdims=True)
    y = vc * lax.rsqrt(var + EPS) * lng_ref[...] + lnb_ref[...]
    o_ref[...] = (y * jax.nn.sigmoid(y)).astype(o_ref.dtype)


def _conv_module(proj, conv_w, conv_b, ln_g, ln_b, batch, seq, ts=256):
    m = proj.shape[0]
    nt = seq // ts
    hb = ts // CONV_HALO
    kern = functools.partial(_conv_kernel, ts=ts)

    def prev_idx(col):
        return lambda b, i: (jnp.maximum((b * nt + i) * hb - 1, 0), col)

    return pl.pallas_call(
        kern,
        grid=(batch, nt),
        in_specs=[
            pl.BlockSpec((ts, D_CONV), lambda b, i: (b * nt + i, 0)),
            pl.BlockSpec((ts, D_CONV), lambda b, i: (b * nt + i, 1)),
            pl.BlockSpec((CONV_HALO, D_CONV), prev_idx(0)),
            pl.BlockSpec((CONV_HALO, D_CONV), prev_idx(1)),
            pl.BlockSpec((CONV_WIDTH, D_CONV), lambda b, i: (0, 0)),
            pl.BlockSpec((1, D_CONV), lambda b, i: (0, 0)),
            pl.BlockSpec((1, D_CONV), lambda b, i: (0, 0)),
            pl.BlockSpec((1, D_CONV), lambda b, i: (0, 0)),
        ],
        out_specs=pl.BlockSpec((ts, D_CONV), lambda b, i: (b * nt + i, 0)),
        out_shape=jax.ShapeDtypeStruct((m, D_CONV), BF16),
        scratch_shapes=[pltpu.VMEM((CONV_HALO + ts, D_CONV), F32), pltpu.VMEM((ts, D_CONV), F32)],
        compiler_params=_cparams(2),
        name="conv_module",
    )(proj, proj, proj, proj, conv_w, conv_b.reshape(1, -1), ln_g.reshape(1, -1), ln_b.reshape(1, -1))


def _sb_kernel(q_ref, k_ref, v_ref, o_ref):
    i = pl.program_id(2)
    q = q_ref[...]
    row = lax.broadcasted_iota(jnp.int32, (TQ, TQ), 0)
    col = lax.broadcasted_iota(jnp.int32, (TQ, TQ), 1)
    later = jnp.where(row > col, 1.0, 0.0).astype(BF16)
    before = col < row

    def tile(j, rsum, acc, diag):
        k0 = pl.multiple_of(j * TQ, TQ)
        k = k_ref[pl.ds(k0, TQ), :]
        v = v_ref[pl.ds(k0, TQ), :]
        z = _dot_t(q, k) * SCALE
        sp = jnp.log1p(jnp.exp(-jnp.abs(z)))
        log_beta = jnp.minimum(z, 0.0) - sp
        log_keep = -jnp.maximum(z, 0.0) - sp
        if diag:
            log_keep = jnp.where(before, log_keep, 0.0)
        log_survive = _split_dot(log_keep, later) + rsum
        a = jnp.exp(log_beta + log_survive)
        if diag:
            a = jnp.where(before, a, 0.0)
        acc = acc + _dot(a.astype(BF16), v)
        rsum = rsum + jnp.sum(log_keep, axis=-1, keepdims=True)
        return rsum, acc

    rsum, acc = tile(i, jnp.zeros((TQ, 1), F32), jnp.zeros((TQ, HEAD_DIM), F32), True)

    def body(jj, carry):
        return tile(i - 1 - jj, carry[0], carry[1], False)

    rsum, acc = lax.fori_loop(0, i, body, (rsum, acc))
    o_ref[...] = acc.astype(o_ref.dtype)


def _sb_attention(proj3):
    b, s, _ = proj3.shape
    return pl.pallas_call(
        _sb_kernel,
        grid=(b, SB_HEADS, s // TQ),
        in_specs=[
            pl.BlockSpec((None, TQ, HEAD_DIM), lambda bi, h, i: (bi, i, CB_SB_Q + h)),
            pl.BlockSpec((None, s, HEAD_DIM), lambda bi, h, i: (bi, 0, CB_SB_K + h)),
            pl.BlockSpec((None, s, HEAD_DIM), lambda bi, h, i: (bi, 0, CB_SB_V + h)),
        ],
        out_specs=pl.BlockSpec((None, TQ, HEAD_DIM), lambda bi, h, i: (bi, i, h)),
        out_shape=jax.ShapeDtypeStruct((b, s, SB_HEADS * HEAD_DIM), BF16),
        compiler_params=_cparams(3),
        name="sb_attention",
    )(proj3, proj3, proj3)


def _nsa_prep_kernel(kc_ref, vc_ref, ks_ref, kw_ref, pk_ref, pv_ref, wk_ref, wv_ref, kcg_ref, ksg_ref, kwg_ref,
                     kcb_ref, vcb_ref, ksn_ref, kwn_ref):
    half = CMP_STRIDE * HEAD_DIM

    def compress(a_ref, p_ref, w_ref):
        a = a_ref[...].astype(F32)
        top = _dot((a + p_ref[0:1, :]).astype(BF16), w_ref[0:half, :])
        bot = _dot((a + p_ref[1:2, :]).astype(BF16), w_ref[half:2 * half, :])
        return top + pltpu.roll(bot, bot.shape[0] - 1, axis=0)

    kcb_ref[...] = _rms(compress(kc_ref, pk_ref, wk_ref), kcg_ref[...]).astype(BF16)
    vcb_ref[...] = compress(vc_ref, pv_ref, wv_ref).astype(BF16)
    ksn_ref[...] = _rms(ks_ref[...].astype(F32), ksg_ref[...]).astype(BF16)
    kwn_ref[...] = _rms(kw_ref[...].astype(F32), kwg_ref[...]).astype(BF16)


def _nsa_prep(proj3, kc_chunks, vc_chunks, pos_k, pos_v, wk, wv, kc_g, ks_g, kw_g):
    b, s, _ = proj3.shape
    g_n = NSA_KV_HEADS
    nch = s // CMP_STRIDE
    half = CMP_STRIDE * HEAD_DIM
    vec = lambda: pl.BlockSpec((1, HEAD_DIM), lambda bi, g: (0, 0))
    small = jax.ShapeDtypeStruct((b, g_n, nch, HEAD_DIM), BF16)
    full = jax.ShapeDtypeStruct((b, g_n, s, HEAD_DIM), BF16)
    return pl.pallas_call(
        _nsa_prep_kernel,
        grid=(b, g_n),
        in_specs=[
            pl.BlockSpec((None, None, nch, half), lambda bi, g: (bi, g, 0, 0)),
            pl.BlockSpec((None, None, nch, half), lambda bi, g: (bi, g, 0, 0)),
            pl.BlockSpec((None, s, HEAD_DIM), lambda bi, g: (bi, 0, CB_KS + g)),
            pl.BlockSpec((None, s, HEAD_DIM), lambda bi, g: (bi, 0, CB_KW + g)),
            pl.BlockSpec((2, half), lambda bi, g: (0, 0)),
            pl.BlockSpec((2, half), lambda bi, g: (0, 0)),
            pl.BlockSpec((2 * half, HEAD_DIM), lambda bi, g: (0, 0)),
            pl.BlockSpec((2 * half, HEAD_DIM), lambda bi, g: (0, 0)),
            vec(), vec(), vec(),
        ],
        out_specs=[
            pl.BlockSpec((None, None, nch, HEAD_DIM), lambda bi, g: (bi, g, 0, 0)),
            pl.BlockSpec((None, None, nch, HEAD_DIM), lambda bi, g: (bi, g, 0, 0)),
            pl.BlockSpec((None, None, s, HEAD_DIM), lambda bi, g: (bi, g, 0, 0)),
            pl.BlockSpec((None, None, s, HEAD_DIM), lambda bi, g: (bi, g, 0, 0)),
        ],
        out_shape=[small, small, full, full],
        compiler_params=_cparams(2),
        name="nsa_prep",
    )(kc_chunks, vc_chunks, proj3, proj3, pos_k.reshape(2, half), pos_v.reshape(2, half),
      wk.astype(BF16), wv.astype(BF16), kc_g.reshape(1, -1), ks_g.reshape(1, -1), kw_g.reshape(1, -1))


def _nsa_kernel(q_ref, kcb_ref, vcb_ref, ksn_ref, vs_ref, kwn_ref, vw_ref, gate_ref, bc_ref, bt_ref, qg_ref, ov_ref,
                o_ref):
    g = pl.program_id(1)
    i = pl.program_id(2)
    r_n = NSA_REP
    rows = r_n * TQ

    qs = []
    for r in range(r_n):
        qr = q_ref[:, r * HEAD_DIM:(r + 1) * HEAD_DIM].astype(F32)
        qs.append(_rms(qr, qg_ref[...]).astype(BF16))
    q = jnp.concatenate(qs, axis=0)

    rowq = lax.broadcasted_iota(jnp.int32, (TQ, LANES), 0)
    lane = lax.broadcasted_iota(jnp.int32, (TQ, LANES), 1)
    tpos = i * TQ + rowq
    lane_r = lax.broadcasted_iota(jnp.int32, (rows, LANES), 1)
    tpos_r = i * TQ + (lax.broadcasted_iota(jnp.int32, (rows, LANES), 0) & (TQ - 1))

    def rep(x):
        return jnp.concatenate([x] * r_n, axis=0)

    mask_c = lane_r * CMP_STRIDE + (CMP_LEN - 1) <= tpos_r
    logit_c = _dot_t(q, kcb_ref[...]) * SCALE + bc_ref[...].reshape(rows, LANES)
    logit_c = jnp.where(mask_c, logit_c, NEG_INF)
    m_c = jnp.max(logit_c, axis=-1, keepdims=True)
    p_c = jnp.where(mask_c, jnp.exp(logit_c - m_c), 0.0)
    p_c = p_c / jnp.maximum(jnp.sum(p_c, axis=-1, keepdims=True), TINY)
    o_c = _dot(p_c.astype(BF16), vcb_ref[...])

    p_sum = p_c[0:TQ]
    for r in range(1, r_n):
        p_sum = p_sum + p_c[r * TQ:(r + 1) * TQ]
    imp = _split_dot(p_sum, ov_ref[...])
    n_sel = ov_ref.shape[0] * CMP_STRIDE // SLC_LEN
    tb = jnp.right_shift(tpos, int(math.log2(SLC_LEN)))
    causal_blk = lane <= tb
    forced = (lane == 0) | (causal_blk & (lane > tb - N_LOCAL_BLOCKS))
    score = jnp.where(forced, FORCE_SCORE, jnp.where(causal_blk, imp, -FORCE_SCORE))
    score = jnp.where(lane < n_sel, score, -4.0 * FORCE_SCORE)
    rank = jnp.zeros((TQ, LANES), F32)
    for b in range(n_sel):
        sb = jnp.broadcast_to(score[:, b:b + 1], (TQ, LANES))
        beats = (sb > score) | ((sb == score) & (lane > b))
        rank = rank + jnp.where(beats, 1.0, 0.0)
    sel = jnp.where((rank < float(min(SLC_TOP_N, n_sel))) & causal_blk, 1.0, 0.0).astype(BF16)

    blk_row = lax.broadcasted_iota(jnp.int32, (LANES, TQ), 0)
    key_off = lax.broadcasted_iota(jnp.int32, (LANES, TQ), 1)

    def attend(k_ref, v_ref, first_tile, n_tiles, mask_fn):
        def body(jj, carry):
            m, l, acc = carry
            kt = first_tile + jj
            k0 = pl.multiple_of(kt * TQ, TQ)
            mask = mask_fn(kt, kt * TQ + lane_r)
            s = _dot_t(q, k_ref[pl.ds(k0, TQ), :]) * SCALE + bt_ref[jnp.minimum(i - kt, 2)]
            s = jnp.where(mask, s, NEG_INF)
            m_new = jnp.maximum(m, jnp.max(s, axis=-1, keepdims=True))
            p = jnp.where(mask, jnp.exp(s - m_new), 0.0)
            alpha = jnp.exp(m - m_new)
            l = alpha * l + jnp.sum(p, axis=-1, keepdims=True)
            acc = alpha * acc + _dot(p.astype(BF16), v_ref[pl.ds(k0, TQ), :])
            return m_new, l, acc

        init = (jnp.full((rows, 1), NEG_INF, F32), jnp.zeros((rows, 1), F32), jnp.zeros((rows, HEAD_DIM), F32))
        _, l, acc = lax.fori_loop(0, n_tiles, body, init)
        return acc / jnp.maximum(l, TINY)

    def sel_mask(kt, kpos):
        expand = jnp.where(blk_row == jnp.right_shift(kt * TQ + key_off, int(math.log2(SLC_LEN))), 1.0, 0.0).astype(BF16)
        return (rep(_dot(sel, expand)) > 0.5) & (kpos <= tpos_r)

    def win_mask(kt, kpos):
        dist = tpos_r - kpos
        return (dist >= 0) & (dist < WINDOW)

    o_s = attend(ksn_ref, vs_ref, 0, i + 1, sel_mask)
    first_w = jnp.maximum(i - WINDOW // TQ, 0)
    o_w = attend(kwn_ref, vw_ref, first_w, i + 1 - first_w, win_mask)

    gates = jax.nn.sigmoid(gate_ref[...].astype(F32))

    def gate_col(idx):
        return jnp.sum(jnp.where(lane == idx, gates, 0.0), axis=-1, keepdims=True)

    for r in range(r_n):
        h = g * r_n + r
        sl = slice(r * TQ, (r + 1) * TQ)
        o = gate_col(h) * o_c[sl] + gate_col(NSA_HEADS + h) * o_s[sl] + gate_col(2 * NSA_HEADS + h) * o_w[sl]
        o_ref[:, r * HEAD_DIM:(r + 1) * HEAD_DIM] = o.astype(o_ref.dtype)


def _overlap_matrix(s):
    nch = s // CMP_STRIDE
    nsel = s // SLC_LEN
    ci = np.arange(nch)[:, None]
    sj = np.arange(LANES)[None, :]
    ov = (ci * CMP_STRIDE <= sj * SLC_LEN + SLC_LEN - 1) & (ci * CMP_STRIDE + CMP_LEN - 1 >= sj * SLC_LEN)
    ov = ov & (sj < nsel) & (ci < nch - 1)
    return jnp.asarray(ov, BF16)


def _nsa_attention(proj3, kcb, vcb, ksn, kwn, bias_c, bias_t, q_g):
    b, s, _ = proj3.shape
    g_n, r_n = NSA_KV_HEADS, NSA_REP
    nch = s // CMP_STRIDE
    gw = r_n * HEAD_DIM
    return pl.pallas_call(
        _nsa_kernel,
        grid=(b, g_n, s // TQ),
        in_specs=[
            pl.BlockSpec((None, TQ, gw), lambda bi, g, i: (bi, i, CB_NQ // r_n + g)),
            pl.BlockSpec((None, None, nch, HEAD_DIM), lambda bi, g, i: (bi, g, 0, 0)),
            pl.BlockSpec((None, None, nch, HEAD_DIM), lambda bi, g, i: (bi, g, 0, 0)),
            pl.BlockSpec((None, None, s, HEAD_DIM), lambda bi, g, i: (bi, g, 0, 0)),
            pl.BlockSpec((None, s, HEAD_DIM), lambda bi, g, i: (bi, 0, CB_VS + g)),
            pl.BlockSpec((None, None, s, HEAD_DIM), lambda bi, g, i: (bi, g, 0, 0)),
            pl.BlockSpec((None, s, HEAD_DIM), lambda bi, g, i: (bi, 0, CB_VW + g)),
            pl.BlockSpec((None, TQ, LANES), lambda bi, g, i: (bi, i, CB_NGATE)),
            pl.BlockSpec((None, r_n, TQ, LANES), lambda bi, g, i: (g, 0, i, 0)),
            pl.BlockSpec((None, 3, r_n * TQ, TQ), lambda bi, g, i: (g, 0, 0, 0)),
            pl.BlockSpec((1, HEAD_DIM), lambda bi, g, i: (0, 0)),
            pl.BlockSpec((nch, LANES), lambda bi, g, i: (0, 0)),
        ],
        out_specs=pl.BlockSpec((None, TQ, gw), lambda bi, g, i: (bi, i, g)),
        out_shape=jax.ShapeDtypeStruct((b, s, NSA_HEADS * HEAD_DIM), BF16),
        compiler_params=_cparams(3),
        name="nsa_attention",
    )(proj3, kcb, vcb, ksn, proj3, kwn, proj3, proj3, bias_c, bias_t, q_g.reshape(1, -1), _overlap_matrix(s))


def _merge_kernel(u_ref, sb_ref, ns_ref, gc_ref, gs_ref, gn_ref, wc_ref, ws_ref, wn_ref, o_ref):
    m = jax.nn.sigmoid(gc_ref[...].astype(F32)) * _dot(u_ref[...], wc_ref[...])
    m = m + jax.nn.sigmoid(gs_ref[...].astype(F32)) * _dot(sb_ref[...], ws_ref[...])
    m = m + jax.nn.sigmoid(gn_ref[...].astype(F32)) * _dot(ns_ref[...], wn_ref[...])
    o_ref[...] = m.astype(o_ref.dtype)


def _merge(u_act, sb, nsa, proj, wc, ws, wn, tm=512, tn=1024):
    m = u_act.shape[0]
    d = wc.shape[1]
    kc = wc.shape[0]
    gate0 = CB_MERGE * LANES // tn
    gstep = d // tn
    act = lambda: pl.BlockSpec((tm, kc), lambda i, j: (i, 0))
    gate = lambda n: pl.BlockSpec((tm, tn), lambda i, j: (i, gate0 + n * gstep + j))
    wsp = lambda: pl.BlockSpec((kc, tn), lambda i, j: (0, j))
    return pl.pallas_call(
        _merge_kernel,
        grid=(m // tm, d // tn),
        in_specs=[act(), act(), act(), gate(0), gate(1), gate(2), wsp(), wsp(), wsp()],
        out_specs=pl.BlockSpec((tm, tn), lambda i, j: (i, j)),
        out_shape=jax.ShapeDtypeStruct((m, d), BF16),
        compiler_params=_cparams(2),
        name="merge",
    )(u_act, sb, nsa, proj, proj, proj, wc, ws, wn)


def _mm_res_kernel(a_ref, w_ref, x_ref, mod_ref, o_ref, acc_ref, *, row_ga, nk):
    k = pl.program_id(2)

    @pl.when(k == 0)
    def _():
        acc_ref[...] = jnp.zeros_like(acc_ref)

    acc_ref[...] += _dot(a_ref[...], w_ref[...])

    @pl.when(k == nk - 1)
    def _():
        o_ref[...] = x_ref[...] + mod_ref[row_ga:row_ga + 1, :] * acc_ref[...]


def _mm_res(a, w, x, mod, row_ga, seq, tm=1024, tn=1024, tk=None):
    m, ka = a.shape
    n = x.shape[1]
    tpb = seq // tm
    if w.ndim == 3:
        tk = w.shape[1]
        nk = w.shape[0]
        w_spec = pl.BlockSpec((None, tk, tn), lambda i, j, k: (k, 0, j))
    else:
        tk = tk or ka
        nk = ka // tk
        w_spec = pl.BlockSpec((tk, tn), lambda i, j, k: (k, j))
    kern = functools.partial(_mm_res_kernel, row_ga=row_ga, nk=nk)
    return pl.pallas_call(
        kern,
        grid=(m // tm, n // tn, nk),
        in_specs=[
            pl.BlockSpec((tm, tk), lambda i, j, k: (i, k)),
            w_spec,
            pl.BlockSpec((tm, tn), lambda i, j, k: (i, j)),
            pl.BlockSpec((None, 6, tn), lambda i, j, k: (i // tpb, 0, j)),
        ],
        out_specs=pl.BlockSpec((tm, tn), lambda i, j, k: (i, j)),
        out_shape=jax.ShapeDtypeStruct((m, n), F32),
        scratch_shapes=[pltpu.VMEM((tm, tn), F32)],
        compiler_params=_cparams(3),
        name="matmul_residual",
    )(a, w, x, mod)


def _ffn_up_kernel(x_ref, mod_ref, g_ref, w1_ref, w3_ref, o_ref, h_ref):
    @pl.when(pl.program_id(1) == 0)
    def _():
        h = _norm_mod(x_ref[...], g_ref[...], mod_ref[ROW_SH_FFN:ROW_SH_FFN + 1, :], mod_ref[ROW_SC_FFN:ROW_SC_FFN + 1, :])
        h_ref[...] = h.astype(BF16)

    h = h_ref[...]
    a = _dot(h, w1_ref[...])
    o_ref[...] = (a * jax.nn.sigmoid(a) * _dot(h, w3_ref[...])).astype(o_ref.dtype)


def _ffn_up(x, mod, g, w1, w3, seq, tm=1024, tn=512):
    m, d = x.shape
    f = w1.shape[1]
    tpb = seq // tm
    return pl.pallas_call(
        _ffn_up_kernel,
        grid=(m // tm, f // tn),
        in_specs=[
            pl.BlockSpec((tm, d), lambda i, j: (i, 0)),
            pl.BlockSpec((None, 6, d), lambda i, j: (i // tpb, 0, 0)),
            pl.BlockSpec((1, d), lambda i, j: (0, 0)),
            pl.BlockSpec((d, tn), lambda i, j: (0, j)),
            pl.BlockSpec((d, tn), lambda i, j: (0, j)),
        ],
        out_specs=pl.BlockSpec((tm, tn), lambda i, j: (i, j)),
        out_shape=jax.ShapeDtypeStruct((m, f), BF16),
        scratch_shapes=[pltpu.VMEM((tm, d), BF16)],
        compiler_params=_cparams(2),
        name="ffn_up",
    )(x, mod, g.reshape(1, d), w1, w3)


def _moe_up_kernel(x_ref, mod_ref, g_ref, cmb_ref, w1_ref, w3_ref, o_ref, h_ref):
    e = pl.program_id(1)

    @pl.when((e == 0) & (pl.program_id(2) == 0))
    def _():
        h = _norm_mod(x_ref[...], g_ref[...], mod_ref[ROW_SH_FFN:ROW_SH_FFN + 1, :], mod_ref[ROW_SC_FFN:ROW_SC_FFN + 1, :])
        h_ref[...] = h.astype(BF16)

    h = h_ref[...]
    cmb = cmb_ref[...]
    lane = lax.broadcasted_iota(jnp.int32, cmb.shape, 1)
    ce = jnp.sum(jnp.where(lane == e, cmb, 0.0), axis=-1, keepdims=True)
    a = _dot(h, w1_ref[...])
    o_ref[...] = (ce * (a * jax.nn.sigmoid(a) * _dot(h, w3_ref[...]))).astype(o_ref.dtype)


def _moe_up(x, mod, g, combine, w1, w3, seq, tm=1024, tn=256):
    m, d = x.shape
    n_e, _, fe = w1.shape
    nj = fe // tn
    tpb = seq // tm
    return pl.pallas_call(
        _moe_up_kernel,
        grid=(m // tm, n_e, nj),
        in_specs=[
            pl.BlockSpec((tm, d), lambda i, e, j: (i, 0)),
            pl.BlockSpec((None, 6, d), lambda i, e, j: (i // tpb, 0, 0)),
            pl.BlockSpec((1, d), lambda i, e, j: (0, 0)),
            pl.BlockSpec((tm, LANES), lambda i, e, j: (i, 0)),
            pl.BlockSpec((None, d, tn), lambda i, e, j: (e, 0, j)),
            pl.BlockSpec((None, d, tn), lambda i, e, j: (e, 0, j)),
        ],
        out_specs=pl.BlockSpec((tm, tn), lambda i, e, j: (i, e * nj + j)),
        out_shape=jax.ShapeDtypeStruct((m, n_e * fe), BF16),
        scratch_shapes=[pltpu.VMEM((tm, d), BF16)],
        compiler_params=_cparams(3),
        name="moe_up",
    )(x, mod, g.reshape(1, d), combine, w1, w3)


def _router_kernel(x_ref, mod_ref, g_ref, wr_ref, br_ref, o_ref):
    h = _norm_mod(x_ref[...], g_ref[...], mod_ref[ROW_SH_FFN:ROW_SH_FFN + 1, :], mod_ref[ROW_SC_FFN:ROW_SC_FFN + 1, :])
    logits = jnp.dot(h, wr_ref[...], precision=lax.Precision.HIGHEST, preferred_element_type=F32) + br_ref[...]
    lane = lax.broadcasted_iota(jnp.int32, logits.shape, 1).astype(F32)
    pad = float(LANES)
    m1 = jnp.max(logits, axis=-1, keepdims=True)
    i1 = jnp.min(jnp.where(logits == m1, lane, pad), axis=-1, keepdims=True)
    rest = jnp.where(lane == i1, NEG_INF, logits)
    m2 = jnp.max(rest, axis=-1, keepdims=True)
    i2 = jnp.min(jnp.where(rest == m2, lane, pad), axis=-1, keepdims=True)
    e2 = jnp.exp(m2 - m1)
    w1 = 1.0 / (1.0 + e2)
    w2 = e2 / (1.0 + e2)
    o_ref[...] = jnp.where(lane == i1, w1, 0.0) + jnp.where(lane == i2, w2, 0.0)


def _router(x, mod, g, w_router, b_router, seq, tm=512):
    m, d = x.shape
    n_e = w_router.shape[1]
    tpb = seq // tm
    wr = jnp.zeros((d, LANES), F32).at[:, :n_e].set(w_router)
    br = jnp.full((1, LANES), 2.0 * NEG_INF, F32).at[0, :n_e].set(b_router)
    return pl.pallas_call(
        _router_kernel,
        grid=(m // tm,),
        in_specs=[
            pl.BlockSpec((tm, d), lambda i: (i, 0)),
            pl.BlockSpec((None, 6, d), lambda i: (i // tpb, 0, 0)),
            pl.BlockSpec((1, d), lambda i: (0, 0)),
            pl.BlockSpec((d, LANES), lambda i: (0, 0)),
            pl.BlockSpec((1, LANES), lambda i: (0, 0)),
        ],
        out_specs=pl.BlockSpec((tm, LANES), lambda i: (i, 0)),
        out_shape=jax.ShapeDtypeStruct((m, LANES), F32),
        compiler_params=_cparams(1),
        name="router",
    )(x, mod, g.reshape(1, d), wr, br)


def _pad_w_in(w):
    d = w.shape[0]
    gap = CB_MERGE * LANES - RAW_MERGE_COL
    return jnp.concatenate([w[:, :RAW_MERGE_COL], jnp.zeros((d, gap), w.dtype), w[:, RAW_MERGE_COL:]], axis=1).astype(BF16)


def _kv_chunks(proj3, cb):
    b, s, _ = proj3.shape
    t = proj3[:, :, cb * LANES:(cb + NSA_KV_HEADS) * LANES].reshape(b, s, NSA_KV_HEADS, HEAD_DIM)
    return t.transpose(0, 2, 1, 3).reshape(b, NSA_KV_HEADS, s // CMP_STRIDE, CMP_STRIDE * HEAD_DIM)


def kernel(x, c, w_ada, b_ada, g_mix, g_ffn, w_in, conv_w, conv_b, conv_ln_g, conv_ln_b, w_conv_out, w_sb_out,
           nsa_cmp_pos_k, nsa_cmp_pos_v, nsa_cmp_wk, nsa_cmp_wv, nsa_q_g, nsa_kc_g, nsa_ks_g, nsa_kw_g, w_nsa_out, w_o,
           rel_bias, ffn_w1, ffn_w3, ffn_w2, moe_router, moe_router_b, moe_w1, moe_w3, moe_w2):
    b, s, d = x.shape
    m = b * s
    depth = w_ada.shape[0]
    mod_all = _ada_all(c, w_ada, b_ada)
    bias_c, bias_t = _bias_tables(rel_bias, s)
    xf = x.reshape(m, d)
    for l in range(depth):
        mod = mod_all[l].reshape(b, 6, d)
        proj = _norm_matmul(xf, mod, g_mix[l], _pad_w_in(w_in[l]), ROW_SH_MIX, ROW_SC_MIX, s)
        proj3 = proj.reshape(b, s, NP)
        u_act = _conv_module(proj, conv_w[l], conv_b[l], conv_ln_g[l], conv_ln_b[l], b, s)
        sb = _sb_attention(proj3)
        kcb, vcb, ksn, kwn = _nsa_prep(proj3, _kv_chunks(proj3, CB_KC), _kv_chunks(proj3, CB_VC), nsa_cmp_pos_k[l],
                                       nsa_cmp_pos_v[l], nsa_cmp_wk[l], nsa_cmp_wv[l], nsa_kc_g[l], nsa_ks_g[l],
                                       nsa_kw_g[l])
        nsa = _nsa_attention(proj3, kcb, vcb, ksn, kwn, bias_c, bias_t, nsa_q_g[l])
        merged = _merge(u_act, sb.reshape(m, -1), nsa.reshape(m, -1), proj, w_conv_out[l].astype(BF16),
                        w_sb_out[l].astype(BF16), w_nsa_out[l].astype(BF16))
        xf = _mm_res(merged, w_o[l].astype(BF16), xf, mod, ROW_GA_MIX, s)
        i = l // 2
        if l % 2 == 0:
            act = _ffn_up(xf, mod, g_ffn[l], ffn_w1[i].astype(BF16), ffn_w3[i].astype(BF16), s)
            xf = _mm_res(act, ffn_w2[i].astype(BF16), xf, mod, ROW_GA_FFN, s, tm=512, tn=512)
        else:
            combine = _router(xf, mod, g_ffn[l], moe_router[i], moe_router_b[i], s)
            act = _moe_up(xf, mod, g_ffn[l], combine, moe_w1[i].astype(BF16), moe_w3[i].astype(BF16), s)
            xf = _mm_res(act, moe_w2[i].astype(BF16), xf, mod, ROW_GA_FFN, s)
    return xf.reshape(b, s, d)
```

```python
import functools
import math

import numpy as np
import jax
import jax.numpy as jnp
from jax import lax
from jax.experimental import pallas as pl
from jax.experimental.pallas import tpu as pltpu

F32 = jnp.float32
BF16 = jnp.bfloat16

D_MODEL = 2048
DEPTH = 4
D_CONV = 1024
CONV_WIDTH = 31
SB_HEADS = 8
HEAD_DIM = 128
NSA_HEADS = 8
NSA_KV_HEADS = 2
NSA_REP = NSA_HEADS // NSA_KV_HEADS
CMP_LEN = 32
CMP_STRIDE = 16
SLC_LEN = 64
SLC_TOP_N = 16
N_LOCAL_BLOCKS = 2
WINDOW = 512
FORCE_SCORE = 1e4
REL_BUCKETS = 32
REL_MAX_DIST = 128
D_FF = 5632
N_EXPERTS = 8
D_FF_EXPERT = 2816
EPS = 1e-6
NEG_INF = -1e30
TINY = 1e-20

LANES = 128
V7X_VMEM_LIMIT_BYTES = 56 * 1024 * 1024

NP_BLOCKS = 112
NP = NP_BLOCKS * LANES
CB_GLU_A, CB_GLU_G = 0, 8
CB_SB_Q, CB_SB_K, CB_SB_V = 16, 24, 32
CB_NQ = 40
CB_KC, CB_VC, CB_KS, CB_VS, CB_KW, CB_VW = 48, 50, 52, 54, 56, 58
CB_NGATE = 60
CB_MERGE = 64
RAW_GATE_COL = 7680
RAW_MERGE_COL = 7704
ROW_SH_MIX, ROW_SC_MIX, ROW_GA_MIX, ROW_SH_FFN, ROW_SC_FFN, ROW_GA_FFN = range(6)

SCALE = HEAD_DIM ** -0.5
TQ = 128
SB_TILE = 256
SB_GROUP = 4
SEL_CHUNK = 512
WIN_SPAN = WINDOW + TQ
BT_DIAG, BT_SUB, BT_FAR, BT_FAR_UPPER, BT_MASKED = range(5)
CONV_HALO = 32


def _cparams(n_axes):
    return pltpu.CompilerParams(
        dimension_semantics=("arbitrary",) * n_axes,
        vmem_limit_bytes=V7X_VMEM_LIMIT_BYTES,
    )


def _dot(a, b):
    return jnp.dot(a, b, preferred_element_type=F32)


def _dot_t(a, b):
    return lax.dot_general(a, b, (((1,), (1,)), ((), ())), preferred_element_type=F32)


def _split_dot(x, w01):
    hi = x.astype(BF16)
    lo = (x - hi.astype(F32)).astype(BF16)
    return _dot(hi, w01) + _dot(lo, w01)


def _norm_mod(x, g, sh, sc):
    ms = jnp.mean(x * x, axis=-1, keepdims=True)
    return (x * lax.rsqrt(ms + EPS) * g) * (1.0 + sc) + sh


def _rms(x, g):
    ms = jnp.mean(x * x, axis=-1, keepdims=True)
    return x * lax.rsqrt(ms + EPS) * g


def _ada_kernel(c_ref, w_ref, b_ref, o_ref):
    c = c_ref[...]
    ca = (c * jax.nn.sigmoid(c)).astype(BF16)
    o_ref[...] = _dot(ca, w_ref[...].astype(BF16)) + b_ref[...]


def _ada_all(c, w_ada, b_ada):
    depth, d, n = w_ada.shape
    nb = c.shape[0]
    b = 16
    c = jnp.zeros((b, d), c.dtype).at[:nb].set(c)
    tn = 1024
    out = pl.pallas_call(
        _ada_kernel,
        grid=(depth, n // tn),
        in_specs=[
            pl.BlockSpec((b, d), lambda l, j: (0, 0)),
            pl.BlockSpec((None, d, tn), lambda l, j: (l, 0, j)),
            pl.BlockSpec((None, 1, tn), lambda l, j: (l, 0, j)),
        ],
        out_specs=pl.BlockSpec((None, b, tn), lambda l, j: (l, 0, j)),
        out_shape=jax.ShapeDtypeStruct((depth, b, n), F32),
        compiler_params=_cparams(2),
        name="ada",
    )(c, w_ada, b_ada.reshape(depth, 1, n))
    return out[:, :nb]


def _bias_kernel(tab_ref, bk_ref, o_ref):
    h = pl.program_id(0)
    bk = bk_ref[...]
    acc = jnp.zeros(bk.shape, F32)
    for b in range(REL_BUCKETS):
        acc = jnp.where(bk == b, tab_ref[b, h], acc)
    o_ref[...] = acc


def _bias_expand(rel_bias, buckets):
    r = buckets.shape[0]
    tr = r
    return pl.pallas_call(
        _bias_kernel,
        grid=(NSA_HEADS, r // tr),
        in_specs=[
            pl.BlockSpec(memory_space=pltpu.SMEM),
            pl.BlockSpec((tr, LANES), lambda h, i: (i, 0)),
        ],
        out_specs=pl.BlockSpec((None, tr, LANES), lambda h, i: (h, i, 0)),
        out_shape=jax.ShapeDtypeStruct((NSA_HEADS, r, LANES), F32),
        compiler_params=_cparams(2),
        name="bias_expand",
    )(rel_bias, buckets)


def _rel_bucket(dist):
    n = jnp.maximum(dist, 0)
    max_exact = REL_BUCKETS // 2
    nf = jnp.maximum(n, 1).astype(F32)
    large = max_exact + (jnp.log(nf / max_exact) / math.log(REL_MAX_DIST / max_exact) * (REL_BUCKETS - max_exact)).astype(jnp.int32)
    large = jnp.minimum(large, REL_BUCKETS - 1)
    return jnp.where(n < max_exact, n, large)


def _bias_tables(rel_bias, s):
    t = jnp.arange(s, dtype=jnp.int32)[:, None]
    cend = jnp.arange(LANES, dtype=jnp.int32)[None, :] * CMP_STRIDE + (CMP_LEN - 1)
    q = jnp.arange(TQ, dtype=jnp.int32)[:, None]
    k = jnp.arange(TQ, dtype=jnp.int32)[None, :]
    bk_t = jnp.concatenate([_rel_bucket(q - k), _rel_bucket(TQ + q - k), _rel_bucket(2 * TQ + q - k)], axis=0)
    out = _bias_expand(rel_bias, jnp.concatenate([_rel_bucket(t - cend), bk_t], axis=0))
    bias_c = jnp.where(t >= cend, out[:, :s], NEG_INF).reshape(NSA_KV_HEADS, NSA_REP, s, LANES)
    diag, sub, far = out[:, s:s + TQ], out[:, s + TQ:s + 2 * TQ], out[:, s + 2 * TQ:]
    tiles = jnp.stack([jnp.where(q >= k, diag, NEG_INF), sub, far, jnp.where(k > q, far, NEG_INF),
                       jnp.full_like(far, NEG_INF)], axis=1)
    tiles = tiles.reshape(NSA_KV_HEADS, NSA_REP, 5, TQ, TQ).transpose(0, 2, 1, 3, 4)
    return bias_c, tiles.reshape(NSA_KV_HEADS, 5, NSA_REP * TQ, TQ)


def _norm_matmul_kernel(x_ref, mod_ref, g_ref, w_ref, o_ref, h_ref, *, row_sh, row_sc):
    @pl.when(pl.program_id(1) == 0)
    def _():
        h = _norm_mod(x_ref[...], g_ref[...], mod_ref[row_sh:row_sh + 1, :], mod_ref[row_sc:row_sc + 1, :])
        h_ref[...] = h.astype(BF16)

    o_ref[...] = _dot(h_ref[...], w_ref[...]).astype(o_ref.dtype)


def _norm_matmul(x, mod, g, w, row_sh, row_sc, seq, tm=1024, tn=1024):
    m, d = x.shape
    n = w.shape[1]
    tpb = seq // tm
    kern = functools.partial(_norm_matmul_kernel, row_sh=row_sh, row_sc=row_sc)
    return pl.pallas_call(
        kern,
        grid=(m // tm, n // tn),
        in_specs=[
            pl.BlockSpec((tm, d), lambda i, j: (i, 0)),
            pl.BlockSpec((None, 6, d), lambda i, j: (i // tpb, 0, 0)),
            pl.BlockSpec((1, d), lambda i, j: (0, 0)),
            pl.BlockSpec((d, tn), lambda i, j: (0, j)),
        ],
        out_specs=pl.BlockSpec((tm, tn), lambda i, j: (i, j)),
        out_shape=jax.ShapeDtypeStruct((m, n), BF16),
        scratch_shapes=[pltpu.VMEM((tm, d), BF16)],
        compiler_params=_cparams(2),
        name="norm_matmul",
    )(x, mod, g.reshape(1, d), w)


def _conv_kernel(a_ref, g_ref, ap_ref, gp_ref, cw_ref, cb_ref, lng_ref, lnb_ref, o_ref, ubuf, vbuf, *, ts):
    i = pl.program_id(1)
    ubuf[CONV_HALO:, :] = a_ref[...].astype(F32) * jax.nn.sigmoid(g_ref[...].astype(F32))
    up = ap_ref[...].astype(F32) * jax.nn.sigmoid(gp_ref[...].astype(F32))
    ubuf[:CONV_HALO, :] = jnp.where(i > 0, up, 0.0)

    def chunk(c, carry):
        c0 = pl.multiple_of(c * LANES, LANES)
        acc = jnp.zeros((ts, LANES), F32) + cb_ref[:, pl.ds(c0, LANES)]
        for k in range(CONV_WIDTH):
            acc = acc + cw_ref[k:k + 1, pl.ds(c0, LANES)] * ubuf[pl.ds(CONV_HALO - (CONV_WIDTH - 1) + k, ts), pl.ds(c0, LANES)]
        vbuf[:, pl.ds(c0, LANES)] = acc
        return carry

    lax.fori_loop(0, D_CONV // LANES, chunk, 0)
    v = vbuf[...]
    mu = jnp.mean(v, axis=-1, keepdims=True)
    vc = v - mu
    var = jnp.mean(vc * vc, axis=-1, keepdims=True)
    y = vc * lax.rsqrt(var + EPS) * lng_ref[...] + lnb_ref[...]
    o_ref[...] = (y * jax.nn.sigmoid(y)).astype(o_ref.dtype)


def _conv_module(proj, conv_w, conv_b, ln_g, ln_b, batch, seq, ts=256):
    m = proj.shape[0]
    nt = seq // ts
    hb = ts // CONV_HALO
    kern = functools.partial(_conv_kernel, ts=ts)

    def prev_idx(col):
        return lambda b, i: (jnp.maximum((b * nt + i) * hb - 1, 0), col)

    return pl.pallas_call(
        kern,
        grid=(batch, nt),
        in_specs=[
            pl.BlockSpec((ts, D_CONV), lambda b, i: (b * nt + i, 0)),
            pl.BlockSpec((ts, D_CONV), lambda b, i: (b * nt + i, 1)),
            pl.BlockSpec((CONV_HALO, D_CONV), prev_idx(0)),
            pl.BlockSpec((CONV_HALO, D_CONV), prev_idx(1)),
            pl.BlockSpec((CONV_WIDTH, D_CONV), lambda b, i: (0, 0)),
            pl.BlockSpec((1, D_CONV), lambda b, i: (0, 0)),
            pl.BlockSpec((1, D_CONV), lambda b, i: (0, 0)),
            pl.BlockSpec((1, D_CONV), lambda b, i: (0, 0)),
        ],
        out_specs=pl.BlockSpec((ts, D_CONV), lambda b, i: (b * nt + i, 0)),
        out_shape=jax.ShapeDtypeStruct((m, D_CONV), BF16),
        scratch_shapes=[pltpu.VMEM((CONV_HALO + ts, D_CONV), F32), pltpu.VMEM((ts, D_CONV), F32)],
        compiler_params=_cparams(2),
        name="conv_module",
    )(proj, proj, proj, proj, conv_w, conv_b.reshape(1, -1), ln_g.reshape(1, -1), ln_b.reshape(1, -1))


def _sb_kernel(q_ref, k_ref, v_ref, o_ref):
    i = pl.program_id(2)
    t = SB_TILE
    row = lax.broadcasted_iota(jnp.int32, (t, t), 0)
    col = lax.broadcasted_iota(jnp.int32, (t, t), 1)
    later = jnp.where(row > col, 1.0, 0.0).astype(BF16)
    later2 = jnp.concatenate([later, later], axis=0)
    before = col < row

    def tile(h, j, rsum, acc, diag):
        k0 = pl.multiple_of(j * t, t)
        hs = slice(h * HEAD_DIM, (h + 1) * HEAD_DIM)
        z = _dot_t(q_ref[:, hs], k_ref[pl.ds(k0, t), hs])
        if diag:
            z = jnp.where(before, z, NEG_INF)
        sp = jnp.log(1.0 + jnp.exp(-jnp.abs(z)))
        log_beta = jnp.minimum(z, 0.0) - sp
        log_keep = log_beta - z
        hi = log_keep.astype(BF16)
        lo = (log_keep - hi.astype(F32)).astype(BF16)
        log_survive = _dot(jnp.concatenate([hi, lo], axis=1), later2) + rsum
        a = jnp.exp(log_beta + log_survive)
        acc = acc + _dot(a.astype(BF16), v_ref[pl.ds(k0, t), hs])
        rsum = rsum + jnp.sum(log_keep, axis=-1, keepdims=True)
        return rsum, acc

    carry = []
    for h in range(SB_GROUP):
        carry += tile(h, i, jnp.zeros((t, 1), F32), jnp.zeros((t, HEAD_DIM), F32), True)

    def body(jj, carry):
        out = []
        for h in range(SB_GROUP):
            out += tile(h, i - 1 - jj, carry[2 * h], carry[2 * h + 1], False)
        return tuple(out)

    carry = lax.fori_loop(0, i, body, tuple(carry))
    for h in range(SB_GROUP):
        o_ref[:, h * HEAD_DIM:(h + 1) * HEAD_DIM] = carry[2 * h + 1].astype(o_ref.dtype)


def _sb_attention(proj3):
    b, s, _ = proj3.shape
    gw = SB_GROUP * HEAD_DIM
    return pl.pallas_call(
        _sb_kernel,
        grid=(b, SB_HEADS // SB_GROUP, s // SB_TILE),
        in_specs=[
            pl.BlockSpec((None, SB_TILE, gw), lambda bi, hg, i: (bi, i, CB_SB_Q // SB_GROUP + hg)),
            pl.BlockSpec((None, s, gw), lambda bi, hg, i: (bi, 0, CB_SB_K // SB_GROUP + hg)),
            pl.BlockSpec((None, s, gw), lambda bi, hg, i: (bi, 0, CB_SB_V // SB_GROUP + hg)),
        ],
        out_specs=pl.BlockSpec((None, SB_TILE, gw), lambda bi, hg, i: (bi, i, hg)),
        out_shape=jax.ShapeDtypeStruct((b, s, SB_HEADS * HEAD_DIM), BF16),
        compiler_params=_cparams(3),
        name="sb_attention",
    )(proj3, proj3, proj3)


def _nsa_prep_kernel(kc_ref, vc_ref, ks_ref, kw_ref, pk_ref, pv_ref, wk_ref, wv_ref, kcg_ref, ksg_ref, kwg_ref,
                     kcb_ref, vcb_ref, ksn_ref, kwn_ref):
    half = CMP_STRIDE * HEAD_DIM

    def compress(a_ref, p_ref, w_ref):
        a = a_ref[...].astype(F32)
        top = _dot((a + p_ref[0:1, :]).astype(BF16), w_ref[0:half, :])
        bot = _dot((a + p_ref[1:2, :]).astype(BF16), w_ref[half:2 * half, :])
        return top + pltpu.roll(bot, bot.shape[0] - 1, axis=0)

    kcb_ref[...] = _rms(compress(kc_ref, pk_ref, wk_ref), kcg_ref[...]).astype(BF16)
    vcb_ref[...] = compress(vc_ref, pv_ref, wv_ref).astype(BF16)
    kwn_ref[...] = _rms(kw_ref[...].astype(F32), kwg_ref[...]).astype(BF16)
    ksn_ref[:, 0:HEAD_DIM] = _rms(ks_ref[...].astype(F32), ksg_ref[...]).astype(BF16)
    s = ks_ref.shape[0]
    blk = jnp.right_shift(lax.broadcasted_iota(jnp.int32, (s, LANES), 0), int(math.log2(SLC_LEN)))
    onehot = blk == lax.broadcasted_iota(jnp.int32, (s, LANES), 1)
    ksn_ref[:, HEAD_DIM:HEAD_DIM + LANES] = jnp.where(onehot, 1.0, 0.0).astype(BF16)


def _nsa_prep(proj3, kc_chunks, vc_chunks, pos_k, pos_v, wk, wv, kc_g, ks_g, kw_g):
    b, s, _ = proj3.shape
    g_n = NSA_KV_HEADS
    nch = s // CMP_STRIDE
    half = CMP_STRIDE * HEAD_DIM
    vec = lambda: pl.BlockSpec((1, HEAD_DIM), lambda bi, g: (0, 0))
    small = jax.ShapeDtypeStruct((b, g_n, nch, HEAD_DIM), BF16)
    full = jax.ShapeDtypeStruct((b, g_n, s, HEAD_DIM), BF16)
    aug = jax.ShapeDtypeStruct((b, g_n, s, HEAD_DIM + LANES), BF16)
    return pl.pallas_call(
        _nsa_prep_kernel,
        grid=(b, g_n),
        in_specs=[
            pl.BlockSpec((None, None, nch, half), lambda bi, g: (bi, g, 0, 0)),
            pl.BlockSpec((None, None, nch, half), lambda bi, g: (bi, g, 0, 0)),
            pl.BlockSpec((None, s, HEAD_DIM), lambda bi, g: (bi, 0, CB_KS + g)),
            pl.BlockSpec((None, s, HEAD_DIM), lambda bi, g: (bi, 0, CB_KW + g)),
            pl.BlockSpec((2, half), lambda bi, g: (0, 0)),
            pl.BlockSpec((2, half), lambda bi, g: (0, 0)),
            pl.BlockSpec((2 * half, HEAD_DIM), lambda bi, g: (0, 0)),
            pl.BlockSpec((2 * half, HEAD_DIM), lambda bi, g: (0, 0)),
            vec(), vec(), vec(),
        ],
        out_specs=[
            pl.BlockSpec((None, None, nch, HEAD_DIM), lambda bi, g: (bi, g, 0, 0)),
            pl.BlockSpec((None, None, nch, HEAD_DIM), lambda bi, g: (bi, g, 0, 0)),
            pl.BlockSpec((None, None, s, HEAD_DIM + LANES), lambda bi, g: (bi, g, 0, 0)),
            pl.BlockSpec((None, None, s, HEAD_DIM), lambda bi, g: (bi, g, 0, 0)),
        ],
        out_shape=[small, small, aug, full],
        compiler_params=_cparams(2),
        name="nsa_prep",
    )(kc_chunks, vc_chunks, proj3, proj3, pos_k.reshape(2, half), pos_v.reshape(2, half),
      wk.astype(BF16), wv.astype(BF16), kc_g.reshape(1, -1), ks_g.reshape(1, -1), kw_g.reshape(1, -1))


def _nsa_kernel(q_ref, kcb_ref, vcb_ref, ksa_ref, vs_ref, kwn_ref, vw_ref, gate_ref, bc_ref, bt_ref, qg_ref, ovt_ref,
                o_ref):
    g = pl.program_id(1)
    i = pl.program_id(2)
    r_n = NSA_REP
    rows = r_n * TQ

    qs = []
    for r in range(r_n):
        qr = q_ref[:, r * HEAD_DIM:(r + 1) * HEAD_DIM].astype(F32)
        qs.append(_rms(qr, qg_ref[...]).astype(BF16))
    q = jnp.concatenate(qs, axis=0)

    lane = lax.broadcasted_iota(jnp.int32, (TQ, LANES), 1)

    logit_c = _dot_t(q, kcb_ref[...]) + bc_ref[...].reshape(rows, LANES)
    m_c = jnp.max(logit_c, axis=-1, keepdims=True)
    p_c = jnp.where(logit_c > 0.5 * NEG_INF, jnp.exp(logit_c - m_c), 0.0)
    p_c = p_c / jnp.maximum(jnp.sum(p_c, axis=-1, keepdims=True), TINY)
    o_c = _dot(p_c.astype(BF16), vcb_ref[...])

    p_sum = p_c[0:TQ]
    for r in range(1, r_n):
        p_sum = p_sum + p_c[r * TQ:(r + 1) * TQ]
    hi = p_sum.astype(BF16)
    lo = (p_sum - hi.astype(F32)).astype(BF16)
    n_sel = ovt_ref.shape[0]
    imp = _dot_t(ovt_ref[...], hi) + _dot_t(ovt_ref[...], lo)
    blk = lax.broadcasted_iota(jnp.int32, (n_sel, TQ), 0)
    tb = jnp.right_shift(i * TQ + lax.broadcasted_iota(jnp.int32, (n_sel, TQ), 1), int(math.log2(SLC_LEN)))
    causal_blk = blk <= tb
    forced = (blk == 0) | (causal_blk & (blk > tb - N_LOCAL_BLOCKS))
    score = jnp.where(forced, FORCE_SCORE, jnp.where(causal_blk, imp, -FORCE_SCORE))
    rank = jnp.zeros((n_sel, TQ), F32)
    for b in range(n_sel):
        sb = score[b:b + 1, :]
        beats = (sb > score) | ((sb == score) & (blk > b))
        rank = rank + jnp.where(beats, 1.0, 0.0)
    chosen = (rank < float(min(SLC_TOP_N, n_sel))) & causal_blk
    sel_neg = jnp.where(chosen, 0.0, NEG_INF)
    sel_neg = jnp.concatenate([sel_neg, jnp.zeros((LANES - n_sel, TQ), F32)], axis=0).T.astype(BF16)
    q_aug = jnp.concatenate([q, jnp.concatenate([sel_neg] * r_n, axis=0)], axis=1)

    def biased(s, first_tile, idx_fn):
        n_t = s.shape[1] // TQ
        return jnp.concatenate(
            [s[:, c * TQ:(c + 1) * TQ] + bt_ref[idx_fn(i - (first_tile + c))] for c in range(n_t)], axis=1)

    w_tile = jnp.maximum(i - WINDOW // TQ, 0)
    w0 = pl.multiple_of(w_tile * TQ, TQ)

    def win_idx(off):
        far = jnp.where(off == WINDOW // TQ, BT_FAR_UPPER, jnp.minimum(off, BT_FAR))
        return jnp.where(off < 0, BT_MASKED, far)

    s_w = biased(_dot_t(q, kwn_ref[pl.ds(w0, WIN_SPAN), :]), w_tile, win_idx)
    p_w = jnp.exp(s_w - jnp.max(s_w, axis=-1, keepdims=True))
    o_w = _dot(p_w.astype(BF16), vw_ref[pl.ds(w0, WIN_SPAN), :]) / jnp.sum(p_w, axis=-1, keepdims=True)

    def sel_chunk(kc, carry, diag):
        c0 = pl.multiple_of(kc * SEL_CHUNK, SEL_CHUNK)
        idx_fn = (lambda off: jnp.where(off < 0, BT_MASKED, jnp.minimum(off, BT_FAR))) if diag else (
            lambda off: jnp.minimum(off, BT_FAR))
        s = biased(_dot_t(q_aug, ksa_ref[pl.ds(c0, SEL_CHUNK), :]), kc * (SEL_CHUNK // TQ), idx_fn)
        m_blk = jnp.max(s, axis=-1, keepdims=True)
        if diag:
            p = jnp.exp(s - m_blk)
            return m_blk, jnp.sum(p, axis=-1, keepdims=True), _dot(p.astype(BF16), vs_ref[pl.ds(c0, SEL_CHUNK), :])
        m, l, acc = carry
        m_new = jnp.maximum(m, m_blk)
        p = jnp.exp(s - m_new)
        alpha = jnp.exp(m - m_new)
        l = alpha * l + jnp.sum(p, axis=-1, keepdims=True)
        acc = alpha * acc + _dot(p.astype(BF16), vs_ref[pl.ds(c0, SEL_CHUNK), :])
        return m_new, l, acc

    kc_diag = lax.shift_right_logical(i, jnp.int32(int(math.log2(SEL_CHUNK // TQ))))
    carry = sel_chunk(kc_diag, None, True)
    _, l_s, acc_s = lax.fori_loop(0, kc_diag, lambda jj, c: sel_chunk(kc_diag - 1 - jj, c, False), carry)
    o_s = acc_s / l_s

    gates = jax.nn.sigmoid(gate_ref[...].astype(F32))

    def gate_col(idx):
        return jnp.sum(jnp.where(lane == idx, gates, 0.0), axis=-1, keepdims=True)

    for r in range(r_n):
        h = g * r_n + r
        sl = slice(r * TQ, (r + 1) * TQ)
        o = gate_col(h) * o_c[sl] + gate_col(NSA_HEADS + h) * o_s[sl] + gate_col(2 * NSA_HEADS + h) * o_w[sl]
        o_ref[:, r * HEAD_DIM:(r + 1) * HEAD_DIM] = o.astype(o_ref.dtype)


def _overlap_matrix(s):
    nch = s // CMP_STRIDE
    nsel = s // SLC_LEN
    ci = np.arange(nch)[:, None]
    sj = np.arange(nsel)[None, :]
    ov = (ci * CMP_STRIDE <= sj * SLC_LEN + SLC_LEN - 1) & (ci * CMP_STRIDE + CMP_LEN - 1 >= sj * SLC_LEN)
    ov = ov & (ci < nch - 1)
    return jnp.asarray(ov.T, BF16)


def _nsa_attention(proj3, kcb, vcb, ksa, kwn, bias_c, bias_t, q_g):
    b, s, _ = proj3.shape
    g_n, r_n = NSA_KV_HEADS, NSA_REP
    nch = s // CMP_STRIDE
    nsel = s // SLC_LEN
    gw = r_n * HEAD_DIM
    return pl.pallas_call(
        _nsa_kernel,
        grid=(b, g_n, s // TQ),
        in_specs=[
            pl.BlockSpec((None, TQ, gw), lambda bi, g, i: (bi, i, CB_NQ // r_n + g)),
            pl.BlockSpec((None, None, nch, HEAD_DIM), lambda bi, g, i: (bi, g, 0, 0)),
            pl.BlockSpec((None, None, nch, HEAD_DIM), lambda bi, g, i: (bi, g, 0, 0)),
            pl.BlockSpec((None, None, s, HEAD_DIM + LANES), lambda bi, g, i: (bi, g, 0, 0)),
            pl.BlockSpec((None, s, HEAD_DIM), lambda bi, g, i: (bi, 0, CB_VS + g)),
            pl.BlockSpec((None, None, s, HEAD_DIM), lambda bi, g, i: (bi, g, 0, 0)),
            pl.BlockSpec((None, s, HEAD_DIM), lambda bi, g, i: (bi, 0, CB_VW + g)),
            pl.BlockSpec((None, TQ, LANES), lambda bi, g, i: (bi, i, CB_NGATE)),
            pl.BlockSpec((None, r_n, TQ, LANES), lambda bi, g, i: (g, 0, i, 0)),
            pl.BlockSpec((None, 5, r_n * TQ, TQ), lambda bi, g, i: (g, 0, 0, 0)),
            pl.BlockSpec((1, HEAD_DIM), lambda bi, g, i: (0, 0)),
            pl.BlockSpec((nsel, nch), lambda bi, g, i: (0, 0)),
        ],
        out_specs=pl.BlockSpec((None, TQ, gw), lambda bi, g, i: (bi, i, g)),
        out_shape=jax.ShapeDtypeStruct((b, s, NSA_HEADS * HEAD_DIM), BF16),
        compiler_params=_cparams(3),
        name="nsa_attention",
    )(proj3, kcb, vcb, ksa, proj3, kwn, proj3, proj3, bias_c, bias_t, q_g.reshape(1, -1) * SCALE, _overlap_matrix(s))


def _merge_kernel(u_ref, sb_ref, ns_ref, gc_ref, gs_ref, gn_ref, wc_ref, ws_ref, wn_ref, o_ref):
    m = jax.nn.sigmoid(gc_ref[...].astype(F32)) * _dot(u_ref[...], wc_ref[...])
    m = m + jax.nn.sigmoid(gs_ref[...].astype(F32)) * _dot(sb_ref[...], ws_ref[...])
    m = m + jax.nn.sigmoid(gn_ref[...].astype(F32)) * _dot(ns_ref[...], wn_ref[...])
    o_ref[...] = m.astype(o_ref.dtype)


def _merge(u_act, sb, nsa, proj, wc, ws, wn, tm=512, tn=1024):
    m = u_act.shape[0]
    d = wc.shape[1]
    kc = wc.shape[0]
    gate0 = CB_MERGE * LANES // tn
    gstep = d // tn
    act = lambda: pl.BlockSpec((tm, kc), lambda i, j: (i, 0))
    gate = lambda n: pl.BlockSpec((tm, tn), lambda i, j: (i, gate0 + n * gstep + j))
    wsp = lambda: pl.BlockSpec((kc, tn), lambda i, j: (0, j))
    return pl.pallas_call(
        _merge_kernel,
        grid=(m // tm, d // tn),
        in_specs=[act(), act(), act(), gate(0), gate(1), gate(2), wsp(), wsp(), wsp()],
        out_specs=pl.BlockSpec((tm, tn), lambda i, j: (i, j)),
        out_shape=jax.ShapeDtypeStruct((m, d), BF16),
        compiler_params=_cparams(2),
        name="merge",
    )(u_act, sb, nsa, proj, proj, proj, wc, ws, wn)


def _mm_res_kernel(a_ref, w_ref, x_ref, mod_ref, o_ref, acc_ref, *, row_ga, nk):
    k = pl.program_id(2)

    @pl.when(k == 0)
    def _():
        acc_ref[...] = jnp.zeros_like(acc_ref)

    acc_ref[...] += _dot(a_ref[...], w_ref[...])

    @pl.when(k == nk - 1)
    def _():
        o_ref[...] = x_ref[...] + mod_ref[row_ga:row_ga + 1, :] * acc_ref[...]


def _mm_res(a, w, x, mod, row_ga, seq, tm=1024, tn=1024, tk=None):
    m, ka = a.shape
    n = x.shape[1]
    tpb = seq // tm
    if w.ndim == 3:
        tk = w.shape[1]
        nk = w.shape[0]
        w_spec = pl.BlockSpec((None, tk, tn), lambda i, j, k: (k, 0, j))
    else:
        tk = tk or ka
        nk = ka // tk
        w_spec = pl.BlockSpec((tk, tn), lambda i, j, k: (k, j))
    kern = functools.partial(_mm_res_kernel, row_ga=row_ga, nk=nk)
    return pl.pallas_call(
        kern,
        grid=(m // tm, n // tn, nk),
        in_specs=[
            pl.BlockSpec((tm, tk), lambda i, j, k: (i, k)),
            w_spec,
            pl.BlockSpec((tm, tn), lambda i, j, k: (i, j)),
            pl.BlockSpec((None, 6, tn), lambda i, j, k: (i // tpb, 0, j)),
        ],
        out_specs=pl.BlockSpec((tm, tn), lambda i, j, k: (i, j)),
        out_shape=jax.ShapeDtypeStruct((m, n), F32),
        scratch_shapes=[pltpu.VMEM((tm, tn), F32)],
        compiler_params=_cparams(3),
        name="matmul_residual",
    )(a, w, x, mod)


def _ffn_up_kernel(x_ref, mod_ref, g_ref, w1_ref, w3_ref, o_ref, h_ref):
    @pl.when(pl.program_id(1) == 0)
    def _():
        h = _norm_mod(x_ref[...], g_ref[...], mod_ref[ROW_SH_FFN:ROW_SH_FFN + 1, :], mod_ref[ROW_SC_FFN:ROW_SC_FFN + 1, :])
        h_ref[...] = h.astype(BF16)

    h = h_ref[...]
    a = _dot(h, w1_ref[...])
    o_ref[...] = (a * jax.nn.sigmoid(a) * _dot(h, w3_ref[...])).astype(o_ref.dtype)


def _ffn_up(x, mod, g, w1, w3, seq, tm=1024, tn=512):
    m, d = x.shape
    f = w1.shape[1]
    tpb = seq // tm
    return pl.pallas_call(
        _ffn_up_kernel,
        grid=(m // tm, f // tn),
        in_specs=[
            pl.BlockSpec((tm, d), lambda i, j: (i, 0)),
            pl.BlockSpec((None, 6, d), lambda i, j: (i // tpb, 0, 0)),
            pl.BlockSpec((1, d), lambda i, j: (0, 0)),
            pl.BlockSpec((d, tn), lambda i, j: (0, j)),
            pl.BlockSpec((d, tn), lambda i, j: (0, j)),
        ],
        out_specs=pl.BlockSpec((tm, tn), lambda i, j: (i, j)),
        out_shape=jax.ShapeDtypeStruct((m, f), BF16),
        scratch_shapes=[pltpu.VMEM((tm, d), BF16)],
        compiler_params=_cparams(2),
        name="ffn_up",
    )(x, mod, g.reshape(1, d), w1, w3)


def _moe_up_kernel(x_ref, mod_ref, g_ref, cmb_ref, w1_ref, w3_ref, o_ref, h_ref):
    e = pl.program_id(1)

    @pl.when((e == 0) & (pl.program_id(2) == 0))
    def _():
        h = _norm_mod(x_ref[...], g_ref[...], mod_ref[ROW_SH_FFN:ROW_SH_FFN + 1, :], mod_ref[ROW_SC_FFN:ROW_SC_FFN + 1, :])
        h_ref[...] = h.astype(BF16)

    h = h_ref[...]
    cmb = cmb_ref[...]
    lane = lax.broadcasted_iota(jnp.int32, cmb.shape, 1)
    ce = jnp.sum(jnp.where(lane == e, cmb, 0.0), axis=-1, keepdims=True)
    a = _dot(h, w1_ref[...])
    o_ref[...] = (ce * (a * jax.nn.sigmoid(a) * _dot(h, w3_ref[...]))).astype(o_ref.dtype)


def _moe_up(x, mod, g, combine, w1, w3, seq, tm=1024, tn=256):
    m, d = x.shape
    n_e, _, fe = w1.shape
    nj = fe // tn
    tpb = seq // tm
    return pl.pallas_call(
        _moe_up_kernel,
        grid=(m // tm, n_e, nj),
        in_specs=[
            pl.BlockSpec((tm, d), lambda i, e, j: (i, 0)),
            pl.BlockSpec((None, 6, d), lambda i, e, j: (i // tpb, 0, 0)),
            pl.BlockSpec((1, d), lambda i, e, j: (0, 0)),
            pl.BlockSpec((tm, LANES), lambda i, e, j: (i, 0)),
            pl.BlockSpec((None, d, tn), lambda i, e, j: (e, 0, j)),
            pl.BlockSpec((None, d, tn), lambda i, e, j: (e, 0, j)),
        ],
        out_specs=pl.BlockSpec((tm, tn), lambda i, e, j: (i, e * nj + j)),
        out_shape=jax.ShapeDtypeStruct((m, n_e * fe), BF16),
        scratch_shapes=[pltpu.VMEM((tm, d), BF16)],
        compiler_params=_cparams(3),
        name="moe_up",
    )(x, mod, g.reshape(1, d), combine, w1, w3)


def _router_kernel(x_ref, mod_ref, g_ref, wr_ref, br_ref, o_ref):
    h = _norm_mod(x_ref[...], g_ref[...], mod_ref[ROW_SH_FFN:ROW_SH_FFN + 1, :], mod_ref[ROW_SC_FFN:ROW_SC_FFN + 1, :])
    logits = jnp.dot(h, wr_ref[...], precision=lax.Precision.HIGHEST, preferred_element_type=F32) + br_ref[...]
    lane = lax.broadcasted_iota(jnp.int32, logits.shape, 1).astype(F32)
    pad = float(LANES)
    m1 = jnp.max(logits, axis=-1, keepdims=True)
    i1 = jnp.min(jnp.where(logits == m1, lane, pad), axis=-1, keepdims=True)
    rest = jnp.where(lane == i1, NEG_INF, logits)
    m2 = jnp.max(rest, axis=-1, keepdims=True)
    i2 = jnp.min(jnp.where(rest == m2, lane, pad), axis=-1, keepdims=True)
    e2 = jnp.exp(m2 - m1)
    w1 = 1.0 / (1.0 + e2)
    w2 = e2 / (1.0 + e2)
    o_ref[...] = jnp.where(lane == i1, w1, 0.0) + jnp.where(lane == i2, w2, 0.0)


def _router(x, mod, g, w_router, b_router, seq, tm=512):
    m, d = x.shape
    n_e = w_router.shape[1]
    tpb = seq // tm
    wr = jnp.zeros((d, LANES), F32).at[:, :n_e].set(w_router)
    br = jnp.full((1, LANES), 2.0 * NEG_INF, F32).at[0, :n_e].set(b_router)
    return pl.pallas_call(
        _router_kernel,
        grid=(m // tm,),
        in_specs=[
            pl.BlockSpec((tm, d), lambda i: (i, 0)),
            pl.BlockSpec((None, 6, d), lambda i: (i // tpb, 0, 0)),
            pl.BlockSpec((1, d), lambda i: (0, 0)),
            pl.BlockSpec((d, LANES), lambda i: (0, 0)),
            pl.BlockSpec((1, LANES), lambda i: (0, 0)),
        ],
        out_specs=pl.BlockSpec((tm, LANES), lambda i: (i, 0)),
        out_shape=jax.ShapeDtypeStruct((m, LANES), F32),
        compiler_params=_cparams(1),
        name="router",
    )(x, mod, g.reshape(1, d), wr, br)


def _pad_w_in(w):
    d = w.shape[0]
    gap = CB_MERGE * LANES - RAW_MERGE_COL
    q0, q1 = CB_SB_Q * LANES, CB_SB_K * LANES
    return jnp.concatenate([w[:, :q0], w[:, q0:q1] * SCALE, w[:, q1:RAW_MERGE_COL], jnp.zeros((d, gap), w.dtype),
                            w[:, RAW_MERGE_COL:]], axis=1).astype(BF16)


def _kv_chunks(proj3, cb):
    b, s, _ = proj3.shape
    t = proj3[:, :, cb * LANES:(cb + NSA_KV_HEADS) * LANES].reshape(b, s, NSA_KV_HEADS, HEAD_DIM)
    return t.transpose(0, 2, 1, 3).reshape(b, NSA_KV_HEADS, s // CMP_STRIDE, CMP_STRIDE * HEAD_DIM)


def kernel(x, c, w_ada, b_ada, g_mix, g_ffn, w_in, conv_w, conv_b, conv_ln_g, conv_ln_b, w_conv_out, w_sb_out,
           nsa_cmp_pos_k, nsa_cmp_pos_v, nsa_cmp_wk, nsa_cmp_wv, nsa_q_g, nsa_kc_g, nsa_ks_g, nsa_kw_g, w_nsa_out, w_o,
           rel_bias, ffn_w1, ffn_w3, ffn_w2, moe_router, moe_router_b, moe_w1, moe_w3, moe_w2):
    b, s, d = x.shape
    m = b * s
    depth = w_ada.shape[0]
    mod_all = _ada_all(c, w_ada, b_ada)
    bias_c, bias_t = _bias_tables(rel_bias, s)
    xf = x.reshape(m, d)
    for l in range(depth):
        mod = mod_all[l].reshape(b, 6, d)
        proj = _norm_matmul(xf, mod, g_mix[l], _pad_w_in(w_in[l]), ROW_SH_MIX, ROW_SC_MIX, s)
        proj3 = proj.reshape(b, s, NP)
        u_act = _conv_module(proj, conv_w[l], conv_b[l], conv_ln_g[l], conv_ln_b[l], b, s)
        sb = _sb_attention(proj3)
        kcb, vcb, ksn, kwn = _nsa_prep(proj3, _kv_chunks(proj3, CB_KC), _kv_chunks(proj3, CB_VC), nsa_cmp_pos_k[l],
                                       nsa_cmp_pos_v[l], nsa_cmp_wk[l], nsa_cmp_wv[l], nsa_kc_g[l], nsa_ks_g[l],
                                       nsa_kw_g[l])
        nsa = _nsa_attention(proj3, kcb, vcb, ksn, kwn, bias_c, bias_t, nsa_q_g[l])
        merged = _merge(u_act, sb.reshape(m, -1), nsa.reshape(m, -1), proj, w_conv_out[l].astype(BF16),
                        w_sb_out[l].astype(BF16), w_nsa_out[l].astype(BF16))
        xf = _mm_res(merged, w_o[l].astype(BF16), xf, mod, ROW_GA_MIX, s)
        i = l // 2
        if l % 2 == 0:
            act = _ffn_up(xf, mod, g_ffn[l], ffn_w1[i].astype(BF16), ffn_w3[i].astype(BF16), s)
            xf = _mm_res(act, ffn_w2[i].astype(BF16), xf, mod, ROW_GA_FFN, s, tm=512, tn=512)
        else:
            combine = _router(xf, mod, g_ffn[l], moe_router[i], moe_router_b[i], s)
            act = _moe_up(xf, mod, g_ffn[l], combine, moe_w1[i].astype(BF16), moe_w3[i].astype(BF16), s)
            xf = _mm_res(act, moe_w2[i].astype(BF16), xf, mod, ROW_GA_FFN, s)
    return xf.reshape(b, s, d)
```

```python
import functools
import math

import numpy as np
import jax
import jax.numpy as jnp
from jax import lax
from jax.experimental import pallas as pl
from jax.experimental.pallas import tpu as pltpu

F32 = jnp.float32
BF16 = jnp.bfloat16

D_MODEL = 2048
DEPTH = 4
D_CONV = 1024
CONV_WIDTH = 31
SB_HEADS = 8
HEAD_DIM = 128
NSA_HEADS = 8
NSA_KV_HEADS = 2
NSA_REP = NSA_HEADS // NSA_KV_HEADS
CMP_LEN = 32
CMP_STRIDE = 16
SLC_LEN = 64
SLC_TOP_N = 16
N_LOCAL_BLOCKS = 2
WINDOW = 512
FORCE_SCORE = 1e4
REL_BUCKETS = 32
REL_MAX_DIST = 128
D_FF = 5632
N_EXPERTS = 8
D_FF_EXPERT = 2816
EPS = 1e-6
NEG_INF = -1e30
TINY = 1e-20

LANES = 128
V7X_VMEM_LIMIT_BYTES = 56 * 1024 * 1024

NP_BLOCKS = 112
NP = NP_BLOCKS * LANES
CB_GLU_A, CB_GLU_G = 0, 8
CB_SB_Q, CB_SB_K, CB_SB_V = 16, 24, 32
CB_NQ = 40
CB_KC, CB_VC, CB_KS, CB_VS, CB_KW, CB_VW = 48, 50, 52, 54, 56, 58
CB_NGATE = 60
CB_MERGE = 64
RAW_GATE_COL = 7680
RAW_MERGE_COL = 7704
ROW_SH_MIX, ROW_SC_MIX, ROW_GA_MIX, ROW_SH_FFN, ROW_SC_FFN, ROW_GA_FFN = range(6)

SCALE = HEAD_DIM ** -0.5
TQ = 128
SB_TILE = 256
SB_GROUP = 4
SEL_CHUNK = 512
WIN_SPAN = WINDOW + TQ
BT_DIAG, BT_SUB, BT_FAR, BT_FAR_UPPER, BT_MASKED = range(5)
CONV_HALO = 32
MOE_TM = 1024
MOE_BLK = 128
MOE_MT = 512
MOE_FC = 1408
MOE_SUB = 256


def _cparams(n_axes):
    return pltpu.CompilerParams(
        dimension_semantics=("arbitrary",) * n_axes,
        vmem_limit_bytes=V7X_VMEM_LIMIT_BYTES,
    )


def _dot(a, b):
    return jnp.dot(a, b, preferred_element_type=F32)


def _dot_t(a, b):
    return lax.dot_general(a, b, (((1,), (1,)), ((), ())), preferred_element_type=F32)


def _split_dot(x, w01):
    hi = x.astype(BF16)
    lo = (x - hi.astype(F32)).astype(BF16)
    return _dot(hi, w01) + _dot(lo, w01)


def _norm_mod(x, g, sh, sc):
    ms = jnp.mean(x * x, axis=-1, keepdims=True)
    return (x * lax.rsqrt(ms + EPS) * g) * (1.0 + sc) + sh


def _rms(x, g):
    ms = jnp.mean(x * x, axis=-1, keepdims=True)
    return x * lax.rsqrt(ms + EPS) * g


def _ada_kernel(c_ref, w_ref, b_ref, o_ref):
    c = c_ref[...]
    ca = (c * jax.nn.sigmoid(c)).astype(BF16)
    o_ref[...] = _dot(ca, w_ref[...].astype(BF16)) + b_ref[...]


def _ada_all(c, w_ada, b_ada):
    depth, d, n = w_ada.shape
    nb = c.shape[0]
    b = 16
    c = jnp.zeros((b, d), c.dtype).at[:nb].set(c)
    tn = 1024
    out = pl.pallas_call(
        _ada_kernel,
        grid=(depth, n // tn),
        in_specs=[
            pl.BlockSpec((b, d), lambda l, j: (0, 0)),
            pl.BlockSpec((None, d, tn), lambda l, j: (l, 0, j)),
            pl.BlockSpec((None, 1, tn), lambda l, j: (l, 0, j)),
        ],
        out_specs=pl.BlockSpec((None, b, tn), lambda l, j: (l, 0, j)),
        out_shape=jax.ShapeDtypeStruct((depth, b, n), F32),
        compiler_params=_cparams(2),
        name="ada",
    )(c, w_ada, b_ada.reshape(depth, 1, n))
    return out[:, :nb]


def _bias_kernel(tab_ref, bk_ref, o_ref):
    h = pl.program_id(0)
    bk = bk_ref[...]
    acc = jnp.zeros(bk.shape, F32)
    for b in range(REL_BUCKETS):
        acc = jnp.where(bk == b, tab_ref[b, h], acc)
    o_ref[...] = acc


def _bias_expand(rel_bias, buckets):
    r = buckets.shape[0]
    tr = r
    return pl.pallas_call(
        _bias_kernel,
        grid=(NSA_HEADS, r // tr),
        in_specs=[
            pl.BlockSpec(memory_space=pltpu.SMEM),
            pl.BlockSpec((tr, LANES), lambda h, i: (i, 0)),
        ],
        out_specs=pl.BlockSpec((None, tr, LANES), lambda h, i: (h, i, 0)),
        out_shape=jax.ShapeDtypeStruct((NSA_HEADS, r, LANES), F32),
        compiler_params=_cparams(2),
        name="bias_expand",
    )(rel_bias, buckets)


def _rel_bucket(dist):
    n = jnp.maximum(dist, 0)
    max_exact = REL_BUCKETS // 2
    nf = jnp.maximum(n, 1).astype(F32)
    large = max_exact + (jnp.log(nf / max_exact) / math.log(REL_MAX_DIST / max_exact) * (REL_BUCKETS - max_exact)).astype(jnp.int32)
    large = jnp.minimum(large, REL_BUCKETS - 1)
    return jnp.where(n < max_exact, n, large)


def _bias_tables(rel_bias, s):
    t = jnp.arange(s, dtype=jnp.int32)[:, None]
    cend = jnp.arange(LANES, dtype=jnp.int32)[None, :] * CMP_STRIDE + (CMP_LEN - 1)
    q = jnp.arange(TQ, dtype=jnp.int32)[:, None]
    k = jnp.arange(TQ, dtype=jnp.int32)[None, :]
    bk_t = jnp.concatenate([_rel_bucket(q - k), _rel_bucket(TQ + q - k), _rel_bucket(2 * TQ + q - k)], axis=0)
    out = _bias_expand(rel_bias, jnp.concatenate([_rel_bucket(t - cend), bk_t], axis=0))
    bias_c = jnp.where(t >= cend, out[:, :s], NEG_INF).reshape(NSA_KV_HEADS, NSA_REP, s, LANES)
    diag, sub, far = out[:, s:s + TQ], out[:, s + TQ:s + 2 * TQ], out[:, s + 2 * TQ:]
    tiles = jnp.stack([jnp.where(q >= k, diag, NEG_INF), sub, far, jnp.where(k > q, far, NEG_INF),
                       jnp.full_like(far, NEG_INF)], axis=1)
    tiles = tiles.reshape(NSA_KV_HEADS, NSA_REP, 5, TQ, TQ).transpose(0, 2, 1, 3, 4)
    return bias_c, tiles.reshape(NSA_KV_HEADS, 5, NSA_REP * TQ, TQ)


def _norm_matmul_kernel(x_ref, mod_ref, g_ref, w_ref, o_ref, h_ref, *, row_sh, row_sc):
    @pl.when(pl.program_id(1) == 0)
    def _():
        h = _norm_mod(x_ref[...], g_ref[...], mod_ref[row_sh:row_sh + 1, :], mod_ref[row_sc:row_sc + 1, :])
        h_ref[...] = h.astype(BF16)

    o_ref[...] = _dot(h_ref[...], w_ref[...]).astype(o_ref.dtype)


def _norm_matmul(x, mod, g, w, row_sh, row_sc, seq, tm=1024, tn=1024):
    m, d = x.shape
    n = w.shape[1]
    tpb = seq // tm
    kern = functools.partial(_norm_matmul_kernel, row_sh=row_sh, row_sc=row_sc)
    return pl.pallas_call(
        kern,
        grid=(m // tm, n // tn),
        in_specs=[
            pl.BlockSpec((tm, d), lambda i, j: (i, 0)),
            pl.BlockSpec((None, 6, d), lambda i, j: (i // tpb, 0, 0)),
            pl.BlockSpec((1, d), lambda i, j: (0, 0)),
            pl.BlockSpec((d, tn), lambda i, j: (0, j)),
        ],
        out_specs=pl.BlockSpec((tm, tn), lambda i, j: (i, j)),
        out_shape=jax.ShapeDtypeStruct((m, n), BF16),
        scratch_shapes=[pltpu.VMEM((tm, d), BF16)],
        compiler_params=_cparams(2),
        name="norm_matmul",
    )(x, mod, g.reshape(1, d), w)


def _conv_kernel(a_ref, g_ref, ap_ref, gp_ref, cw_ref, cb_ref, lng_ref, lnb_ref, o_ref, ubuf, vbuf, *, ts):
    i = pl.program_id(1)
    ubuf[CONV_HALO:, :] = a_ref[...].astype(F32) * jax.nn.sigmoid(g_ref[...].astype(F32))
    up = ap_ref[...].astype(F32) * jax.nn.sigmoid(gp_ref[...].astype(F32))
    ubuf[:CONV_HALO, :] = jnp.where(i > 0, up, 0.0)

    def chunk(c, carry):
        c0 = pl.multiple_of(c * LANES, LANES)
        acc = jnp.zeros((ts, LANES), F32) + cb_ref[:, pl.ds(c0, LANES)]
        for k in range(CONV_WIDTH):
            acc = acc + cw_ref[k:k + 1, pl.ds(c0, LANES)] * ubuf[pl.ds(CONV_HALO - (CONV_WIDTH - 1) + k, ts), pl.ds(c0, LANES)]
        vbuf[:, pl.ds(c0, LANES)] = acc
        return carry

    lax.fori_loop(0, D_CONV // LANES, chunk, 0)
    v = vbuf[...]
    mu = jnp.mean(v, axis=-1, keepdims=True)
    vc = v - mu
    var = jnp.mean(vc * vc, axis=-1, keepdims=True)
    y = vc * lax.rsqrt(var + EPS) * lng_ref[...] + lnb_ref[...]
    o_ref[...] = (y * jax.nn.sigmoid(y)).astype(o_ref.dtype)


def _conv_module(proj, conv_w, conv_b, ln_g, ln_b, batch, seq, ts=256):
    m = proj.shape[0]
    nt = seq // ts
    hb = ts // CONV_HALO
    kern = functools.partial(_conv_kernel, ts=ts)

    def prev_idx(col):
        return lambda b, i: (jnp.maximum((b * nt + i) * hb - 1, 0), col)

    return pl.pallas_call(
        kern,
        grid=(batch, nt),
        in_specs=[
            pl.BlockSpec((ts, D_CONV), lambda b, i: (b * nt + i, 0)),
            pl.BlockSpec((ts, D_CONV), lambda b, i: (b * nt + i, 1)),
            pl.BlockSpec((CONV_HALO, D_CONV), prev_idx(0)),
            pl.BlockSpec((CONV_HALO, D_CONV), prev_idx(1)),
            pl.BlockSpec((CONV_WIDTH, D_CONV), lambda b, i: (0, 0)),
            pl.BlockSpec((1, D_CONV), lambda b, i: (0, 0)),
            pl.BlockSpec((1, D_CONV), lambda b, i: (0, 0)),
            pl.BlockSpec((1, D_CONV), lambda b, i: (0, 0)),
        ],
        out_specs=pl.BlockSpec((ts, D_CONV), lambda b, i: (b * nt + i, 0)),
        out_shape=jax.ShapeDtypeStruct((m, D_CONV), BF16),
        scratch_shapes=[pltpu.VMEM((CONV_HALO + ts, D_CONV), F32), pltpu.VMEM((ts, D_CONV), F32)],
        compiler_params=_cparams(2),
        name="conv_module",
    )(proj, proj, proj, proj, conv_w, conv_b.reshape(1, -1), ln_g.reshape(1, -1), ln_b.reshape(1, -1))


def _sb_kernel(q_ref, k_ref, v_ref, o_ref):
    i = pl.program_id(2)
    t = SB_TILE
    row = lax.broadcasted_iota(jnp.int32, (t, t), 0)
    col = lax.broadcasted_iota(jnp.int32, (t, t), 1)
    later = jnp.where(row > col, 1.0, 0.0).astype(BF16)
    later2 = jnp.concatenate([later, later], axis=0)
    before = col < row

    def tile(h, j, rsum, acc, diag):
        k0 = pl.multiple_of(j * t, t)
        hs = slice(h * HEAD_DIM, (h + 1) * HEAD_DIM)
        z = _dot_t(q_ref[:, hs], k_ref[pl.ds(k0, t), hs])
        if diag:
            z = jnp.where(before, z, NEG_INF)
        sp = jnp.log(1.0 + jnp.exp(-jnp.abs(z)))
        log_beta = jnp.minimum(z, 0.0) - sp
        log_keep = log_beta - z
        hi = log_keep.astype(BF16)
        lo = (log_keep - hi.astype(F32)).astype(BF16)
        log_survive = _dot(jnp.concatenate([hi, lo], axis=1), later2) + rsum
        a = jnp.exp(log_beta + log_survive)
        acc = acc + _dot(a.astype(BF16), v_ref[pl.ds(k0, t), hs])
        rsum = rsum + jnp.sum(log_keep, axis=-1, keepdims=True)
        return rsum, acc

    carry = []
    for h in range(SB_GROUP):
        carry += tile(h, i, jnp.zeros((t, 1), F32), jnp.zeros((t, HEAD_DIM), F32), True)

    def body(jj, carry):
        out = []
        for h in range(SB_GROUP):
            out += tile(h, i - 1 - jj, carry[2 * h], carry[2 * h + 1], False)
        return tuple(out)

    carry = lax.fori_loop(0, i, body, tuple(carry))
    for h in range(SB_GROUP):
        o_ref[:, h * HEAD_DIM:(h + 1) * HEAD_DIM] = carry[2 * h + 1].astype(o_ref.dtype)


def _sb_attention(proj3):
    b, s, _ = proj3.shape
    gw = SB_GROUP * HEAD_DIM
    return pl.pallas_call(
        _sb_kernel,
        grid=(b, SB_HEADS // SB_GROUP, s // SB_TILE),
        in_specs=[
            pl.BlockSpec((None, SB_TILE, gw), lambda bi, hg, i: (bi, i, CB_SB_Q // SB_GROUP + hg)),
            pl.BlockSpec((None, s, gw), lambda bi, hg, i: (bi, 0, CB_SB_K // SB_GROUP + hg)),
            pl.BlockSpec((None, s, gw), lambda bi, hg, i: (bi, 0, CB_SB_V // SB_GROUP + hg)),
        ],
        out_specs=pl.BlockSpec((None, SB_TILE, gw), lambda bi, hg, i: (bi, i, hg)),
        out_shape=jax.ShapeDtypeStruct((b, s, SB_HEADS * HEAD_DIM), BF16),
        compiler_params=_cparams(3),
        name="sb_attention",
    )(proj3, proj3, proj3)


def _nsa_prep_kernel(kc_ref, vc_ref, ks_ref, kw_ref, pk_ref, pv_ref, wk_ref, wv_ref, kcg_ref, ksg_ref, kwg_ref,
                     kcb_ref, vcb_ref, ksn_ref, kwn_ref):
    half = CMP_STRIDE * HEAD_DIM

    def compress(a_ref, p_ref, w_ref):
        a = a_ref[...].astype(F32)
        top = _dot((a + p_ref[0:1, :]).astype(BF16), w_ref[0:half, :])
        bot = _dot((a + p_ref[1:2, :]).astype(BF16), w_ref[half:2 * half, :])
        return top + pltpu.roll(bot, bot.shape[0] - 1, axis=0)

    kcb_ref[...] = _rms(compress(kc_ref, pk_ref, wk_ref), kcg_ref[...]).astype(BF16)
    vcb_ref[...] = compress(vc_ref, pv_ref, wv_ref).astype(BF16)
    kwn_ref[...] = _rms(kw_ref[...].astype(F32), kwg_ref[...]).astype(BF16)
    ksn_ref[:, 0:HEAD_DIM] = _rms(ks_ref[...].astype(F32), ksg_ref[...]).astype(BF16)
    s = ks_ref.shape[0]
    blk = jnp.right_shift(lax.broadcasted_iota(jnp.int32, (s, LANES), 0), int(math.log2(SLC_LEN)))
    onehot = blk == lax.broadcasted_iota(jnp.int32, (s, LANES), 1)
    ksn_ref[:, HEAD_DIM:HEAD_DIM + LANES] = jnp.where(onehot, 1.0, 0.0).astype(BF16)


def _nsa_prep(proj3, kc_chunks, vc_chunks, pos_k, pos_v, wk, wv, kc_g, ks_g, kw_g):
    b, s, _ = proj3.shape
    g_n = NSA_KV_HEADS
    nch = s // CMP_STRIDE
    half = CMP_STRIDE * HEAD_DIM
    vec = lambda: pl.BlockSpec((1, HEAD_DIM), lambda bi, g: (0, 0))
    small = jax.ShapeDtypeStruct((b, g_n, nch, HEAD_DIM), BF16)
    full = jax.ShapeDtypeStruct((b, g_n, s, HEAD_DIM), BF16)
    aug = jax.ShapeDtypeStruct((b, g_n, s, HEAD_DIM + LANES), BF16)
    return pl.pallas_call(
        _nsa_prep_kernel,
        grid=(b, g_n),
        in_specs=[
            pl.BlockSpec((None, None, nch, half), lambda bi, g: (bi, g, 0, 0)),
            pl.BlockSpec((None, None, nch, half), lambda bi, g: (bi, g, 0, 0)),
            pl.BlockSpec((None, s, HEAD_DIM), lambda bi, g: (bi, 0, CB_KS + g)),
            pl.BlockSpec((None, s, HEAD_DIM), lambda bi, g: (bi, 0, CB_KW + g)),
            pl.BlockSpec((2, half), lambda bi, g: (0, 0)),
            pl.BlockSpec((2, half), lambda bi, g: (0, 0)),
            pl.BlockSpec((2 * half, HEAD_DIM), lambda bi, g: (0, 0)),
            pl.BlockSpec((2 * half, HEAD_DIM), lambda bi, g: (0, 0)),
            vec(), vec(), vec(),
        ],
        out_specs=[
            pl.BlockSpec((None, None, nch, HEAD_DIM), lambda bi, g: (bi, g, 0, 0)),
            pl.BlockSpec((None, None, nch, HEAD_DIM), lambda bi, g: (bi, g, 0, 0)),
            pl.BlockSpec((None, None, s, HEAD_DIM + LANES), lambda bi, g: (bi, g, 0, 0)),
            pl.BlockSpec((None, None, s, HEAD_DIM), lambda bi, g: (bi, g, 0, 0)),
        ],
        out_shape=[small, small, aug, full],
        compiler_params=_cparams(2),
        name="nsa_prep",
    )(kc_chunks, vc_chunks, proj3, proj3, pos_k.reshape(2, half), pos_v.reshape(2, half),
      wk.astype(BF16), wv.astype(BF16), kc_g.reshape(1, -1), ks_g.reshape(1, -1), kw_g.reshape(1, -1))


def _nsa_kernel(q_ref, kcb_ref, vcb_ref, ksa_ref, vs_ref, kwn_ref, vw_ref, gate_ref, bc_ref, bt_ref, qg_ref, ovt_ref,
                o_ref):
    g = pl.program_id(1)
    i = pl.program_id(2)
    r_n = NSA_REP
    rows = r_n * TQ

    qs = []
    for r in range(r_n):
        qr = q_ref[:, r * HEAD_DIM:(r + 1) * HEAD_DIM].astype(F32)
        qs.append(_rms(qr, qg_ref[...]).astype(BF16))
    q = jnp.concatenate(qs, axis=0)

    lane = lax.broadcasted_iota(jnp.int32, (TQ, LANES), 1)

    logit_c = _dot_t(q, kcb_ref[...]) + bc_ref[...].reshape(rows, LANES)
    m_c = jnp.max(logit_c, axis=-1, keepdims=True)
    p_c = jnp.where(logit_c > 0.5 * NEG_INF, jnp.exp(logit_c - m_c), 0.0)
    p_c = p_c / jnp.maximum(jnp.sum(p_c, axis=-1, keepdims=True), TINY)
    o_c = _dot(p_c.astype(BF16), vcb_ref[...])

    p_sum = p_c[0:TQ]
    for r in range(1, r_n):
        p_sum = p_sum + p_c[r * TQ:(r + 1) * TQ]
    hi = p_sum.astype(BF16)
    lo = (p_sum - hi.astype(F32)).astype(BF16)
    n_sel = ovt_ref.shape[0]
    imp = _dot_t(ovt_ref[...], hi) + _dot_t(ovt_ref[...], lo)
    blk = lax.broadcasted_iota(jnp.int32, (n_sel, TQ), 0)
    tb = jnp.right_shift(i * TQ + lax.broadcasted_iota(jnp.int32, (n_sel, TQ), 1), int(math.log2(SLC_LEN)))
    causal_blk = blk <= tb
    forced = (blk == 0) | (causal_blk & (blk > tb - N_LOCAL_BLOCKS))
    score = jnp.where(forced, FORCE_SCORE, jnp.where(causal_blk, imp, -FORCE_SCORE))
    rank = jnp.zeros((n_sel, TQ), F32)
    for b in range(n_sel):
        sb = score[b:b + 1, :]
        beats = (sb > score) | ((sb == score) & (blk > b))
        rank = rank + jnp.where(beats, 1.0, 0.0)
    chosen = (rank < float(min(SLC_TOP_N, n_sel))) & causal_blk
    sel_neg = jnp.where(chosen, 0.0, NEG_INF)
    sel_neg = jnp.concatenate([sel_neg, jnp.zeros((LANES - n_sel, TQ), F32)], axis=0).T.astype(BF16)
    q_aug = jnp.concatenate([q, jnp.concatenate([sel_neg] * r_n, axis=0)], axis=1)

    def biased(s, first_tile, idx_fn):
        n_t = s.shape[1] // TQ
        return jnp.concatenate(
            [s[:, c * TQ:(c + 1) * TQ] + bt_ref[idx_fn(i - (first_tile + c))] for c in range(n_t)], axis=1)

    w_tile = jnp.maximum(i - WINDOW // TQ, 0)
    w0 = pl.multiple_of(w_tile * TQ, TQ)

    def win_idx(off):
        far = jnp.where(off == WINDOW // TQ, BT_FAR_UPPER, jnp.minimum(off, BT_FAR))
        return jnp.where(off < 0, BT_MASKED, far)

    s_w = biased(_dot_t(q, kwn_ref[pl.ds(w0, WIN_SPAN), :]), w_tile, win_idx)
    p_w = jnp.exp(s_w - jnp.max(s_w, axis=-1, keepdims=True))
    o_w = _dot(p_w.astype(BF16), vw_ref[pl.ds(w0, WIN_SPAN), :]) / jnp.sum(p_w, axis=-1, keepdims=True)

    def sel_chunk(kc, carry, diag):
        c0 = pl.multiple_of(kc * SEL_CHUNK, SEL_CHUNK)
        idx_fn = (lambda off: jnp.where(off < 0, BT_MASKED, jnp.minimum(off, BT_FAR))) if diag else (
            lambda off: jnp.minimum(off, BT_FAR))
        s = biased(_dot_t(q_aug, ksa_ref[pl.ds(c0, SEL_CHUNK), :]), kc * (SEL_CHUNK // TQ), idx_fn)
        m_blk = jnp.max(s, axis=-1, keepdims=True)
        if diag:
            p = jnp.exp(s - m_blk)
            return m_blk, jnp.sum(p, axis=-1, keepdims=True), _dot(p.astype(BF16), vs_ref[pl.ds(c0, SEL_CHUNK), :])
        m, l, acc = carry
        m_new = jnp.maximum(m, m_blk)
        p = jnp.exp(s - m_new)
        alpha = jnp.exp(m - m_new)
        l = alpha * l + jnp.sum(p, axis=-1, keepdims=True)
        acc = alpha * acc + _dot(p.astype(BF16), vs_ref[pl.ds(c0, SEL_CHUNK), :])
        return m_new, l, acc

    kc_diag = lax.shift_right_logical(i, jnp.int32(int(math.log2(SEL_CHUNK // TQ))))
    carry = sel_chunk(kc_diag, None, True)
    _, l_s, acc_s = lax.fori_loop(0, kc_diag, lambda jj, c: sel_chunk(kc_diag - 1 - jj, c, False), carry)
    o_s = acc_s / l_s

    gates = jax.nn.sigmoid(gate_ref[...].astype(F32))

    def gate_col(idx):
        return jnp.sum(jnp.where(lane == idx, gates, 0.0), axis=-1, keepdims=True)

    for r in range(r_n):
        h = g * r_n + r
        sl = slice(r * TQ, (r + 1) * TQ)
        o = gate_col(h) * o_c[sl] + gate_col(NSA_HEADS + h) * o_s[sl] + gate_col(2 * NSA_HEADS + h) * o_w[sl]
        o_ref[:, r * HEAD_DIM:(r + 1) * HEAD_DIM] = o.astype(o_ref.dtype)


def _overlap_matrix(s):
    nch = s // CMP_STRIDE
    nsel = s // SLC_LEN
    ci = np.arange(nch)[:, None]
    sj = np.arange(nsel)[None, :]
    ov = (ci * CMP_STRIDE <= sj * SLC_LEN + SLC_LEN - 1) & (ci * CMP_STRIDE + CMP_LEN - 1 >= sj * SLC_LEN)
    ov = ov & (ci < nch - 1)
    return jnp.asarray(ov.T, BF16)


def _nsa_attention(proj3, kcb, vcb, ksa, kwn, bias_c, bias_t, q_g):
    b, s, _ = proj3.shape
    g_n, r_n = NSA_KV_HEADS, NSA_REP
    nch = s // CMP_STRIDE
    nsel = s // SLC_LEN
    gw = r_n * HEAD_DIM
    return pl.pallas_call(
        _nsa_kernel,
        grid=(b, g_n, s // TQ),
        in_specs=[
            pl.BlockSpec((None, TQ, gw), lambda bi, g, i: (bi, i, CB_NQ // r_n + g)),
            pl.BlockSpec((None, None, nch, HEAD_DIM), lambda bi, g, i: (bi, g, 0, 0)),
            pl.BlockSpec((None, None, nch, HEAD_DIM), lambda bi, g, i: (bi, g, 0, 0)),
            pl.BlockSpec((None, None, s, HEAD_DIM + LANES), lambda bi, g, i: (bi, g, 0, 0)),
            pl.BlockSpec((None, s, HEAD_DIM), lambda bi, g, i: (bi, 0, CB_VS + g)),
            pl.BlockSpec((None, None, s, HEAD_DIM), lambda bi, g, i: (bi, g, 0, 0)),
            pl.BlockSpec((None, s, HEAD_DIM), lambda bi, g, i: (bi, 0, CB_VW + g)),
            pl.BlockSpec((None, TQ, LANES), lambda bi, g, i: (bi, i, CB_NGATE)),
            pl.BlockSpec((None, r_n, TQ, LANES), lambda bi, g, i: (g, 0, i, 0)),
            pl.BlockSpec((None, 5, r_n * TQ, TQ), lambda bi, g, i: (g, 0, 0, 0)),
            pl.BlockSpec((1, HEAD_DIM), lambda bi, g, i: (0, 0)),
            pl.BlockSpec((nsel, nch), lambda bi, g, i: (0, 0)),
        ],
        out_specs=pl.BlockSpec((None, TQ, gw), lambda bi, g, i: (bi, i, g)),
        out_shape=jax.ShapeDtypeStruct((b, s, NSA_HEADS * HEAD_DIM), BF16),
        compiler_params=_cparams(3),
        name="nsa_attention",
    )(proj3, kcb, vcb, ksa, proj3, kwn, proj3, proj3, bias_c, bias_t, q_g.reshape(1, -1) * SCALE, _overlap_matrix(s))


def _merge_kernel(u_ref, sb_ref, ns_ref, gc_ref, gs_ref, gn_ref, wc_ref, ws_ref, wn_ref, o_ref):
    m = jax.nn.sigmoid(gc_ref[...].astype(F32)) * _dot(u_ref[...], wc_ref[...])
    m = m + jax.nn.sigmoid(gs_ref[...].astype(F32)) * _dot(sb_ref[...], ws_ref[...])
    m = m + jax.nn.sigmoid(gn_ref[...].astype(F32)) * _dot(ns_ref[...], wn_ref[...])
    o_ref[...] = m.astype(o_ref.dtype)


def _merge(u_act, sb, nsa, proj, wc, ws, wn, tm=512, tn=1024):
    m = u_act.shape[0]
    d = wc.shape[1]
    kc = wc.shape[0]
    gate0 = CB_MERGE * LANES // tn
    gstep = d // tn
    act = lambda: pl.BlockSpec((tm, kc), lambda i, j: (i, 0))
    gate = lambda n: pl.BlockSpec((tm, tn), lambda i, j: (i, gate0 + n * gstep + j))
    wsp = lambda: pl.BlockSpec((kc, tn), lambda i, j: (0, j))
    return pl.pallas_call(
        _merge_kernel,
        grid=(m // tm, d // tn),
        in_specs=[act(), act(), act(), gate(0), gate(1), gate(2), wsp(), wsp(), wsp()],
        out_specs=pl.BlockSpec((tm, tn), lambda i, j: (i, j)),
        out_shape=jax.ShapeDtypeStruct((m, d), BF16),
        compiler_params=_cparams(2),
        name="merge",
    )(u_act, sb, nsa, proj, proj, proj, wc, ws, wn)


def _mm_res_kernel(a_ref, w_ref, x_ref, mod_ref, o_ref, acc_ref, *, row_ga, nk):
    k = pl.program_id(2)

    @pl.when(k == 0)
    def _():
        acc_ref[...] = jnp.zeros_like(acc_ref)

    acc_ref[...] += _dot(a_ref[...], w_ref[...])

    @pl.when(k == nk - 1)
    def _():
        o_ref[...] = x_ref[...] + mod_ref[row_ga:row_ga + 1, :] * acc_ref[...]


def _mm_res(a, w, x, mod, row_ga, seq, tm=1024, tn=1024, tk=None):
    m, ka = a.shape
    n = x.shape[1]
    tpb = seq // tm
    tk = tk or ka
    nk = ka // tk
    w_spec = pl.BlockSpec((tk, tn), lambda i, j, k: (k, j))
    kern = functools.partial(_mm_res_kernel, row_ga=row_ga, nk=nk)
    return pl.pallas_call(
        kern,
        grid=(m // tm, n // tn, nk),
        in_specs=[
            pl.BlockSpec((tm, tk), lambda i, j, k: (i, k)),
            w_spec,
            pl.BlockSpec((tm, tn), lambda i, j, k: (i, j)),
            pl.BlockSpec((None, 6, tn), lambda i, j, k: (i // tpb, 0, j)),
        ],
        out_specs=pl.BlockSpec((tm, tn), lambda i, j, k: (i, j)),
        out_shape=jax.ShapeDtypeStruct((m, n), F32),
        scratch_shapes=[pltpu.VMEM((tm, tn), F32)],
        compiler_params=_cparams(3),
        name="matmul_residual",
    )(a, w, x, mod)


def _ffn_up_kernel(x_ref, mod_ref, g_ref, w1_ref, w3_ref, o_ref, h_ref):
    @pl.when(pl.program_id(1) == 0)
    def _():
        h = _norm_mod(x_ref[...], g_ref[...], mod_ref[ROW_SH_FFN:ROW_SH_FFN + 1, :], mod_ref[ROW_SC_FFN:ROW_SC_FFN + 1, :])
        h_ref[...] = h.astype(BF16)

    h = h_ref[...]
    a = _dot(h, w1_ref[...])
    o_ref[...] = (a * jax.nn.sigmoid(a) * _dot(h, w3_ref[...])).astype(o_ref.dtype)


def _ffn_up(x, mod, g, w1, w3, seq, tm=1024, tn=512):
    m, d = x.shape
    f = w1.shape[1]
    tpb = seq // tm
    return pl.pallas_call(
        _ffn_up_kernel,
        grid=(m // tm, f // tn),
        in_specs=[
            pl.BlockSpec((tm, d), lambda i, j: (i, 0)),
            pl.BlockSpec((None, 6, d), lambda i, j: (i // tpb, 0, 0)),
            pl.BlockSpec((1, d), lambda i, j: (0, 0)),
            pl.BlockSpec((d, tn), lambda i, j: (0, j)),
            pl.BlockSpec((d, tn), lambda i, j: (0, j)),
        ],
        out_specs=pl.BlockSpec((tm, tn), lambda i, j: (i, j)),
        out_shape=jax.ShapeDtypeStruct((m, f), BF16),
        scratch_shapes=[pltpu.VMEM((tm, d), BF16)],
        compiler_params=_cparams(2),
        name="ffn_up",
    )(x, mod, g.reshape(1, d), w1, w3)


def _router_kernel(x_ref, mod_ref, g_ref, wr_ref, br_ref, cmbt_ref, pos_ref, post_ref, cnt_ref, h_ref):
    h = _norm_mod(x_ref[...], g_ref[...], mod_ref[ROW_SH_FFN:ROW_SH_FFN + 1, :], mod_ref[ROW_SC_FFN:ROW_SC_FFN + 1, :])
    h_ref[...] = h.astype(BF16)
    logits = jnp.dot(h, wr_ref[...], precision=lax.Precision.HIGHEST, preferred_element_type=F32) + br_ref[...]
    lane = lax.broadcasted_iota(jnp.int32, logits.shape, 1).astype(F32)
    pad = float(LANES)
    m1 = jnp.max(logits, axis=-1, keepdims=True)
    i1 = jnp.min(jnp.where(logits == m1, lane, pad), axis=-1, keepdims=True)
    rest = jnp.where(lane == i1, NEG_INF, logits)
    m2 = jnp.max(rest, axis=-1, keepdims=True)
    i2 = jnp.min(jnp.where(rest == m2, lane, pad), axis=-1, keepdims=True)
    e2 = jnp.exp(m2 - m1)
    w1 = 1.0 / (1.0 + e2)
    w2 = e2 / (1.0 + e2)
    cmb = jnp.where(lane == i1, w1, 0.0) + jnp.where(lane == i2, w2, 0.0)
    cmbt_ref[...] = cmb.T[0:cmbt_ref.shape[0], :]
    sel = jnp.where((lane == i1) | (lane == i2), 1.0, 0.0)
    tm = sel.shape[0]
    earlier = lax.broadcasted_iota(jnp.int32, (tm, tm), 1) < lax.broadcasted_iota(jnp.int32, (tm, tm), 0)
    pos = _dot(jnp.where(earlier, 1.0, 0.0).astype(BF16), sel.astype(BF16))
    pos = jnp.where(sel > 0.0, pos, -1.0)
    pos_ref[...] = pos
    post_ref[...] = pos.T[0:post_ref.shape[0], :]
    cnt_ref[...] = jnp.broadcast_to(jnp.sum(sel, axis=0, keepdims=True), cnt_ref.shape)


def _router(x, mod, g, w_router, b_router, seq):
    m, d = x.shape
    tm = MOE_TM
    n_e = w_router.shape[1]
    tpb = seq // tm
    nt = m // tm
    wr = jnp.zeros((d, LANES), F32).at[:, :n_e].set(w_router)
    br = jnp.full((1, LANES), 2.0 * NEG_INF, F32).at[0, :n_e].set(b_router)
    return pl.pallas_call(
        _router_kernel,
        grid=(nt,),
        in_specs=[
            pl.BlockSpec((tm, d), lambda i: (i, 0)),
            pl.BlockSpec((None, 6, d), lambda i: (i // tpb, 0, 0)),
            pl.BlockSpec((1, d), lambda i: (0, 0)),
            pl.BlockSpec((d, LANES), lambda i: (0, 0)),
            pl.BlockSpec((1, LANES), lambda i: (0, 0)),
        ],
        out_specs=[
            pl.BlockSpec((n_e, tm), lambda i: (0, i)),
            pl.BlockSpec((tm, LANES), lambda i: (i, 0)),
            pl.BlockSpec((n_e, tm), lambda i: (0, i)),
            pl.BlockSpec((None, 8, LANES), lambda i: (i, 0, 0)),
            pl.BlockSpec((tm, d), lambda i: (i, 0)),
        ],
        out_shape=[
            jax.ShapeDtypeStruct((n_e, m), F32),
            jax.ShapeDtypeStruct((m, LANES), F32),
            jax.ShapeDtypeStruct((n_e, m), F32),
            jax.ShapeDtypeStruct((nt, 8, LANES), F32),
            jax.ShapeDtypeStruct((m, d), BF16),
        ],
        compiler_params=_cparams(1),
        name="router",
    )(x, mod, g.reshape(1, d), wr, br)


def _moe_plan(counts, m):
    t_n, e_n = counts.shape
    per_mt = MOE_MT // MOE_BLK
    g_max = 2 * m // MOE_BLK + t_n * e_n
    pad_max = (per_mt - 1) * e_n
    nblk = (counts + MOE_BLK - 1) // MOE_BLK
    nb_e = jnp.sum(nblk, axis=0)
    cap_e = (nb_e + per_mt - 1) // per_mt * per_mt
    start_e = jnp.cumsum(cap_e) - cap_e
    dst0 = (start_e[None, :] + jnp.cumsum(nblk, axis=0) - nblk).reshape(-1)
    flat = nblk.reshape(-1)
    cum = jnp.cumsum(flat)
    g = jnp.minimum(jnp.arange(g_max, dtype=jnp.int32), cum[-1] - 1)
    p = jnp.minimum(jnp.searchsorted(cum, g, side="right"), t_n * e_n - 1).astype(jnp.int32)
    rb = g - (cum[p] - flat[p])
    k = jnp.arange(per_mt - 1, dtype=jnp.int32)[None, :]
    pad_dst = jnp.where(k < (cap_e - nb_e)[:, None], (start_e + nb_e)[:, None] + k, g_max + pad_max).reshape(-1)
    pad_dst = jnp.sort(pad_dst)
    never = jnp.full((pad_max,), 1 << 20, jnp.int32)
    g_tile = jnp.concatenate([p // e_n, jnp.full((pad_max,), t_n - 1, jnp.int32)])
    g_exp = jnp.concatenate([p % e_n, jnp.zeros((pad_max,), jnp.int32)])
    g_rb = jnp.concatenate([rb, never])
    g_dst = jnp.concatenate([dst0[p] + rb, pad_dst])
    mt_max = (g_max + pad_max) // per_mt
    n_mt = (jnp.sum(cap_e) // per_mt).astype(jnp.int32)
    mt = jnp.minimum(jnp.arange(mt_max, dtype=jnp.int32), n_mt - 1)
    mt_exp = jnp.minimum(jnp.searchsorted(jnp.cumsum(cap_e) // per_mt, mt, side="right"), e_n - 1).astype(jnp.int32)
    s_max = 2 * MOE_TM // MOE_BLK + e_n
    cum_t = jnp.cumsum(nblk, axis=1)
    n_t = cum_t[:, -1:]
    s_all = jnp.arange(s_max, dtype=jnp.int32)[None, :]
    s = jnp.minimum(s_all, n_t - 1)
    s_exp = jnp.minimum(jax.vmap(lambda c, q: jnp.searchsorted(c, q, side="right"))(cum_t, s), e_n - 1).astype(jnp.int32)
    s_rb = s - (jnp.take_along_axis(cum_t, s_exp, axis=1) - jnp.take_along_axis(nblk, s_exp, axis=1))
    s_src = jnp.take_along_axis(dst0.reshape(t_n, e_n), s_exp, axis=1) + s_rb
    s_rb = jnp.where(s_all < n_t, s_rb, 1 << 20)
    i32 = lambda a: a.astype(jnp.int32)
    return dict(g_tile=i32(g_tile), g_exp=i32(g_exp), g_rb=i32(g_rb), g_dst=i32(g_dst), n_mt=i32(n_mt.reshape(1)),
                mt_exp=i32(mt_exp), s_exp=i32(s_exp.reshape(-1)), s_rb=i32(s_rb.reshape(-1)),
                s_src=i32(s_src.reshape(-1)), n_blocks=g_max + pad_max + 1, s_max=s_max)


def _moe_gather_kernel(tile_ref, exp_ref, rb_ref, dst_ref, h_ref, post_ref, cmbt_ref, o_ref, w_ref):
    g = pl.program_id(0)
    row = post_ref[pl.ds(exp_ref[g], 1), :]
    tm = row.shape[1]
    want = (lax.broadcasted_iota(jnp.int32, (MOE_BLK, tm), 0) + rb_ref[g] * MOE_BLK).astype(F32)
    match = row == want
    o_ref[...] = _dot(jnp.where(match, 1.0, 0.0).astype(BF16), h_ref[...]).astype(o_ref.dtype)
    cw = jnp.sum(jnp.where(match, cmbt_ref[pl.ds(exp_ref[g], 1), :], 0.0), axis=-1, keepdims=True)
    w_ref[...] = jnp.broadcast_to(cw, w_ref.shape)


def _moe_gather(h, post, cmbt, plan):
    m, d = h.shape
    n_e = post.shape[0]
    n = plan["g_tile"].shape[0]
    rows = plan["n_blocks"] * MOE_BLK
    return pl.pallas_call(
        _moe_gather_kernel,
        grid_spec=pltpu.PrefetchScalarGridSpec(
            num_scalar_prefetch=4,
            grid=(n,),
            in_specs=[
                pl.BlockSpec((MOE_TM, d), lambda g, t, e, r, ds: (t[g], 0)),
                pl.BlockSpec((n_e, MOE_TM), lambda g, t, e, r, ds: (0, t[g])),
                pl.BlockSpec((n_e, MOE_TM), lambda g, t, e, r, ds: (0, t[g])),
            ],
            out_specs=[
                pl.BlockSpec((MOE_BLK, d), lambda g, t, e, r, ds: (ds[g], 0)),
                pl.BlockSpec((MOE_BLK, LANES), lambda g, t, e, r, ds: (ds[g], 0)),
            ],
        ),
        out_shape=[jax.ShapeDtypeStruct((rows, d), BF16), jax.ShapeDtypeStruct((rows, LANES), F32)],
        compiler_params=_cparams(1),
        name="moe_gather",
    )(plan["g_tile"], plan["g_exp"], plan["g_rb"], plan["g_dst"], h, post, cmbt)


def _moe_ffn_kernel(exp_ref, n_ref, x_ref, rw_ref, w1_ref, w3_ref, w2_ref, o_ref, acc_ref, *, nc):
    c = pl.program_id(1)

    @pl.when(pl.program_id(0) < n_ref[0])
    def _():
        @pl.when(c == 0)
        def _():
            acc_ref[...] = jnp.zeros_like(acc_ref)

        x = x_ref[...]
        fc = w1_ref.shape[1]
        for k0 in range(0, fc, MOE_SUB):
            k1 = min(k0 + MOE_SUB, fc)
            a = _dot(x, w1_ref[:, k0:k1])
            a = (a * jax.nn.sigmoid(a) * _dot(x, w3_ref[:, k0:k1])).astype(BF16)
            acc_ref[...] += _dot(a, w2_ref[k0:k1, :])

        @pl.when(c == nc - 1)
        def _():
            o_ref[...] = (rw_ref[:, 0:1] * acc_ref[...]).astype(o_ref.dtype)


def _moe_ffn(xg, row_w, w1, w3, w2, plan, fc):
    d = xg.shape[1]
    fe = w1.shape[2]
    nc = fe // fc
    n_mt = plan["mt_exp"].shape[0]

    def tile(mt, n):
        return jnp.minimum(mt, n[0] - 1)

    def chunk(mt, c, n):
        return jnp.where(mt < n[0], c, nc - 1)

    kern = functools.partial(_moe_ffn_kernel, nc=nc)
    return pl.pallas_call(
        kern,
        grid_spec=pltpu.PrefetchScalarGridSpec(
            num_scalar_prefetch=2,
            grid=(n_mt, nc),
            in_specs=[
                pl.BlockSpec((MOE_MT, d), lambda mt, c, e, n: (tile(mt, n), 0)),
                pl.BlockSpec((MOE_MT, LANES), lambda mt, c, e, n: (tile(mt, n), 0)),
                pl.BlockSpec((None, d, fc), lambda mt, c, e, n: (e[mt], 0, chunk(mt, c, n))),
                pl.BlockSpec((None, d, fc), lambda mt, c, e, n: (e[mt], 0, chunk(mt, c, n))),
                pl.BlockSpec((None, fc, d), lambda mt, c, e, n: (e[mt], chunk(mt, c, n), 0)),
            ],
            out_specs=pl.BlockSpec((MOE_MT, d), lambda mt, c, e, n: (tile(mt, n), 0)),
            scratch_shapes=[pltpu.VMEM((MOE_MT, d), F32)],
        ),
        out_shape=jax.ShapeDtypeStruct((n_mt * MOE_MT, d), BF16),
        compiler_params=_cparams(2),
        name="moe_ffn",
    )(plan["mt_exp"], plan["n_mt"], xg, row_w, w1, w3, w2)


def _moe_scatter_kernel(src_ref, exp_ref, rb_ref, ya_ref, yb_ref, pos_ref, x_ref, mod_ref, o_ref, acc_ref, *, ns):
    i = pl.program_id(0)
    s = pl.program_id(1)

    @pl.when(s == 0)
    def _():
        acc_ref[...] = jnp.zeros_like(acc_ref)

    lane = lax.broadcasted_iota(jnp.int32, pos_ref.shape, 1)
    lane_f = lane.astype(F32)

    def onehot(slot):
        pos = jnp.sum(jnp.where(lane == exp_ref[slot], pos_ref[...], 0.0), axis=-1, keepdims=True)
        return jnp.where(pos - (rb_ref[slot] * MOE_BLK).astype(F32) == lane_f, 1.0, 0.0).astype(BF16)

    s0 = i * ns + 2 * s
    pt = jnp.concatenate([onehot(s0), onehot(s0 + 1)], axis=1)
    acc_ref[...] += _dot(pt, jnp.concatenate([ya_ref[...], yb_ref[...]], axis=0))

    @pl.when(s == ns // 2 - 1)
    def _():
        o_ref[...] = x_ref[...] + mod_ref[ROW_GA_FFN:ROW_GA_FFN + 1, :] * acc_ref[...]


def _moe_scatter(yg, pos, x, mod, plan, seq):
    m, d = x.shape
    ns = plan["s_max"]
    tpb = seq // MOE_TM
    kern = functools.partial(_moe_scatter_kernel, ns=ns)
    return pl.pallas_call(
        kern,
        grid_spec=pltpu.PrefetchScalarGridSpec(
            num_scalar_prefetch=3,
            grid=(m // MOE_TM, ns // 2),
            in_specs=[
                pl.BlockSpec((MOE_BLK, d), lambda i, s, sr, ex, rb: (sr[i * ns + 2 * s], 0)),
                pl.BlockSpec((MOE_BLK, d), lambda i, s, sr, ex, rb: (sr[i * ns + 2 * s + 1], 0)),
                pl.BlockSpec((MOE_TM, LANES), lambda i, s, sr, ex, rb: (i, 0)),
                pl.BlockSpec((MOE_TM, d), lambda i, s, sr, ex, rb: (i, 0)),
                pl.BlockSpec((None, 6, d), lambda i, s, sr, ex, rb: (i // tpb, 0, 0)),
            ],
            out_specs=pl.BlockSpec((MOE_TM, d), lambda i, s, sr, ex, rb: (i, 0)),
            scratch_shapes=[pltpu.VMEM((MOE_TM, d), F32)],
        ),
        out_shape=jax.ShapeDtypeStruct((m, d), F32),
        compiler_params=_cparams(2),
        name="moe_scatter",
    )(plan["s_src"], plan["s_exp"], plan["s_rb"], yg, yg, pos, x, mod)


def _moe_layer(x, mod, g, w_router, b_router, w1, w3, w2, seq, fc):
    m = x.shape[0]
    cmbt, pos, post, counts, h = _router(x, mod, g, w_router, b_router, seq)
    plan = _moe_plan(counts[:, 0, :w_router.shape[1]].astype(jnp.int32), m)
    xg, row_w = _moe_gather(h, post, cmbt, plan)
    yg = _moe_ffn(xg, row_w, w1, w3, w2, plan, fc)
    return _moe_scatter(yg, pos, x, mod, plan, seq)


def _pad_w_in(w):
    d = w.shape[0]
    gap = CB_MERGE * LANES - RAW_MERGE_COL
    q0, q1 = CB_SB_Q * LANES, CB_SB_K * LANES
    return jnp.concatenate([w[:, :q0], w[:, q0:q1] * SCALE, w[:, q1:RAW_MERGE_COL], jnp.zeros((d, gap), w.dtype),
                            w[:, RAW_MERGE_COL:]], axis=1).astype(BF16)


def _kv_chunks(proj3, cb):
    b, s, _ = proj3.shape
    t = proj3[:, :, cb * LANES:(cb + NSA_KV_HEADS) * LANES].reshape(b, s, NSA_KV_HEADS, HEAD_DIM)
    return t.transpose(0, 2, 1, 3).reshape(b, NSA_KV_HEADS, s // CMP_STRIDE, CMP_STRIDE * HEAD_DIM)


def kernel(x, c, w_ada, b_ada, g_mix, g_ffn, w_in, conv_w, conv_b, conv_ln_g, conv_ln_b, w_conv_out, w_sb_out,
           nsa_cmp_pos_k, nsa_cmp_pos_v, nsa_cmp_wk, nsa_cmp_wv, nsa_q_g, nsa_kc_g, nsa_ks_g, nsa_kw_g, w_nsa_out, w_o,
           rel_bias, ffn_w1, ffn_w3, ffn_w2, moe_router, moe_router_b, moe_w1, moe_w3, moe_w2):
    b, s, d = x.shape
    m = b * s
    depth = w_ada.shape[0]
    mod_all = _ada_all(c, w_ada, b_ada)
    bias_c, bias_t = _bias_tables(rel_bias, s)
    xf = x.reshape(m, d)
    for l in range(depth):
        mod = mod_all[l].reshape(b, 6, d)
        proj = _norm_matmul(xf, mod, g_mix[l], _pad_w_in(w_in[l]), ROW_SH_MIX, ROW_SC_MIX, s)
        proj3 = proj.reshape(b, s, NP)
        u_act = _conv_module(proj, conv_w[l], conv_b[l], conv_ln_g[l], conv_ln_b[l], b, s)
        sb = _sb_attention(proj3)
        kcb, vcb, ksn, kwn = _nsa_prep(proj3, _kv_chunks(proj3, CB_KC), _kv_chunks(proj3, CB_VC), nsa_cmp_pos_k[l],
                                       nsa_cmp_pos_v[l], nsa_cmp_wk[l], nsa_cmp_wv[l], nsa_kc_g[l], nsa_ks_g[l],
                                       nsa_kw_g[l])
        nsa = _nsa_attention(proj3, kcb, vcb, ksn, kwn, bias_c, bias_t, nsa_q_g[l])
        merged = _merge(u_act, sb.reshape(m, -1), nsa.reshape(m, -1), proj, w_conv_out[l].astype(BF16),
                        w_sb_out[l].astype(BF16), w_nsa_out[l].astype(BF16))
        xf = _mm_res(merged, w_o[l].astype(BF16), xf, mod, ROW_GA_MIX, s)
        i = l // 2
        if l % 2 == 0:
            act = _ffn_up(xf, mod, g_ffn[l], ffn_w1[i].astype(BF16), ffn_w3[i].astype(BF16), s)
            xf = _mm_res(act, ffn_w2[i].astype(BF16), xf, mod, ROW_GA_FFN, s, tm=512, tn=512)
        else:
            xf = _moe_layer(xf, mod, g_ffn[l], moe_router[i], moe_router_b[i], moe_w1[i].astype(BF16),
                            moe_w3[i].astype(BF16), moe_w2[i].astype(BF16), s, MOE_FC)
    return xf.reshape(b, s, d)
```

```python
import functools
import math

import numpy as np
import jax
import jax.numpy as jnp
from jax import lax
from jax.experimental import pallas as pl
from jax.experimental.pallas import tpu as pltpu

F32 = jnp.float32
BF16 = jnp.bfloat16

D_MODEL = 2048
DEPTH = 4
D_CONV = 1024
CONV_WIDTH = 31
SB_HEADS = 8
HEAD_DIM = 128
NSA_HEADS = 8
NSA_KV_HEADS = 2
NSA_REP = NSA_HEADS // NSA_KV_HEADS
CMP_LEN = 32
CMP_STRIDE = 16
SLC_LEN = 64
SLC_TOP_N = 16
N_LOCAL_BLOCKS = 2
WINDOW = 512
FORCE_SCORE = 1e4
REL_BUCKETS = 32
REL_MAX_DIST = 128
D_FF = 5632
N_EXPERTS = 8
D_FF_EXPERT = 2816
EPS = 1e-6
NEG_INF = -1e30
TINY = 1e-20

LANES = 128
V7X_VMEM_LIMIT_BYTES = 56 * 1024 * 1024

NP_BLOCKS = 112
NP = NP_BLOCKS * LANES
CB_GLU_A, CB_GLU_G = 0, 8
CB_SB_Q, CB_SB_K, CB_SB_V = 16, 24, 32
CB_NQ = 40
CB_KC, CB_VC, CB_KS, CB_VS, CB_KW, CB_VW = 48, 50, 52, 54, 56, 58
CB_NGATE = 60
CB_MERGE = 64
RAW_GATE_COL = 7680
RAW_MERGE_COL = 7704
ROW_SH_MIX, ROW_SC_MIX, ROW_GA_MIX, ROW_SH_FFN, ROW_SC_FFN, ROW_GA_FFN = range(6)

SCALE = HEAD_DIM ** -0.5
SB_Q_SCALE = SCALE * math.log2(math.e)
TQ = 128
SB_TILE = 256
SB_GROUP = 8
SEL_CHUNK = 512
WIN_SPAN = WINDOW + TQ
BT_DIAG, BT_SUB, BT_FAR, BT_FAR_UPPER, BT_MASKED = range(5)
CONV_HALO = 32
MOE_TM = 1024
MOE_BLK = 128
MOE_MT = 512
MOE_FC = 1408
MOE_SUB = 256
MOE_NEVER = 1 << 20


def _cparams(n_axes):
    return pltpu.CompilerParams(
        dimension_semantics=("arbitrary",) * n_axes,
        vmem_limit_bytes=V7X_VMEM_LIMIT_BYTES,
    )


def _dot(a, b):
    return jnp.dot(a, b, preferred_element_type=F32)


def _dot_t(a, b):
    return lax.dot_general(a, b, (((1,), (1,)), ((), ())), preferred_element_type=F32)


def _split_dot(x, w01):
    hi = x.astype(BF16)
    lo = (x - hi.astype(F32)).astype(BF16)
    return _dot(hi, w01) + _dot(lo, w01)


def _norm_mod(x, g, sh, sc):
    ms = jnp.mean(x * x, axis=-1, keepdims=True)
    return (x * lax.rsqrt(ms + EPS) * g) * (1.0 + sc) + sh


def _rms(x, g):
    ms = jnp.mean(x * x, axis=-1, keepdims=True)
    return x * lax.rsqrt(ms + EPS) * g


def _ada_kernel(c_ref, w_ref, b_ref, o_ref):
    c = c_ref[...]
    ca = (c * jax.nn.sigmoid(c)).astype(BF16)
    o_ref[...] = _dot(ca, w_ref[...].astype(BF16)) + b_ref[...]


def _ada_all(c, w_ada, b_ada):
    depth, d, n = w_ada.shape
    nb = c.shape[0]
    b = 16
    c = jnp.zeros((b, d), c.dtype).at[:nb].set(c)
    tn = 1024
    out = pl.pallas_call(
        _ada_kernel,
        grid=(depth, n // tn),
        in_specs=[
            pl.BlockSpec((b, d), lambda l, j: (0, 0)),
            pl.BlockSpec((None, d, tn), lambda l, j: (l, 0, j)),
            pl.BlockSpec((None, 1, tn), lambda l, j: (l, 0, j)),
        ],
        out_specs=pl.BlockSpec((None, b, tn), lambda l, j: (l, 0, j)),
        out_shape=jax.ShapeDtypeStruct((depth, b, n), F32),
        compiler_params=_cparams(2),
        name="ada",
    )(c, w_ada, b_ada.reshape(depth, 1, n))
    return out[:, :nb]


def _bias_kernel(tab_ref, bk_ref, o_ref):
    h = pl.program_id(0)
    bk = bk_ref[...]
    acc = jnp.zeros(bk.shape, F32)
    for b in range(REL_BUCKETS):
        acc = jnp.where(bk == b, tab_ref[b, h], acc)
    o_ref[...] = acc


def _bias_expand(rel_bias, buckets):
    r = buckets.shape[0]
    tr = r
    return pl.pallas_call(
        _bias_kernel,
        grid=(NSA_HEADS, r // tr),
        in_specs=[
            pl.BlockSpec(memory_space=pltpu.SMEM),
            pl.BlockSpec((tr, LANES), lambda h, i: (i, 0)),
        ],
        out_specs=pl.BlockSpec((None, tr, LANES), lambda h, i: (h, i, 0)),
        out_shape=jax.ShapeDtypeStruct((NSA_HEADS, r, LANES), F32),
        compiler_params=_cparams(2),
        name="bias_expand",
    )(rel_bias, buckets)


def _rel_bucket(dist):
    n = jnp.maximum(dist, 0)
    max_exact = REL_BUCKETS // 2
    nf = jnp.maximum(n, 1).astype(F32)
    large = max_exact + (jnp.log(nf / max_exact) / math.log(REL_MAX_DIST / max_exact) * (REL_BUCKETS - max_exact)).astype(jnp.int32)
    large = jnp.minimum(large, REL_BUCKETS - 1)
    return jnp.where(n < max_exact, n, large)


def _bias_tables(rel_bias, s):
    t = jnp.arange(s, dtype=jnp.int32)[:, None]
    cend = jnp.arange(LANES, dtype=jnp.int32)[None, :] * CMP_STRIDE + (CMP_LEN - 1)
    q = jnp.arange(TQ, dtype=jnp.int32)[:, None]
    k = jnp.arange(TQ, dtype=jnp.int32)[None, :]
    bk_t = jnp.concatenate([_rel_bucket(q - k), _rel_bucket(TQ + q - k), _rel_bucket(2 * TQ + q - k)], axis=0)
    out = _bias_expand(rel_bias, jnp.concatenate([_rel_bucket(t - cend), bk_t], axis=0))
    bias_c = jnp.where(t >= cend, out[:, :s], NEG_INF).reshape(NSA_KV_HEADS, NSA_REP, s, LANES)
    diag, sub, far = out[:, s:s + TQ], out[:, s + TQ:s + 2 * TQ], out[:, s + 2 * TQ:]
    tiles = jnp.stack([jnp.where(q >= k, diag, NEG_INF), sub, far, jnp.where(k > q, far, NEG_INF),
                       jnp.full_like(far, NEG_INF)], axis=1)
    tiles = tiles.reshape(NSA_KV_HEADS, NSA_REP, 5, TQ, TQ).transpose(0, 2, 1, 3, 4)
    return bias_c, tiles.reshape(NSA_KV_HEADS, 5, NSA_REP * TQ, TQ)


def _norm_matmul_kernel(x_ref, mod_ref, g_ref, w_ref, o_ref, h_ref, *, row_sh, row_sc):
    @pl.when(pl.program_id(1) == 0)
    def _():
        h = _norm_mod(x_ref[...], g_ref[...], mod_ref[row_sh:row_sh + 1, :], mod_ref[row_sc:row_sc + 1, :])
        h_ref[...] = h.astype(BF16)

    o_ref[...] = _dot(h_ref[...], w_ref[...]).astype(o_ref.dtype)


def _norm_matmul(x, mod, g, w, wl, row_sh, row_sc, seq, tm=1024, tn=1024):
    m, d = x.shape
    n = w.shape[2]
    tpb = seq // tm
    kern = functools.partial(_norm_matmul_kernel, row_sh=row_sh, row_sc=row_sc)
    return pl.pallas_call(
        kern,
        grid=(m // tm, n // tn),
        in_specs=[
            pl.BlockSpec((tm, d), lambda i, j: (i, 0)),
            pl.BlockSpec((None, 6, d), lambda i, j: (i // tpb, 0, 0)),
            pl.BlockSpec((1, d), lambda i, j: (0, 0)),
            pl.BlockSpec((None, d, tn), lambda i, j: (wl, 0, j)),
        ],
        out_specs=pl.BlockSpec((tm, tn), lambda i, j: (i, j)),
        out_shape=jax.ShapeDtypeStruct((m, n), BF16),
        scratch_shapes=[pltpu.VMEM((tm, d), BF16)],
        compiler_params=_cparams(2),
        name="norm_matmul",
    )(x, mod, g.reshape(1, d), w)


def _conv_kernel(a_ref, g_ref, ap_ref, gp_ref, cw_ref, cb_ref, lng_ref, lnb_ref, o_ref, ubuf, vbuf, *, ts):
    i = pl.program_id(1)
    ubuf[CONV_HALO:, :] = a_ref[...].astype(F32) * jax.nn.sigmoid(g_ref[...].astype(F32))
    up = ap_ref[...].astype(F32) * jax.nn.sigmoid(gp_ref[...].astype(F32))
    ubuf[:CONV_HALO, :] = jnp.where(i > 0, up, 0.0)

    def chunk(c, carry):
        c0 = pl.multiple_of(c * LANES, LANES)
        acc = jnp.zeros((ts, LANES), F32) + cb_ref[:, pl.ds(c0, LANES)]
        for k in range(CONV_WIDTH):
            acc = acc + cw_ref[k:k + 1, pl.ds(c0, LANES)] * ubuf[pl.ds(CONV_HALO - (CONV_WIDTH - 1) + k, ts), pl.ds(c0, LANES)]
        vbuf[:, pl.ds(c0, LANES)] = acc
        return carry

    lax.fori_loop(0, D_CONV // LANES, chunk, 0)
    v = vbuf[...]
    mu = jnp.mean(v, axis=-1, keepdims=True)
    vc = v - mu
    var = jnp.mean(vc * vc, axis=-1, keepdims=True)
    y = vc * lax.rsqrt(var + EPS) * lng_ref[...] + lnb_ref[...]
    o_ref[...] = (y * jax.nn.sigmoid(y)).astype(o_ref.dtype)


def _conv_module(proj, conv_w, conv_b, ln_g, ln_b, batch, seq, ts=256):
    m = proj.shape[0]
    nt = seq // ts
    hb = ts // CONV_HALO
    kern = functools.partial(_conv_kernel, ts=ts)

    def prev_idx(col):
        return lambda b, i: (jnp.maximum((b * nt + i) * hb - 1, 0), col)

    return pl.pallas_call(
        kern,
        grid=(batch, nt),
        in_specs=[
            pl.BlockSpec((ts, D_CONV), lambda b, i: (b * nt + i, 0)),
            pl.BlockSpec((ts, D_CONV), lambda b, i: (b * nt + i, 1)),
            pl.BlockSpec((CONV_HALO, D_CONV), prev_idx(0)),
            pl.BlockSpec((CONV_HALO, D_CONV), prev_idx(1)),
            pl.BlockSpec((CONV_WIDTH, D_CONV), lambda b, i: (0, 0)),
            pl.BlockSpec((1, D_CONV), lambda b, i: (0, 0)),
            pl.BlockSpec((1, D_CONV), lambda b, i: (0, 0)),
            pl.BlockSpec((1, D_CONV), lambda b, i: (0, 0)),
        ],
        out_specs=pl.BlockSpec((ts, D_CONV), lambda b, i: (b * nt + i, 0)),
        out_shape=jax.ShapeDtypeStruct((m, D_CONV), BF16),
        scratch_shapes=[pltpu.VMEM((CONV_HALO + ts, D_CONV), F32), pltpu.VMEM((ts, D_CONV), F32)],
        compiler_params=_cparams(2),
        name="conv_module",
    )(proj, proj, proj, proj, conv_w, conv_b.reshape(1, -1), ln_g.reshape(1, -1), ln_b.reshape(1, -1))


def _sb_kernel(q_ref, k_ref, v_ref, o_ref):
    i = pl.program_id(2)
    t = SB_TILE
    rows = SB_GROUP * t
    row = lax.broadcasted_iota(jnp.int32, (t, t), 0)
    col = lax.broadcasted_iota(jnp.int32, (t, t), 1)
    later = jnp.where(row > col, 1.0, 0.0).astype(BF16)
    heads = [slice(h * HEAD_DIM, (h + 1) * HEAD_DIM) for h in range(SB_GROUP)]

    def tile(j, rsum, acc, diag):
        k0 = pl.multiple_of(j * t, t)
        z = jnp.concatenate([_dot_t(q_ref[:, hs], k_ref[pl.ds(k0, t), hs]) for hs in heads], axis=0)
        if diag:
            qpos = lax.broadcasted_iota(jnp.int32, (rows, t), 0) & (t - 1)
            z = jnp.where(lax.broadcasted_iota(jnp.int32, (rows, t), 1) < qpos, z, NEG_INF)
        sp = jnp.log2(1.0 + jnp.exp2(-jnp.abs(z)))
        log_beta = jnp.minimum(z, 0.0) - sp
        log_keep = log_beta - z
        log_survive = _dot(log_keep.astype(BF16), later) + rsum
        a = jnp.exp2(log_beta + log_survive).astype(BF16)
        pv = [_dot(a[h * t:(h + 1) * t], v_ref[pl.ds(k0, t), hs]) for h, hs in enumerate(heads)]
        acc = acc + jnp.concatenate(pv, axis=0)
        rsum = rsum + jnp.sum(log_keep, axis=-1, keepdims=True)
        return rsum, acc

    carry = tile(i, jnp.zeros((rows, 1), F32), jnp.zeros((rows, HEAD_DIM), F32), True)
    _, acc = lax.fori_loop(0, i, lambda jj, c: tile(i - 1 - jj, c[0], c[1], False), carry)
    for h in range(SB_GROUP):
        o_ref[:, h * HEAD_DIM:(h + 1) * HEAD_DIM] = acc[h * t:(h + 1) * t].astype(o_ref.dtype)


def _sb_attention(proj3):
    b, s, _ = proj3.shape
    gw = SB_GROUP * HEAD_DIM
    return pl.pallas_call(
        _sb_kernel,
        grid=(b, SB_HEADS // SB_GROUP, s // SB_TILE),
        in_specs=[
            pl.BlockSpec((None, SB_TILE, gw), lambda bi, hg, i: (bi, i, CB_SB_Q // SB_GROUP + hg)),
            pl.BlockSpec((None, s, gw), lambda bi, hg, i: (bi, 0, CB_SB_K // SB_GROUP + hg)),
            pl.BlockSpec((None, s, gw), lambda bi, hg, i: (bi, 0, CB_SB_V // SB_GROUP + hg)),
        ],
        out_specs=pl.BlockSpec((None, SB_TILE, gw), lambda bi, hg, i: (bi, i, hg)),
        out_shape=jax.ShapeDtypeStruct((b, s, SB_HEADS * HEAD_DIM), BF16),
        compiler_params=_cparams(3),
        name="sb_attention",
    )(proj3, proj3, proj3)


def _nsa_prep_kernel(kc_ref, vc_ref, ks_ref, kw_ref, pk_ref, pv_ref, wk_ref, wv_ref, kcg_ref, ksg_ref, kwg_ref,
                     kcb_ref, vcb_ref, ksn_ref, kwn_ref):
    half = CMP_STRIDE * HEAD_DIM

    def compress(a_ref, p_ref, w_ref):
        a = a_ref[...].astype(F32)
        top = _dot((a + p_ref[0:1, :]).astype(BF16), w_ref[0:half, :])
        bot = _dot((a + p_ref[1:2, :]).astype(BF16), w_ref[half:2 * half, :])
        return top + pltpu.roll(bot, bot.shape[0] - 1, axis=0)

    kcb_ref[...] = _rms(compress(kc_ref, pk_ref, wk_ref), kcg_ref[...]).astype(BF16)
    vcb_ref[...] = compress(vc_ref, pv_ref, wv_ref).astype(BF16)
    kwn_ref[...] = _rms(kw_ref[...].astype(F32), kwg_ref[...]).astype(BF16)
    ksn_ref[:, 0:HEAD_DIM] = _rms(ks_ref[...].astype(F32), ksg_ref[...]).astype(BF16)
    s = ks_ref.shape[0]
    blk = jnp.right_shift(lax.broadcasted_iota(jnp.int32, (s, LANES), 0), int(math.log2(SLC_LEN)))
    onehot = blk == lax.broadcasted_iota(jnp.int32, (s, LANES), 1)
    ksn_ref[:, HEAD_DIM:HEAD_DIM + LANES] = jnp.where(onehot, 1.0, 0.0).astype(BF16)


def _nsa_prep(proj3, kc_chunks, vc_chunks, pos_k, pos_v, wk, wv, kc_g, ks_g, kw_g):
    b, s, _ = proj3.shape
    g_n = NSA_KV_HEADS
    nch = s // CMP_STRIDE
    half = CMP_STRIDE * HEAD_DIM
    vec = lambda: pl.BlockSpec((1, HEAD_DIM), lambda bi, g: (0, 0))
    small = jax.ShapeDtypeStruct((b, g_n, nch, HEAD_DIM), BF16)
    full = jax.ShapeDtypeStruct((b, g_n, s, HEAD_DIM), BF16)
    aug = jax.ShapeDtypeStruct((b, g_n, s, HEAD_DIM + LANES), BF16)
    return pl.pallas_call(
        _nsa_prep_kernel,
        grid=(b, g_n),
        in_specs=[
            pl.BlockSpec((None, None, nch, half), lambda bi, g: (bi, g, 0, 0)),
            pl.BlockSpec((None, None, nch, half), lambda bi, g: (bi, g, 0, 0)),
            pl.BlockSpec((None, s, HEAD_DIM), lambda bi, g: (bi, 0, CB_KS + g)),
            pl.BlockSpec((None, s, HEAD_DIM), lambda bi, g: (bi, 0, CB_KW + g)),
            pl.BlockSpec((2, half), lambda bi, g: (0, 0)),
            pl.BlockSpec((2, half), lambda bi, g: (0, 0)),
            pl.BlockSpec((2 * half, HEAD_DIM), lambda bi, g: (0, 0)),
            pl.BlockSpec((2 * half, HEAD_DIM), lambda bi, g: (0, 0)),
            vec(), vec(), vec(),
        ],
        out_specs=[
            pl.BlockSpec((None, None, nch, HEAD_DIM), lambda bi, g: (bi, g, 0, 0)),
            pl.BlockSpec((None, None, nch, HEAD_DIM), lambda bi, g: (bi, g, 0, 0)),
            pl.BlockSpec((None, None, s, HEAD_DIM + LANES), lambda bi, g: (bi, g, 0, 0)),
            pl.BlockSpec((None, None, s, HEAD_DIM), lambda bi, g: (bi, g, 0, 0)),
        ],
        out_shape=[small, small, aug, full],
        compiler_params=_cparams(2),
        name="nsa_prep",
    )(kc_chunks, vc_chunks, proj3, proj3, pos_k.reshape(2, half), pos_v.reshape(2, half),
      wk.astype(BF16), wv.astype(BF16), kc_g.reshape(1, -1), ks_g.reshape(1, -1), kw_g.reshape(1, -1))


def _nsa_kernel(q_ref, kcb_ref, vcb_ref, ksa_ref, vs_ref, kwn_ref, vw_ref, gate_ref, bc_ref, bt_ref, qg_ref, ovt_ref,
                o_ref):
    g = pl.program_id(1)
    i = pl.program_id(2)
    r_n = NSA_REP
    rows = r_n * TQ

    qs = []
    for r in range(r_n):
        qr = q_ref[:, r * HEAD_DIM:(r + 1) * HEAD_DIM].astype(F32)
        qs.append(_rms(qr, qg_ref[...]).astype(BF16))
    q = jnp.concatenate(qs, axis=0)

    lane = lax.broadcasted_iota(jnp.int32, (TQ, LANES), 1)

    logit_c = _dot_t(q, kcb_ref[...]) + bc_ref[...].reshape(rows, LANES)
    m_c = jnp.max(logit_c, axis=-1, keepdims=True)
    p_c = jnp.where(logit_c > 0.5 * NEG_INF, jnp.exp(logit_c - m_c), 0.0)
    p_c = p_c / jnp.maximum(jnp.sum(p_c, axis=-1, keepdims=True), TINY)
    o_c = _dot(p_c.astype(BF16), vcb_ref[...])

    p_sum = p_c[0:TQ]
    for r in range(1, r_n):
        p_sum = p_sum + p_c[r * TQ:(r + 1) * TQ]
    hi = p_sum.astype(BF16)
    lo = (p_sum - hi.astype(F32)).astype(BF16)
    n_sel = ovt_ref.shape[0]
    imp = _dot_t(ovt_ref[...], hi) + _dot_t(ovt_ref[...], lo)
    blk = lax.broadcasted_iota(jnp.int32, (n_sel, TQ), 0)
    tb = jnp.right_shift(i * TQ + lax.broadcasted_iota(jnp.int32, (n_sel, TQ), 1), int(math.log2(SLC_LEN)))
    causal_blk = blk <= tb
    forced = (blk == 0) | (causal_blk & (blk > tb - N_LOCAL_BLOCKS))
    score = jnp.where(forced, FORCE_SCORE, jnp.where(causal_blk, imp, -FORCE_SCORE))
    rank = jnp.zeros((n_sel, TQ), F32)
    for b in range(n_sel):
        sb = score[b:b + 1, :]
        beats = (sb > score) | ((sb == score) & (blk > b))
        rank = rank + jnp.where(beats, 1.0, 0.0)
    chosen = (rank < float(min(SLC_TOP_N, n_sel))) & causal_blk
    sel_neg = jnp.where(chosen, 0.0, NEG_INF)
    sel_neg = jnp.concatenate([sel_neg, jnp.zeros((LANES - n_sel, TQ), F32)], axis=0).T.astype(BF16)
    q_aug = jnp.concatenate([q, jnp.concatenate([sel_neg] * r_n, axis=0)], axis=1)

    def biased(s, first_tile, idx_fn):
        n_t = s.shape[1] // TQ
        return jnp.concatenate(
            [s[:, c * TQ:(c + 1) * TQ] + bt_ref[idx_fn(i - (first_tile + c))] for c in range(n_t)], axis=1)

    w_tile = jnp.maximum(i - WINDOW // TQ, 0)
    w0 = pl.multiple_of(w_tile * TQ, TQ)

    def win_idx(off):
        far = jnp.where(off == WINDOW // TQ, BT_FAR_UPPER, jnp.minimum(off, BT_FAR))
        return jnp.where(off < 0, BT_MASKED, far)

    s_w = biased(_dot_t(q, kwn_ref[pl.ds(w0, WIN_SPAN), :]), w_tile, win_idx)
    p_w = jnp.exp(s_w - jnp.max(s_w, axis=-1, keepdims=True))
    o_w = _dot(p_w.astype(BF16), vw_ref[pl.ds(w0, WIN_SPAN), :]) / jnp.sum(p_w, axis=-1, keepdims=True)

    def sel_chunk(kc, carry, diag):
        c0 = pl.multiple_of(kc * SEL_CHUNK, SEL_CHUNK)
        idx_fn = (lambda off: jnp.where(off < 0, BT_MASKED, jnp.minimum(off, BT_FAR))) if diag else (
            lambda off: jnp.minimum(off, BT_FAR))
        s = biased(_dot_t(q_aug, ksa_ref[pl.ds(c0, SEL_CHUNK), :]), kc * (SEL_CHUNK // TQ), idx_fn)
        m_blk = jnp.max(s, axis=-1, keepdims=True)
        if diag:
            p = jnp.exp(s - m_blk)
            return m_blk, jnp.sum(p, axis=-1, keepdims=True), _dot(p.astype(BF16), vs_ref[pl.ds(c0, SEL_CHUNK), :])
        m, l, acc = carry
        m_new = jnp.maximum(m, m_blk)
        p = jnp.exp(s - m_new)
        alpha = jnp.exp(m - m_new)
        l = alpha * l + jnp.sum(p, axis=-1, keepdims=True)
        acc = alpha * acc + _dot(p.astype(BF16), vs_ref[pl.ds(c0, SEL_CHUNK), :])
        return m_new, l, acc

    kc_diag = lax.shift_right_logical(i, jnp.int32(int(math.log2(SEL_CHUNK // TQ))))
    carry = sel_chunk(kc_diag, None, True)
    _, l_s, acc_s = lax.fori_loop(0, kc_diag, lambda jj, c: sel_chunk(kc_diag - 1 - jj, c, False), carry)
    o_s = acc_s / l_s

    gates = jax.nn.sigmoid(gate_ref[...].astype(F32))

    def gate_col(idx):
        return jnp.sum(jnp.where(lane == idx, gates, 0.0), axis=-1, keepdims=True)

    for r in range(r_n):
        h = g * r_n + r
        sl = slice(r * TQ, (r + 1) * TQ)
        o = gate_col(h) * o_c[sl] + gate_col(NSA_HEADS + h) * o_s[sl] + gate_col(2 * NSA_HEADS + h) * o_w[sl]
        o_ref[:, r * HEAD_DIM:(r + 1) * HEAD_DIM] = o.astype(o_ref.dtype)


def _overlap_matrix(s):
    nch = s // CMP_STRIDE
    nsel = s // SLC_LEN
    ci = np.arange(nch)[:, None]
    sj = np.arange(nsel)[None, :]
    ov = (ci * CMP_STRIDE <= sj * SLC_LEN + SLC_LEN - 1) & (ci * CMP_STRIDE + CMP_LEN - 1 >= sj * SLC_LEN)
    ov = ov & (ci < nch - 1)
    return jnp.asarray(ov.T, BF16)


def _nsa_attention(proj3, kcb, vcb, ksa, kwn, bias_c, bias_t, q_g):
    b, s, _ = proj3.shape
    g_n, r_n = NSA_KV_HEADS, NSA_REP
    nch = s // CMP_STRIDE
    nsel = s // SLC_LEN
    gw = r_n * HEAD_DIM
    return pl.pallas_call(
        _nsa_kernel,
        grid=(b, g_n, s // TQ),
        in_specs=[
            pl.BlockSpec((None, TQ, gw), lambda bi, g, i: (bi, i, CB_NQ // r_n + g)),
            pl.BlockSpec((None, None, nch, HEAD_DIM), lambda bi, g, i: (bi, g, 0, 0)),
            pl.BlockSpec((None, None, nch, HEAD_DIM), lambda bi, g, i: (bi, g, 0, 0)),
            pl.BlockSpec((None, None, s, HEAD_DIM + LANES), lambda bi, g, i: (bi, g, 0, 0)),
            pl.BlockSpec((None, s, HEAD_DIM), lambda bi, g, i: (bi, 0, CB_VS + g)),
            pl.BlockSpec((None, None, s, HEAD_DIM), lambda bi, g, i: (bi, g, 0, 0)),
            pl.BlockSpec((None, s, HEAD_DIM), lambda bi, g, i: (bi, 0, CB_VW + g)),
            pl.BlockSpec((None, TQ, LANES), lambda bi, g, i: (bi, i, CB_NGATE)),
            pl.BlockSpec((None, r_n, TQ, LANES), lambda bi, g, i: (g, 0, i, 0)),
            pl.BlockSpec((None, 5, r_n * TQ, TQ), lambda bi, g, i: (g, 0, 0, 0)),
            pl.BlockSpec((1, HEAD_DIM), lambda bi, g, i: (0, 0)),
            pl.BlockSpec((nsel, nch), lambda bi, g, i: (0, 0)),
        ],
        out_specs=pl.BlockSpec((None, TQ, gw), lambda bi, g, i: (bi, i, g)),
        out_shape=jax.ShapeDtypeStruct((b, s, NSA_HEADS * HEAD_DIM), BF16),
        compiler_params=_cparams(3),
        name="nsa_attention",
    )(proj3, kcb, vcb, ksa, proj3, kwn, proj3, proj3, bias_c, bias_t, q_g.reshape(1, -1) * SCALE, _overlap_matrix(s))


def _merge_kernel(u_ref, sb_ref, ns_ref, gc_ref, gs_ref, gn_ref, wc_ref, ws_ref, wn_ref, o_ref):
    m = jax.nn.sigmoid(gc_ref[...].astype(F32)) * _dot(u_ref[...], wc_ref[...])
    m = m + jax.nn.sigmoid(gs_ref[...].astype(F32)) * _dot(sb_ref[...], ws_ref[...])
    m = m + jax.nn.sigmoid(gn_ref[...].astype(F32)) * _dot(ns_ref[...], wn_ref[...])
    o_ref[...] = m.astype(o_ref.dtype)


def _merge(u_act, sb, nsa, proj, wc, ws, wn, wl, tm=512, tn=1024):
    m = u_act.shape[0]
    d = wc.shape[2]
    kc = wc.shape[1]
    gate0 = CB_MERGE * LANES // tn
    gstep = d // tn
    act = lambda: pl.BlockSpec((tm, kc), lambda i, j: (i, 0))
    gate = lambda n: pl.BlockSpec((tm, tn), lambda i, j: (i, gate0 + n * gstep + j))
    wsp = lambda: pl.BlockSpec((None, kc, tn), lambda i, j: (wl, 0, j))
    return pl.pallas_call(
        _merge_kernel,
        grid=(m // tm, d // tn),
        in_specs=[act(), act(), act(), gate(0), gate(1), gate(2), wsp(), wsp(), wsp()],
        out_specs=pl.BlockSpec((tm, tn), lambda i, j: (i, j)),
        out_shape=jax.ShapeDtypeStruct((m, d), BF16),
        compiler_params=_cparams(2),
        name="merge",
    )(u_act, sb, nsa, proj, proj, proj, wc, ws, wn)


def _mm_res_kernel(a_ref, w_ref, x_ref, mod_ref, o_ref, acc_ref, *, row_ga, nk):
    k = pl.program_id(2)

    @pl.when(k == 0)
    def _():
        acc_ref[...] = jnp.zeros_like(acc_ref)

    acc_ref[...] += _dot(a_ref[...], w_ref[...])

    @pl.when(k == nk - 1)
    def _():
        o_ref[...] = x_ref[...] + mod_ref[row_ga:row_ga + 1, :] * acc_ref[...]


def _mm_res(a, w, wl, x, mod, row_ga, seq, tm=1024, tn=1024, tk=None):
    m, ka = a.shape
    n = x.shape[1]
    tpb = seq // tm
    tk = tk or ka
    nk = ka // tk
    w_spec = pl.BlockSpec((None, tk, tn), lambda i, j, k: (wl, k, j))
    kern = functools.partial(_mm_res_kernel, row_ga=row_ga, nk=nk)
    return pl.pallas_call(
        kern,
        grid=(m // tm, n // tn, nk),
        in_specs=[
            pl.BlockSpec((tm, tk), lambda i, j, k: (i, k)),
            w_spec,
            pl.BlockSpec((tm, tn), lambda i, j, k: (i, j)),
            pl.BlockSpec((None, 6, tn), lambda i, j, k: (i // tpb, 0, j)),
        ],
        out_specs=pl.BlockSpec((tm, tn), lambda i, j, k: (i, j)),
        out_shape=jax.ShapeDtypeStruct((m, n), F32),
        scratch_shapes=[pltpu.VMEM((tm, tn), F32)],
        compiler_params=_cparams(3),
        name="matmul_residual",
    )(a, w, x, mod)


def _ffn_up_kernel(x_ref, mod_ref, g_ref, w1_ref, w3_ref, o_ref, h_ref):
    @pl.when(pl.program_id(1) == 0)
    def _():
        h = _norm_mod(x_ref[...], g_ref[...], mod_ref[ROW_SH_FFN:ROW_SH_FFN + 1, :], mod_ref[ROW_SC_FFN:ROW_SC_FFN + 1, :])
        h_ref[...] = h.astype(BF16)

    h = h_ref[...]
    a = _dot(h, w1_ref[...])
    o_ref[...] = (a * jax.nn.sigmoid(a) * _dot(h, w3_ref[...])).astype(o_ref.dtype)


def _ffn_up(x, mod, g, w1, w3, wl, seq, tm=1024, tn=512):
    m, d = x.shape
    f = w1.shape[2]
    tpb = seq // tm
    return pl.pallas_call(
        _ffn_up_kernel,
        grid=(m // tm, f // tn),
        in_specs=[
            pl.BlockSpec((tm, d), lambda i, j: (i, 0)),
            pl.BlockSpec((None, 6, d), lambda i, j: (i // tpb, 0, 0)),
            pl.BlockSpec((1, d), lambda i, j: (0, 0)),
            pl.BlockSpec((None, d, tn), lambda i, j: (wl, 0, j)),
            pl.BlockSpec((None, d, tn), lambda i, j: (wl, 0, j)),
        ],
        out_specs=pl.BlockSpec((tm, tn), lambda i, j: (i, j)),
        out_shape=jax.ShapeDtypeStruct((m, f), BF16),
        scratch_shapes=[pltpu.VMEM((tm, d), BF16)],
        compiler_params=_cparams(2),
        name="ffn_up",
    )(x, mod, g.reshape(1, d), w1, w3)


def _router_kernel(x_ref, mod_ref, g_ref, wr_ref, br_ref, cmbt_ref, pos_ref, post_ref, cnt_ref, h_ref):
    h = _norm_mod(x_ref[...], g_ref[...], mod_ref[ROW_SH_FFN:ROW_SH_FFN + 1, :], mod_ref[ROW_SC_FFN:ROW_SC_FFN + 1, :])
    h_ref[...] = h.astype(BF16)
    logits = jnp.dot(h, wr_ref[...], precision=lax.Precision.HIGHEST, preferred_element_type=F32) + br_ref[...]
    lane = lax.broadcasted_iota(jnp.int32, logits.shape, 1).astype(F32)
    pad = float(LANES)
    m1 = jnp.max(logits, axis=-1, keepdims=True)
    i1 = jnp.min(jnp.where(logits == m1, lane, pad), axis=-1, keepdims=True)
    rest = jnp.where(lane == i1, NEG_INF, logits)
    m2 = jnp.max(rest, axis=-1, keepdims=True)
    i2 = jnp.min(jnp.where(rest == m2, lane, pad), axis=-1, keepdims=True)
    e2 = jnp.exp(m2 - m1)
    w1 = 1.0 / (1.0 + e2)
    w2 = e2 / (1.0 + e2)
    cmb = jnp.where(lane == i1, w1, 0.0) + jnp.where(lane == i2, w2, 0.0)
    cmbt_ref[...] = cmb.T[0:cmbt_ref.shape[0], :]
    sel = jnp.where((lane == i1) | (lane == i2), 1.0, 0.0)
    tm = sel.shape[0]
    earlier = lax.broadcasted_iota(jnp.int32, (tm, tm), 1) < lax.broadcasted_iota(jnp.int32, (tm, tm), 0)
    pos = _dot(jnp.where(earlier, 1.0, 0.0).astype(BF16), sel.astype(BF16))
    pos = jnp.where(sel > 0.0, pos, -1.0)
    pos_ref[...] = pos
    post_ref[...] = pos.T[0:post_ref.shape[0], :]
    cnt_ref[...] = jnp.broadcast_to(jnp.sum(sel, axis=0, keepdims=True), cnt_ref.shape)


def _router(x, mod, g, w_router, b_router, seq):
    m, d = x.shape
    tm = MOE_TM
    n_e = w_router.shape[1]
    tpb = seq // tm
    nt = m // tm
    wr = jnp.zeros((d, LANES), F32).at[:, :n_e].set(w_router)
    br = jnp.full((1, LANES), 2.0 * NEG_INF, F32).at[0, :n_e].set(b_router)
    return pl.pallas_call(
        _router_kernel,
        grid=(nt,),
        in_specs=[
            pl.BlockSpec((tm, d), lambda i: (i, 0)),
            pl.BlockSpec((None, 6, d), lambda i: (i // tpb, 0, 0)),
            pl.BlockSpec((1, d), lambda i: (0, 0)),
            pl.BlockSpec((d, LANES), lambda i: (0, 0)),
            pl.BlockSpec((1, LANES), lambda i: (0, 0)),
        ],
        out_specs=[
            pl.BlockSpec((n_e, tm), lambda i: (0, i)),
            pl.BlockSpec((tm, LANES), lambda i: (i, 0)),
            pl.BlockSpec((n_e, tm), lambda i: (0, i)),
            pl.BlockSpec((None, 8, LANES), lambda i: (i, 0, 0)),
            pl.BlockSpec((tm, d), lambda i: (i, 0)),
        ],
        out_shape=[
            jax.ShapeDtypeStruct((n_e, m), F32),
            jax.ShapeDtypeStruct((m, LANES), F32),
            jax.ShapeDtypeStruct((n_e, m), F32),
            jax.ShapeDtypeStruct((nt, 8, LANES), F32),
            jax.ShapeDtypeStruct((m, d), BF16),
        ],
        compiler_params=_cparams(1),
        name="router",
    )(x, mod, g.reshape(1, d), wr, br)


def _moe_plan(counts, m):
    t_n, e_n = counts.shape
    per_mt = MOE_MT // MOE_BLK
    g_max = 2 * m // MOE_BLK + t_n * e_n
    pad_max = (per_mt - 1) * e_n
    nblk = (counts + MOE_BLK - 1) // MOE_BLK
    nb_e = jnp.sum(nblk, axis=0)
    cap_e = (nb_e + per_mt - 1) // per_mt * per_mt
    start_e = jnp.cumsum(cap_e) - cap_e
    dst0 = (start_e[None, :] + jnp.cumsum(nblk, axis=0) - nblk).reshape(-1)
    flat = nblk.reshape(-1)
    cum = jnp.cumsum(flat)
    g = jnp.minimum(jnp.arange(g_max, dtype=jnp.int32), cum[-1] - 1)
    p = jnp.minimum(jnp.searchsorted(cum, g, side="right"), t_n * e_n - 1).astype(jnp.int32)
    rb = g - (cum[p] - flat[p])
    n_blocks = g_max + pad_max
    z_max = n_blocks - 2 * m // MOE_BLK
    used = jnp.zeros((n_blocks,), jnp.int32).at[dst0[p] + rb].set(1)
    free = jnp.argsort(used, stable=True).astype(jnp.int32)
    z = jnp.minimum(jnp.arange(z_max, dtype=jnp.int32), n_blocks - cum[-1] - 1)
    never = jnp.full((z_max,), MOE_NEVER, jnp.int32)
    g_tile = jnp.concatenate([p // e_n, jnp.full((z_max,), t_n - 1, jnp.int32)])
    g_exp = jnp.concatenate([p % e_n, jnp.zeros((z_max,), jnp.int32)])
    g_rb = jnp.concatenate([rb, never])
    g_dst = jnp.concatenate([dst0[p] + rb, free[z]])
    mt_max = (g_max + pad_max) // per_mt
    n_mt = (jnp.sum(cap_e) // per_mt).astype(jnp.int32)
    mt = jnp.minimum(jnp.arange(mt_max, dtype=jnp.int32), n_mt - 1)
    mt_exp = jnp.minimum(jnp.searchsorted(jnp.cumsum(cap_e) // per_mt, mt, side="right"), e_n - 1).astype(jnp.int32)
    s_max = 2 * MOE_TM // MOE_BLK + e_n
    cum_t = jnp.cumsum(nblk, axis=1)
    n_t = cum_t[:, -1:]
    s_all = jnp.arange(s_max, dtype=jnp.int32)[None, :]
    s = jnp.minimum(s_all, n_t - 1)
    s_exp = jnp.minimum(jax.vmap(lambda c, q: jnp.searchsorted(c, q, side="right"))(cum_t, s), e_n - 1).astype(jnp.int32)
    s_rb = s - (jnp.take_along_axis(cum_t, s_exp, axis=1) - jnp.take_along_axis(nblk, s_exp, axis=1))
    s_src = jnp.take_along_axis(dst0.reshape(t_n, e_n), s_exp, axis=1) + s_rb
    s_rb = jnp.where(s_all < n_t, s_rb, MOE_NEVER)
    i32 = lambda a: a.astype(jnp.int32)
    return dict(g_tile=i32(g_tile), g_exp=i32(g_exp), g_rb=i32(g_rb), g_dst=i32(g_dst), n_mt=i32(n_mt.reshape(1)),
                mt_exp=i32(mt_exp), s_exp=i32(s_exp.reshape(-1)), s_rb=i32(s_rb.reshape(-1)),
                s_src=i32(s_src.reshape(-1)), n_blocks=n_blocks, s_max=s_max)


def _moe_gather_kernel(tile_ref, exp_ref, rb_ref, dst_ref, h_ref, post_ref, cmbt_ref, o_ref, w_ref):
    g = pl.program_id(0)
    fill = rb_ref[g] >= MOE_NEVER

    @pl.when(fill)
    def _():
        o_ref[...] = jnp.zeros_like(o_ref)
        w_ref[...] = jnp.zeros_like(w_ref)

    @pl.when(jnp.logical_not(fill))
    def _():
        row = post_ref[pl.ds(exp_ref[g], 1), :]
        tm = row.shape[1]
        want = (lax.broadcasted_iota(jnp.int32, (MOE_BLK, tm), 0) + rb_ref[g] * MOE_BLK).astype(F32)
        match = row == want
        o_ref[...] = _dot(jnp.where(match, 1.0, 0.0).astype(BF16), h_ref[...]).astype(o_ref.dtype)
        cw = jnp.sum(jnp.where(match, cmbt_ref[pl.ds(exp_ref[g], 1), :], 0.0), axis=-1, keepdims=True)
        w_ref[...] = jnp.broadcast_to(cw, w_ref.shape)


def _moe_gather(h, post, cmbt, plan):
    m, d = h.shape
    n_e = post.shape[0]
    n = plan["g_tile"].shape[0]
    rows = plan["n_blocks"] * MOE_BLK
    return pl.pallas_call(
        _moe_gather_kernel,
        grid_spec=pltpu.PrefetchScalarGridSpec(
            num_scalar_prefetch=4,
            grid=(n,),
            in_specs=[
                pl.BlockSpec((MOE_TM, d), lambda g, t, e, r, ds: (t[g], 0)),
                pl.BlockSpec((n_e, MOE_TM), lambda g, t, e, r, ds: (0, t[g])),
                pl.BlockSpec((n_e, MOE_TM), lambda g, t, e, r, ds: (0, t[g])),
            ],
            out_specs=[
                pl.BlockSpec((MOE_BLK, d), lambda g, t, e, r, ds: (ds[g], 0)),
                pl.BlockSpec((MOE_BLK, LANES), lambda g, t, e, r, ds: (ds[g], 0)),
            ],
        ),
        out_shape=[jax.ShapeDtypeStruct((rows, d), BF16), jax.ShapeDtypeStruct((rows, LANES), F32)],
        compiler_params=_cparams(1),
        name="moe_gather",
    )(plan["g_tile"], plan["g_exp"], plan["g_rb"], plan["g_dst"], h, post, cmbt)


def _moe_ffn_kernel(exp_ref, n_ref, x_ref, rw_ref, w1_ref, w3_ref, w2_ref, o_ref, acc_ref, *, nc):
    c = pl.program_id(1)

    @pl.when(pl.program_id(0) < n_ref[0])
    def _():
        @pl.when(c == 0)
        def _():
            acc_ref[...] = jnp.zeros_like(acc_ref)

        x = x_ref[...]
        fc = w1_ref.shape[1]
        for k0 in range(0, fc, MOE_SUB):
            k1 = min(k0 + MOE_SUB, fc)
            a = _dot(x, w1_ref[:, k0:k1])
            a = (a * jax.nn.sigmoid(a) * _dot(x, w3_ref[:, k0:k1])).astype(BF16)
            acc_ref[...] += _dot(a, w2_ref[k0:k1, :])

        @pl.when(c == nc - 1)
        def _():
            o_ref[...] = (rw_ref[:, 0:1] * acc_ref[...]).astype(o_ref.dtype)

    @pl.when(pl.program_id(0) >= n_ref[0])
    def _():
        o_ref[...] = jnp.zeros_like(o_ref)


def _moe_ffn(xg, row_w, w1, w3, w2, wl, plan, fc):
    d = xg.shape[1]
    fe = w1.shape[3]
    nc = fe // fc
    n_mt = plan["mt_exp"].shape[0]

    def tile(mt, n):
        return jnp.minimum(mt, n[0] - 1)

    def chunk(mt, c, n):
        return jnp.where(mt < n[0], c, nc - 1)

    kern = functools.partial(_moe_ffn_kernel, nc=nc)
    return pl.pallas_call(
        kern,
        grid_spec=pltpu.PrefetchScalarGridSpec(
            num_scalar_prefetch=2,
            grid=(n_mt, nc),
            in_specs=[
                pl.BlockSpec((MOE_MT, d), lambda mt, c, e, n: (tile(mt, n), 0)),
                pl.BlockSpec((MOE_MT, LANES), lambda mt, c, e, n: (tile(mt, n), 0)),
                pl.BlockSpec((None, None, d, fc), lambda mt, c, e, n: (wl, e[mt], 0, chunk(mt, c, n))),
                pl.BlockSpec((None, None, d, fc), lambda mt, c, e, n: (wl, e[mt], 0, chunk(mt, c, n))),
                pl.BlockSpec((None, None, fc, d), lambda mt, c, e, n: (wl, e[mt], chunk(mt, c, n), 0)),
            ],
            out_specs=pl.BlockSpec((MOE_MT, d), lambda mt, c, e, n: (mt, 0)),
            scratch_shapes=[pltpu.VMEM((MOE_MT, d), F32)],
        ),
        out_shape=jax.ShapeDtypeStruct((n_mt * MOE_MT, d), BF16),
        compiler_params=_cparams(2),
        name="moe_ffn",
    )(plan["mt_exp"], plan["n_mt"], xg, row_w, w1, w3, w2)


def _moe_scatter_kernel(src_ref, exp_ref, rb_ref, ya_ref, yb_ref, pos_ref, x_ref, mod_ref, o_ref, acc_ref, *, ns):
    i = pl.program_id(0)
    s = pl.program_id(1)

    @pl.when(s == 0)
    def _():
        acc_ref[...] = jnp.zeros_like(acc_ref)

    lane = lax.broadcasted_iota(jnp.int32, pos_ref.shape, 1)
    lane_f = lane.astype(F32)

    def onehot(slot):
        pos = jnp.sum(jnp.where(lane == exp_ref[slot], pos_ref[...], 0.0), axis=-1, keepdims=True)
        return jnp.where(pos - (rb_ref[slot] * MOE_BLK).astype(F32) == lane_f, 1.0, 0.0).astype(BF16)

    s0 = i * ns + 2 * s
    pt = jnp.concatenate([onehot(s0), onehot(s0 + 1)], axis=1)
    acc_ref[...] += _dot(pt, jnp.concatenate([ya_ref[...], yb_ref[...]], axis=0))

    @pl.when(s == ns // 2 - 1)
    def _():
        o_ref[...] = x_ref[...] + mod_ref[ROW_GA_FFN:ROW_GA_FFN + 1, :] * acc_ref[...]


def _moe_scatter(yg, pos, x, mod, plan, seq):
    m, d = x.shape
    ns = plan["s_max"]
    tpb = seq // MOE_TM
    kern = functools.partial(_moe_scatter_kernel, ns=ns)
    return pl.pallas_call(
        kern,
        grid_spec=pltpu.PrefetchScalarGridSpec(
            num_scalar_prefetch=3,
            grid=(m // MOE_TM, ns // 2),
            in_specs=[
                pl.BlockSpec((MOE_BLK, d), lambda i, s, sr, ex, rb: (sr[i * ns + 2 * s], 0)),
                pl.BlockSpec((MOE_BLK, d), lambda i, s, sr, ex, rb: (sr[i * ns + 2 * s + 1], 0)),
                pl.BlockSpec((MOE_TM, LANES), lambda i, s, sr, ex, rb: (i, 0)),
                pl.BlockSpec((MOE_TM, d), lambda i, s, sr, ex, rb: (i, 0)),
                pl.BlockSpec((None, 6, d), lambda i, s, sr, ex, rb: (i // tpb, 0, 0)),
            ],
            out_specs=pl.BlockSpec((MOE_TM, d), lambda i, s, sr, ex, rb: (i, 0)),
            scratch_shapes=[pltpu.VMEM((MOE_TM, d), F32)],
        ),
        out_shape=jax.ShapeDtypeStruct((m, d), F32),
        compiler_params=_cparams(2),
        name="moe_scatter",
    )(plan["s_src"], plan["s_exp"], plan["s_rb"], yg, yg, pos, x, mod)


def _moe_layer(x, mod, g, w_router, b_router, w1, w3, w2, wl, seq, fc):
    m = x.shape[0]
    cmbt, pos, post, counts, h = _router(x, mod, g, w_router, b_router, seq)
    plan = _moe_plan(counts[:, 0, :w_router.shape[1]].astype(jnp.int32), m)
    xg, row_w = _moe_gather(h, post, cmbt, plan)
    yg = _moe_ffn(xg, row_w, w1, w3, w2, wl, plan, fc)
    return _moe_scatter(yg, pos, x, mod, plan, seq)


def _pad_w_in(w):
    gap = jnp.zeros(w.shape[:2] + (CB_MERGE * LANES - RAW_MERGE_COL,), w.dtype)
    q0, q1 = CB_SB_Q * LANES, CB_SB_K * LANES
    return jnp.concatenate([w[..., :q0], w[..., q0:q1] * SB_Q_SCALE, w[..., q1:RAW_MERGE_COL], gap,
                            w[..., RAW_MERGE_COL:]], axis=-1).astype(BF16)


def _kv_chunks(proj3, cb):
    b, s, _ = proj3.shape
    t = proj3[:, :, cb * LANES:(cb + NSA_KV_HEADS) * LANES].reshape(b, s, NSA_KV_HEADS, HEAD_DIM)
    return t.transpose(0, 2, 1, 3).reshape(b, NSA_KV_HEADS, s // CMP_STRIDE, CMP_STRIDE * HEAD_DIM)


def kernel(x, c, w_ada, b_ada, g_mix, g_ffn, w_in, conv_w, conv_b, conv_ln_g, conv_ln_b, w_conv_out, w_sb_out,
           nsa_cmp_pos_k, nsa_cmp_pos_v, nsa_cmp_wk, nsa_cmp_wv, nsa_q_g, nsa_kc_g, nsa_ks_g, nsa_kw_g, w_nsa_out, w_o,
           rel_bias, ffn_w1, ffn_w3, ffn_w2, moe_router, moe_router_b, moe_w1, moe_w3, moe_w2):
    b, s, d = x.shape
    m = b * s
    depth = w_ada.shape[0]
    mod_all = _ada_all(c, w_ada, b_ada)
    bias_c, bias_t = _bias_tables(rel_bias, s)
    w_in_p = _pad_w_in(w_in)
    w_conv_out, w_sb_out, w_nsa_out, w_o, ffn_w1, ffn_w3, ffn_w2, moe_w1, moe_w3, moe_w2 = (
        w.astype(BF16) for w in (w_conv_out, w_sb_out, w_nsa_out, w_o, ffn_w1, ffn_w3, ffn_w2, moe_w1, moe_w3, moe_w2))
    xf = x.reshape(m, d)
    for l in range(depth):
        mod = mod_all[l].reshape(b, 6, d)
        proj = _norm_matmul(xf, mod, g_mix[l], w_in_p, l, ROW_SH_MIX, ROW_SC_MIX, s)
        proj3 = proj.reshape(b, s, NP)
        u_act = _conv_module(proj, conv_w[l], conv_b[l], conv_ln_g[l], conv_ln_b[l], b, s)
        sb = _sb_attention(proj3)
        kcb, vcb, ksn, kwn = _nsa_prep(proj3, _kv_chunks(proj3, CB_KC), _kv_chunks(proj3, CB_VC), nsa_cmp_pos_k[l],
                                       nsa_cmp_pos_v[l], nsa_cmp_wk[l], nsa_cmp_wv[l], nsa_kc_g[l], nsa_ks_g[l],
                                       nsa_kw_g[l])
        nsa = _nsa_attention(proj3, kcb, vcb, ksn, kwn, bias_c, bias_t, nsa_q_g[l])
        merged = _merge(u_act, sb.reshape(m, -1), nsa.reshape(m, -1), proj, w_conv_out, w_sb_out, w_nsa_out, l)
        xf = _mm_res(merged, w_o, l, xf, mod, ROW_GA_MIX, s)
        i = l // 2
        if l % 2 == 0:
            act = _ffn_up(xf, mod, g_ffn[l], ffn_w1, ffn_w3, i, s)
            xf = _mm_res(act, ffn_w2, i, xf, mod, ROW_GA_FFN, s, tm=512, tn=512)
        else:
            xf = _moe_layer(xf, mod, g_ffn[l], moe_router[i], moe_router_b[i], moe_w1, moe_w3, moe_w2, i, s, MOE_FC)
    return xf.reshape(b, s, d)
```

```python
import functools
import math

import numpy as np
import jax
import jax.numpy as jnp
from jax import lax
from jax.experimental import pallas as pl
from jax.experimental.pallas import tpu as pltpu

F32 = jnp.float32
BF16 = jnp.bfloat16

D_MODEL = 2048
DEPTH = 4
D_CONV = 1024
CONV_WIDTH = 31
SB_HEADS = 8
HEAD_DIM = 128
NSA_HEADS = 8
NSA_KV_HEADS = 2
NSA_REP = NSA_HEADS // NSA_KV_HEADS
CMP_LEN = 32
CMP_STRIDE = 16
SLC_LEN = 64
SLC_TOP_N = 16
N_LOCAL_BLOCKS = 2
WINDOW = 512
FORCE_SCORE = 1e4
REL_BUCKETS = 32
REL_MAX_DIST = 128
D_FF = 5632
N_EXPERTS = 8
D_FF_EXPERT = 2816
EPS = 1e-6
NEG_INF = -1e30
TINY = 1e-20

LANES = 128
SUBLANES = 8
V7X_VMEM_LIMIT_BYTES = 56 * 1024 * 1024

NP_BLOCKS = 112
NP = NP_BLOCKS * LANES
CB_GLU_A, CB_GLU_G = 0, 8
CB_SB_Q, CB_SB_K, CB_SB_V = 16, 24, 32
CB_NQ = 40
CB_KC, CB_VC, CB_KS, CB_VS, CB_KW, CB_VW = 48, 50, 52, 54, 56, 58
CB_NGATE = 60
CB_MERGE = 64
RAW_GATE_COL = 7680
RAW_MERGE_COL = 7704
ROW_SH_MIX, ROW_SC_MIX, ROW_GA_MIX, ROW_SH_FFN, ROW_SC_FFN, ROW_GA_FFN = range(6)

SCALE = HEAD_DIM ** -0.5
SB_Q_SCALE = SCALE * math.log2(math.e)
TQ = 128
SB_TILE = 256
SB_GROUP = 8
SEL_CHUNK = 512
WIN_SPAN = WINDOW + TQ
BT_DIAG, BT_SUB, BT_FAR, BT_FAR_UPPER, BT_MASKED = range(5)
CONV_HALO = 32
MOE_TM = 1024
MOE_BLK = 128
MOE_MT = 512
MOE_FC = 1408
MOE_SUB = 256
MOE_NEVER = 1 << 20


def _cparams(n_axes):
    return pltpu.CompilerParams(
        dimension_semantics=("arbitrary",) * n_axes,
        vmem_limit_bytes=V7X_VMEM_LIMIT_BYTES,
    )


def _dot(a, b):
    return jnp.dot(a, b, preferred_element_type=F32)


def _dot_t(a, b):
    return lax.dot_general(a, b, (((1,), (1,)), ((), ())), preferred_element_type=F32)


def _split_dot(x, w01):
    hi = x.astype(BF16)
    lo = (x - hi.astype(F32)).astype(BF16)
    return _dot(hi, w01) + _dot(lo, w01)


def _norm_mod(x, g, sh, sc):
    ms = jnp.mean(x * x, axis=-1, keepdims=True)
    return (x * lax.rsqrt(ms + EPS) * g) * (1.0 + sc) + sh


def _rms(x, g):
    ms = jnp.mean(x * x, axis=-1, keepdims=True)
    return x * lax.rsqrt(ms + EPS) * g


def _ada_kernel(c_ref, w_ref, b_ref, o_ref):
    c = c_ref[...]
    ca = (c * jax.nn.sigmoid(c)).astype(BF16)
    o_ref[...] = _dot(ca, w_ref[...].astype(BF16)) + b_ref[...]


def _ada_all(c, w_ada, b_ada):
    depth, d, n = w_ada.shape
    nb = c.shape[0]
    b = 16
    c = jnp.zeros((b, d), c.dtype).at[:nb].set(c)
    tn = 1024
    out = pl.pallas_call(
        _ada_kernel,
        grid=(depth, n // tn),
        in_specs=[
            pl.BlockSpec((b, d), lambda l, j: (0, 0)),
            pl.BlockSpec((None, d, tn), lambda l, j: (l, 0, j)),
            pl.BlockSpec((None, 1, tn), lambda l, j: (l, 0, j)),
        ],
        out_specs=pl.BlockSpec((None, b, tn), lambda l, j: (l, 0, j)),
        out_shape=jax.ShapeDtypeStruct((depth, b, n), F32),
        compiler_params=_cparams(2),
        name="ada",
    )(c, w_ada, b_ada.reshape(depth, 1, n))
    return out[:, :nb]


def _bias_kernel(tab_ref, bk_ref, o_ref):
    h = pl.program_id(0)
    bk = bk_ref[...]
    acc = jnp.zeros(bk.shape, F32)
    for b in range(REL_BUCKETS):
        acc = jnp.where(bk == b, tab_ref[b, h], acc)
    o_ref[...] = acc


def _bias_expand(rel_bias, buckets):
    r = buckets.shape[0]
    tr = r
    return pl.pallas_call(
        _bias_kernel,
        grid=(NSA_HEADS, r // tr),
        in_specs=[
            pl.BlockSpec(memory_space=pltpu.SMEM),
            pl.BlockSpec((tr, LANES), lambda h, i: (i, 0)),
        ],
        out_specs=pl.BlockSpec((None, tr, LANES), lambda h, i: (h, i, 0)),
        out_shape=jax.ShapeDtypeStruct((NSA_HEADS, r, LANES), F32),
        compiler_params=_cparams(2),
        name="bias_expand",
    )(rel_bias, buckets)


def _rel_bucket(dist):
    n = jnp.maximum(dist, 0)
    max_exact = REL_BUCKETS // 2
    nf = jnp.maximum(n, 1).astype(F32)
    large = max_exact + (jnp.log(nf / max_exact) / math.log(REL_MAX_DIST / max_exact) * (REL_BUCKETS - max_exact)).astype(jnp.int32)
    large = jnp.minimum(large, REL_BUCKETS - 1)
    return jnp.where(n < max_exact, n, large)


def _bias_tables(rel_bias, s):
    t = jnp.arange(s, dtype=jnp.int32)[:, None]
    cend = jnp.arange(LANES, dtype=jnp.int32)[None, :] * CMP_STRIDE + (CMP_LEN - 1)
    q = jnp.arange(TQ, dtype=jnp.int32)[:, None]
    k = jnp.arange(TQ, dtype=jnp.int32)[None, :]
    bk_t = jnp.concatenate([_rel_bucket(q - k), _rel_bucket(TQ + q - k), _rel_bucket(2 * TQ + q - k)], axis=0)
    out = _bias_expand(rel_bias, jnp.concatenate([_rel_bucket(t - cend), bk_t], axis=0))
    bias_c = jnp.where(t >= cend, out[:, :s], NEG_INF).reshape(NSA_KV_HEADS, NSA_REP, s, LANES)
    diag, sub, far = out[:, s:s + TQ], out[:, s + TQ:s + 2 * TQ], out[:, s + 2 * TQ:]
    tiles = jnp.stack([jnp.where(q >= k, diag, NEG_INF), sub, far, jnp.where(k > q, far, NEG_INF),
                       jnp.full_like(far, NEG_INF)], axis=1)
    tiles = tiles.reshape(NSA_KV_HEADS, NSA_REP, 5, TQ, TQ).transpose(0, 2, 1, 3, 4)
    return bias_c, tiles.reshape(NSA_KV_HEADS, 5, NSA_REP * TQ, TQ)


def _norm_matmul_kernel(x_ref, mod_ref, g_ref, w_ref, o_ref, h_ref, *, row_sh, row_sc):
    @pl.when(pl.program_id(1) == 0)
    def _():
        h = _norm_mod(x_ref[...], g_ref[...], mod_ref[row_sh:row_sh + 1, :], mod_ref[row_sc:row_sc + 1, :])
        h_ref[...] = h.astype(BF16)

    o_ref[...] = _dot_t(h_ref[...], w_ref[...]).astype(o_ref.dtype)


def _norm_matmul(x, mod, g, wt, wl, row_sh, row_sc, seq, tm=1024, tn=1024):
    m, d = x.shape
    n = wt.shape[1]
    tpb = seq // tm
    kern = functools.partial(_norm_matmul_kernel, row_sh=row_sh, row_sc=row_sc)
    return pl.pallas_call(
        kern,
        grid=(m // tm, n // tn),
        in_specs=[
            pl.BlockSpec((tm, d), lambda i, j: (i, 0)),
            pl.BlockSpec((None, 6, d), lambda i, j: (i // tpb, 0, 0)),
            pl.BlockSpec((1, d), lambda i, j: (0, 0)),
            pl.BlockSpec((None, tn, d), lambda i, j: (wl, j, 0)),
        ],
        out_specs=pl.BlockSpec((tm, tn), lambda i, j: (i, j)),
        out_shape=jax.ShapeDtypeStruct((m, n), BF16),
        scratch_shapes=[pltpu.VMEM((tm, d), BF16)],
        compiler_params=_cparams(2),
        name="norm_matmul",
    )(x, mod, g.reshape(1, d), wt)


def _conv_kernel(a_ref, g_ref, ap_ref, gp_ref, cw_ref, cb_ref, lng_ref, lnb_ref, o_ref, ubuf, vbuf, shifted, *, ts):
    i = pl.program_id(1)
    ubuf[CONV_HALO:, :] = a_ref[...].astype(F32) * jax.nn.sigmoid(g_ref[...].astype(F32))
    up = ap_ref[...].astype(F32) * jax.nn.sigmoid(gp_ref[...].astype(F32))
    ubuf[:CONV_HALO, :] = jnp.where(i > 0, up, 0.0)

    first = CONV_HALO - (CONV_WIDTH - 1)
    span = ts + CONV_HALO - SUBLANES

    def chunk(c, carry):
        c0 = pl.multiple_of(c * LANES, LANES)
        for s in range(1, SUBLANES):
            shifted[s - 1] = ubuf[pl.ds(s, span), pl.ds(c0, LANES)]
        acc = jnp.zeros((ts, LANES), F32) + cb_ref[:, pl.ds(c0, LANES)]
        for k in range(CONV_WIDTH):
            a, s = divmod(first + k, SUBLANES)
            rows = ubuf[pl.ds(a * SUBLANES, ts), pl.ds(c0, LANES)] if s == 0 else shifted[s - 1, pl.ds(a * SUBLANES, ts), :]
            acc = acc + cw_ref[k:k + 1, pl.ds(c0, LANES)] * rows
        vbuf[:, pl.ds(c0, LANES)] = acc
        return carry

    lax.fori_loop(0, D_CONV // LANES, chunk, 0)
    v = vbuf[...]
    mu = jnp.mean(v, axis=-1, keepdims=True)
    vc = v - mu
    var = jnp.mean(vc * vc, axis=-1, keepdims=True)
    y = vc * lax.rsqrt(var + EPS) * lng_ref[...] + lnb_ref[...]
    o_ref[...] = (y * jax.nn.sigmoid(y)).astype(o_ref.dtype)


def _conv_module(proj, conv_w, conv_b, ln_g, ln_b, batch, seq, ts=256):
    m = proj.shape[0]
    nt = seq // ts
    hb = ts // CONV_HALO
    kern = functools.partial(_conv_kernel, ts=ts)

    def prev_idx(col):
        return lambda b, i: (jnp.maximum((b * nt + i) * hb - 1, 0), col)

    return pl.pallas_call(
        kern,
        grid=(batch, nt),
        in_specs=[
            pl.BlockSpec((ts, D_CONV), lambda b, i: (b * nt + i, 0)),
            pl.BlockSpec((ts, D_CONV), lambda b, i: (b * nt + i, 1)),
            pl.BlockSpec((CONV_HALO, D_CONV), prev_idx(0)),
            pl.BlockSpec((CONV_HALO, D_CONV), prev_idx(1)),
            pl.BlockSpec((CONV_WIDTH, D_CONV), lambda b, i: (0, 0)),
            pl.BlockSpec((1, D_CONV), lambda b, i: (0, 0)),
            pl.BlockSpec((1, D_CONV), lambda b, i: (0, 0)),
            pl.BlockSpec((1, D_CONV), lambda b, i: (0, 0)),
        ],
        out_specs=pl.BlockSpec((ts, D_CONV), lambda b, i: (b * nt + i, 0)),
        out_shape=jax.ShapeDtypeStruct((m, D_CONV), BF16),
        scratch_shapes=[pltpu.VMEM((CONV_HALO + ts, D_CONV), F32), pltpu.VMEM((ts, D_CONV), F32),
                        pltpu.VMEM((SUBLANES - 1, ts + CONV_HALO - SUBLANES, LANES), F32)],
        compiler_params=_cparams(2),
        name="conv_module",
    )(proj, proj, proj, proj, conv_w, conv_b.reshape(1, -1), ln_g.reshape(1, -1), ln_b.reshape(1, -1))


def _sb_kernel(q_ref, k_ref, v_ref, o_ref):
    i = pl.program_id(2)
    t = SB_TILE
    rows = SB_GROUP * t
    row = lax.broadcasted_iota(jnp.int32, (t, t), 0)
    col = lax.broadcasted_iota(jnp.int32, (t, t), 1)
    later = jnp.where(row > col, 1.0, 0.0).astype(BF16)
    heads = [slice(h * HEAD_DIM, (h + 1) * HEAD_DIM) for h in range(SB_GROUP)]

    def tile(j, rsum, acc, diag):
        k0 = pl.multiple_of(j * t, t)
        z = jnp.concatenate([_dot_t(q_ref[:, hs], k_ref[pl.ds(k0, t), hs]) for hs in heads], axis=0)
        if diag:
            qpos = lax.broadcasted_iota(jnp.int32, (rows, t), 0) & (t - 1)
            z = jnp.where(lax.broadcasted_iota(jnp.int32, (rows, t), 1) < qpos, z, NEG_INF)
        sp = jnp.log2(1.0 + jnp.exp2(-jnp.abs(z)))
        log_beta = jnp.minimum(z, 0.0) - sp
        log_keep = log_beta - z
        log_survive = _dot(log_keep.astype(BF16), later) + rsum
        a = jnp.exp2(log_beta + log_survive).astype(BF16)
        pv = [_dot(a[h * t:(h + 1) * t], v_ref[pl.ds(k0, t), hs]) for h, hs in enumerate(heads)]
        acc = acc + jnp.concatenate(pv, axis=0)
        rsum = rsum + jnp.sum(log_keep, axis=-1, keepdims=True)
        return rsum, acc

    carry = tile(i, jnp.zeros((rows, 1), F32), jnp.zeros((rows, HEAD_DIM), F32), True)
    _, acc = lax.fori_loop(0, i, lambda jj, c: tile(i - 1 - jj, c[0], c[1], False), carry)
    for h in range(SB_GROUP):
        o_ref[:, h * HEAD_DIM:(h + 1) * HEAD_DIM] = acc[h * t:(h + 1) * t].astype(o_ref.dtype)


def _sb_attention(proj3):
    b, s, _ = proj3.shape
    gw = SB_GROUP * HEAD_DIM
    return pl.pallas_call(
        _sb_kernel,
        grid=(b, SB_HEADS // SB_GROUP, s // SB_TILE),
        in_specs=[
            pl.BlockSpec((None, SB_TILE, gw), lambda bi, hg, i: (bi, i, CB_SB_Q // SB_GROUP + hg)),
            pl.BlockSpec((None, s, gw), lambda bi, hg, i: (bi, 0, CB_SB_K // SB_GROUP + hg)),
            pl.BlockSpec((None, s, gw), lambda bi, hg, i: (bi, 0, CB_SB_V // SB_GROUP + hg)),
        ],
        out_specs=pl.BlockSpec((None, SB_TILE, gw), lambda bi, hg, i: (bi, i, hg)),
        out_shape=jax.ShapeDtypeStruct((b, s, SB_HEADS * HEAD_DIM), BF16),
        compiler_params=_cparams(3),
        name="sb_attention",
    )(proj3, proj3, proj3)


def _nsa_prep_kernel(kc_ref, vc_ref, ks_ref, kw_ref, pk_ref, pv_ref, wk_ref, wv_ref, kcg_ref, ksg_ref, kwg_ref,
                     kcb_ref, vcb_ref, ksn_ref, kwn_ref):
    half = CMP_STRIDE * HEAD_DIM

    def compress(a_ref, p_ref, w_ref):
        a = a_ref[...].astype(F32)
        top = _dot((a + p_ref[0:1, :]).astype(BF16), w_ref[0:half, :])
        bot = _dot((a + p_ref[1:2, :]).astype(BF16), w_ref[half:2 * half, :])
        return top + pltpu.roll(bot, bot.shape[0] - 1, axis=0)

    kcb_ref[...] = _rms(compress(kc_ref, pk_ref, wk_ref), kcg_ref[...]).astype(BF16)
    vcb_ref[...] = compress(vc_ref, pv_ref, wv_ref).astype(BF16)
    kwn_ref[...] = _rms(kw_ref[...].astype(F32), kwg_ref[...]).astype(BF16)
    ksn_ref[:, 0:HEAD_DIM] = _rms(ks_ref[...].astype(F32), ksg_ref[...]).astype(BF16)
    s = ks_ref.shape[0]
    blk = jnp.right_shift(lax.broadcasted_iota(jnp.int32, (s, LANES), 0), int(math.log2(SLC_LEN)))
    onehot = blk == lax.broadcasted_iota(jnp.int32, (s, LANES), 1)
    ksn_ref[:, HEAD_DIM:HEAD_DIM + LANES] = jnp.where(onehot, 1.0, 0.0).astype(BF16)


def _nsa_prep(proj3, kc_chunks, vc_chunks, pos_k, pos_v, wk, wv, kc_g, ks_g, kw_g):
    b, s, _ = proj3.shape
    g_n = NSA_KV_HEADS
    nch = s // CMP_STRIDE
    half = CMP_STRIDE * HEAD_DIM
    vec = lambda: pl.BlockSpec((1, HEAD_DIM), lambda bi, g: (0, 0))
    small = jax.ShapeDtypeStruct((b, g_n, nch, HEAD_DIM), BF16)
    full = jax.ShapeDtypeStruct((b, g_n, s, HEAD_DIM), BF16)
    aug = jax.ShapeDtypeStruct((b, g_n, s, HEAD_DIM + LANES), BF16)
    return pl.pallas_call(
        _nsa_prep_kernel,
        grid=(b, g_n),
        in_specs=[
            pl.BlockSpec((None, None, nch, half), lambda bi, g: (bi, g, 0, 0)),
            pl.BlockSpec((None, None, nch, half), lambda bi, g: (bi, g, 0, 0)),
            pl.BlockSpec((None, s, HEAD_DIM), lambda bi, g: (bi, 0, CB_KS + g)),
            pl.BlockSpec((None, s, HEAD_DIM), lambda bi, g: (bi, 0, CB_KW + g)),
            pl.BlockSpec((2, half), lambda bi, g: (0, 0)),
            pl.BlockSpec((2, half), lambda bi, g: (0, 0)),
            pl.BlockSpec((2 * half, HEAD_DIM), lambda bi, g: (0, 0)),
            pl.BlockSpec((2 * half, HEAD_DIM), lambda bi, g: (0, 0)),
            vec(), vec(), vec(),
        ],
        out_specs=[
            pl.BlockSpec((None, None, nch, HEAD_DIM), lambda bi, g: (bi, g, 0, 0)),
            pl.BlockSpec((None, None, nch, HEAD_DIM), lambda bi, g: (bi, g, 0, 0)),
            pl.BlockSpec((None, None, s, HEAD_DIM + LANES), lambda bi, g: (bi, g, 0, 0)),
            pl.BlockSpec((None, None, s, HEAD_DIM), lambda bi, g: (bi, g, 0, 0)),
        ],
        out_shape=[small, small, aug, full],
        compiler_params=_cparams(2),
        name="nsa_prep",
    )(kc_chunks, vc_chunks, proj3, proj3, pos_k.reshape(2, half), pos_v.reshape(2, half),
      wk.astype(BF16), wv.astype(BF16), kc_g.reshape(1, -1), ks_g.reshape(1, -1), kw_g.reshape(1, -1))


def _nsa_kernel(q_ref, kcb_ref, vcb_ref, ksa_ref, vs_ref, kwn_ref, vw_ref, gate_ref, bc_ref, bt_ref, qg_ref, ovt_ref,
                o_ref):
    g = pl.program_id(1)
    i = pl.program_id(2)
    r_n = NSA_REP
    rows = r_n * TQ

    qs = []
    for r in range(r_n):
        qr = q_ref[:, r * HEAD_DIM:(r + 1) * HEAD_DIM].astype(F32)
        qs.append(_rms(qr, qg_ref[...]).astype(BF16))
    q = jnp.concatenate(qs, axis=0)

    lane = lax.broadcasted_iota(jnp.int32, (TQ, LANES), 1)

    logit_c = _dot_t(q, kcb_ref[...]) + bc_ref[...].reshape(rows, LANES)
    m_c = jnp.max(logit_c, axis=-1, keepdims=True)
    p_c = jnp.where(logit_c > 0.5 * NEG_INF, jnp.exp(logit_c - m_c), 0.0)
    p_c = p_c / jnp.maximum(jnp.sum(p_c, axis=-1, keepdims=True), TINY)
    o_c = _dot(p_c.astype(BF16), vcb_ref[...])

    p_sum = p_c[0:TQ]
    for r in range(1, r_n):
        p_sum = p_sum + p_c[r * TQ:(r + 1) * TQ]
    hi = p_sum.astype(BF16)
    lo = (p_sum - hi.astype(F32)).astype(BF16)
    n_sel = ovt_ref.shape[0]
    imp = _dot_t(ovt_ref[...], hi) + _dot_t(ovt_ref[...], lo)
    blk = lax.broadcasted_iota(jnp.int32, (n_sel, TQ), 0)
    tb = jnp.right_shift(i * TQ + lax.broadcasted_iota(jnp.int32, (n_sel, TQ), 1), int(math.log2(SLC_LEN)))
    causal_blk = blk <= tb
    forced = (blk == 0) | (causal_blk & (blk > tb - N_LOCAL_BLOCKS))
    score = jnp.where(forced, FORCE_SCORE, jnp.where(causal_blk, imp, -FORCE_SCORE))
    rank = jnp.zeros((n_sel, TQ), F32)
    for b in range(n_sel):
        sb = score[b:b + 1, :]
        beats = (sb > score) | ((sb == score) & (blk > b))
        rank = rank + jnp.where(beats, 1.0, 0.0)
    chosen = (rank < float(min(SLC_TOP_N, n_sel))) & causal_blk
    sel_neg = jnp.where(chosen, 0.0, NEG_INF)
    sel_neg = jnp.concatenate([sel_neg, jnp.zeros((LANES - n_sel, TQ), F32)], axis=0).T.astype(BF16)
    q_aug = jnp.concatenate([q, jnp.concatenate([sel_neg] * r_n, axis=0)], axis=1)

    def biased(s, first_tile, idx_fn):
        n_t = s.shape[1] // TQ
        return jnp.concatenate(
            [s[:, c * TQ:(c + 1) * TQ] + bt_ref[idx_fn(i - (first_tile + c))] for c in range(n_t)], axis=1)

    w_tile = jnp.maximum(i - WINDOW // TQ, 0)
    w0 = pl.multiple_of(w_tile * TQ, TQ)

    def win_idx(off):
        far = jnp.where(off == WINDOW // TQ, BT_FAR_UPPER, jnp.minimum(off, BT_FAR))
        return jnp.where(off < 0, BT_MASKED, far)

    s_w = biased(_dot_t(q, kwn_ref[pl.ds(w0, WIN_SPAN), :]), w_tile, win_idx)
    p_w = jnp.exp(s_w - jnp.max(s_w, axis=-1, keepdims=True))
    o_w = _dot(p_w.astype(BF16), vw_ref[pl.ds(w0, WIN_SPAN), :]) / jnp.sum(p_w, axis=-1, keepdims=True)

    def sel_chunk(kc, carry, diag):
        c0 = pl.multiple_of(kc * SEL_CHUNK, SEL_CHUNK)
        idx_fn = (lambda off: jnp.where(off < 0, BT_MASKED, jnp.minimum(off, BT_FAR))) if diag else (
            lambda off: jnp.minimum(off, BT_FAR))
        s = biased(_dot_t(q_aug, ksa_ref[pl.ds(c0, SEL_CHUNK), :]), kc * (SEL_CHUNK // TQ), idx_fn)
        m_blk = jnp.max(s, axis=-1, keepdims=True)
        if diag:
            p = jnp.exp(s - m_blk)
            return m_blk, jnp.sum(p, axis=-1, keepdims=True), _dot(p.astype(BF16), vs_ref[pl.ds(c0, SEL_CHUNK), :])
        m, l, acc = carry
        m_new = jnp.maximum(m, m_blk)
        p = jnp.exp(s - m_new)
        alpha = jnp.exp(m - m_new)
        l = alpha * l + jnp.sum(p, axis=-1, keepdims=True)
        acc = alpha * acc + _dot(p.astype(BF16), vs_ref[pl.ds(c0, SEL_CHUNK), :])
        return m_new, l, acc

    kc_diag = lax.shift_right_logical(i, jnp.int32(int(math.log2(SEL_CHUNK // TQ))))
    carry = sel_chunk(kc_diag, None, True)
    _, l_s, acc_s = lax.fori_loop(0, kc_diag, lambda jj, c: sel_chunk(kc_diag - 1 - jj, c, False), carry)
    o_s = acc_s / l_s

    gates = jax.nn.sigmoid(gate_ref[...].astype(F32))

    def gate_col(idx):
        return jnp.sum(jnp.where(lane == idx, gates, 0.0), axis=-1, keepdims=True)

    for r in range(r_n):
        h = g * r_n + r
        sl = slice(r * TQ, (r + 1) * TQ)
        o = gate_col(h) * o_c[sl] + gate_col(NSA_HEADS + h) * o_s[sl] + gate_col(2 * NSA_HEADS + h) * o_w[sl]
        o_ref[:, r * HEAD_DIM:(r + 1) * HEAD_DIM] = o.astype(o_ref.dtype)


def _overlap_matrix(s):
    nch = s // CMP_STRIDE
    nsel = s // SLC_LEN
    ci = np.arange(nch)[:, None]
    sj = np.arange(nsel)[None, :]
    ov = (ci * CMP_STRIDE <= sj * SLC_LEN + SLC_LEN - 1) & (ci * CMP_STRIDE + CMP_LEN - 1 >= sj * SLC_LEN)
    ov = ov & (ci < nch - 1)
    return jnp.asarray(ov.T, BF16)


def _nsa_attention(proj3, kcb, vcb, ksa, kwn, bias_c, bias_t, q_g):
    b, s, _ = proj3.shape
    g_n, r_n = NSA_KV_HEADS, NSA_REP
    nch = s // CMP_STRIDE
    nsel = s // SLC_LEN
    gw = r_n * HEAD_DIM
    return pl.pallas_call(
        _nsa_kernel,
        grid=(b, g_n, s // TQ),
        in_specs=[
            pl.BlockSpec((None, TQ, gw), lambda bi, g, i: (bi, i, CB_NQ // r_n + g)),
            pl.BlockSpec((None, None, nch, HEAD_DIM), lambda bi, g, i: (bi, g, 0, 0)),
            pl.BlockSpec((None, None, nch, HEAD_DIM), lambda bi, g, i: (bi, g, 0, 0)),
            pl.BlockSpec((None, None, s, HEAD_DIM + LANES), lambda bi, g, i: (bi, g, 0, 0)),
            pl.BlockSpec((None, s, HEAD_DIM), lambda bi, g, i: (bi, 0, CB_VS + g)),
            pl.BlockSpec((None, None, s, HEAD_DIM), lambda bi, g, i: (bi, g, 0, 0)),
            pl.BlockSpec((None, s, HEAD_DIM), lambda bi, g, i: (bi, 0, CB_VW + g)),
            pl.BlockSpec((None, TQ, LANES), lambda bi, g, i: (bi, i, CB_NGATE)),
            pl.BlockSpec((None, r_n, TQ, LANES), lambda bi, g, i: (g, 0, i, 0)),
            pl.BlockSpec((None, 5, r_n * TQ, TQ), lambda bi, g, i: (g, 0, 0, 0)),
            pl.BlockSpec((1, HEAD_DIM), lambda bi, g, i: (0, 0)),
            pl.BlockSpec((nsel, nch), lambda bi, g, i: (0, 0)),
        ],
        out_specs=pl.BlockSpec((None, TQ, gw), lambda bi, g, i: (bi, i, g)),
        out_shape=jax.ShapeDtypeStruct((b, s, NSA_HEADS * HEAD_DIM), BF16),
        compiler_params=_cparams(3),
        name="nsa_attention",
    )(proj3, kcb, vcb, ksa, proj3, kwn, proj3, proj3, bias_c, bias_t, q_g.reshape(1, -1) * SCALE, _overlap_matrix(s))


def _merge_kernel(u_ref, sb_ref, ns_ref, gc_ref, gs_ref, gn_ref, wc_ref, ws_ref, wn_ref, o_ref):
    m = jax.nn.sigmoid(gc_ref[...].astype(F32)) * _dot(u_ref[...], wc_ref[...])
    m = m + jax.nn.sigmoid(gs_ref[...].astype(F32)) * _dot(sb_ref[...], ws_ref[...])
    m = m + jax.nn.sigmoid(gn_ref[...].astype(F32)) * _dot(ns_ref[...], wn_ref[...])
    o_ref[...] = m.astype(o_ref.dtype)


def _merge(u_act, sb, nsa, proj, wc, ws, wn, wl, tm=512, tn=1024):
    m = u_act.shape[0]
    d = wc.shape[2]
    kc = wc.shape[1]
    gate0 = CB_MERGE * LANES // tn
    gstep = d // tn
    act = lambda: pl.BlockSpec((tm, kc), lambda i, j: (i, 0))
    gate = lambda n: pl.BlockSpec((tm, tn), lambda i, j: (i, gate0 + n * gstep + j))
    wsp = lambda: pl.BlockSpec((None, kc, tn), lambda i, j: (wl, 0, j))
    return pl.pallas_call(
        _merge_kernel,
        grid=(m // tm, d // tn),
        in_specs=[act(), act(), act(), gate(0), gate(1), gate(2), wsp(), wsp(), wsp()],
        out_specs=pl.BlockSpec((tm, tn), lambda i, j: (i, j)),
        out_shape=jax.ShapeDtypeStruct((m, d), BF16),
        compiler_params=_cparams(2),
        name="merge",
    )(u_act, sb, nsa, proj, proj, proj, wc, ws, wn)


def _mm_res_kernel(a_ref, w_ref, x_ref, mod_ref, o_ref, acc_ref, *, row_ga, nk):
    k = pl.program_id(2)

    @pl.when(k == 0)
    def _():
        acc_ref[...] = jnp.zeros_like(acc_ref)

    acc_ref[...] += _dot(a_ref[...], w_ref[...])

    @pl.when(k == nk - 1)
    def _():
        o_ref[...] = x_ref[...] + mod_ref[row_ga:row_ga + 1, :] * acc_ref[...]


def _mm_res(a, w, wl, x, mod, row_ga, seq, tm=1024, tn=1024, tk=None):
    m, ka = a.shape
    n = x.shape[1]
    tpb = seq // tm
    tk = tk or ka
    nk = ka // tk
    w_spec = pl.BlockSpec((None, tk, tn), lambda i, j, k: (wl, k, j))
    kern = functools.partial(_mm_res_kernel, row_ga=row_ga, nk=nk)
    return pl.pallas_call(
        kern,
        grid=(m // tm, n // tn, nk),
        in_specs=[
            pl.BlockSpec((tm, tk), lambda i, j, k: (i, k)),
            w_spec,
            pl.BlockSpec((tm, tn), lambda i, j, k: (i, j)),
            pl.BlockSpec((None, 6, tn), lambda i, j, k: (i // tpb, 0, j)),
        ],
        out_specs=pl.BlockSpec((tm, tn), lambda i, j, k: (i, j)),
        out_shape=jax.ShapeDtypeStruct((m, n), F32),
        scratch_shapes=[pltpu.VMEM((tm, tn), F32)],
        compiler_params=_cparams(3),
        name="matmul_residual",
    )(a, w, x, mod)


def _ffn_up_kernel(x_ref, mod_ref, g_ref, w1_ref, w3_ref, o_ref, h_ref):
    @pl.when(pl.program_id(1) == 0)
    def _():
        h = _norm_mod(x_ref[...], g_ref[...], mod_ref[ROW_SH_FFN:ROW_SH_FFN + 1, :], mod_ref[ROW_SC_FFN:ROW_SC_FFN + 1, :])
        h_ref[...] = h.astype(BF16)

    h = h_ref[...]
    a = _dot(h, w1_ref[...])
    o_ref[...] = (a * jax.nn.sigmoid(a) * _dot(h, w3_ref[...])).astype(o_ref.dtype)


def _ffn_up(x, mod, g, w1, w3, wl, seq, tm=1024, tn=512):
    m, d = x.shape
    f = w1.shape[2]
    tpb = seq // tm
    return pl.pallas_call(
        _ffn_up_kernel,
        grid=(m // tm, f // tn),
        in_specs=[
            pl.BlockSpec((tm, d), lambda i, j: (i, 0)),
            pl.BlockSpec((None, 6, d), lambda i, j: (i // tpb, 0, 0)),
            pl.BlockSpec((1, d), lambda i, j: (0, 0)),
            pl.BlockSpec((None, d, tn), lambda i, j: (wl, 0, j)),
            pl.BlockSpec((None, d, tn), lambda i, j: (wl, 0, j)),
        ],
        out_specs=pl.BlockSpec((tm, tn), lambda i, j: (i, j)),
        out_shape=jax.ShapeDtypeStruct((m, f), BF16),
        scratch_shapes=[pltpu.VMEM((tm, d), BF16)],
        compiler_params=_cparams(2),
        name="ffn_up",
    )(x, mod, g.reshape(1, d), w1, w3)


def _router_kernel(x_ref, mod_ref, g_ref, wr_ref, br_ref, cmbt_ref, pos_ref, post_ref, cnt_ref, h_ref):
    h = _norm_mod(x_ref[...], g_ref[...], mod_ref[ROW_SH_FFN:ROW_SH_FFN + 1, :], mod_ref[ROW_SC_FFN:ROW_SC_FFN + 1, :])
    h_ref[...] = h.astype(BF16)
    logits = jnp.dot(h, wr_ref[...], precision=lax.Precision.HIGHEST, preferred_element_type=F32) + br_ref[...]
    lane = lax.broadcasted_iota(jnp.int32, logits.shape, 1).astype(F32)
    pad = float(LANES)
    m1 = jnp.max(logits, axis=-1, keepdims=True)
    i1 = jnp.min(jnp.where(logits == m1, lane, pad), axis=-1, keepdims=True)
    rest = jnp.where(lane == i1, NEG_INF, logits)
    m2 = jnp.max(rest, axis=-1, keepdims=True)
    i2 = jnp.min(jnp.where(rest == m2, lane, pad), axis=-1, keepdims=True)
    e2 = jnp.exp(m2 - m1)
    w1 = 1.0 / (1.0 + e2)
    w2 = e2 / (1.0 + e2)
    cmb = jnp.where(lane == i1, w1, 0.0) + jnp.where(lane == i2, w2, 0.0)
    cmbt_ref[...] = cmb.T[0:cmbt_ref.shape[0], :]
    sel = jnp.where((lane == i1) | (lane == i2), 1.0, 0.0)
    tm = sel.shape[0]
    earlier = lax.broadcasted_iota(jnp.int32, (tm, tm), 1) < lax.broadcasted_iota(jnp.int32, (tm, tm), 0)
    pos = _dot(jnp.where(earlier, 1.0, 0.0).astype(BF16), sel.astype(BF16))
    pos = jnp.where(sel > 0.0, pos, -1.0)
    pos_ref[...] = pos
    post_ref[...] = pos.T[0:post_ref.shape[0], :]
    cnt_ref[...] = jnp.broadcast_to(jnp.sum(sel, axis=0, keepdims=True), cnt_ref.shape)


def _router(x, mod, g, w_router, b_router, seq):
    m, d = x.shape
    tm = MOE_TM
    n_e = w_router.shape[1]
    tpb = seq // tm
    nt = m // tm
    wr = jnp.zeros((d, LANES), F32).at[:, :n_e].set(w_router)
    br = jnp.full((1, LANES), 2.0 * NEG_INF, F32).at[0, :n_e].set(b_router)
    return pl.pallas_call(
        _router_kernel,
        grid=(nt,),
        in_specs=[
            pl.BlockSpec((tm, d), lambda i: (i, 0)),
            pl.BlockSpec((None, 6, d), lambda i: (i // tpb, 0, 0)),
            pl.BlockSpec((1, d), lambda i: (0, 0)),
            pl.BlockSpec((d, LANES), lambda i: (0, 0)),
            pl.BlockSpec((1, LANES), lambda i: (0, 0)),
        ],
        out_specs=[
            pl.BlockSpec((n_e, tm), lambda i: (0, i)),
            pl.BlockSpec((tm, LANES), lambda i: (i, 0)),
            pl.BlockSpec((n_e, tm), lambda i: (0, i)),
            pl.BlockSpec((None, 8, LANES), lambda i: (i, 0, 0)),
            pl.BlockSpec((tm, d), lambda i: (i, 0)),
        ],
        out_shape=[
            jax.ShapeDtypeStruct((n_e, m), F32),
            jax.ShapeDtypeStruct((m, LANES), F32),
            jax.ShapeDtypeStruct((n_e, m), F32),
            jax.ShapeDtypeStruct((nt, 8, LANES), F32),
            jax.ShapeDtypeStruct((m, d), BF16),
        ],
        compiler_params=_cparams(1),
        name="router",
    )(x, mod, g.reshape(1, d), wr, br)


def _moe_plan(counts, m):
    t_n, e_n = counts.shape
    per_mt = MOE_MT // MOE_BLK
    g_max = 2 * m // MOE_BLK + t_n * e_n
    pad_max = (per_mt - 1) * e_n
    nblk = (counts + MOE_BLK - 1) // MOE_BLK
    nb_e = jnp.sum(nblk, axis=0)
    cap_e = (nb_e + per_mt - 1) // per_mt * per_mt
    start_e = jnp.cumsum(cap_e) - cap_e
    dst0 = (start_e[None, :] + jnp.cumsum(nblk, axis=0) - nblk).reshape(-1)
    flat = nblk.reshape(-1)
    cum = jnp.cumsum(flat)
    g = jnp.minimum(jnp.arange(g_max, dtype=jnp.int32), cum[-1] - 1)
    p = jnp.minimum(jnp.searchsorted(cum, g, side="right"), t_n * e_n - 1).astype(jnp.int32)
    rb = g - (cum[p] - flat[p])
    n_blocks = g_max + pad_max
    z_max = n_blocks - 2 * m // MOE_BLK
    used = jnp.zeros((n_blocks,), jnp.int32).at[dst0[p] + rb].set(1)
    free = jnp.argsort(used, stable=True).astype(jnp.int32)
    z = jnp.minimum(jnp.arange(z_max, dtype=jnp.int32), n_blocks - cum[-1] - 1)
    never = jnp.full((z_max,), MOE_NEVER, jnp.int32)
    g_tile = jnp.concatenate([p // e_n, jnp.full((z_max,), t_n - 1, jnp.int32)])
    g_exp = jnp.concatenate([p % e_n, jnp.zeros((z_max,), jnp.int32)])
    g_rb = jnp.concatenate([rb, never])
    g_dst = jnp.concatenate([dst0[p] + rb, free[z]])
    mt_max = (g_max + pad_max) // per_mt
    n_mt = (jnp.sum(cap_e) // per_mt).astype(jnp.int32)
    mt = jnp.minimum(jnp.arange(mt_max, dtype=jnp.int32), n_mt - 1)
    mt_exp = jnp.minimum(jnp.searchsorted(jnp.cumsum(cap_e) // per_mt, mt, side="right"), e_n - 1).astype(jnp.int32)
    s_max = 2 * MOE_TM // MOE_BLK + e_n
    cum_t = jnp.cumsum(nblk, axis=1)
    n_t = cum_t[:, -1:]
    s_all = jnp.arange(s_max, dtype=jnp.int32)[None, :]
    s = jnp.minimum(s_all, n_t - 1)
    s_exp = jnp.minimum(jax.vmap(lambda c, q: jnp.searchsorted(c, q, side="right"))(cum_t, s), e_n - 1).astype(jnp.int32)
    s_rb = s - (jnp.take_along_axis(cum_t, s_exp, axis=1) - jnp.take_along_axis(nblk, s_exp, axis=1))
    s_src = jnp.take_along_axis(dst0.reshape(t_n, e_n), s_exp, axis=1) + s_rb
    s_rb = jnp.where(s_all < n_t, s_rb, MOE_NEVER)
    i32 = lambda a: a.astype(jnp.int32)
    return dict(g_tile=i32(g_tile), g_exp=i32(g_exp), g_rb=i32(g_rb), g_dst=i32(g_dst), n_mt=i32(n_mt.reshape(1)),
                mt_exp=i32(mt_exp), s_exp=i32(s_exp.reshape(-1)), s_rb=i32(s_rb.reshape(-1)),
                s_src=i32(s_src.reshape(-1)), n_blocks=n_blocks, s_max=s_max)


def _moe_gather_kernel(tile_ref, exp_ref, rb_ref, dst_ref, h_ref, post_ref, cmbt_ref, o_ref, w_ref):
    g = pl.program_id(0)
    fill = rb_ref[g] >= MOE_NEVER

    @pl.when(fill)
    def _():
        o_ref[...] = jnp.zeros_like(o_ref)
        w_ref[...] = jnp.zeros_like(w_ref)

    @pl.when(jnp.logical_not(fill))
    def _():
        row = post_ref[pl.ds(exp_ref[g], 1), :]
        tm = row.shape[1]
        want = (lax.broadcasted_iota(jnp.int32, (MOE_BLK, tm), 0) + rb_ref[g] * MOE_BLK).astype(F32)
        match = row == want
        o_ref[...] = _dot(jnp.where(match, 1.0, 0.0).astype(BF16), h_ref[...]).astype(o_ref.dtype)
        cw = jnp.sum(jnp.where(match, cmbt_ref[pl.ds(exp_ref[g], 1), :], 0.0), axis=-1, keepdims=True)
        w_ref[...] = jnp.broadcast_to(cw, w_ref.shape)


def _moe_gather(h, post, cmbt, plan):
    m, d = h.shape
    n_e = post.shape[0]
    n = plan["g_tile"].shape[0]
    rows = plan["n_blocks"] * MOE_BLK
    return pl.pallas_call(
        _moe_gather_kernel,
        grid_spec=pltpu.PrefetchScalarGridSpec(
            num_scalar_prefetch=4,
            grid=(n,),
            in_specs=[
                pl.BlockSpec((MOE_TM, d), lambda g, t, e, r, ds: (t[g], 0)),
                pl.BlockSpec((n_e, MOE_TM), lambda g, t, e, r, ds: (0, t[g])),
                pl.BlockSpec((n_e, MOE_TM), lambda g, t, e, r, ds: (0, t[g])),
            ],
            out_specs=[
                pl.BlockSpec((MOE_BLK, d), lambda g, t, e, r, ds: (ds[g], 0)),
                pl.BlockSpec((MOE_BLK, LANES), lambda g, t, e, r, ds: (ds[g], 0)),
            ],
        ),
        out_shape=[jax.ShapeDtypeStruct((rows, d), BF16), jax.ShapeDtypeStruct((rows, LANES), F32)],
        compiler_params=_cparams(1),
        name="moe_gather",
    )(plan["g_tile"], plan["g_exp"], plan["g_rb"], plan["g_dst"], h, post, cmbt)


def _moe_ffn_kernel(exp_ref, n_ref, x_ref, rw_ref, w1_ref, w3_ref, w2_ref, o_ref, acc_ref, *, nc):
    c = pl.program_id(1)

    @pl.when(pl.program_id(0) < n_ref[0])
    def _():
        @pl.when(c == 0)
        def _():
            acc_ref[...] = jnp.zeros_like(acc_ref)

        x = x_ref[...]
        fc = w1_ref.shape[1]
        for k0 in range(0, fc, MOE_SUB):
            k1 = min(k0 + MOE_SUB, fc)
            a = _dot(x, w1_ref[:, k0:k1])
            a = (a * jax.nn.sigmoid(a) * _dot(x, w3_ref[:, k0:k1])).astype(BF16)
            acc_ref[...] += _dot(a, w2_ref[k0:k1, :])

        @pl.when(c == nc - 1)
        def _():
            o_ref[...] = (rw_ref[:, 0:1] * acc_ref[...]).astype(o_ref.dtype)

    @pl.when(pl.program_id(0) >= n_ref[0])
    def _():
        o_ref[...] = jnp.zeros_like(o_ref)


def _moe_ffn(xg, row_w, w1, w3, w2, wl, plan, fc):
    d = xg.shape[1]
    fe = w1.shape[3]
    nc = fe // fc
    n_mt = plan["mt_exp"].shape[0]

    def tile(mt, n):
        return jnp.minimum(mt, n[0] - 1)

    def chunk(mt, c, n):
        return jnp.where(mt < n[0], c, nc - 1)

    kern = functools.partial(_moe_ffn_kernel, nc=nc)
    return pl.pallas_call(
        kern,
        grid_spec=pltpu.PrefetchScalarGridSpec(
            num_scalar_prefetch=2,
            grid=(n_mt, nc),
            in_specs=[
                pl.BlockSpec((MOE_MT, d), lambda mt, c, e, n: (tile(mt, n), 0)),
                pl.BlockSpec((MOE_MT, LANES), lambda mt, c, e, n: (tile(mt, n), 0)),
                pl.BlockSpec((None, None, d, fc), lambda mt, c, e, n: (wl, e[mt], 0, chunk(mt, c, n))),
                pl.BlockSpec((None, None, d, fc), lambda mt, c, e, n: (wl, e[mt], 0, chunk(mt, c, n))),
                pl.BlockSpec((None, None, fc, d), lambda mt, c, e, n: (wl, e[mt], chunk(mt, c, n), 0)),
            ],
            out_specs=pl.BlockSpec((MOE_MT, d), lambda mt, c, e, n: (mt, 0)),
            scratch_shapes=[pltpu.VMEM((MOE_MT, d), F32)],
        ),
        out_shape=jax.ShapeDtypeStruct((n_mt * MOE_MT, d), BF16),
        compiler_params=_cparams(2),
        name="moe_ffn",
    )(plan["mt_exp"], plan["n_mt"], xg, row_w, w1, w3, w2)


def _moe_scatter_kernel(src_ref, exp_ref, rb_ref, ya_ref, yb_ref, pos_ref, x_ref, mod_ref, o_ref, acc_ref, *, ns):
    i = pl.program_id(0)
    s = pl.program_id(1)

    @pl.when(s == 0)
    def _():
        acc_ref[...] = jnp.zeros_like(acc_ref)

    lane = lax.broadcasted_iota(jnp.int32, pos_ref.shape, 1)
    lane_f = lane.astype(F32)

    def onehot(slot):
        pos = jnp.sum(jnp.where(lane == exp_ref[slot], pos_ref[...], 0.0), axis=-1, keepdims=True)
        return jnp.where(pos - (rb_ref[slot] * MOE_BLK).astype(F32) == lane_f, 1.0, 0.0).astype(BF16)

    s0 = i * ns + 2 * s
    pt = jnp.concatenate([onehot(s0), onehot(s0 + 1)], axis=1)
    acc_ref[...] += _dot(pt, jnp.concatenate([ya_ref[...], yb_ref[...]], axis=0))

    @pl.when(s == ns // 2 - 1)
    def _():
        o_ref[...] = x_ref[...] + mod_ref[ROW_GA_FFN:ROW_GA_FFN + 1, :] * acc_ref[...]


def _moe_scatter(yg, pos, x, mod, plan, seq):
    m, d = x.shape
    ns = plan["s_max"]
    tpb = seq // MOE_TM
    kern = functools.partial(_moe_scatter_kernel, ns=ns)
    return pl.pallas_call(
        kern,
        grid_spec=pltpu.PrefetchScalarGridSpec(
            num_scalar_prefetch=3,
            grid=(m // MOE_TM, ns // 2),
            in_specs=[
                pl.BlockSpec((MOE_BLK, d), lambda i, s, sr, ex, rb: (sr[i * ns + 2 * s], 0)),
                pl.BlockSpec((MOE_BLK, d), lambda i, s, sr, ex, rb: (sr[i * ns + 2 * s + 1], 0)),
                pl.BlockSpec((MOE_TM, LANES), lambda i, s, sr, ex, rb: (i, 0)),
                pl.BlockSpec((MOE_TM, d), lambda i, s, sr, ex, rb: (i, 0)),
                pl.BlockSpec((None, 6, d), lambda i, s, sr, ex, rb: (i // tpb, 0, 0)),
            ],
            out_specs=pl.BlockSpec((MOE_TM, d), lambda i, s, sr, ex, rb: (i, 0)),
            scratch_shapes=[pltpu.VMEM((MOE_TM, d), F32)],
        ),
        out_shape=jax.ShapeDtypeStruct((m, d), F32),
        compiler_params=_cparams(2),
        name="moe_scatter",
    )(plan["s_src"], plan["s_exp"], plan["s_rb"], yg, yg, pos, x, mod)


def _moe_layer(x, mod, g, w_router, b_router, w1, w3, w2, wl, seq, fc):
    m = x.shape[0]
    cmbt, pos, post, counts, h = _router(x, mod, g, w_router, b_router, seq)
    plan = _moe_plan(counts[:, 0, :w_router.shape[1]].astype(jnp.int32), m)
    xg, row_w = _moe_gather(h, post, cmbt, plan)
    yg = _moe_ffn(xg, row_w, w1, w3, w2, wl, plan, fc)
    return _moe_scatter(yg, pos, x, mod, plan, seq)


def _pad_w_in(w):
    wt = jnp.swapaxes(w, 1, 2)
    gap = jnp.zeros((w.shape[0], CB_MERGE * LANES - RAW_MERGE_COL, w.shape[1]), w.dtype)
    q0, q1 = CB_SB_Q * LANES, CB_SB_K * LANES
    return jnp.concatenate([wt[:, :q0], wt[:, q0:q1] * SB_Q_SCALE, wt[:, q1:RAW_MERGE_COL], gap,
                            wt[:, RAW_MERGE_COL:]], axis=1).astype(BF16)


def _kv_chunks(proj3, cb):
    b, s, _ = proj3.shape
    t = proj3[:, :, cb * LANES:(cb + NSA_KV_HEADS) * LANES].reshape(b, s, NSA_KV_HEADS, HEAD_DIM)
    return t.transpose(0, 2, 1, 3).reshape(b, NSA_KV_HEADS, s // CMP_STRIDE, CMP_STRIDE * HEAD_DIM)


def kernel(x, c, w_ada, b_ada, g_mix, g_ffn, w_in, conv_w, conv_b, conv_ln_g, conv_ln_b, w_conv_out, w_sb_out,
           nsa_cmp_pos_k, nsa_cmp_pos_v, nsa_cmp_wk, nsa_cmp_wv, nsa_q_g, nsa_kc_g, nsa_ks_g, nsa_kw_g, w_nsa_out, w_o,
           rel_bias, ffn_w1, ffn_w3, ffn_w2, moe_router, moe_router_b, moe_w1, moe_w3, moe_w2):
    b, s, d = x.shape
    m = b * s
    depth = w_ada.shape[0]
    mod_all = _ada_all(c, w_ada, b_ada)
    bias_c, bias_t = _bias_tables(rel_bias, s)
    w_in_p = _pad_w_in(w_in)
    w_conv_out, w_sb_out, w_nsa_out, w_o, ffn_w1, ffn_w3, ffn_w2, moe_w1, moe_w3, moe_w2 = (
        w.astype(BF16) for w in (w_conv_out, w_sb_out, w_nsa_out, w_o, ffn_w1, ffn_w3, ffn_w2, moe_w1, moe_w3, moe_w2))
    xf = x.reshape(m, d)
    for l in range(depth):
        mod = mod_all[l].reshape(b, 6, d)
        proj = _norm_matmul(xf, mod, g_mix[l], w_in_p, l, ROW_SH_MIX, ROW_SC_MIX, s)
        proj3 = proj.reshape(b, s, NP)
        u_act = _conv_module(proj, conv_w[l], conv_b[l], conv_ln_g[l], conv_ln_b[l], b, s)
        sb = _sb_attention(proj3)
        kcb, vcb, ksn, kwn = _nsa_prep(proj3, _kv_chunks(proj3, CB_KC), _kv_chunks(proj3, CB_VC), nsa_cmp_pos_k[l],
                                       nsa_cmp_pos_v[l], nsa_cmp_wk[l], nsa_cmp_wv[l], nsa_kc_g[l], nsa_ks_g[l],
                                       nsa_kw_g[l])
        nsa = _nsa_attention(proj3, kcb, vcb, ksn, kwn, bias_c, bias_t, nsa_q_g[l])
        merged = _merge(u_act, sb.reshape(m, -1), nsa.reshape(m, -1), proj, w_conv_out, w_sb_out, w_nsa_out, l)
        xf = _mm_res(merged, w_o, l, xf, mod, ROW_GA_MIX, s)
        i = l // 2
        if l % 2 == 0:
            act = _ffn_up(xf, mod, g_ffn[l], ffn_w1, ffn_w3, i, s)
            xf = _mm_res(act, ffn_w2, i, xf, mod, ROW_GA_FFN, s, tm=512, tn=512)
        else:
            xf = _moe_layer(xf, mod, g_ffn[l], moe_router[i], moe_router_b[i], moe_w1, moe_w3, moe_w2, i, s, MOE_FC)
    return xf.reshape(b, s, d)
```

```python
import functools
import math

import numpy as np
import jax
import jax.numpy as jnp
from jax import lax
from jax.experimental import pallas as pl
from jax.experimental.pallas import tpu as pltpu

F32 = jnp.float32
BF16 = jnp.bfloat16

D_MODEL = 2048
DEPTH = 4
D_CONV = 1024
CONV_WIDTH = 31
SB_HEADS = 8
HEAD_DIM = 128
NSA_HEADS = 8
NSA_KV_HEADS = 2
NSA_REP = NSA_HEADS // NSA_KV_HEADS
CMP_LEN = 32
CMP_STRIDE = 16
SLC_LEN = 64
SLC_TOP_N = 16
N_LOCAL_BLOCKS = 2
WINDOW = 512
FORCE_SCORE = 1e4
REL_BUCKETS = 32
REL_MAX_DIST = 128
D_FF = 5632
N_EXPERTS = 8
D_FF_EXPERT = 2816
EPS = 1e-6
NEG_INF = -1e30
TINY = 1e-20

LANES = 128
SUBLANES = 8
V7X_VMEM_LIMIT_BYTES = 56 * 1024 * 1024

NP_BLOCKS = 112
NP = NP_BLOCKS * LANES
CB_GLU_A, CB_GLU_G = 0, 8
CB_SB_Q, CB_SB_K, CB_SB_V = 16, 24, 32
CB_NQ = 40
CB_KC, CB_VC, CB_KS, CB_VS, CB_KW, CB_VW = 48, 50, 52, 54, 56, 58
CB_NGATE = 60
CB_MERGE = 64
RAW_GATE_COL = 7680
RAW_MERGE_COL = 7704
ROW_SH_MIX, ROW_SC_MIX, ROW_GA_MIX, ROW_SH_FFN, ROW_SC_FFN, ROW_GA_FFN = range(6)

SCALE = HEAD_DIM ** -0.5
SB_Q_SCALE = SCALE * math.log2(math.e)
TQ = 128
SB_TILE = 256
SB_GROUP = 8
SEL_CHUNK = 512
NSA_QT = 256
NSA_SUB = NSA_QT // TQ
WIN_SPAN = WINDOW + NSA_QT
BT_DIAG, BT_SUB, BT_FAR, BT_FAR_UPPER, BT_MASKED = range(5)
CONV_HALO = 32
MOE_TM = 1024
MOE_BLK = 128
MOE_MT = 512
MOE_FC = 1408
MOE_SUB = 256
MOE_NEVER = 1 << 20


def _cparams(n_axes):
    return pltpu.CompilerParams(
        dimension_semantics=("arbitrary",) * n_axes,
        vmem_limit_bytes=V7X_VMEM_LIMIT_BYTES,
    )


def _dot(a, b):
    return jnp.dot(a, b, preferred_element_type=F32)


def _dot_t(a, b):
    return lax.dot_general(a, b, (((1,), (1,)), ((), ())), preferred_element_type=F32)


def _split_dot(x, w01):
    hi = x.astype(BF16)
    lo = (x - hi.astype(F32)).astype(BF16)
    return _dot(hi, w01) + _dot(lo, w01)


def _norm_mod(x, g, sh, sc):
    ms = jnp.mean(x * x, axis=-1, keepdims=True)
    return (x * lax.rsqrt(ms + EPS) * g) * (1.0 + sc) + sh


def _rms(x, g):
    ms = jnp.mean(x * x, axis=-1, keepdims=True)
    return x * lax.rsqrt(ms + EPS) * g


def _ada_kernel(c_ref, w_ref, b_ref, o_ref):
    c = c_ref[...]
    ca = (c * jax.nn.sigmoid(c)).astype(BF16)
    o_ref[...] = _dot(ca, w_ref[...].astype(BF16)) + b_ref[...]


def _ada_all(c, w_ada, b_ada):
    depth, d, n = w_ada.shape
    nb = c.shape[0]
    b = 16
    c = jnp.zeros((b, d), c.dtype).at[:nb].set(c)
    tn = 1024
    out = pl.pallas_call(
        _ada_kernel,
        grid=(depth, n // tn),
        in_specs=[
            pl.BlockSpec((b, d), lambda l, j: (0, 0)),
            pl.BlockSpec((None, d, tn), lambda l, j: (l, 0, j)),
            pl.BlockSpec((None, 1, tn), lambda l, j: (l, 0, j)),
        ],
        out_specs=pl.BlockSpec((None, b, tn), lambda l, j: (l, 0, j)),
        out_shape=jax.ShapeDtypeStruct((depth, b, n), F32),
        compiler_params=_cparams(2),
        name="ada",
    )(c, w_ada, b_ada.reshape(depth, 1, n))
    return out[:, :nb]


def _bias_kernel(tab_ref, bk_ref, o_ref):
    h = pl.program_id(0)
    bk = bk_ref[...]
    acc = jnp.zeros(bk.shape, F32)
    for b in range(REL_BUCKETS):
        acc = jnp.where(bk == b, tab_ref[b, h], acc)
    o_ref[...] = acc


def _bias_expand(rel_bias, buckets):
    r = buckets.shape[0]
    tr = r
    return pl.pallas_call(
        _bias_kernel,
        grid=(NSA_HEADS, r // tr),
        in_specs=[
            pl.BlockSpec(memory_space=pltpu.SMEM),
            pl.BlockSpec((tr, LANES), lambda h, i: (i, 0)),
        ],
        out_specs=pl.BlockSpec((None, tr, LANES), lambda h, i: (h, i, 0)),
        out_shape=jax.ShapeDtypeStruct((NSA_HEADS, r, LANES), F32),
        compiler_params=_cparams(2),
        name="bias_expand",
    )(rel_bias, buckets)


def _rel_bucket(dist):
    n = jnp.maximum(dist, 0)
    max_exact = REL_BUCKETS // 2
    nf = jnp.maximum(n, 1).astype(F32)
    large = max_exact + (jnp.log(nf / max_exact) / math.log(REL_MAX_DIST / max_exact) * (REL_BUCKETS - max_exact)).astype(jnp.int32)
    large = jnp.minimum(large, REL_BUCKETS - 1)
    return jnp.where(n < max_exact, n, large)


def _bias_tables(rel_bias, s):
    t = jnp.arange(s, dtype=jnp.int32)[:, None]
    cend = jnp.arange(LANES, dtype=jnp.int32)[None, :] * CMP_STRIDE + (CMP_LEN - 1)
    q = jnp.arange(TQ, dtype=jnp.int32)[:, None]
    k = jnp.arange(TQ, dtype=jnp.int32)[None, :]
    bk_t = jnp.concatenate([_rel_bucket(q - k), _rel_bucket(TQ + q - k), _rel_bucket(2 * TQ + q - k)], axis=0)
    out = _bias_expand(rel_bias, jnp.concatenate([_rel_bucket(t - cend), bk_t], axis=0))
    bias_c = jnp.where(t >= cend, out[:, :s], NEG_INF).reshape(NSA_KV_HEADS, NSA_REP, s, LANES)
    diag, sub, far = out[:, s:s + TQ], out[:, s + TQ:s + 2 * TQ], out[:, s + 2 * TQ:]
    tiles = jnp.stack([jnp.where(q >= k, diag, NEG_INF), sub, far, jnp.where(k > q, far, NEG_INF),
                       jnp.full_like(far, NEG_INF)], axis=1)
    tiles = tiles.reshape(NSA_KV_HEADS, NSA_REP, 5, TQ, TQ).transpose(0, 2, 1, 3, 4)
    return bias_c, tiles.reshape(NSA_KV_HEADS, 5, NSA_REP * TQ, TQ)


def _norm_matmul_kernel(x_ref, mod_ref, g_ref, w_ref, o_ref, h_ref, *, row_sh, row_sc):
    @pl.when(pl.program_id(1) == 0)
    def _():
        h = _norm_mod(x_ref[...], g_ref[...], mod_ref[row_sh:row_sh + 1, :], mod_ref[row_sc:row_sc + 1, :])
        h_ref[...] = h.astype(BF16)

    o_ref[...] = _dot_t(h_ref[...], w_ref[...]).astype(o_ref.dtype)


def _norm_matmul(x, mod, g, wt, wl, row_sh, row_sc, seq, tm=1024, tn=1024):
    m, d = x.shape
    n = wt.shape[1]
    tpb = seq // tm
    kern = functools.partial(_norm_matmul_kernel, row_sh=row_sh, row_sc=row_sc)
    return pl.pallas_call(
        kern,
        grid=(m // tm, n // tn),
        in_specs=[
            pl.BlockSpec((tm, d), lambda i, j: (i, 0)),
            pl.BlockSpec((None, 6, d), lambda i, j: (i // tpb, 0, 0)),
            pl.BlockSpec((1, d), lambda i, j: (0, 0)),
            pl.BlockSpec((None, tn, d), lambda i, j: (wl, j, 0)),
        ],
        out_specs=pl.BlockSpec((tm, tn), lambda i, j: (i, j)),
        out_shape=jax.ShapeDtypeStruct((m, n), BF16),
        scratch_shapes=[pltpu.VMEM((tm, d), BF16)],
        compiler_params=_cparams(2),
        name="norm_matmul",
    )(x, mod, g.reshape(1, d), wt)


def _conv_kernel(a_ref, g_ref, ap_ref, gp_ref, cw_ref, cb_ref, lng_ref, lnb_ref, o_ref, ubuf, vbuf, shifted, *, ts):
    i = pl.program_id(1)
    ubuf[CONV_HALO:, :] = a_ref[...].astype(F32) * jax.nn.sigmoid(g_ref[...].astype(F32))
    up = ap_ref[...].astype(F32) * jax.nn.sigmoid(gp_ref[...].astype(F32))
    ubuf[:CONV_HALO, :] = jnp.where(i > 0, up, 0.0)

    first = CONV_HALO - (CONV_WIDTH - 1)
    span = ts + CONV_HALO - SUBLANES

    def chunk(c, carry):
        c0 = pl.multiple_of(c * LANES, LANES)
        for s in range(1, SUBLANES):
            shifted[s - 1] = ubuf[pl.ds(s, span), pl.ds(c0, LANES)]
        acc = jnp.zeros((ts, LANES), F32) + cb_ref[:, pl.ds(c0, LANES)]
        for k in range(CONV_WIDTH):
            a, s = divmod(first + k, SUBLANES)
            rows = ubuf[pl.ds(a * SUBLANES, ts), pl.ds(c0, LANES)] if s == 0 else shifted[s - 1, pl.ds(a * SUBLANES, ts), :]
            acc = acc + cw_ref[k:k + 1, pl.ds(c0, LANES)] * rows
        vbuf[:, pl.ds(c0, LANES)] = acc
        return carry

    lax.fori_loop(0, D_CONV // LANES, chunk, 0)
    v = vbuf[...]
    mu = jnp.mean(v, axis=-1, keepdims=True)
    vc = v - mu
    var = jnp.mean(vc * vc, axis=-1, keepdims=True)
    y = vc * lax.rsqrt(var + EPS) * lng_ref[...] + lnb_ref[...]
    o_ref[...] = (y * jax.nn.sigmoid(y)).astype(o_ref.dtype)


def _conv_module(proj, conv_w, conv_b, ln_g, ln_b, batch, seq, ts=256):
    m = proj.shape[0]
    nt = seq // ts
    hb = ts // CONV_HALO
    kern = functools.partial(_conv_kernel, ts=ts)

    def prev_idx(col):
        return lambda b, i: (jnp.maximum((b * nt + i) * hb - 1, 0), col)

    return pl.pallas_call(
        kern,
        grid=(batch, nt),
        in_specs=[
            pl.BlockSpec((ts, D_CONV), lambda b, i: (b * nt + i, 0)),
            pl.BlockSpec((ts, D_CONV), lambda b, i: (b * nt + i, 1)),
            pl.BlockSpec((CONV_HALO, D_CONV), prev_idx(0)),
            pl.BlockSpec((CONV_HALO, D_CONV), prev_idx(1)),
            pl.BlockSpec((CONV_WIDTH, D_CONV), lambda b, i: (0, 0)),
            pl.BlockSpec((1, D_CONV), lambda b, i: (0, 0)),
            pl.BlockSpec((1, D_CONV), lambda b, i: (0, 0)),
            pl.BlockSpec((1, D_CONV), lambda b, i: (0, 0)),
        ],
        out_specs=pl.BlockSpec((ts, D_CONV), lambda b, i: (b * nt + i, 0)),
        out_shape=jax.ShapeDtypeStruct((m, D_CONV), BF16),
        scratch_shapes=[pltpu.VMEM((CONV_HALO + ts, D_CONV), F32), pltpu.VMEM((ts, D_CONV), F32),
                        pltpu.VMEM((SUBLANES - 1, ts + CONV_HALO - SUBLANES, LANES), F32)],
        compiler_params=_cparams(2),
        name="conv_module",
    )(proj, proj, proj, proj, conv_w, conv_b.reshape(1, -1), ln_g.reshape(1, -1), ln_b.reshape(1, -1))


def _sb_kernel(q_ref, k_ref, v_ref, o_ref):
    i = pl.program_id(2)
    t = SB_TILE
    rows = SB_GROUP * t
    row = lax.broadcasted_iota(jnp.int32, (t, t), 0)
    col = lax.broadcasted_iota(jnp.int32, (t, t), 1)
    later = jnp.where(row > col, 1.0, 0.0).astype(BF16)
    heads = [slice(h * HEAD_DIM, (h + 1) * HEAD_DIM) for h in range(SB_GROUP)]

    def tile(j, rsum, acc, diag):
        k0 = pl.multiple_of(j * t, t)
        z = jnp.concatenate([_dot_t(q_ref[:, hs], k_ref[pl.ds(k0, t), hs]) for hs in heads], axis=0)
        if diag:
            qpos = lax.broadcasted_iota(jnp.int32, (rows, t), 0) & (t - 1)
            z = jnp.where(lax.broadcasted_iota(jnp.int32, (rows, t), 1) < qpos, z, NEG_INF)
        sp = jnp.log2(1.0 + jnp.exp2(-jnp.abs(z)))
        log_beta = jnp.minimum(z, 0.0) - sp
        log_keep = log_beta - z
        log_survive = _dot(log_keep.astype(BF16), later) + rsum
        a = jnp.exp2(log_beta + log_survive).astype(BF16)
        pv = [_dot(a[h * t:(h + 1) * t], v_ref[pl.ds(k0, t), hs]) for h, hs in enumerate(heads)]
        acc = acc + jnp.concatenate(pv, axis=0)
        rsum = rsum + jnp.sum(log_keep, axis=-1, keepdims=True)
        return rsum, acc

    carry = tile(i, jnp.zeros((rows, 1), F32), jnp.zeros((rows, HEAD_DIM), F32), True)
    _, acc = lax.fori_loop(0, i, lambda jj, c: tile(i - 1 - jj, c[0], c[1], False), carry)
    for h in range(SB_GROUP):
        o_ref[:, h * HEAD_DIM:(h + 1) * HEAD_DIM] = acc[h * t:(h + 1) * t].astype(o_ref.dtype)


def _sb_attention(proj3):
    b, s, _ = proj3.shape
    gw = SB_GROUP * HEAD_DIM
    return pl.pallas_call(
        _sb_kernel,
        grid=(b, SB_HEADS // SB_GROUP, s // SB_TILE),
        in_specs=[
            pl.BlockSpec((None, SB_TILE, gw), lambda bi, hg, i: (bi, i, CB_SB_Q // SB_GROUP + hg)),
            pl.BlockSpec((None, s, gw), lambda bi, hg, i: (bi, 0, CB_SB_K // SB_GROUP + hg)),
            pl.BlockSpec((None, s, gw), lambda bi, hg, i: (bi, 0, CB_SB_V // SB_GROUP + hg)),
        ],
        out_specs=pl.BlockSpec((None, SB_TILE, gw), lambda bi, hg, i: (bi, i, hg)),
        out_shape=jax.ShapeDtypeStruct((b, s, SB_HEADS * HEAD_DIM), BF16),
        compiler_params=_cparams(3),
        name="sb_attention",
    )(proj3, proj3, proj3)


def _nsa_prep_kernel(kc_ref, vc_ref, ks_ref, kw_ref, pk_ref, pv_ref, wk_ref, wv_ref, kcg_ref, ksg_ref, kwg_ref,
                     kcb_ref, vcb_ref, ksn_ref, kwn_ref):
    half = CMP_STRIDE * HEAD_DIM

    def compress(a_ref, p_ref, w_ref):
        a = a_ref[...].astype(F32)
        top = _dot((a + p_ref[0:1, :]).astype(BF16), w_ref[0:half, :])
        bot = _dot((a + p_ref[1:2, :]).astype(BF16), w_ref[half:2 * half, :])
        return top + pltpu.roll(bot, bot.shape[0] - 1, axis=0)

    kcb_ref[...] = _rms(compress(kc_ref, pk_ref, wk_ref), kcg_ref[...]).astype(BF16)
    vcb_ref[...] = compress(vc_ref, pv_ref, wv_ref).astype(BF16)
    kwn_ref[...] = _rms(kw_ref[...].astype(F32), kwg_ref[...]).astype(BF16)
    ksn_ref[:, 0:HEAD_DIM] = _rms(ks_ref[...].astype(F32), ksg_ref[...]).astype(BF16)
    s = ks_ref.shape[0]
    blk = jnp.right_shift(lax.broadcasted_iota(jnp.int32, (s, LANES), 0), int(math.log2(SLC_LEN)))
    onehot = blk == lax.broadcasted_iota(jnp.int32, (s, LANES), 1)
    ksn_ref[:, HEAD_DIM:HEAD_DIM + LANES] = jnp.where(onehot, 1.0, 0.0).astype(BF16)


def _nsa_prep(proj3, pos_k, pos_v, wk, wv, kc_g, ks_g, kw_g):
    b, s, _ = proj3.shape
    g_n = NSA_KV_HEADS
    kv_chunks = _kv_chunks(proj3)
    nch = s // CMP_STRIDE
    half = CMP_STRIDE * HEAD_DIM
    vec = lambda: pl.BlockSpec((1, HEAD_DIM), lambda bi, g: (0, 0))
    small = jax.ShapeDtypeStruct((b, g_n, nch, HEAD_DIM), BF16)
    full = jax.ShapeDtypeStruct((b, g_n, s, HEAD_DIM), BF16)
    aug = jax.ShapeDtypeStruct((b, g_n, s, HEAD_DIM + LANES), BF16)
    return pl.pallas_call(
        _nsa_prep_kernel,
        grid=(b, g_n),
        in_specs=[
            pl.BlockSpec((None, None, nch, half), lambda bi, g: (bi, g, 0, 0)),
            pl.BlockSpec((None, None, nch, half), lambda bi, g: (bi, g_n + g, 0, 0)),
            pl.BlockSpec((None, s, HEAD_DIM), lambda bi, g: (bi, 0, CB_KS + g)),
            pl.BlockSpec((None, s, HEAD_DIM), lambda bi, g: (bi, 0, CB_KW + g)),
            pl.BlockSpec((2, half), lambda bi, g: (0, 0)),
            pl.BlockSpec((2, half), lambda bi, g: (0, 0)),
            pl.BlockSpec((2 * half, HEAD_DIM), lambda bi, g: (0, 0)),
            pl.BlockSpec((2 * half, HEAD_DIM), lambda bi, g: (0, 0)),
            vec(), vec(), vec(),
        ],
        out_specs=[
            pl.BlockSpec((None, None, nch, HEAD_DIM), lambda bi, g: (bi, g, 0, 0)),
            pl.BlockSpec((None, None, nch, HEAD_DIM), lambda bi, g: (bi, g, 0, 0)),
            pl.BlockSpec((None, None, s, HEAD_DIM + LANES), lambda bi, g: (bi, g, 0, 0)),
            pl.BlockSpec((None, None, s, HEAD_DIM), lambda bi, g: (bi, g, 0, 0)),
        ],
        out_shape=[small, small, aug, full],
        compiler_params=_cparams(2),
        name="nsa_prep",
    )(kv_chunks, kv_chunks, proj3, proj3, pos_k.reshape(2, half), pos_v.reshape(2, half),
      wk.astype(BF16), wv.astype(BF16), kc_g.reshape(1, -1), ks_g.reshape(1, -1), kw_g.reshape(1, -1))


def _nsa_kernel(q_ref, kcb_ref, vcb_ref, ksa_ref, vs_ref, kwn_ref, vw_ref, gate_ref, bc_ref, bt_ref, qg_ref, ovt_ref,
                o_ref):
    g = pl.program_id(1)
    i = pl.program_id(2)
    r_n = NSA_REP
    groups = [(a, r) for a in range(NSA_SUB) for r in range(r_n)]
    sub = lambda a: slice(a * TQ, (a + 1) * TQ)
    head = lambda r: slice(r * HEAD_DIM, (r + 1) * HEAD_DIM)

    q = jnp.concatenate([_rms(q_ref[sub(a), head(r)].astype(F32), qg_ref[...]).astype(BF16) for a, r in groups], axis=0)

    logit_c = _dot_t(q, kcb_ref[...]) + jnp.concatenate([bc_ref[r, sub(a), :] for a, r in groups], axis=0)
    m_c = jnp.max(logit_c, axis=-1, keepdims=True)
    p_c = jnp.where(logit_c > 0.5 * NEG_INF, jnp.exp(logit_c - m_c), 0.0)
    p_c = p_c / jnp.maximum(jnp.sum(p_c, axis=-1, keepdims=True), TINY)
    o_c = _dot(p_c.astype(BF16), vcb_ref[...])

    p_sum = jnp.concatenate(
        [sum(p_c[(a * r_n + r) * TQ:(a * r_n + r + 1) * TQ] for r in range(r_n)) for a in range(NSA_SUB)], axis=0)
    hi = p_sum.astype(BF16)
    lo = (p_sum - hi.astype(F32)).astype(BF16)
    n_sel = ovt_ref.shape[0]
    imp = _dot_t(ovt_ref[...], hi) + _dot_t(ovt_ref[...], lo)
    blk = lax.broadcasted_iota(jnp.int32, (n_sel, NSA_QT), 0)
    tb = jnp.right_shift(i * NSA_QT + lax.broadcasted_iota(jnp.int32, (n_sel, NSA_QT), 1), int(math.log2(SLC_LEN)))
    causal_blk = blk <= tb
    forced = (blk == 0) | (causal_blk & (blk > tb - N_LOCAL_BLOCKS))
    score = jnp.where(forced, FORCE_SCORE, jnp.where(causal_blk, imp, -FORCE_SCORE))
    rank = jnp.zeros((n_sel, NSA_QT), F32)
    for b in range(n_sel):
        sb = score[b:b + 1, :]
        beats = (sb > score) | ((sb == score) & (blk > b))
        rank = rank + jnp.where(beats, 1.0, 0.0)
    chosen = (rank < float(min(SLC_TOP_N, n_sel))) & causal_blk
    sel_neg = jnp.where(chosen, 0.0, NEG_INF)
    sel_neg = jnp.concatenate([sel_neg, jnp.zeros((LANES - n_sel, NSA_QT), F32)], axis=0).T.astype(BF16)
    q_aug = jnp.concatenate([q, jnp.concatenate([sel_neg[sub(a)] for a, _ in groups], axis=0)], axis=1)

    def biased(s, first_tile, idx_fn):
        return jnp.concatenate(
            [s[:, c * TQ:(c + 1) * TQ]
             + jnp.concatenate([bt_ref[idx_fn(i * NSA_SUB + a - (first_tile + c))] for a in range(NSA_SUB)], axis=0)
             for c in range(s.shape[1] // TQ)], axis=1)

    w_tile = jnp.maximum(i * NSA_SUB - WINDOW // TQ, 0)
    w0 = pl.multiple_of(w_tile * TQ, TQ)

    def win_idx(off):
        near = jnp.where(off == WINDOW // TQ, BT_FAR_UPPER, jnp.minimum(off, BT_FAR))
        return jnp.where((off < 0) | (off > WINDOW // TQ), BT_MASKED, near)

    s_w = biased(_dot_t(q, kwn_ref[pl.ds(w0, WIN_SPAN), :]), w_tile, win_idx)
    p_w = jnp.exp(s_w - jnp.max(s_w, axis=-1, keepdims=True))
    o_w = _dot(p_w.astype(BF16), vw_ref[pl.ds(w0, WIN_SPAN), :]) / jnp.sum(p_w, axis=-1, keepdims=True)

    def sel_chunk(kc, carry, diag):
        c0 = pl.multiple_of(kc * SEL_CHUNK, SEL_CHUNK)
        idx_fn = (lambda off: jnp.where(off < 0, BT_MASKED, jnp.minimum(off, BT_FAR))) if diag else (
            lambda off: jnp.minimum(off, BT_FAR))
        s = biased(_dot_t(q_aug, ksa_ref[pl.ds(c0, SEL_CHUNK), :]), kc * (SEL_CHUNK // TQ), idx_fn)
        m_blk = jnp.max(s, axis=-1, keepdims=True)
        if diag:
            p = jnp.exp(s - m_blk)
            return m_blk, jnp.sum(p, axis=-1, keepdims=True), _dot(p.astype(BF16), vs_ref[pl.ds(c0, SEL_CHUNK), :])
        m, l, acc = carry
        m_new = jnp.maximum(m, m_blk)
        p = jnp.exp(s - m_new)
        alpha = jnp.exp(m - m_new)
        l = alpha * l + jnp.sum(p, axis=-1, keepdims=True)
        acc = alpha * acc + _dot(p.astype(BF16), vs_ref[pl.ds(c0, SEL_CHUNK), :])
        return m_new, l, acc

    kc_diag = lax.shift_right_logical(i * NSA_SUB + (NSA_SUB - 1), jnp.int32(int(math.log2(SEL_CHUNK // TQ))))
    carry = sel_chunk(kc_diag, None, True)
    _, l_s, acc_s = lax.fori_loop(0, kc_diag, lambda jj, c: sel_chunk(kc_diag - 1 - jj, c, False), carry)
    o_s = acc_s / l_s

    gates = jax.nn.sigmoid(gate_ref[...].astype(F32))
    lane = lax.broadcasted_iota(jnp.int32, gates.shape, 1)

    def gate_col(idx):
        return jnp.sum(jnp.where(lane == idx, gates, 0.0), axis=-1, keepdims=True)

    for r in range(r_n):
        h = g * r_n + r
        g_c, g_s, g_w = gate_col(h), gate_col(NSA_HEADS + h), gate_col(2 * NSA_HEADS + h)
        for a in range(NSA_SUB):
            k = slice((a * r_n + r) * TQ, (a * r_n + r + 1) * TQ)
            o = g_c[sub(a)] * o_c[k] + g_s[sub(a)] * o_s[k] + g_w[sub(a)] * o_w[k]
            o_ref[sub(a), head(r)] = o.astype(o_ref.dtype)


def _overlap_matrix(s):
    nch = s // CMP_STRIDE
    nsel = s // SLC_LEN
    ci = np.arange(nch)[:, None]
    sj = np.arange(nsel)[None, :]
    ov = (ci * CMP_STRIDE <= sj * SLC_LEN + SLC_LEN - 1) & (ci * CMP_STRIDE + CMP_LEN - 1 >= sj * SLC_LEN)
    ov = ov & (ci < nch - 1)
    return jnp.asarray(ov.T, BF16)


def _nsa_attention(proj3, kcb, vcb, ksa, kwn, bias_c, bias_t, q_g):
    b, s, _ = proj3.shape
    g_n, r_n = NSA_KV_HEADS, NSA_REP
    nch = s // CMP_STRIDE
    nsel = s // SLC_LEN
    gw = r_n * HEAD_DIM
    return pl.pallas_call(
        _nsa_kernel,
        grid=(b, g_n, s // NSA_QT),
        in_specs=[
            pl.BlockSpec((None, NSA_QT, gw), lambda bi, g, i: (bi, i, CB_NQ // r_n + g)),
            pl.BlockSpec((None, None, nch, HEAD_DIM), lambda bi, g, i: (bi, g, 0, 0)),
            pl.BlockSpec((None, None, nch, HEAD_DIM), lambda bi, g, i: (bi, g, 0, 0)),
            pl.BlockSpec((None, None, s, HEAD_DIM + LANES), lambda bi, g, i: (bi, g, 0, 0)),
            pl.BlockSpec((None, s, HEAD_DIM), lambda bi, g, i: (bi, 0, CB_VS + g)),
            pl.BlockSpec((None, None, s, HEAD_DIM), lambda bi, g, i: (bi, g, 0, 0)),
            pl.BlockSpec((None, s, HEAD_DIM), lambda bi, g, i: (bi, 0, CB_VW + g)),
            pl.BlockSpec((None, NSA_QT, LANES), lambda bi, g, i: (bi, i, CB_NGATE)),
            pl.BlockSpec((None, r_n, NSA_QT, LANES), lambda bi, g, i: (g, 0, i, 0)),
            pl.BlockSpec((None, 5, r_n * TQ, TQ), lambda bi, g, i: (g, 0, 0, 0)),
            pl.BlockSpec((1, HEAD_DIM), lambda bi, g, i: (0, 0)),
            pl.BlockSpec((nsel, nch), lambda bi, g, i: (0, 0)),
        ],
        out_specs=pl.BlockSpec((None, NSA_QT, gw), lambda bi, g, i: (bi, i, g)),
        out_shape=jax.ShapeDtypeStruct((b, s, NSA_HEADS * HEAD_DIM), BF16),
        compiler_params=_cparams(3),
        name="nsa_attention",
    )(proj3, kcb, vcb, ksa, proj3, kwn, proj3, proj3, bias_c, bias_t, q_g.reshape(1, -1) * SCALE, _overlap_matrix(s))


def _merge_kernel(u_ref, sb_ref, ns_ref, gc_ref, gs_ref, gn_ref, wc_ref, ws_ref, wn_ref, o_ref):
    m = jax.nn.sigmoid(gc_ref[...].astype(F32)) * _dot(u_ref[...], wc_ref[...])
    m = m + jax.nn.sigmoid(gs_ref[...].astype(F32)) * _dot(sb_ref[...], ws_ref[...])
    m = m + jax.nn.sigmoid(gn_ref[...].astype(F32)) * _dot(ns_ref[...], wn_ref[...])
    o_ref[...] = m.astype(o_ref.dtype)


def _merge(u_act, sb, nsa, proj, wc, ws, wn, wl, tm=512, tn=1024):
    m = u_act.shape[0]
    d = wc.shape[2]
    kc = wc.shape[1]
    gate0 = CB_MERGE * LANES // tn
    gstep = d // tn
    act = lambda: pl.BlockSpec((tm, kc), lambda i, j: (i, 0))
    gate = lambda n: pl.BlockSpec((tm, tn), lambda i, j: (i, gate0 + n * gstep + j))
    wsp = lambda: pl.BlockSpec((None, kc, tn), lambda i, j: (wl, 0, j))
    return pl.pallas_call(
        _merge_kernel,
        grid=(m // tm, d // tn),
        in_specs=[act(), act(), act(), gate(0), gate(1), gate(2), wsp(), wsp(), wsp()],
        out_specs=pl.BlockSpec((tm, tn), lambda i, j: (i, j)),
        out_shape=jax.ShapeDtypeStruct((m, d), BF16),
        compiler_params=_cparams(2),
        name="merge",
    )(u_act, sb, nsa, proj, proj, proj, wc, ws, wn)


def _mm_res_kernel(a_ref, w_ref, x_ref, mod_ref, o_ref, acc_ref, *, row_ga, nk):
    k = pl.program_id(2)

    @pl.when(k == 0)
    def _():
        acc_ref[...] = jnp.zeros_like(acc_ref)

    acc_ref[...] += _dot(a_ref[...], w_ref[...])

    @pl.when(k == nk - 1)
    def _():
        o_ref[...] = x_ref[...] + mod_ref[row_ga:row_ga + 1, :] * acc_ref[...]


def _mm_res(a, w, wl, x, mod, row_ga, seq, tm=1024, tn=1024, tk=None):
    m, ka = a.shape
    n = x.shape[1]
    tpb = seq // tm
    tk = tk or ka
    nk = ka // tk
    w_spec = pl.BlockSpec((None, tk, tn), lambda i, j, k: (wl, k, j))
    kern = functools.partial(_mm_res_kernel, row_ga=row_ga, nk=nk)
    return pl.pallas_call(
        kern,
        grid=(m // tm, n // tn, nk),
        in_specs=[
            pl.BlockSpec((tm, tk), lambda i, j, k: (i, k)),
            w_spec,
            pl.BlockSpec((tm, tn), lambda i, j, k: (i, j)),
            pl.BlockSpec((None, 6, tn), lambda i, j, k: (i // tpb, 0, j)),
        ],
        out_specs=pl.BlockSpec((tm, tn), lambda i, j, k: (i, j)),
        out_shape=jax.ShapeDtypeStruct((m, n), F32),
        scratch_shapes=[pltpu.VMEM((tm, tn), F32)],
        compiler_params=_cparams(3),
        name="matmul_residual",
    )(a, w, x, mod)


def _ffn_up_kernel(x_ref, mod_ref, g_ref, w1_ref, w3_ref, o_ref, h_ref):
    @pl.when(pl.program_id(1) == 0)
    def _():
        h = _norm_mod(x_ref[...], g_ref[...], mod_ref[ROW_SH_FFN:ROW_SH_FFN + 1, :], mod_ref[ROW_SC_FFN:ROW_SC_FFN + 1, :])
        h_ref[...] = h.astype(BF16)

    h = h_ref[...]
    a = _dot(h, w1_ref[...])
    o_ref[...] = (a * jax.nn.sigmoid(a) * _dot(h, w3_ref[...])).astype(o_ref.dtype)


def _ffn_up(x, mod, g, w1, w3, wl, seq, tm=1024, tn=512):
    m, d = x.shape
    f = w1.shape[2]
    tpb = seq // tm
    return pl.pallas_call(
        _ffn_up_kernel,
        grid=(m // tm, f // tn),
        in_specs=[
            pl.BlockSpec((tm, d), lambda i, j: (i, 0)),
            pl.BlockSpec((None, 6, d), lambda i, j: (i // tpb, 0, 0)),
            pl.BlockSpec((1, d), lambda i, j: (0, 0)),
            pl.BlockSpec((None, d, tn), lambda i, j: (wl, 0, j)),
            pl.BlockSpec((None, d, tn), lambda i, j: (wl, 0, j)),
        ],
        out_specs=pl.BlockSpec((tm, tn), lambda i, j: (i, j)),
        out_shape=jax.ShapeDtypeStruct((m, f), BF16),
        scratch_shapes=[pltpu.VMEM((tm, d), BF16)],
        compiler_params=_cparams(2),
        name="ffn_up",
    )(x, mod, g.reshape(1, d), w1, w3)


def _router_kernel(x_ref, mod_ref, g_ref, wr_ref, br_ref, cmbt_ref, pos_ref, post_ref, cnt_ref, h_ref):
    h = _norm_mod(x_ref[...], g_ref[...], mod_ref[ROW_SH_FFN:ROW_SH_FFN + 1, :], mod_ref[ROW_SC_FFN:ROW_SC_FFN + 1, :])
    h_ref[...] = h.astype(BF16)
    logits = jnp.dot(h, wr_ref[...], precision=lax.Precision.HIGHEST, preferred_element_type=F32) + br_ref[...]
    lane = lax.broadcasted_iota(jnp.int32, logits.shape, 1).astype(F32)
    pad = float(LANES)
    m1 = jnp.max(logits, axis=-1, keepdims=True)
    i1 = jnp.min(jnp.where(logits == m1, lane, pad), axis=-1, keepdims=True)
    rest = jnp.where(lane == i1, NEG_INF, logits)
    m2 = jnp.max(rest, axis=-1, keepdims=True)
    i2 = jnp.min(jnp.where(rest == m2, lane, pad), axis=-1, keepdims=True)
    e2 = jnp.exp(m2 - m1)
    w1 = 1.0 / (1.0 + e2)
    w2 = e2 / (1.0 + e2)
    cmb = jnp.where(lane == i1, w1, 0.0) + jnp.where(lane == i2, w2, 0.0)
    cmbt_ref[...] = cmb.T[0:cmbt_ref.shape[0], :]
    sel = jnp.where((lane == i1) | (lane == i2), 1.0, 0.0)
    tm = sel.shape[0]
    earlier = lax.broadcasted_iota(jnp.int32, (tm, tm), 1) < lax.broadcasted_iota(jnp.int32, (tm, tm), 0)
    pos = _dot(jnp.where(earlier, 1.0, 0.0).astype(BF16), sel.astype(BF16))
    pos = jnp.where(sel > 0.0, pos, -1.0)
    pos_ref[...] = pos
    post_ref[...] = pos.T[0:post_ref.shape[0], :]
    cnt_ref[...] = jnp.broadcast_to(jnp.sum(sel, axis=0, keepdims=True), cnt_ref.shape)


def _router(x, mod, g, w_router, b_router, seq):
    m, d = x.shape
    tm = MOE_TM
    n_e = w_router.shape[1]
    tpb = seq // tm
    nt = m // tm
    wr = jnp.zeros((d, LANES), F32).at[:, :n_e].set(w_router)
    br = jnp.full((1, LANES), 2.0 * NEG_INF, F32).at[0, :n_e].set(b_router)
    return pl.pallas_call(
        _router_kernel,
        grid=(nt,),
        in_specs=[
            pl.BlockSpec((tm, d), lambda i: (i, 0)),
            pl.BlockSpec((None, 6, d), lambda i: (i // tpb, 0, 0)),
            pl.BlockSpec((1, d), lambda i: (0, 0)),
            pl.BlockSpec((d, LANES), lambda i: (0, 0)),
            pl.BlockSpec((1, LANES), lambda i: (0, 0)),
        ],
        out_specs=[
            pl.BlockSpec((n_e, tm), lambda i: (0, i)),
            pl.BlockSpec((tm, LANES), lambda i: (i, 0)),
            pl.BlockSpec((n_e, tm), lambda i: (0, i)),
            pl.BlockSpec((None, 8, LANES), lambda i: (i, 0, 0)),
            pl.BlockSpec((tm, d), lambda i: (i, 0)),
        ],
        out_shape=[
            jax.ShapeDtypeStruct((n_e, m), F32),
            jax.ShapeDtypeStruct((m, LANES), F32),
            jax.ShapeDtypeStruct((n_e, m), F32),
            jax.ShapeDtypeStruct((nt, 8, LANES), F32),
            jax.ShapeDtypeStruct((m, d), BF16),
        ],
        compiler_params=_cparams(1),
        name="router",
    )(x, mod, g.reshape(1, d), wr, br)


def _count_le(sorted_vals, queries):
    return jnp.sum(sorted_vals[..., None, :] <= queries[..., :, None], axis=-1).astype(jnp.int32)


def _moe_plan(counts, m):
    t_n, e_n = counts.shape
    per_mt = MOE_MT // MOE_BLK
    g_max = 2 * m // MOE_BLK + t_n * e_n
    pad_max = (per_mt - 1) * e_n
    nblk = (counts + MOE_BLK - 1) // MOE_BLK
    nb_e = jnp.sum(nblk, axis=0)
    cap_e = (nb_e + per_mt - 1) // per_mt * per_mt
    start_e = jnp.cumsum(cap_e) - cap_e
    dst0 = (start_e[None, :] + jnp.cumsum(nblk, axis=0) - nblk).reshape(-1)
    flat = nblk.reshape(-1)
    cum = jnp.cumsum(flat)
    g = jnp.minimum(jnp.arange(g_max, dtype=jnp.int32), cum[-1] - 1)
    p = jnp.minimum(_count_le(cum, g), t_n * e_n - 1)
    rb = g - (cum[p] - flat[p])
    n_blocks = g_max + pad_max
    z_max = n_blocks - 2 * m // MOE_BLK
    used = jnp.zeros((n_blocks,), jnp.int32).at[dst0[p] + rb].set(1)
    free = jnp.argsort(used, stable=True).astype(jnp.int32)
    z = jnp.minimum(jnp.arange(z_max, dtype=jnp.int32), n_blocks - cum[-1] - 1)
    never = jnp.full((z_max,), MOE_NEVER, jnp.int32)
    g_tile = jnp.concatenate([p // e_n, jnp.full((z_max,), t_n - 1, jnp.int32)])
    g_exp = jnp.concatenate([p % e_n, jnp.zeros((z_max,), jnp.int32)])
    g_rb = jnp.concatenate([rb, never])
    g_dst = jnp.concatenate([dst0[p] + rb, free[z]])
    mt_max = (g_max + pad_max) // per_mt
    n_mt = (jnp.sum(cap_e) // per_mt).astype(jnp.int32)
    mt = jnp.minimum(jnp.arange(mt_max, dtype=jnp.int32), n_mt - 1)
    mt_exp = jnp.minimum(_count_le(jnp.cumsum(cap_e) // per_mt, mt), e_n - 1)
    s_max = 2 * MOE_TM // MOE_BLK + e_n
    cum_t = jnp.cumsum(nblk, axis=1)
    n_t = cum_t[:, -1:]
    s_all = jnp.arange(s_max, dtype=jnp.int32)[None, :]
    s = jnp.minimum(s_all, n_t - 1)
    s_exp = jnp.minimum(_count_le(cum_t, s), e_n - 1)
    s_rb = s - (jnp.take_along_axis(cum_t, s_exp, axis=1) - jnp.take_along_axis(nblk, s_exp, axis=1))
    s_src = jnp.take_along_axis(dst0.reshape(t_n, e_n), s_exp, axis=1) + s_rb
    s_rb = jnp.where(s_all < n_t, s_rb, MOE_NEVER)
    i32 = lambda a: a.astype(jnp.int32)
    return dict(g_tile=i32(g_tile), g_exp=i32(g_exp), g_rb=i32(g_rb), g_dst=i32(g_dst), n_mt=i32(n_mt.reshape(1)),
                mt_exp=i32(mt_exp), s_exp=i32(s_exp.reshape(-1)), s_rb=i32(s_rb.reshape(-1)),
                s_src=i32(s_src.reshape(-1)), n_blocks=n_blocks, s_max=s_max)


def _moe_gather_kernel(tile_ref, exp_ref, rb_ref, dst_ref, h_ref, post_ref, cmbt_ref, o_ref, w_ref):
    g = pl.program_id(0)
    fill = rb_ref[g] >= MOE_NEVER

    @pl.when(fill)
    def _():
        o_ref[...] = jnp.zeros_like(o_ref)
        w_ref[...] = jnp.zeros_like(w_ref)

    @pl.when(jnp.logical_not(fill))
    def _():
        row = post_ref[pl.ds(exp_ref[g], 1), :]
        tm = row.shape[1]
        want = (lax.broadcasted_iota(jnp.int32, (MOE_BLK, tm), 0) + rb_ref[g] * MOE_BLK).astype(F32)
        match = row == want
        o_ref[...] = _dot(jnp.where(match, 1.0, 0.0).astype(BF16), h_ref[...]).astype(o_ref.dtype)
        cw = jnp.sum(jnp.where(match, cmbt_ref[pl.ds(exp_ref[g], 1), :], 0.0), axis=-1, keepdims=True)
        w_ref[...] = jnp.broadcast_to(cw, w_ref.shape)


def _moe_gather(h, post, cmbt, plan):
    m, d = h.shape
    n_e = post.shape[0]
    n = plan["g_tile"].shape[0]
    rows = plan["n_blocks"] * MOE_BLK
    return pl.pallas_call(
        _moe_gather_kernel,
        grid_spec=pltpu.PrefetchScalarGridSpec(
            num_scalar_prefetch=4,
            grid=(n,),
            in_specs=[
                pl.BlockSpec((MOE_TM, d), lambda g, t, e, r, ds: (t[g], 0)),
                pl.BlockSpec((n_e, MOE_TM), lambda g, t, e, r, ds: (0, t[g])),
                pl.BlockSpec((n_e, MOE_TM), lambda g, t, e, r, ds: (0, t[g])),
            ],
            out_specs=[
                pl.BlockSpec((MOE_BLK, d), lambda g, t, e, r, ds: (ds[g], 0)),
                pl.BlockSpec((MOE_BLK, LANES), lambda g, t, e, r, ds: (ds[g], 0)),
            ],
        ),
        out_shape=[jax.ShapeDtypeStruct((rows, d), BF16), jax.ShapeDtypeStruct((rows, LANES), F32)],
        compiler_params=_cparams(1),
        name="moe_gather",
    )(plan["g_tile"], plan["g_exp"], plan["g_rb"], plan["g_dst"], h, post, cmbt)


def _moe_ffn_kernel(exp_ref, n_ref, x_ref, rw_ref, w1_ref, w3_ref, w2_ref, o_ref, acc_ref, *, nc):
    c = pl.program_id(1)

    @pl.when(pl.program_id(0) < n_ref[0])
    def _():
        @pl.when(c == 0)
        def _():
            acc_ref[...] = jnp.zeros_like(acc_ref)

        x = x_ref[...]
        fc = w1_ref.shape[1]
        for k0 in range(0, fc, MOE_SUB):
            k1 = min(k0 + MOE_SUB, fc)
            a = _dot(x, w1_ref[:, k0:k1])
            a = (a * jax.nn.sigmoid(a) * _dot(x, w3_ref[:, k0:k1])).astype(BF16)
            acc_ref[...] += _dot(a, w2_ref[k0:k1, :])

        @pl.when(c == nc - 1)
        def _():
            o_ref[...] = (rw_ref[:, 0:1] * acc_ref[...]).astype(o_ref.dtype)

    @pl.when(pl.program_id(0) >= n_ref[0])
    def _():
        o_ref[...] = jnp.zeros_like(o_ref)


def _moe_ffn(xg, row_w, w1, w3, w2, wl, plan, fc):
    d = xg.shape[1]
    fe = w1.shape[3]
    nc = fe // fc
    n_mt = plan["mt_exp"].shape[0]

    def tile(mt, n):
        return jnp.minimum(mt, n[0] - 1)

    def chunk(mt, c, n):
        return jnp.where(mt < n[0], c, nc - 1)

    kern = functools.partial(_moe_ffn_kernel, nc=nc)
    return pl.pallas_call(
        kern,
        grid_spec=pltpu.PrefetchScalarGridSpec(
            num_scalar_prefetch=2,
            grid=(n_mt, nc),
            in_specs=[
                pl.BlockSpec((MOE_MT, d), lambda mt, c, e, n: (tile(mt, n), 0)),
                pl.BlockSpec((MOE_MT, LANES), lambda mt, c, e, n: (tile(mt, n), 0)),
                pl.BlockSpec((None, None, d, fc), lambda mt, c, e, n: (wl, e[mt], 0, chunk(mt, c, n))),
                pl.BlockSpec((None, None, d, fc), lambda mt, c, e, n: (wl, e[mt], 0, chunk(mt, c, n))),
                pl.BlockSpec((None, None, fc, d), lambda mt, c, e, n: (wl, e[mt], chunk(mt, c, n), 0)),
            ],
            out_specs=pl.BlockSpec((MOE_MT, d), lambda mt, c, e, n: (mt, 0)),
            scratch_shapes=[pltpu.VMEM((MOE_MT, d), F32)],
        ),
        out_shape=jax.ShapeDtypeStruct((n_mt * MOE_MT, d), BF16),
        compiler_params=_cparams(2),
        name="moe_ffn",
    )(plan["mt_exp"], plan["n_mt"], xg, row_w, w1, w3, w2)


def _moe_scatter_kernel(src_ref, exp_ref, rb_ref, ya_ref, yb_ref, pos_ref, x_ref, mod_ref, o_ref, acc_ref, *, ns):
    i = pl.program_id(0)
    s = pl.program_id(1)

    @pl.when(s == 0)
    def _():
        acc_ref[...] = jnp.zeros_like(acc_ref)

    lane = lax.broadcasted_iota(jnp.int32, pos_ref.shape, 1)
    lane_f = lane.astype(F32)

    def onehot(slot):
        pos = jnp.sum(jnp.where(lane == exp_ref[slot], pos_ref[...], 0.0), axis=-1, keepdims=True)
        return jnp.where(pos - (rb_ref[slot] * MOE_BLK).astype(F32) == lane_f, 1.0, 0.0).astype(BF16)

    s0 = i * ns + 2 * s
    pt = jnp.concatenate([onehot(s0), onehot(s0 + 1)], axis=1)
    acc_ref[...] += _dot(pt, jnp.concatenate([ya_ref[...], yb_ref[...]], axis=0))

    @pl.when(s == ns // 2 - 1)
    def _():
        o_ref[...] = x_ref[...] + mod_ref[ROW_GA_FFN:ROW_GA_FFN + 1, :] * acc_ref[...]


def _moe_scatter(yg, pos, x, mod, plan, seq):
    m, d = x.shape
    ns = plan["s_max"]
    tpb = seq // MOE_TM
    kern = functools.partial(_moe_scatter_kernel, ns=ns)
    return pl.pallas_call(
        kern,
        grid_spec=pltpu.PrefetchScalarGridSpec(
            num_scalar_prefetch=3,
            grid=(m // MOE_TM, ns // 2),
            in_specs=[
                pl.BlockSpec((MOE_BLK, d), lambda i, s, sr, ex, rb: (sr[i * ns + 2 * s], 0)),
                pl.BlockSpec((MOE_BLK, d), lambda i, s, sr, ex, rb: (sr[i * ns + 2 * s + 1], 0)),
                pl.BlockSpec((MOE_TM, LANES), lambda i, s, sr, ex, rb: (i, 0)),
                pl.BlockSpec((MOE_TM, d), lambda i, s, sr, ex, rb: (i, 0)),
                pl.BlockSpec((None, 6, d), lambda i, s, sr, ex, rb: (i // tpb, 0, 0)),
            ],
            out_specs=pl.BlockSpec((MOE_TM, d), lambda i, s, sr, ex, rb: (i, 0)),
            scratch_shapes=[pltpu.VMEM((MOE_TM, d), F32)],
        ),
        out_shape=jax.ShapeDtypeStruct((m, d), F32),
        compiler_params=_cparams(2),
        name="moe_scatter",
    )(plan["s_src"], plan["s_exp"], plan["s_rb"], yg, yg, pos, x, mod)


def _moe_layer(x, mod, g, w_router, b_router, w1, w3, w2, wl, seq, fc):
    m = x.shape[0]
    cmbt, pos, post, counts, h = _router(x, mod, g, w_router, b_router, seq)
    plan = _moe_plan(counts[:, 0, :w_router.shape[1]].astype(jnp.int32), m)
    xg, row_w = _moe_gather(h, post, cmbt, plan)
    yg = _moe_ffn(xg, row_w, w1, w3, w2, wl, plan, fc)
    return _moe_scatter(yg, pos, x, mod, plan, seq)


def _pad_w_in(w):
    wt = jnp.swapaxes(w, 1, 2)
    gap = jnp.zeros((w.shape[0], CB_MERGE * LANES - RAW_MERGE_COL, w.shape[1]), w.dtype)
    q0, q1 = CB_SB_Q * LANES, CB_SB_K * LANES
    return jnp.concatenate([wt[:, :q0], wt[:, q0:q1] * SB_Q_SCALE, wt[:, q1:RAW_MERGE_COL], gap,
                            wt[:, RAW_MERGE_COL:]], axis=1).astype(BF16)


def _kv_chunks(proj3):
    b, s, _ = proj3.shape
    n = 2 * NSA_KV_HEADS
    t = proj3[:, :, CB_KC * LANES:(CB_KC + n) * LANES].reshape(b, s, n, HEAD_DIM)
    return t.transpose(0, 2, 1, 3).reshape(b, n, s // CMP_STRIDE, CMP_STRIDE * HEAD_DIM)


def kernel(x, c, w_ada, b_ada, g_mix, g_ffn, w_in, conv_w, conv_b, conv_ln_g, conv_ln_b, w_conv_out, w_sb_out,
           nsa_cmp_pos_k, nsa_cmp_pos_v, nsa_cmp_wk, nsa_cmp_wv, nsa_q_g, nsa_kc_g, nsa_ks_g, nsa_kw_g, w_nsa_out, w_o,
           rel_bias, ffn_w1, ffn_w3, ffn_w2, moe_router, moe_router_b, moe_w1, moe_w3, moe_w2):
    b, s, d = x.shape
    m = b * s
    depth = w_ada.shape[0]
    mod_all = _ada_all(c, w_ada, b_ada)
    bias_c, bias_t = _bias_tables(rel_bias, s)
    w_in_p = _pad_w_in(w_in)
    w_conv_out, w_sb_out, w_nsa_out, w_o, ffn_w1, ffn_w3, ffn_w2, moe_w1, moe_w3, moe_w2 = (
        w.astype(BF16) for w in (w_conv_out, w_sb_out, w_nsa_out, w_o, ffn_w1, ffn_w3, ffn_w2, moe_w1, moe_w3, moe_w2))
    xf = x.reshape(m, d)
    for l in range(depth):
        mod = mod_all[l].reshape(b, 6, d)
        proj = _norm_matmul(xf, mod, g_mix[l], w_in_p, l, ROW_SH_MIX, ROW_SC_MIX, s)
        proj3 = proj.reshape(b, s, NP)
        u_act = _conv_module(proj, conv_w[l], conv_b[l], conv_ln_g[l], conv_ln_b[l], b, s)
        sb = _sb_attention(proj3)
        kcb, vcb, ksn, kwn = _nsa_prep(proj3, nsa_cmp_pos_k[l], nsa_cmp_pos_v[l], nsa_cmp_wk[l], nsa_cmp_wv[l],
                                       nsa_kc_g[l], nsa_ks_g[l], nsa_kw_g[l])
        nsa = _nsa_attention(proj3, kcb, vcb, ksn, kwn, bias_c, bias_t, nsa_q_g[l])
        merged = _merge(u_act, sb.reshape(m, -1), nsa.reshape(m, -1), proj, w_conv_out, w_sb_out, w_nsa_out, l)
        xf = _mm_res(merged, w_o, l, xf, mod, ROW_GA_MIX, s)
        i = l // 2
        if l % 2 == 0:
            act = _ffn_up(xf, mod, g_ffn[l], ffn_w1, ffn_w3, i, s)
            xf = _mm_res(act, ffn_w2, i, xf, mod, ROW_GA_FFN, s, tm=512, tn=512)
        else:
            xf = _moe_layer(xf, mod, g_ffn[l], moe_router[i], moe_router_b[i], moe_w1, moe_w3, moe_w2, i, s, MOE_FC)
    return xf.reshape(b, s, d)
```

```python
import functools
import math

import numpy as np
import jax
import jax.numpy as jnp
from jax import lax
from jax.experimental import pallas as pl
from jax.experimental.pallas import tpu as pltpu

F32 = jnp.float32
BF16 = jnp.bfloat16

D_MODEL = 2048
DEPTH = 4
D_CONV = 1024
CONV_WIDTH = 31
SB_HEADS = 8
HEAD_DIM = 128
NSA_HEADS = 8
NSA_KV_HEADS = 2
NSA_REP = NSA_HEADS // NSA_KV_HEADS
CMP_LEN = 32
CMP_STRIDE = 16
SLC_LEN = 64
SLC_TOP_N = 16
N_LOCAL_BLOCKS = 2
WINDOW = 512
FORCE_SCORE = 1e4
REL_BUCKETS = 32
REL_MAX_DIST = 128
D_FF = 5632
N_EXPERTS = 8
D_FF_EXPERT = 2816
EPS = 1e-6
NEG_INF = -1e30
TINY = 1e-20

LANES = 128
SUBLANES = 8
V7X_VMEM_LIMIT_BYTES = 56 * 1024 * 1024

NP_BLOCKS = 112
NP = NP_BLOCKS * LANES
CB_GLU_A, CB_GLU_G = 0, 8
CB_SB_Q, CB_SB_K, CB_SB_V = 16, 24, 32
CB_NQ = 40
CB_KC, CB_VC, CB_KS, CB_VS, CB_KW, CB_VW = 48, 50, 52, 54, 56, 58
CB_NGATE = 60
CB_MERGE = 64
RAW_GATE_COL = 7680
RAW_MERGE_COL = 7704
ROW_SH_MIX, ROW_SC_MIX, ROW_GA_MIX, ROW_SH_FFN, ROW_SC_FFN, ROW_GA_FFN = range(6)

SCALE = HEAD_DIM ** -0.5
SB_Q_SCALE = SCALE * math.log2(math.e)
TQ = 128
SB_TILE = 256
SB_GROUP = 8
SEL_CHUNK = 512
NSA_QT = 256
NSA_SUB = NSA_QT // TQ
WIN_SPAN = WINDOW + NSA_QT
BT_DIAG, BT_SUB, BT_FAR, BT_FAR_UPPER, BT_MASKED = range(5)
CONV_HALO = 32
MOE_TM = 1024
MOE_BLK = 128
MOE_MT = 512
MOE_FC = 1408
MOE_SUB = 256
MOE_NEVER = 1 << 20


def _cparams(n_axes):
    return pltpu.CompilerParams(
        dimension_semantics=("arbitrary",) * n_axes,
        vmem_limit_bytes=V7X_VMEM_LIMIT_BYTES,
    )


def _dot(a, b):
    return jnp.dot(a, b, preferred_element_type=F32)


def _dot_t(a, b):
    return lax.dot_general(a, b, (((1,), (1,)), ((), ())), preferred_element_type=F32)


def _split_dot(x, w01):
    hi = x.astype(BF16)
    lo = (x - hi.astype(F32)).astype(BF16)
    return _dot(hi, w01) + _dot(lo, w01)


def _norm_mod(x, g, sh, sc):
    ms = jnp.mean(x * x, axis=-1, keepdims=True)
    return (x * lax.rsqrt(ms + EPS) * g) * (1.0 + sc) + sh


def _rms(x, g):
    ms = jnp.mean(x * x, axis=-1, keepdims=True)
    return x * lax.rsqrt(ms + EPS) * g


def _ada_kernel(c_ref, w_ref, b_ref, o_ref):
    c = c_ref[...]
    ca = (c * jax.nn.sigmoid(c)).astype(BF16)
    o_ref[...] = _dot(ca, w_ref[...].astype(BF16)) + b_ref[...]


def _ada_all(c, w_ada, b_ada):
    depth, d, n = w_ada.shape
    nb = c.shape[0]
    b = 16
    c = jnp.zeros((b, d), c.dtype).at[:nb].set(c)
    tn = 1024
    out = pl.pallas_call(
        _ada_kernel,
        grid=(depth, n // tn),
        in_specs=[
            pl.BlockSpec((b, d), lambda l, j: (0, 0)),
            pl.BlockSpec((None, d, tn), lambda l, j: (l, 0, j)),
            pl.BlockSpec((None, 1, tn), lambda l, j: (l, 0, j)),
        ],
        out_specs=pl.BlockSpec((None, b, tn), lambda l, j: (l, 0, j)),
        out_shape=jax.ShapeDtypeStruct((depth, b, n), F32),
        compiler_params=_cparams(2),
        name="ada",
    )(c, w_ada, b_ada.reshape(depth, 1, n))
    return out[:, :nb]


def _bias_kernel(tab_ref, bk_ref, o_ref):
    h = pl.program_id(0)
    bk = bk_ref[...]
    acc = jnp.zeros(bk.shape, F32)
    for b in range(REL_BUCKETS):
        acc = jnp.where(bk == b, tab_ref[b, h], acc)
    o_ref[...] = acc


def _bias_expand(rel_bias, buckets):
    r = buckets.shape[0]
    tr = r
    return pl.pallas_call(
        _bias_kernel,
        grid=(NSA_HEADS, r // tr),
        in_specs=[
            pl.BlockSpec(memory_space=pltpu.SMEM),
            pl.BlockSpec((tr, LANES), lambda h, i: (i, 0)),
        ],
        out_specs=pl.BlockSpec((None, tr, LANES), lambda h, i: (h, i, 0)),
        out_shape=jax.ShapeDtypeStruct((NSA_HEADS, r, LANES), F32),
        compiler_params=_cparams(2),
        name="bias_expand",
    )(rel_bias, buckets)


def _rel_bucket(dist):
    n = jnp.maximum(dist, 0)
    max_exact = REL_BUCKETS // 2
    nf = jnp.maximum(n, 1).astype(F32)
    large = max_exact + (jnp.log(nf / max_exact) / math.log(REL_MAX_DIST / max_exact) * (REL_BUCKETS - max_exact)).astype(jnp.int32)
    large = jnp.minimum(large, REL_BUCKETS - 1)
    return jnp.where(n < max_exact, n, large)


def _bias_tables(rel_bias, s):
    t = jnp.arange(s, dtype=jnp.int32)[:, None]
    cend = jnp.arange(LANES, dtype=jnp.int32)[None, :] * CMP_STRIDE + (CMP_LEN - 1)
    q = jnp.arange(TQ, dtype=jnp.int32)[:, None]
    k = jnp.arange(TQ, dtype=jnp.int32)[None, :]
    bk_t = jnp.concatenate([_rel_bucket(q - k), _rel_bucket(TQ + q - k), _rel_bucket(2 * TQ + q - k)], axis=0)
    out = _bias_expand(rel_bias, jnp.concatenate([_rel_bucket(t - cend), bk_t], axis=0))
    bias_c = jnp.where(t >= cend, out[:, :s], NEG_INF).reshape(NSA_KV_HEADS, NSA_REP, s, LANES)
    diag, sub, far = out[:, s:s + TQ], out[:, s + TQ:s + 2 * TQ], out[:, s + 2 * TQ:]
    tiles = jnp.stack([jnp.where(q >= k, diag, NEG_INF), sub, far, jnp.where(k > q, far, NEG_INF),
                       jnp.full_like(far, NEG_INF)], axis=1)
    tiles = tiles.reshape(NSA_KV_HEADS, NSA_REP, 5, TQ, TQ).transpose(0, 2, 1, 3, 4)
    return bias_c, tiles.reshape(NSA_KV_HEADS, 5, NSA_REP * TQ, TQ)


def _norm_matmul_kernel(x_ref, mod_ref, g_ref, w_ref, o_ref, h_ref, *, row_sh, row_sc):
    @pl.when(pl.program_id(1) == 0)
    def _():
        h = _norm_mod(x_ref[...], g_ref[...], mod_ref[row_sh:row_sh + 1, :], mod_ref[row_sc:row_sc + 1, :])
        h_ref[...] = h.astype(BF16)

    o_ref[...] = _dot_t(h_ref[...], w_ref[...]).astype(o_ref.dtype)


def _norm_matmul(x, mod, g, wt, wl, row_sh, row_sc, seq, tm=1024, tn=2048):
    m, d = x.shape
    n = wt.shape[1]
    tpb = seq // tm
    kern = functools.partial(_norm_matmul_kernel, row_sh=row_sh, row_sc=row_sc)
    return pl.pallas_call(
        kern,
        grid=(m // tm, n // tn),
        in_specs=[
            pl.BlockSpec((tm, d), lambda i, j: (i, 0)),
            pl.BlockSpec((None, 6, d), lambda i, j: (i // tpb, 0, 0)),
            pl.BlockSpec((1, d), lambda i, j: (0, 0)),
            pl.BlockSpec((None, tn, d), lambda i, j: (wl, j, 0)),
        ],
        out_specs=pl.BlockSpec((tm, tn), lambda i, j: (i, j)),
        out_shape=jax.ShapeDtypeStruct((m, n), BF16),
        scratch_shapes=[pltpu.VMEM((tm, d), BF16)],
        compiler_params=_cparams(2),
        name="norm_matmul",
    )(x, mod, g.reshape(1, d), wt)


def _conv_kernel(a_ref, g_ref, ap_ref, gp_ref, cw_ref, cb_ref, lng_ref, lnb_ref, o_ref, ubuf, vbuf, shifted, *, ts):
    i = pl.program_id(1)
    ubuf[CONV_HALO:, :] = a_ref[...].astype(F32) * jax.nn.sigmoid(g_ref[...].astype(F32))
    up = ap_ref[...].astype(F32) * jax.nn.sigmoid(gp_ref[...].astype(F32))
    ubuf[:CONV_HALO, :] = jnp.where(i > 0, up, 0.0)

    first = CONV_HALO - (CONV_WIDTH - 1)
    span = ts + CONV_HALO - SUBLANES

    def chunk(c, carry):
        c0 = pl.multiple_of(c * LANES, LANES)
        for s in range(1, SUBLANES):
            shifted[s - 1] = ubuf[pl.ds(s, span), pl.ds(c0, LANES)]
        acc = jnp.zeros((ts, LANES), F32) + cb_ref[:, pl.ds(c0, LANES)]
        for k in range(CONV_WIDTH):
            a, s = divmod(first + k, SUBLANES)
            rows = ubuf[pl.ds(a * SUBLANES, ts), pl.ds(c0, LANES)] if s == 0 else shifted[s - 1, pl.ds(a * SUBLANES, ts), :]
            acc = acc + cw_ref[k:k + 1, pl.ds(c0, LANES)] * rows
        vbuf[:, pl.ds(c0, LANES)] = acc
        return carry

    lax.fori_loop(0, D_CONV // LANES, chunk, 0)
    v = vbuf[...]
    mu = jnp.mean(v, axis=-1, keepdims=True)
    vc = v - mu
    var = jnp.mean(vc * vc, axis=-1, keepdims=True)
    y = vc * lax.rsqrt(var + EPS) * lng_ref[...] + lnb_ref[...]
    o_ref[...] = (y * jax.nn.sigmoid(y)).astype(o_ref.dtype)


def _conv_module(proj, conv_w, conv_b, ln_g, ln_b, batch, seq, ts=256):
    m = proj.shape[0]
    nt = seq // ts
    hb = ts // CONV_HALO
    kern = functools.partial(_conv_kernel, ts=ts)

    def prev_idx(col):
        return lambda b, i: (jnp.maximum((b * nt + i) * hb - 1, 0), col)

    return pl.pallas_call(
        kern,
        grid=(batch, nt),
        in_specs=[
            pl.BlockSpec((ts, D_CONV), lambda b, i: (b * nt + i, 0)),
            pl.BlockSpec((ts, D_CONV), lambda b, i: (b * nt + i, 1)),
            pl.BlockSpec((CONV_HALO, D_CONV), prev_idx(0)),
            pl.BlockSpec((CONV_HALO, D_CONV), prev_idx(1)),
            pl.BlockSpec((CONV_WIDTH, D_CONV), lambda b, i: (0, 0)),
            pl.BlockSpec((1, D_CONV), lambda b, i: (0, 0)),
            pl.BlockSpec((1, D_CONV), lambda b, i: (0, 0)),
            pl.BlockSpec((1, D_CONV), lambda b, i: (0, 0)),
        ],
        out_specs=pl.BlockSpec((ts, D_CONV), lambda b, i: (b * nt + i, 0)),
        out_shape=jax.ShapeDtypeStruct((m, D_CONV), BF16),
        scratch_shapes=[pltpu.VMEM((CONV_HALO + ts, D_CONV), F32), pltpu.VMEM((ts, D_CONV), F32),
                        pltpu.VMEM((SUBLANES - 1, ts + CONV_HALO - SUBLANES, LANES), F32)],
        compiler_params=_cparams(2),
        name="conv_module",
    )(proj, proj, proj, proj, conv_w, conv_b.reshape(1, -1), ln_g.reshape(1, -1), ln_b.reshape(1, -1))


def _sb_kernel(q_ref, k_ref, v_ref, o_ref):
    i = pl.program_id(2)
    t = SB_TILE
    rows = SB_GROUP * t
    row = lax.broadcasted_iota(jnp.int32, (t, t), 0)
    col = lax.broadcasted_iota(jnp.int32, (t, t), 1)
    later = jnp.where(row > col, 1.0, 0.0).astype(BF16)
    heads = [slice(h * HEAD_DIM, (h + 1) * HEAD_DIM) for h in range(SB_GROUP)]

    def tile(j, rsum, acc, diag):
        k0 = pl.multiple_of(j * t, t)
        z = jnp.concatenate([_dot_t(q_ref[:, hs], k_ref[pl.ds(k0, t), hs]) for hs in heads], axis=0)
        if diag:
            qpos = lax.broadcasted_iota(jnp.int32, (rows, t), 0) & (t - 1)
            z = jnp.where(lax.broadcasted_iota(jnp.int32, (rows, t), 1) < qpos, z, NEG_INF)
        sp = jnp.log2(1.0 + jnp.exp2(-jnp.abs(z)))
        log_beta = jnp.minimum(z, 0.0) - sp
        log_keep = log_beta - z
        log_survive = _dot(log_keep.astype(BF16), later) + rsum
        a = jnp.exp2(log_beta + log_survive).astype(BF16)
        pv = [_dot(a[h * t:(h + 1) * t], v_ref[pl.ds(k0, t), hs]) for h, hs in enumerate(heads)]
        acc = acc + jnp.concatenate(pv, axis=0)
        rsum = rsum + jnp.sum(log_keep, axis=-1, keepdims=True)
        return rsum, acc

    carry = tile(i, jnp.zeros((rows, 1), F32), jnp.zeros((rows, HEAD_DIM), F32), True)
    _, acc = lax.fori_loop(0, i, lambda jj, c: tile(i - 1 - jj, c[0], c[1], False), carry)
    for h in range(SB_GROUP):
        o_ref[:, h * HEAD_DIM:(h + 1) * HEAD_DIM] = acc[h * t:(h + 1) * t].astype(o_ref.dtype)


def _sb_attention(proj3):
    b, s, _ = proj3.shape
    gw = SB_GROUP * HEAD_DIM
    return pl.pallas_call(
        _sb_kernel,
        grid=(b, SB_HEADS // SB_GROUP, s // SB_TILE),
        in_specs=[
            pl.BlockSpec((None, SB_TILE, gw), lambda bi, hg, i: (bi, i, CB_SB_Q // SB_GROUP + hg)),
            pl.BlockSpec((None, s, gw), lambda bi, hg, i: (bi, 0, CB_SB_K // SB_GROUP + hg)),
            pl.BlockSpec((None, s, gw), lambda bi, hg, i: (bi, 0, CB_SB_V // SB_GROUP + hg)),
        ],
        out_specs=pl.BlockSpec((None, SB_TILE, gw), lambda bi, hg, i: (bi, i, hg)),
        out_shape=jax.ShapeDtypeStruct((b, s, SB_HEADS * HEAD_DIM), BF16),
        compiler_params=_cparams(3),
        name="sb_attention",
    )(proj3, proj3, proj3)


def _nsa_prep_kernel(kc_ref, vc_ref, ks_ref, kw_ref, pk_ref, pv_ref, wk_ref, wv_ref, kcg_ref, ksg_ref, kwg_ref,
                     kcb_ref, vcb_ref, ksn_ref, kwn_ref):
    half = CMP_STRIDE * HEAD_DIM

    def compress(a_ref, p_ref, w_ref):
        a = a_ref[...].astype(F32)
        top = _dot((a + p_ref[0:1, :]).astype(BF16), w_ref[0:half, :])
        bot = _dot((a + p_ref[1:2, :]).astype(BF16), w_ref[half:2 * half, :])
        return top + pltpu.roll(bot, bot.shape[0] - 1, axis=0)

    kcb_ref[...] = _rms(compress(kc_ref, pk_ref, wk_ref), kcg_ref[...]).astype(BF16)
    vcb_ref[...] = compress(vc_ref, pv_ref, wv_ref).astype(BF16)
    kwn_ref[...] = _rms(kw_ref[...].astype(F32), kwg_ref[...]).astype(BF16)
    ksn_ref[:, 0:HEAD_DIM] = _rms(ks_ref[...].astype(F32), ksg_ref[...]).astype(BF16)
    s = ks_ref.shape[0]
    blk = jnp.right_shift(lax.broadcasted_iota(jnp.int32, (s, LANES), 0), int(math.log2(SLC_LEN)))
    onehot = blk == lax.broadcasted_iota(jnp.int32, (s, LANES), 1)
    ksn_ref[:, HEAD_DIM:HEAD_DIM + LANES] = jnp.where(onehot, 1.0, 0.0).astype(BF16)


def _nsa_prep(proj3, pos_k, pos_v, wk, wv, kc_g, ks_g, kw_g):
    b, s, _ = proj3.shape
    g_n = NSA_KV_HEADS
    kv_chunks = _kv_chunks(proj3)
    nch = s // CMP_STRIDE
    half = CMP_STRIDE * HEAD_DIM
    vec = lambda: pl.BlockSpec((1, HEAD_DIM), lambda bi, g: (0, 0))
    small = jax.ShapeDtypeStruct((b, g_n, nch, HEAD_DIM), BF16)
    full = jax.ShapeDtypeStruct((b, g_n, s, HEAD_DIM), BF16)
    aug = jax.ShapeDtypeStruct((b, g_n, s, HEAD_DIM + LANES), BF16)
    return pl.pallas_call(
        _nsa_prep_kernel,
        grid=(b, g_n),
        in_specs=[
            pl.BlockSpec((None, None, nch, half), lambda bi, g: (bi, g, 0, 0)),
            pl.BlockSpec((None, None, nch, half), lambda bi, g: (bi, g_n + g, 0, 0)),
            pl.BlockSpec((None, s, HEAD_DIM), lambda bi, g: (bi, 0, CB_KS + g)),
            pl.BlockSpec((None, s, HEAD_DIM), lambda bi, g: (bi, 0, CB_KW + g)),
            pl.BlockSpec((2, half), lambda bi, g: (0, 0)),
            pl.BlockSpec((2, half), lambda bi, g: (0, 0)),
            pl.BlockSpec((2 * half, HEAD_DIM), lambda bi, g: (0, 0)),
            pl.BlockSpec((2 * half, HEAD_DIM), lambda bi, g: (0, 0)),
            vec(), vec(), vec(),
        ],
        out_specs=[
            pl.BlockSpec((None, None, nch, HEAD_DIM), lambda bi, g: (bi, g, 0, 0)),
            pl.BlockSpec((None, None, nch, HEAD_DIM), lambda bi, g: (bi, g, 0, 0)),
            pl.BlockSpec((None, None, s, HEAD_DIM + LANES), lambda bi, g: (bi, g, 0, 0)),
            pl.BlockSpec((None, None, s, HEAD_DIM), lambda bi, g: (bi, g, 0, 0)),
        ],
        out_shape=[small, small, aug, full],
        compiler_params=_cparams(2),
        name="nsa_prep",
    )(kv_chunks, kv_chunks, proj3, proj3, pos_k.reshape(2, half), pos_v.reshape(2, half),
      wk.astype(BF16), wv.astype(BF16), kc_g.reshape(1, -1), ks_g.reshape(1, -1), kw_g.reshape(1, -1))


def _nsa_kernel(q_ref, kcb_ref, vcb_ref, ksa_ref, vs_ref, kwn_ref, vw_ref, gate_ref, bc_ref, bt_ref, qg_ref, ovt_ref,
                o_ref):
    g = pl.program_id(1)
    i = pl.program_id(2)
    r_n = NSA_REP
    groups = [(a, r) for a in range(NSA_SUB) for r in range(r_n)]
    sub = lambda a: slice(a * TQ, (a + 1) * TQ)
    head = lambda r: slice(r * HEAD_DIM, (r + 1) * HEAD_DIM)

    q = jnp.concatenate([_rms(q_ref[sub(a), head(r)].astype(F32), qg_ref[...]).astype(BF16) for a, r in groups], axis=0)

    logit_c = _dot_t(q, kcb_ref[...]) + jnp.concatenate([bc_ref[r, sub(a), :] for a, r in groups], axis=0)
    m_c = jnp.max(logit_c, axis=-1, keepdims=True)
    p_c = jnp.where(logit_c > 0.5 * NEG_INF, jnp.exp(logit_c - m_c), 0.0)
    p_c = p_c / jnp.maximum(jnp.sum(p_c, axis=-1, keepdims=True), TINY)
    o_c = _dot(p_c.astype(BF16), vcb_ref[...])

    p_sum = jnp.concatenate(
        [sum(p_c[(a * r_n + r) * TQ:(a * r_n + r + 1) * TQ] for r in range(r_n)) for a in range(NSA_SUB)], axis=0)
    hi = p_sum.astype(BF16)
    lo = (p_sum - hi.astype(F32)).astype(BF16)
    n_sel = ovt_ref.shape[0]
    imp = _dot_t(ovt_ref[...], hi) + _dot_t(ovt_ref[...], lo)
    blk = lax.broadcasted_iota(jnp.int32, (n_sel, NSA_QT), 0)
    tb = jnp.right_shift(i * NSA_QT + lax.broadcasted_iota(jnp.int32, (n_sel, NSA_QT), 1), int(math.log2(SLC_LEN)))
    causal_blk = blk <= tb
    forced = (blk == 0) | (causal_blk & (blk > tb - N_LOCAL_BLOCKS))
    score = jnp.where(forced, FORCE_SCORE, jnp.where(causal_blk, imp, -FORCE_SCORE))
    rank = jnp.zeros((n_sel, NSA_QT), F32)
    for b in range(n_sel):
        sb = score[b:b + 1, :]
        beats = (sb > score) | ((sb == score) & (blk > b))
        rank = rank + jnp.where(beats, 1.0, 0.0)
    chosen = (rank < float(min(SLC_TOP_N, n_sel))) & causal_blk
    sel_neg = jnp.where(chosen, 0.0, NEG_INF)
    sel_neg = jnp.concatenate([sel_neg, jnp.zeros((LANES - n_sel, NSA_QT), F32)], axis=0).T.astype(BF16)
    q_aug = jnp.concatenate([q, jnp.concatenate([sel_neg[sub(a)] for a, _ in groups], axis=0)], axis=1)

    def biased(s, first_tile, idx_fn):
        return jnp.concatenate(
            [s[:, c * TQ:(c + 1) * TQ]
             + jnp.concatenate([bt_ref[idx_fn(i * NSA_SUB + a - (first_tile + c))] for a in range(NSA_SUB)], axis=0)
             for c in range(s.shape[1] // TQ)], axis=1)

    w_tile = jnp.maximum(i * NSA_SUB - WINDOW // TQ, 0)
    w0 = pl.multiple_of(w_tile * TQ, TQ)

    def win_idx(off):
        near = jnp.where(off == WINDOW // TQ, BT_FAR_UPPER, jnp.minimum(off, BT_FAR))
        return jnp.where((off < 0) | (off > WINDOW // TQ), BT_MASKED, near)

    s_w = biased(_dot_t(q, kwn_ref[pl.ds(w0, WIN_SPAN), :]), w_tile, win_idx)
    p_w = jnp.exp(s_w - jnp.max(s_w, axis=-1, keepdims=True))
    o_w = _dot(p_w.astype(BF16), vw_ref[pl.ds(w0, WIN_SPAN), :]) / jnp.sum(p_w, axis=-1, keepdims=True)

    def sel_chunk(kc, carry, diag):
        c0 = pl.multiple_of(kc * SEL_CHUNK, SEL_CHUNK)
        idx_fn = (lambda off: jnp.where(off < 0, BT_MASKED, jnp.minimum(off, BT_FAR))) if diag else (
            lambda off: jnp.minimum(off, BT_FAR))
        s = biased(_dot_t(q_aug, ksa_ref[pl.ds(c0, SEL_CHUNK), :]), kc * (SEL_CHUNK // TQ), idx_fn)
        m_blk = jnp.max(s, axis=-1, keepdims=True)
        if diag:
            p = jnp.exp(s - m_blk)
            return m_blk, jnp.sum(p, axis=-1, keepdims=True), _dot(p.astype(BF16), vs_ref[pl.ds(c0, SEL_CHUNK), :])
        m, l, acc = carry
        m_new = jnp.maximum(m, m_blk)
        p = jnp.exp(s - m_new)
        alpha = jnp.exp(m - m_new)
        l = alpha * l + jnp.sum(p, axis=-1, keepdims=True)
        acc = alpha * acc + _dot(p.astype(BF16), vs_ref[pl.ds(c0, SEL_CHUNK), :])
        return m_new, l, acc

    kc_diag = lax.shift_right_logical(i * NSA_SUB + (NSA_SUB - 1), jnp.int32(int(math.log2(SEL_CHUNK // TQ))))
    carry = sel_chunk(kc_diag, None, True)
    _, l_s, acc_s = lax.fori_loop(0, kc_diag, lambda jj, c: sel_chunk(kc_diag - 1 - jj, c, False), carry)
    o_s = acc_s / l_s

    gates = jax.nn.sigmoid(gate_ref[...].astype(F32))
    lane = lax.broadcasted_iota(jnp.int32, gates.shape, 1)

    def gate_col(idx):
        return jnp.sum(jnp.where(lane == idx, gates, 0.0), axis=-1, keepdims=True)

    for r in range(r_n):
        h = g * r_n + r
        g_c, g_s, g_w = gate_col(h), gate_col(NSA_HEADS + h), gate_col(2 * NSA_HEADS + h)
        for a in range(NSA_SUB):
            k = slice((a * r_n + r) * TQ, (a * r_n + r + 1) * TQ)
            o = g_c[sub(a)] * o_c[k] + g_s[sub(a)] * o_s[k] + g_w[sub(a)] * o_w[k]
            o_ref[sub(a), head(r)] = o.astype(o_ref.dtype)


def _overlap_matrix(s):
    nch = s // CMP_STRIDE
    nsel = s // SLC_LEN
    ci = np.arange(nch)[:, None]
    sj = np.arange(nsel)[None, :]
    ov = (ci * CMP_STRIDE <= sj * SLC_LEN + SLC_LEN - 1) & (ci * CMP_STRIDE + CMP_LEN - 1 >= sj * SLC_LEN)
    ov = ov & (ci < nch - 1)
    return jnp.asarray(ov.T, BF16)


def _nsa_attention(proj3, kcb, vcb, ksa, kwn, bias_c, bias_t, q_g):
    b, s, _ = proj3.shape
    g_n, r_n = NSA_KV_HEADS, NSA_REP
    nch = s // CMP_STRIDE
    nsel = s // SLC_LEN
    gw = r_n * HEAD_DIM
    return pl.pallas_call(
        _nsa_kernel,
        grid=(b, g_n, s // NSA_QT),
        in_specs=[
            pl.BlockSpec((None, NSA_QT, gw), lambda bi, g, i: (bi, i, CB_NQ // r_n + g)),
            pl.BlockSpec((None, None, nch, HEAD_DIM), lambda bi, g, i: (bi, g, 0, 0)),
            pl.BlockSpec((None, None, nch, HEAD_DIM), lambda bi, g, i: (bi, g, 0, 0)),
            pl.BlockSpec((None, None, s, HEAD_DIM + LANES), lambda bi, g, i: (bi, g, 0, 0)),
            pl.BlockSpec((None, s, HEAD_DIM), lambda bi, g, i: (bi, 0, CB_VS + g)),
            pl.BlockSpec((None, None, s, HEAD_DIM), lambda bi, g, i: (bi, g, 0, 0)),
            pl.BlockSpec((None, s, HEAD_DIM), lambda bi, g, i: (bi, 0, CB_VW + g)),
            pl.BlockSpec((None, NSA_QT, LANES), lambda bi, g, i: (bi, i, CB_NGATE)),
            pl.BlockSpec((None, r_n, NSA_QT, LANES), lambda bi, g, i: (g, 0, i, 0)),
            pl.BlockSpec((None, 5, r_n * TQ, TQ), lambda bi, g, i: (g, 0, 0, 0)),
            pl.BlockSpec((1, HEAD_DIM), lambda bi, g, i: (0, 0)),
            pl.BlockSpec((nsel, nch), lambda bi, g, i: (0, 0)),
        ],
        out_specs=pl.BlockSpec((None, NSA_QT, gw), lambda bi, g, i: (bi, i, g)),
        out_shape=jax.ShapeDtypeStruct((b, s, NSA_HEADS * HEAD_DIM), BF16),
        compiler_params=_cparams(3),
        name="nsa_attention",
    )(proj3, kcb, vcb, ksa, proj3, kwn, proj3, proj3, bias_c, bias_t, q_g.reshape(1, -1) * SCALE, _overlap_matrix(s))


def _merge_kernel(u_ref, sb_ref, ns_ref, gc_ref, gs_ref, gn_ref, wc_ref, ws_ref, wn_ref, o_ref):
    m = jax.nn.sigmoid(gc_ref[...].astype(F32)) * _dot(u_ref[...], wc_ref[...])
    m = m + jax.nn.sigmoid(gs_ref[...].astype(F32)) * _dot(sb_ref[...], ws_ref[...])
    m = m + jax.nn.sigmoid(gn_ref[...].astype(F32)) * _dot(ns_ref[...], wn_ref[...])
    o_ref[...] = m.astype(o_ref.dtype)


def _merge(u_act, sb, nsa, proj, wc, ws, wn, wl, tm=512, tn=2048):
    m = u_act.shape[0]
    d = wc.shape[2]
    kc = wc.shape[1]
    gate0 = CB_MERGE * LANES // tn
    gstep = d // tn
    act = lambda: pl.BlockSpec((tm, kc), lambda i, j: (i, 0))
    gate = lambda n: pl.BlockSpec((tm, tn), lambda i, j: (i, gate0 + n * gstep + j))
    wsp = lambda: pl.BlockSpec((None, kc, tn), lambda i, j: (wl, 0, j))
    return pl.pallas_call(
        _merge_kernel,
        grid=(m // tm, d // tn),
        in_specs=[act(), act(), act(), gate(0), gate(1), gate(2), wsp(), wsp(), wsp()],
        out_specs=pl.BlockSpec((tm, tn), lambda i, j: (i, j)),
        out_shape=jax.ShapeDtypeStruct((m, d), BF16),
        compiler_params=_cparams(2),
        name="merge",
    )(u_act, sb, nsa, proj, proj, proj, wc, ws, wn)


def _mm_res_kernel(a_ref, w_ref, x_ref, mod_ref, o_ref, acc_ref, *, row_ga, nk):
    k = pl.program_id(2)

    @pl.when(k == 0)
    def _():
        acc_ref[...] = jnp.zeros_like(acc_ref)

    acc_ref[...] += _dot(a_ref[...], w_ref[...])

    @pl.when(k == nk - 1)
    def _():
        o_ref[...] = x_ref[...] + mod_ref[row_ga:row_ga + 1, :] * acc_ref[...]


def _mm_res(a, w, wl, x, mod, row_ga, seq, tm=1024, tn=1024, tk=None):
    m, ka = a.shape
    n = x.shape[1]
    tpb = seq // tm
    tk = tk or ka
    nk = ka // tk
    w_spec = pl.BlockSpec((None, tk, tn), lambda i, j, k: (wl, k, j))
    kern = functools.partial(_mm_res_kernel, row_ga=row_ga, nk=nk)
    return pl.pallas_call(
        kern,
        grid=(m // tm, n // tn, nk),
        in_specs=[
            pl.BlockSpec((tm, tk), lambda i, j, k: (i, k)),
            w_spec,
            pl.BlockSpec((tm, tn), lambda i, j, k: (i, j)),
            pl.BlockSpec((None, 6, tn), lambda i, j, k: (i // tpb, 0, j)),
        ],
        out_specs=pl.BlockSpec((tm, tn), lambda i, j, k: (i, j)),
        out_shape=jax.ShapeDtypeStruct((m, n), F32),
        scratch_shapes=[pltpu.VMEM((tm, tn), F32)],
        compiler_params=_cparams(3),
        name="matmul_residual",
    )(a, w, x, mod)


def _ffn_up_kernel(x_ref, mod_ref, g_ref, w1_ref, w3_ref, o_ref, h_ref):
    @pl.when(pl.program_id(1) == 0)
    def _():
        h = _norm_mod(x_ref[...], g_ref[...], mod_ref[ROW_SH_FFN:ROW_SH_FFN + 1, :], mod_ref[ROW_SC_FFN:ROW_SC_FFN + 1, :])
        h_ref[...] = h.astype(BF16)

    h = h_ref[...]
    a = _dot(h, w1_ref[...])
    o_ref[...] = (a * jax.nn.sigmoid(a) * _dot(h, w3_ref[...])).astype(o_ref.dtype)


def _ffn_up(x, mod, g, w1, w3, wl, seq, tm=1024, tn=512):
    m, d = x.shape
    f = w1.shape[2]
    tpb = seq // tm
    return pl.pallas_call(
        _ffn_up_kernel,
        grid=(m // tm, f // tn),
        in_specs=[
            pl.BlockSpec((tm, d), lambda i, j: (i, 0)),
            pl.BlockSpec((None, 6, d), lambda i, j: (i // tpb, 0, 0)),
            pl.BlockSpec((1, d), lambda i, j: (0, 0)),
            pl.BlockSpec((None, d, tn), lambda i, j: (wl, 0, j)),
            pl.BlockSpec((None, d, tn), lambda i, j: (wl, 0, j)),
        ],
        out_specs=pl.BlockSpec((tm, tn), lambda i, j: (i, j)),
        out_shape=jax.ShapeDtypeStruct((m, f), BF16),
        scratch_shapes=[pltpu.VMEM((tm, d), BF16)],
        compiler_params=_cparams(2),
        name="ffn_up",
    )(x, mod, g.reshape(1, d), w1, w3)


def _router_kernel(x_ref, mod_ref, g_ref, wr_ref, br_ref, cmbt_ref, pos_ref, post_ref, cnt_ref, h_ref):
    h = _norm_mod(x_ref[...], g_ref[...], mod_ref[ROW_SH_FFN:ROW_SH_FFN + 1, :], mod_ref[ROW_SC_FFN:ROW_SC_FFN + 1, :])
    h_ref[...] = h.astype(BF16)
    logits = jnp.dot(h, wr_ref[...], precision=lax.Precision.HIGHEST, preferred_element_type=F32) + br_ref[...]
    lane = lax.broadcasted_iota(jnp.int32, logits.shape, 1).astype(F32)
    pad = float(LANES)
    m1 = jnp.max(logits, axis=-1, keepdims=True)
    i1 = jnp.min(jnp.where(logits == m1, lane, pad), axis=-1, keepdims=True)
    rest = jnp.where(lane == i1, NEG_INF, logits)
    m2 = jnp.max(rest, axis=-1, keepdims=True)
    i2 = jnp.min(jnp.where(rest == m2, lane, pad), axis=-1, keepdims=True)
    e2 = jnp.exp(m2 - m1)
    w1 = 1.0 / (1.0 + e2)
    w2 = e2 / (1.0 + e2)
    cmb = jnp.where(lane == i1, w1, 0.0) + jnp.where(lane == i2, w2, 0.0)
    cmbt_ref[...] = cmb.T[0:cmbt_ref.shape[0], :]
    sel = jnp.where((lane == i1) | (lane == i2), 1.0, 0.0)
    tm = sel.shape[0]
    earlier = lax.broadcasted_iota(jnp.int32, (tm, tm), 1) < lax.broadcasted_iota(jnp.int32, (tm, tm), 0)
    pos = _dot(jnp.where(earlier, 1.0, 0.0).astype(BF16), sel.astype(BF16))
    pos = jnp.where(sel > 0.0, pos, -1.0)
    pos_ref[...] = pos
    post_ref[...] = pos.T[0:post_ref.shape[0], :]
    cnt_ref[...] = jnp.broadcast_to(jnp.sum(sel, axis=0, keepdims=True), cnt_ref.shape)


def _router(x, mod, g, w_router, b_router, seq):
    m, d = x.shape
    tm = MOE_TM
    n_e = w_router.shape[1]
    tpb = seq // tm
    nt = m // tm
    wr = jnp.zeros((d, LANES), F32).at[:, :n_e].set(w_router)
    br = jnp.full((1, LANES), 2.0 * NEG_INF, F32).at[0, :n_e].set(b_router)
    return pl.pallas_call(
        _router_kernel,
        grid=(nt,),
        in_specs=[
            pl.BlockSpec((tm, d), lambda i: (i, 0)),
            pl.BlockSpec((None, 6, d), lambda i: (i // tpb, 0, 0)),
            pl.BlockSpec((1, d), lambda i: (0, 0)),
            pl.BlockSpec((d, LANES), lambda i: (0, 0)),
            pl.BlockSpec((1, LANES), lambda i: (0, 0)),
        ],
        out_specs=[
            pl.BlockSpec((n_e, tm), lambda i: (0, i)),
            pl.BlockSpec((tm, LANES), lambda i: (i, 0)),
            pl.BlockSpec((n_e, tm), lambda i: (0, i)),
            pl.BlockSpec((None, 8, LANES), lambda i: (i, 0, 0)),
            pl.BlockSpec((tm, d), lambda i: (i, 0)),
        ],
        out_shape=[
            jax.ShapeDtypeStruct((n_e, m), F32),
            jax.ShapeDtypeStruct((m, LANES), F32),
            jax.ShapeDtypeStruct((n_e, m), F32),
            jax.ShapeDtypeStruct((nt, 8, LANES), F32),
            jax.ShapeDtypeStruct((m, d), BF16),
        ],
        compiler_params=_cparams(1),
        name="router",
    )(x, mod, g.reshape(1, d), wr, br)


def _count_le(sorted_vals, queries):
    return jnp.sum(sorted_vals[..., None, :] <= queries[..., :, None], axis=-1).astype(jnp.int32)


def _moe_plan(counts, m):
    t_n, e_n = counts.shape
    per_mt = MOE_MT // MOE_BLK
    g_max = 2 * m // MOE_BLK + t_n * e_n
    pad_max = (per_mt - 1) * e_n
    nblk = (counts + MOE_BLK - 1) // MOE_BLK
    nb_e = jnp.sum(nblk, axis=0)
    cap_e = (nb_e + per_mt - 1) // per_mt * per_mt
    start_e = jnp.cumsum(cap_e) - cap_e
    dst0 = (start_e[None, :] + jnp.cumsum(nblk, axis=0) - nblk).reshape(-1)
    flat = nblk.reshape(-1)
    cum = jnp.cumsum(flat)
    g = jnp.minimum(jnp.arange(g_max, dtype=jnp.int32), cum[-1] - 1)
    p = jnp.minimum(_count_le(cum, g), t_n * e_n - 1)
    rb = g - (cum[p] - flat[p])
    n_blocks = g_max + pad_max
    z_max = n_blocks - 2 * m // MOE_BLK
    used = jnp.zeros((n_blocks,), jnp.int32).at[dst0[p] + rb].set(1)
    free = jnp.argsort(used, stable=True).astype(jnp.int32)
    z = jnp.minimum(jnp.arange(z_max, dtype=jnp.int32), n_blocks - cum[-1] - 1)
    never = jnp.full((z_max,), MOE_NEVER, jnp.int32)
    g_tile = jnp.concatenate([p // e_n, jnp.full((z_max,), t_n - 1, jnp.int32)])
    g_exp = jnp.concatenate([p % e_n, jnp.zeros((z_max,), jnp.int32)])
    g_rb = jnp.concatenate([rb, never])
    g_dst = jnp.concatenate([dst0[p] + rb, free[z]])
    mt_max = (g_max + pad_max) // per_mt
    n_mt = (jnp.sum(cap_e) // per_mt).astype(jnp.int32)
    mt = jnp.minimum(jnp.arange(mt_max, dtype=jnp.int32), n_mt - 1)
    mt_exp = jnp.minimum(_count_le(jnp.cumsum(cap_e) // per_mt, mt), e_n - 1)
    s_max = 2 * MOE_TM // MOE_BLK + e_n
    cum_t = jnp.cumsum(nblk, axis=1)
    n_t = cum_t[:, -1:]
    s_all = jnp.arange(s_max, dtype=jnp.int32)[None, :]
    s = jnp.minimum(s_all, n_t - 1)
    s_exp = jnp.minimum(_count_le(cum_t, s), e_n - 1)
    s_rb = s - (jnp.take_along_axis(cum_t, s_exp, axis=1) - jnp.take_along_axis(nblk, s_exp, axis=1))
    s_src = jnp.take_along_axis(dst0.reshape(t_n, e_n), s_exp, axis=1) + s_rb
    s_rb = jnp.where(s_all < n_t, s_rb, MOE_NEVER)
    i32 = lambda a: a.astype(jnp.int32)
    return dict(g_tile=i32(g_tile), g_exp=i32(g_exp), g_rb=i32(g_rb), g_dst=i32(g_dst), n_mt=i32(n_mt.reshape(1)),
                mt_exp=i32(mt_exp), s_exp=i32(s_exp.reshape(-1)), s_rb=i32(s_rb.reshape(-1)),
                s_src=i32(s_src.reshape(-1)), n_blocks=n_blocks, s_max=s_max)


def _moe_gather_kernel(tile_ref, exp_ref, rb_ref, dst_ref, h_ref, post_ref, cmbt_ref, o_ref, w_ref):
    g = pl.program_id(0)
    fill = rb_ref[g] >= MOE_NEVER

    @pl.when(fill)
    def _():
        o_ref[...] = jnp.zeros_like(o_ref)
        w_ref[...] = jnp.zeros_like(w_ref)

    @pl.when(jnp.logical_not(fill))
    def _():
        row = post_ref[pl.ds(exp_ref[g], 1), :]
        tm = row.shape[1]
        want = (lax.broadcasted_iota(jnp.int32, (MOE_BLK, tm), 0) + rb_ref[g] * MOE_BLK).astype(F32)
        match = row == want
        o_ref[...] = _dot(jnp.where(match, 1.0, 0.0).astype(BF16), h_ref[...]).astype(o_ref.dtype)
        cw = jnp.sum(jnp.where(match, cmbt_ref[pl.ds(exp_ref[g], 1), :], 0.0), axis=-1, keepdims=True)
        w_ref[...] = jnp.broadcast_to(cw, w_ref.shape)


def _moe_gather(h, post, cmbt, plan):
    m, d = h.shape
    n_e = post.shape[0]
    n = plan["g_tile"].shape[0]
    rows = plan["n_blocks"] * MOE_BLK
    return pl.pallas_call(
        _moe_gather_kernel,
        grid_spec=pltpu.PrefetchScalarGridSpec(
            num_scalar_prefetch=4,
            grid=(n,),
            in_specs=[
                pl.BlockSpec((MOE_TM, d), lambda g, t, e, r, ds: (t[g], 0)),
                pl.BlockSpec((n_e, MOE_TM), lambda g, t, e, r, ds: (0, t[g])),
                pl.BlockSpec((n_e, MOE_TM), lambda g, t, e, r, ds: (0, t[g])),
            ],
            out_specs=[
                pl.BlockSpec((MOE_BLK, d), lambda g, t, e, r, ds: (ds[g], 0)),
                pl.BlockSpec((MOE_BLK, LANES), lambda g, t, e, r, ds: (ds[g], 0)),
            ],
        ),
        out_shape=[jax.ShapeDtypeStruct((rows, d), BF16), jax.ShapeDtypeStruct((rows, LANES), F32)],
        compiler_params=_cparams(1),
        name="moe_gather",
    )(plan["g_tile"], plan["g_exp"], plan["g_rb"], plan["g_dst"], h, post, cmbt)


def _moe_ffn_kernel(exp_ref, n_ref, x_ref, rw_ref, w1_ref, w3_ref, w2_ref, o_ref, acc_ref, *, nc):
    c = pl.program_id(1)

    @pl.when(pl.program_id(0) < n_ref[0])
    def _():
        @pl.when(c == 0)
        def _():
            acc_ref[...] = jnp.zeros_like(acc_ref)

        x = x_ref[...]
        fc = w1_ref.shape[1]
        for k0 in range(0, fc, MOE_SUB):
            k1 = min(k0 + MOE_SUB, fc)
            a = _dot(x, w1_ref[:, k0:k1])
            a = (a * jax.nn.sigmoid(a) * _dot(x, w3_ref[:, k0:k1])).astype(BF16)
            acc_ref[...] += _dot(a, w2_ref[k0:k1, :])

        @pl.when(c == nc - 1)
        def _():
            o_ref[...] = (rw_ref[:, 0:1] * acc_ref[...]).astype(o_ref.dtype)

    @pl.when(pl.program_id(0) >= n_ref[0])
    def _():
        o_ref[...] = jnp.zeros_like(o_ref)


def _moe_ffn(xg, row_w, w1, w3, w2, wl, plan, fc):
    d = xg.shape[1]
    fe = w1.shape[3]
    nc = fe // fc
    n_mt = plan["mt_exp"].shape[0]

    def tile(mt, n):
        return jnp.minimum(mt, n[0] - 1)

    def chunk(mt, c, n):
        return jnp.where(mt < n[0], c, nc - 1)

    kern = functools.partial(_moe_ffn_kernel, nc=nc)
    return pl.pallas_call(
        kern,
        grid_spec=pltpu.PrefetchScalarGridSpec(
            num_scalar_prefetch=2,
            grid=(n_mt, nc),
            in_specs=[
                pl.BlockSpec((MOE_MT, d), lambda mt, c, e, n: (tile(mt, n), 0)),
                pl.BlockSpec((MOE_MT, LANES), lambda mt, c, e, n: (tile(mt, n), 0)),
                pl.BlockSpec((None, None, d, fc), lambda mt, c, e, n: (wl, e[mt], 0, chunk(mt, c, n))),
                pl.BlockSpec((None, None, d, fc), lambda mt, c, e, n: (wl, e[mt], 0, chunk(mt, c, n))),
                pl.BlockSpec((None, None, fc, d), lambda mt, c, e, n: (wl, e[mt], chunk(mt, c, n), 0)),
            ],
            out_specs=pl.BlockSpec((MOE_MT, d), lambda mt, c, e, n: (mt, 0)),
            scratch_shapes=[pltpu.VMEM((MOE_MT, d), F32)],
        ),
        out_shape=jax.ShapeDtypeStruct((n_mt * MOE_MT, d), BF16),
        compiler_params=_cparams(2),
        name="moe_ffn",
    )(plan["mt_exp"], plan["n_mt"], xg, row_w, w1, w3, w2)


def _moe_scatter_kernel(src_ref, exp_ref, rb_ref, ya_ref, yb_ref, pos_ref, x_ref, mod_ref, o_ref, acc_ref, *, ns):
    i = pl.program_id(0)
    s = pl.program_id(1)

    @pl.when(s == 0)
    def _():
        acc_ref[...] = jnp.zeros_like(acc_ref)

    lane = lax.broadcasted_iota(jnp.int32, pos_ref.shape, 1)
    lane_f = lane.astype(F32)

    def onehot(slot):
        pos = jnp.sum(jnp.where(lane == exp_ref[slot], pos_ref[...], 0.0), axis=-1, keepdims=True)
        return jnp.where(pos - (rb_ref[slot] * MOE_BLK).astype(F32) == lane_f, 1.0, 0.0).astype(BF16)

    s0 = i * ns + 2 * s
    pt = jnp.concatenate([onehot(s0), onehot(s0 + 1)], axis=1)
    acc_ref[...] += _dot(pt, jnp.concatenate([ya_ref[...], yb_ref[...]], axis=0))

    @pl.when(s == ns // 2 - 1)
    def _():
        o_ref[...] = x_ref[...] + mod_ref[ROW_GA_FFN:ROW_GA_FFN + 1, :] * acc_ref[...]


def _moe_scatter(yg, pos, x, mod, plan, seq):
    m, d = x.shape
    ns = plan["s_max"]
    tpb = seq // MOE_TM
    kern = functools.partial(_moe_scatter_kernel, ns=ns)
    return pl.pallas_call(
        kern,
        grid_spec=pltpu.PrefetchScalarGridSpec(
            num_scalar_prefetch=3,
            grid=(m // MOE_TM, ns // 2),
            in_specs=[
                pl.BlockSpec((MOE_BLK, d), lambda i, s, sr, ex, rb: (sr[i * ns + 2 * s], 0)),
                pl.BlockSpec((MOE_BLK, d), lambda i, s, sr, ex, rb: (sr[i * ns + 2 * s + 1], 0)),
                pl.BlockSpec((MOE_TM, LANES), lambda i, s, sr, ex, rb: (i, 0)),
                pl.BlockSpec((MOE_TM, d), lambda i, s, sr, ex, rb: (i, 0)),
                pl.BlockSpec((None, 6, d), lambda i, s, sr, ex, rb: (i // tpb, 0, 0)),
            ],
            out_specs=pl.BlockSpec((MOE_TM, d), lambda i, s, sr, ex, rb: (i, 0)),
            scratch_shapes=[pltpu.VMEM((MOE_TM, d), F32)],
        ),
        out_shape=jax.ShapeDtypeStruct((m, d), F32),
        compiler_params=_cparams(2),
        name="moe_scatter",
    )(plan["s_src"], plan["s_exp"], plan["s_rb"], yg, yg, pos, x, mod)


def _moe_layer(x, mod, g, w_router, b_router, w1, w3, w2, wl, seq, fc):
    m = x.shape[0]
    cmbt, pos, post, counts, h = _router(x, mod, g, w_router, b_router, seq)
    plan = _moe_plan(counts[:, 0, :w_router.shape[1]].astype(jnp.int32), m)
    xg, row_w = _moe_gather(h, post, cmbt, plan)
    yg = _moe_ffn(xg, row_w, w1, w3, w2, wl, plan, fc)
    return _moe_scatter(yg, pos, x, mod, plan, seq)


def _pad_w_in(w):
    wt = jnp.swapaxes(w, 1, 2)
    gap = jnp.zeros((w.shape[0], CB_MERGE * LANES - RAW_MERGE_COL, w.shape[1]), w.dtype)
    q0, q1 = CB_SB_Q * LANES, CB_SB_K * LANES
    return jnp.concatenate([wt[:, :q0], wt[:, q0:q1] * SB_Q_SCALE, wt[:, q1:RAW_MERGE_COL], gap,
                            wt[:, RAW_MERGE_COL:]], axis=1).astype(BF16)


def _kv_chunks(proj3):
    b, s, _ = proj3.shape
    n = 2 * NSA_KV_HEADS
    t = proj3[:, :, CB_KC * LANES:(CB_KC + n) * LANES].reshape(b, s, n, HEAD_DIM)
    return t.transpose(0, 2, 1, 3).reshape(b, n, s // CMP_STRIDE, CMP_STRIDE * HEAD_DIM)


def kernel(x, c, w_ada, b_ada, g_mix, g_ffn, w_in, conv_w, conv_b, conv_ln_g, conv_ln_b, w_conv_out, w_sb_out,
           nsa_cmp_pos_k, nsa_cmp_pos_v, nsa_cmp_wk, nsa_cmp_wv, nsa_q_g, nsa_kc_g, nsa_ks_g, nsa_kw_g, w_nsa_out, w_o,
           rel_bias, ffn_w1, ffn_w3, ffn_w2, moe_router, moe_router_b, moe_w1, moe_w3, moe_w2):
    b, s, d = x.shape
    m = b * s
    depth = w_ada.shape[0]
    mod_all = _ada_all(c, w_ada, b_ada)
    bias_c, bias_t = _bias_tables(rel_bias, s)
    w_in_p = _pad_w_in(w_in)
    w_conv_out, w_sb_out, w_nsa_out, w_o, ffn_w1, ffn_w3, ffn_w2, moe_w1, moe_w3, moe_w2 = (
        w.astype(BF16) for w in (w_conv_out, w_sb_out, w_nsa_out, w_o, ffn_w1, ffn_w3, ffn_w2, moe_w1, moe_w3, moe_w2))
    xf = x.reshape(m, d)
    for l in range(depth):
        mod = mod_all[l].reshape(b, 6, d)
        proj = _norm_matmul(xf, mod, g_mix[l], w_in_p, l, ROW_SH_MIX, ROW_SC_MIX, s)
        proj3 = proj.reshape(b, s, NP)
        u_act = _conv_module(proj, conv_w[l], conv_b[l], conv_ln_g[l], conv_ln_b[l], b, s)
        sb = _sb_attention(proj3)
        kcb, vcb, ksn, kwn = _nsa_prep(proj3, nsa_cmp_pos_k[l], nsa_cmp_pos_v[l], nsa_cmp_wk[l], nsa_cmp_wv[l],
                                       nsa_kc_g[l], nsa_ks_g[l], nsa_kw_g[l])
        nsa = _nsa_attention(proj3, kcb, vcb, ksn, kwn, bias_c, bias_t, nsa_q_g[l])
        merged = _merge(u_act, sb.reshape(m, -1), nsa.reshape(m, -1), proj, w_conv_out, w_sb_out, w_nsa_out, l)
        xf = _mm_res(merged, w_o, l, xf, mod, ROW_GA_MIX, s, tm=512, tn=2048)
        i = l // 2
        if l % 2 == 0:
            act = _ffn_up(xf, mod, g_ffn[l], ffn_w1, ffn_w3, i, s)
            xf = _mm_res(act, ffn_w2, i, xf, mod, ROW_GA_FFN, s, tm=1024, tn=512)
        else:
            xf = _moe_layer(xf, mod, g_ffn[l], moe_router[i], moe_router_b[i], moe_w1, moe_w3, moe_w2, i, s, MOE_FC)
    return xf.reshape(b, s, d)
```

```python
import functools
import math

import numpy as np
import jax
import jax.numpy as jnp
from jax import lax
from jax.experimental import pallas as pl
from jax.experimental.pallas import tpu as pltpu

F32 = jnp.float32
BF16 = jnp.bfloat16

D_MODEL = 2048
DEPTH = 4
D_CONV = 1024
CONV_WIDTH = 31
SB_HEADS = 8
HEAD_DIM = 128
NSA_HEADS = 8
NSA_KV_HEADS = 2
NSA_REP = NSA_HEADS // NSA_KV_HEADS
CMP_LEN = 32
CMP_STRIDE = 16
SLC_LEN = 64
SLC_TOP_N = 16
N_LOCAL_BLOCKS = 2
WINDOW = 512
FORCE_SCORE = 1e4
REL_BUCKETS = 32
REL_MAX_DIST = 128
D_FF = 5632
N_EXPERTS = 8
D_FF_EXPERT = 2816
EPS = 1e-6
NEG_INF = -1e30
TINY = 1e-20

LANES = 128
SUBLANES = 8
V7X_VMEM_LIMIT_BYTES = 56 * 1024 * 1024

NP_BLOCKS = 112
NP = NP_BLOCKS * LANES
CB_GLU_A, CB_GLU_G = 0, 8
CB_SB_Q, CB_SB_K, CB_SB_V = 16, 24, 32
CB_NQ = 40
CB_KC, CB_VC, CB_KS, CB_VS, CB_KW, CB_VW = 48, 50, 52, 54, 56, 58
CB_NGATE = 60
CB_MERGE = 64
RAW_GATE_COL = 7680
RAW_MERGE_COL = 7704
IN_PROJ_TN = 1024
ROW_SH_MIX, ROW_SC_MIX, ROW_GA_MIX, ROW_SH_FFN, ROW_SC_FFN, ROW_GA_FFN = range(6)

SCALE = HEAD_DIM ** -0.5
SB_Q_SCALE = SCALE * math.log2(math.e)
TQ = 128
SB_TILE = 256
SB_GROUP = 8
SEL_CHUNK = 512
NSA_QT = 256
NSA_SUB = NSA_QT // TQ
WIN_SPAN = WINDOW + NSA_QT
BT_DIAG, BT_SUB, BT_FAR, BT_FAR_UPPER, BT_MASKED = range(5)
CONV_HALO = 32
MOE_TM = 1024
MOE_BLK = 128
MOE_MT = 512
MOE_FC = 1408
MOE_SUB = 256
MOE_SCATTER_SLOTS = 4
MOE_NEVER = 1 << 20


def _cparams(n_axes):
    return pltpu.CompilerParams(
        dimension_semantics=("arbitrary",) * n_axes,
        vmem_limit_bytes=V7X_VMEM_LIMIT_BYTES,
    )


def _dot(a, b):
    return jnp.dot(a, b, preferred_element_type=F32)


def _dot_t(a, b):
    return lax.dot_general(a, b, (((1,), (1,)), ((), ())), preferred_element_type=F32)


def _split_dot(x, w01):
    hi = x.astype(BF16)
    lo = (x - hi.astype(F32)).astype(BF16)
    return _dot(hi, w01) + _dot(lo, w01)


def _norm_mod(x, g, sh, sc):
    ms = jnp.mean(x * x, axis=-1, keepdims=True)
    return (x * lax.rsqrt(ms + EPS) * g) * (1.0 + sc) + sh


def _rms(x, g):
    ms = jnp.mean(x * x, axis=-1, keepdims=True)
    return x * lax.rsqrt(ms + EPS) * g


def _ada_kernel(c_ref, w_ref, b_ref, o_ref):
    c = c_ref[...]
    ca = (c * jax.nn.sigmoid(c)).astype(BF16)
    o_ref[...] = _dot(ca, w_ref[...].astype(BF16)) + b_ref[...]


def _ada_all(c, w_ada, b_ada):
    depth, d, n = w_ada.shape
    nb = c.shape[0]
    b = 16
    c = jnp.zeros((b, d), c.dtype).at[:nb].set(c)
    tn = 1024
    out = pl.pallas_call(
        _ada_kernel,
        grid=(depth, n // tn),
        in_specs=[
            pl.BlockSpec((b, d), lambda l, j: (0, 0)),
            pl.BlockSpec((None, d, tn), lambda l, j: (l, 0, j)),
            pl.BlockSpec((None, 1, tn), lambda l, j: (l, 0, j)),
        ],
        out_specs=pl.BlockSpec((None, b, tn), lambda l, j: (l, 0, j)),
        out_shape=jax.ShapeDtypeStruct((depth, b, n), F32),
        compiler_params=_cparams(2),
        name="ada",
    )(c, w_ada, b_ada.reshape(depth, 1, n))
    return out[:, :nb]


def _bias_kernel(tab_ref, bk_ref, o_ref):
    h = pl.program_id(0)
    bk = bk_ref[...]
    acc = jnp.zeros(bk.shape, F32)
    for b in range(REL_BUCKETS):
        acc = jnp.where(bk == b, tab_ref[b, h], acc)
    o_ref[...] = acc


def _bias_expand(rel_bias, buckets):
    r = buckets.shape[0]
    tr = r
    return pl.pallas_call(
        _bias_kernel,
        grid=(NSA_HEADS, r // tr),
        in_specs=[
            pl.BlockSpec(memory_space=pltpu.SMEM),
            pl.BlockSpec((tr, LANES), lambda h, i: (i, 0)),
        ],
        out_specs=pl.BlockSpec((None, tr, LANES), lambda h, i: (h, i, 0)),
        out_shape=jax.ShapeDtypeStruct((NSA_HEADS, r, LANES), F32),
        compiler_params=_cparams(2),
        name="bias_expand",
    )(rel_bias, buckets)


def _rel_bucket(dist):
    n = jnp.maximum(dist, 0)
    max_exact = REL_BUCKETS // 2
    nf = jnp.maximum(n, 1).astype(F32)
    large = max_exact + (jnp.log(nf / max_exact) / math.log(REL_MAX_DIST / max_exact) * (REL_BUCKETS - max_exact)).astype(jnp.int32)
    large = jnp.minimum(large, REL_BUCKETS - 1)
    return jnp.where(n < max_exact, n, large)


def _bias_tables(rel_bias, s):
    t = jnp.arange(s, dtype=jnp.int32)[:, None]
    cend = jnp.arange(LANES, dtype=jnp.int32)[None, :] * CMP_STRIDE + (CMP_LEN - 1)
    q = jnp.arange(TQ, dtype=jnp.int32)[:, None]
    k = jnp.arange(TQ, dtype=jnp.int32)[None, :]
    bk_t = jnp.concatenate([_rel_bucket(q - k), _rel_bucket(TQ + q - k), _rel_bucket(2 * TQ + q - k)], axis=0)
    out = _bias_expand(rel_bias, jnp.concatenate([_rel_bucket(t - cend), bk_t], axis=0))
    bias_c = jnp.where(t >= cend, out[:, :s], NEG_INF).reshape(NSA_KV_HEADS, NSA_REP, s, LANES)
    diag, sub, far = out[:, s:s + TQ], out[:, s + TQ:s + 2 * TQ], out[:, s + 2 * TQ:]
    tiles = jnp.stack([jnp.where(q >= k, diag, NEG_INF), sub, far, jnp.where(k > q, far, NEG_INF),
                       jnp.full_like(far, NEG_INF)], axis=1)
    tiles = tiles.reshape(NSA_KV_HEADS, NSA_REP, 5, TQ, TQ).transpose(0, 2, 1, 3, 4)
    return bias_c, tiles.reshape(NSA_KV_HEADS, 5, NSA_REP * TQ, TQ)


def _in_proj_kernel(x_ref, mod_ref, g_ref, w_ref, o_ref, h_ref):
    j = pl.program_id(1)

    @pl.when(j == 0)
    def _():
        h = _norm_mod(x_ref[...], g_ref[...], mod_ref[ROW_SH_MIX:ROW_SH_MIX + 1, :], mod_ref[ROW_SC_MIX:ROW_SC_MIX + 1, :])
        h_ref[...] = h.astype(BF16)

    scale = jnp.where(j == CB_SB_Q * LANES // IN_PROJ_TN, SB_Q_SCALE, 1.0)
    o_ref[...] = _dot_t(h_ref[...], (w_ref[...] * scale).astype(BF16)).astype(o_ref.dtype)


def _in_proj(x, mod, g, wt, wl, seq, tm=1024):
    m, d = x.shape
    tn = IN_PROJ_TN
    tpb = seq // tm
    assert (CB_SB_K - CB_SB_Q) * LANES == tn and CB_SB_Q * LANES % tn == 0 and CB_MERGE * LANES % tn == 0
    shift = CB_MERGE * LANES - RAW_MERGE_COL

    assert tn % SUBLANES == 0 and shift % SUBLANES == 0

    def w_rows(i, j):
        start = jnp.where(j * tn < CB_MERGE * LANES, j * (tn // SUBLANES), j * (tn // SUBLANES) - shift // SUBLANES)
        return wl, start * SUBLANES, 0

    return pl.pallas_call(
        _in_proj_kernel,
        grid=(m // tm, NP // tn),
        in_specs=[
            pl.BlockSpec((tm, d), lambda i, j: (i, 0)),
            pl.BlockSpec((None, 6, d), lambda i, j: (i // tpb, 0, 0)),
            pl.BlockSpec((1, d), lambda i, j: (0, 0)),
            pl.BlockSpec((None, pl.Element(tn), pl.Element(d)), w_rows),
        ],
        out_specs=pl.BlockSpec((tm, tn), lambda i, j: (i, j)),
        out_shape=jax.ShapeDtypeStruct((m, NP), BF16),
        scratch_shapes=[pltpu.VMEM((tm, d), BF16)],
        compiler_params=_cparams(2),
        name="in_proj",
    )(x, mod, g.reshape(1, d), wt)


def _conv_kernel(a_ref, g_ref, ap_ref, gp_ref, cw_ref, cb_ref, lng_ref, lnb_ref, o_ref, ubuf, vbuf, shifted, *, ts):
    i = pl.program_id(1)
    ubuf[CONV_HALO:, :] = a_ref[...].astype(F32) * jax.nn.sigmoid(g_ref[...].astype(F32))
    up = ap_ref[...].astype(F32) * jax.nn.sigmoid(gp_ref[...].astype(F32))
    ubuf[:CONV_HALO, :] = jnp.where(i > 0, up, 0.0)

    first = CONV_HALO - (CONV_WIDTH - 1)
    span = ts + CONV_HALO - SUBLANES

    def chunk(c, carry):
        c0 = pl.multiple_of(c * LANES, LANES)
        for s in range(1, SUBLANES):
            shifted[s - 1] = ubuf[pl.ds(s, span), pl.ds(c0, LANES)]
        acc = jnp.zeros((ts, LANES), F32) + cb_ref[:, pl.ds(c0, LANES)]
        for k in range(CONV_WIDTH):
            a, s = divmod(first + k, SUBLANES)
            rows = ubuf[pl.ds(a * SUBLANES, ts), pl.ds(c0, LANES)] if s == 0 else shifted[s - 1, pl.ds(a * SUBLANES, ts), :]
            acc = acc + cw_ref[k:k + 1, pl.ds(c0, LANES)] * rows
        vbuf[:, pl.ds(c0, LANES)] = acc
        return carry

    lax.fori_loop(0, D_CONV // LANES, chunk, 0)
    v = vbuf[...]
    mu = jnp.mean(v, axis=-1, keepdims=True)
    vc = v - mu
    var = jnp.mean(vc * vc, axis=-1, keepdims=True)
    y = vc * lax.rsqrt(var + EPS) * lng_ref[...] + lnb_ref[...]
    o_ref[...] = (y * jax.nn.sigmoid(y)).astype(o_ref.dtype)


def _conv_module(proj, conv_w, conv_b, ln_g, ln_b, batch, seq, ts=256):
    m = proj.shape[0]
    nt = seq // ts
    hb = ts // CONV_HALO
    kern = functools.partial(_conv_kernel, ts=ts)

    def prev_idx(col):
        return lambda b, i: (jnp.maximum((b * nt + i) * hb - 1, 0), col)

    return pl.pallas_call(
        kern,
        grid=(batch, nt),
        in_specs=[
            pl.BlockSpec((ts, D_CONV), lambda b, i: (b * nt + i, 0)),
            pl.BlockSpec((ts, D_CONV), lambda b, i: (b * nt + i, 1)),
            pl.BlockSpec((CONV_HALO, D_CONV), prev_idx(0)),
            pl.BlockSpec((CONV_HALO, D_CONV), prev_idx(1)),
            pl.BlockSpec((CONV_WIDTH, D_CONV), lambda b, i: (0, 0)),
            pl.BlockSpec((1, D_CONV), lambda b, i: (0, 0)),
            pl.BlockSpec((1, D_CONV), lambda b, i: (0, 0)),
            pl.BlockSpec((1, D_CONV), lambda b, i: (0, 0)),
        ],
        out_specs=pl.BlockSpec((ts, D_CONV), lambda b, i: (b * nt + i, 0)),
        out_shape=jax.ShapeDtypeStruct((m, D_CONV), BF16),
        scratch_shapes=[pltpu.VMEM((CONV_HALO + ts, D_CONV), F32), pltpu.VMEM((ts, D_CONV), F32),
                        pltpu.VMEM((SUBLANES - 1, ts + CONV_HALO - SUBLANES, LANES), F32)],
        compiler_params=_cparams(2),
        name="conv_module",
    )(proj, proj, proj, proj, conv_w, conv_b.reshape(1, -1), ln_g.reshape(1, -1), ln_b.reshape(1, -1))


def _sb_kernel(q_ref, k_ref, v_ref, o_ref):
    i = pl.program_id(2)
    t = SB_TILE
    rows = SB_GROUP * t
    row = lax.broadcasted_iota(jnp.int32, (t, t), 0)
    col = lax.broadcasted_iota(jnp.int32, (t, t), 1)
    later = jnp.where(row > col, 1.0, 0.0).astype(BF16)
    later = jnp.concatenate([later, jnp.ones((t, LANES), BF16)], axis=1)
    heads = [slice(h * HEAD_DIM, (h + 1) * HEAD_DIM) for h in range(SB_GROUP)]

    def tile(j, rsum, acc, diag):
        k0 = pl.multiple_of(j * t, t)
        z = jnp.concatenate([_dot_t(q_ref[:, hs], k_ref[pl.ds(k0, t), hs]) for hs in heads], axis=0)
        if diag:
            qpos = lax.broadcasted_iota(jnp.int32, (rows, t), 0) & (t - 1)
            z = jnp.where(lax.broadcasted_iota(jnp.int32, (rows, t), 1) < qpos, z, NEG_INF)
        sp = jnp.log2(1.0 + jnp.exp2(-jnp.abs(z)))
        log_beta = jnp.minimum(z, 0.0) - sp
        log_keep = log_beta - z
        sums = _dot(log_keep.astype(BF16), later)
        log_survive = sums[:, :t] + jnp.concatenate([rsum] * (t // LANES), axis=1)
        a = jnp.exp2(log_beta + log_survive).astype(BF16)
        pv = [_dot(a[h * t:(h + 1) * t], v_ref[pl.ds(k0, t), hs]) for h, hs in enumerate(heads)]
        acc = acc + jnp.concatenate(pv, axis=0)
        rsum = rsum + sums[:, t:]
        return rsum, acc

    carry = tile(i, jnp.zeros((rows, LANES), F32), jnp.zeros((rows, HEAD_DIM), F32), True)
    _, acc = lax.fori_loop(0, i, lambda jj, c: tile(i - 1 - jj, c[0], c[1], False), carry)
    for h in range(SB_GROUP):
        o_ref[:, h * HEAD_DIM:(h + 1) * HEAD_DIM] = acc[h * t:(h + 1) * t].astype(o_ref.dtype)


def _sb_attention(proj3):
    b, s, _ = proj3.shape
    gw = SB_GROUP * HEAD_DIM
    return pl.pallas_call(
        _sb_kernel,
        grid=(b, SB_HEADS // SB_GROUP, s // SB_TILE),
        in_specs=[
            pl.BlockSpec((None, SB_TILE, gw), lambda bi, hg, i: (bi, i, CB_SB_Q // SB_GROUP + hg)),
            pl.BlockSpec((None, s, gw), lambda bi, hg, i: (bi, 0, CB_SB_K // SB_GROUP + hg)),
            pl.BlockSpec((None, s, gw), lambda bi, hg, i: (bi, 0, CB_SB_V // SB_GROUP + hg)),
        ],
        out_specs=pl.BlockSpec((None, SB_TILE, gw), lambda bi, hg, i: (bi, i, hg)),
        out_shape=jax.ShapeDtypeStruct((b, s, SB_HEADS * HEAD_DIM), BF16),
        compiler_params=_cparams(3),
        name="sb_attention",
    )(proj3, proj3, proj3)


def _nsa_prep_kernel(kc_ref, vc_ref, ks_ref, kw_ref, pk_ref, pv_ref, wk_ref, wv_ref, kcg_ref, ksg_ref, kwg_ref,
                     kcb_ref, vcb_ref, ksn_ref, kwn_ref):
    half = CMP_STRIDE * HEAD_DIM

    def compress(a_ref, p_ref, w_ref):
        a = a_ref[...].astype(F32)
        top = _dot((a + p_ref[0:1, :]).astype(BF16), w_ref[0:half, :])
        bot = _dot((a + p_ref[1:2, :]).astype(BF16), w_ref[half:2 * half, :])
        return top + pltpu.roll(bot, bot.shape[0] - 1, axis=0)

    kcb_ref[...] = _rms(compress(kc_ref, pk_ref, wk_ref), kcg_ref[...]).astype(BF16)
    vcb_ref[...] = compress(vc_ref, pv_ref, wv_ref).astype(BF16)
    kwn_ref[...] = _rms(kw_ref[...].astype(F32), kwg_ref[...]).astype(BF16)
    ksn_ref[:, 0:HEAD_DIM] = _rms(ks_ref[...].astype(F32), ksg_ref[...]).astype(BF16)
    s = ks_ref.shape[0]
    blk = jnp.right_shift(lax.broadcasted_iota(jnp.int32, (s, LANES), 0), int(math.log2(SLC_LEN)))
    onehot = blk == lax.broadcasted_iota(jnp.int32, (s, LANES), 1)
    ksn_ref[:, HEAD_DIM:HEAD_DIM + LANES] = jnp.where(onehot, 1.0, 0.0).astype(BF16)


def _nsa_prep(proj3, pos_k, pos_v, wk, wv, kc_g, ks_g, kw_g):
    b, s, _ = proj3.shape
    g_n = NSA_KV_HEADS
    kv_chunks = _kv_chunks(proj3)
    nch = s // CMP_STRIDE
    half = CMP_STRIDE * HEAD_DIM
    vec = lambda: pl.BlockSpec((1, HEAD_DIM), lambda bi, g: (0, 0))
    small = jax.ShapeDtypeStruct((b, g_n, nch, HEAD_DIM), BF16)
    full = jax.ShapeDtypeStruct((b, g_n, s, HEAD_DIM), BF16)
    aug = jax.ShapeDtypeStruct((b, g_n, s, HEAD_DIM + LANES), BF16)
    return pl.pallas_call(
        _nsa_prep_kernel,
        grid=(b, g_n),
        in_specs=[
            pl.BlockSpec((None, None, nch, half), lambda bi, g: (bi, g, 0, 0)),
            pl.BlockSpec((None, None, nch, half), lambda bi, g: (bi, g_n + g, 0, 0)),
            pl.BlockSpec((None, s, HEAD_DIM), lambda bi, g: (bi, 0, CB_KS + g)),
            pl.BlockSpec((None, s, HEAD_DIM), lambda bi, g: (bi, 0, CB_KW + g)),
            pl.BlockSpec((2, half), lambda bi, g: (0, 0)),
            pl.BlockSpec((2, half), lambda bi, g: (0, 0)),
            pl.BlockSpec((2 * half, HEAD_DIM), lambda bi, g: (0, 0)),
            pl.BlockSpec((2 * half, HEAD_DIM), lambda bi, g: (0, 0)),
            vec(), vec(), vec(),
        ],
        out_specs=[
            pl.BlockSpec((None, None, nch, HEAD_DIM), lambda bi, g: (bi, g, 0, 0)),
            pl.BlockSpec((None, None, nch, HEAD_DIM), lambda bi, g: (bi, g, 0, 0)),
            pl.BlockSpec((None, None, s, HEAD_DIM + LANES), lambda bi, g: (bi, g, 0, 0)),
            pl.BlockSpec((None, None, s, HEAD_DIM), lambda bi, g: (bi, g, 0, 0)),
        ],
        out_shape=[small, small, aug, full],
        compiler_params=_cparams(2),
        name="nsa_prep",
    )(kv_chunks, kv_chunks, proj3, proj3, pos_k.reshape(2, half), pos_v.reshape(2, half),
      wk.astype(BF16), wv.astype(BF16), kc_g.reshape(1, -1), ks_g.reshape(1, -1), kw_g.reshape(1, -1))


def _nsa_kernel(q_ref, kcb_ref, vcb_ref, ksa_ref, vs_ref, kwn_ref, vw_ref, gate_ref, bc_ref, bt_ref, qg_ref, ovt_ref,
                o_ref):
    g = pl.program_id(1)
    i = pl.program_id(2)
    r_n = NSA_REP
    groups = [(a, r) for a in range(NSA_SUB) for r in range(r_n)]
    sub = lambda a: slice(a * TQ, (a + 1) * TQ)
    head = lambda r: slice(r * HEAD_DIM, (r + 1) * HEAD_DIM)

    q = jnp.concatenate([_rms(q_ref[sub(a), head(r)].astype(F32), qg_ref[...]).astype(BF16) for a, r in groups], axis=0)

    logit_c = _dot_t(q, kcb_ref[...]) + jnp.concatenate([bc_ref[r, sub(a), :] for a, r in groups], axis=0)
    m_c = jnp.max(logit_c, axis=-1, keepdims=True)
    p_c = jnp.where(logit_c > 0.5 * NEG_INF, jnp.exp(logit_c - m_c), 0.0)
    p_c = p_c / jnp.maximum(jnp.sum(p_c, axis=-1, keepdims=True), TINY)
    o_c = _dot(p_c.astype(BF16), vcb_ref[...])

    p_sum = jnp.concatenate(
        [sum(p_c[(a * r_n + r) * TQ:(a * r_n + r + 1) * TQ] for r in range(r_n)) for a in range(NSA_SUB)], axis=0)
    hi = p_sum.astype(BF16)
    lo = (p_sum - hi.astype(F32)).astype(BF16)
    n_sel = ovt_ref.shape[0]
    imp = _dot_t(ovt_ref[...], hi) + _dot_t(ovt_ref[...], lo)
    blk = lax.broadcasted_iota(jnp.int32, (n_sel, NSA_QT), 0)
    tb = jnp.right_shift(i * NSA_QT + lax.broadcasted_iota(jnp.int32, (n_sel, NSA_QT), 1), int(math.log2(SLC_LEN)))
    causal_blk = blk <= tb
    forced = (blk == 0) | (causal_blk & (blk > tb - N_LOCAL_BLOCKS))
    score = jnp.where(forced, FORCE_SCORE, jnp.where(causal_blk, imp, -FORCE_SCORE))
    rank = jnp.zeros((n_sel, NSA_QT), F32)
    for b in range(n_sel):
        sb = score[b:b + 1, :]
        beats = (sb > score) | ((sb == score) & (blk > b))
        rank = rank + jnp.where(beats, 1.0, 0.0)
    chosen = (rank < float(min(SLC_TOP_N, n_sel))) & causal_blk
    sel_neg = jnp.where(chosen, 0.0, NEG_INF)
    sel_neg = jnp.concatenate([sel_neg, jnp.zeros((LANES - n_sel, NSA_QT), F32)], axis=0).T.astype(BF16)
    q_aug = jnp.concatenate([q, jnp.concatenate([sel_neg[sub(a)] for a, _ in groups], axis=0)], axis=1)

    def biased(s, first_tile, idx_fn):
        return jnp.concatenate(
            [s[:, c * TQ:(c + 1) * TQ]
             + jnp.concatenate([bt_ref[idx_fn(i * NSA_SUB + a - (first_tile + c))] for a in range(NSA_SUB)], axis=0)
             for c in range(s.shape[1] // TQ)], axis=1)

    w_tile = jnp.maximum(i * NSA_SUB - WINDOW // TQ, 0)
    w0 = pl.multiple_of(w_tile * TQ, TQ)

    def win_idx(off):
        near = jnp.where(off == WINDOW // TQ, BT_FAR_UPPER, jnp.minimum(off, BT_FAR))
        return jnp.where((off < 0) | (off > WINDOW // TQ), BT_MASKED, near)

    s_w = biased(_dot_t(q, kwn_ref[pl.ds(w0, WIN_SPAN), :]), w_tile, win_idx)
    p_w = jnp.exp(s_w - jnp.max(s_w, axis=-1, keepdims=True))
    o_w = _dot(p_w.astype(BF16), vw_ref[pl.ds(w0, WIN_SPAN), :]) / jnp.sum(p_w, axis=-1, keepdims=True)

    def sel_chunk(kc, carry, diag):
        c0 = pl.multiple_of(kc * SEL_CHUNK, SEL_CHUNK)
        idx_fn = (lambda off: jnp.where(off < 0, BT_MASKED, jnp.minimum(off, BT_FAR))) if diag else (
            lambda off: jnp.minimum(off, BT_FAR))
        s = biased(_dot_t(q_aug, ksa_ref[pl.ds(c0, SEL_CHUNK), :]), kc * (SEL_CHUNK // TQ), idx_fn)
        m_blk = jnp.max(s, axis=-1, keepdims=True)
        if diag:
            p = jnp.exp(s - m_blk)
            return m_blk, jnp.sum(p, axis=-1, keepdims=True), _dot(p.astype(BF16), vs_ref[pl.ds(c0, SEL_CHUNK), :])
        m, l, acc = carry
        m_new = jnp.maximum(m, m_blk)
        p = jnp.exp(s - m_new)
        alpha = jnp.exp(m - m_new)
        l = alpha * l + jnp.sum(p, axis=-1, keepdims=True)
        acc = alpha * acc + _dot(p.astype(BF16), vs_ref[pl.ds(c0, SEL_CHUNK), :])
        return m_new, l, acc

    kc_diag = lax.shift_right_logical(i * NSA_SUB + (NSA_SUB - 1), jnp.int32(int(math.log2(SEL_CHUNK // TQ))))
    carry = sel_chunk(kc_diag, None, True)
    _, l_s, acc_s = lax.fori_loop(0, kc_diag, lambda jj, c: sel_chunk(kc_diag - 1 - jj, c, False), carry)
    o_s = acc_s / l_s

    gates = jax.nn.sigmoid(gate_ref[...].astype(F32))
    lane = lax.broadcasted_iota(jnp.int32, gates.shape, 1)

    def gate_col(idx):
        return jnp.sum(jnp.where(lane == idx, gates, 0.0), axis=-1, keepdims=True)

    for r in range(r_n):
        h = g * r_n + r
        g_c, g_s, g_w = gate_col(h), gate_col(NSA_HEADS + h), gate_col(2 * NSA_HEADS + h)
        for a in range(NSA_SUB):
            k = slice((a * r_n + r) * TQ, (a * r_n + r + 1) * TQ)
            o = g_c[sub(a)] * o_c[k] + g_s[sub(a)] * o_s[k] + g_w[sub(a)] * o_w[k]
            o_ref[sub(a), head(r)] = o.astype(o_ref.dtype)


def _overlap_matrix(s):
    nch = s // CMP_STRIDE
    nsel = s // SLC_LEN
    ci = np.arange(nch)[:, None]
    sj = np.arange(nsel)[None, :]
    ov = (ci * CMP_STRIDE <= sj * SLC_LEN + SLC_LEN - 1) & (ci * CMP_STRIDE + CMP_LEN - 1 >= sj * SLC_LEN)
    ov = ov & (ci < nch - 1)
    return jnp.asarray(ov.T, BF16)


def _nsa_attention(proj3, kcb, vcb, ksa, kwn, bias_c, bias_t, q_g):
    b, s, _ = proj3.shape
    g_n, r_n = NSA_KV_HEADS, NSA_REP
    nch = s // CMP_STRIDE
    nsel = s // SLC_LEN
    gw = r_n * HEAD_DIM
    return pl.pallas_call(
        _nsa_kernel,
        grid=(b, g_n, s // NSA_QT),
        in_specs=[
            pl.BlockSpec((None, NSA_QT, gw), lambda bi, g, i: (bi, i, CB_NQ // r_n + g)),
            pl.BlockSpec((None, None, nch, HEAD_DIM), lambda bi, g, i: (bi, g, 0, 0)),
            pl.BlockSpec((None, None, nch, HEAD_DIM), lambda bi, g, i: (bi, g, 0, 0)),
            pl.BlockSpec((None, None, s, HEAD_DIM + LANES), lambda bi, g, i: (bi, g, 0, 0)),
            pl.BlockSpec((None, s, HEAD_DIM), lambda bi, g, i: (bi, 0, CB_VS + g)),
            pl.BlockSpec((None, None, s, HEAD_DIM), lambda bi, g, i: (bi, g, 0, 0)),
            pl.BlockSpec((None, s, HEAD_DIM), lambda bi, g, i: (bi, 0, CB_VW + g)),
            pl.BlockSpec((None, NSA_QT, LANES), lambda bi, g, i: (bi, i, CB_NGATE)),
            pl.BlockSpec((None, r_n, NSA_QT, LANES), lambda bi, g, i: (g, 0, i, 0)),
            pl.BlockSpec((None, 5, r_n * TQ, TQ), lambda bi, g, i: (g, 0, 0, 0)),
            pl.BlockSpec((1, HEAD_DIM), lambda bi, g, i: (0, 0)),
            pl.BlockSpec((nsel, nch), lambda bi, g, i: (0, 0)),
        ],
        out_specs=pl.BlockSpec((None, NSA_QT, gw), lambda bi, g, i: (bi, i, g)),
        out_shape=jax.ShapeDtypeStruct((b, s, NSA_HEADS * HEAD_DIM), BF16),
        compiler_params=_cparams(3),
        name="nsa_attention",
    )(proj3, kcb, vcb, ksa, proj3, kwn, proj3, proj3, bias_c, bias_t, q_g.reshape(1, -1) * SCALE, _overlap_matrix(s))


def _merge_kernel(u_ref, sb_ref, ns_ref, gc_ref, gs_ref, gn_ref, wc_ref, ws_ref, wn_ref, o_ref):
    m = jax.nn.sigmoid(gc_ref[...].astype(F32)) * _dot(u_ref[...], wc_ref[...])
    m = m + jax.nn.sigmoid(gs_ref[...].astype(F32)) * _dot(sb_ref[...], ws_ref[...])
    m = m + jax.nn.sigmoid(gn_ref[...].astype(F32)) * _dot(ns_ref[...], wn_ref[...])
    o_ref[...] = m.astype(o_ref.dtype)


def _merge(u_act, sb, nsa, proj, wc, ws, wn, wl, tm=512, tn=2048):
    m = u_act.shape[0]
    d = wc.shape[2]
    kc = wc.shape[1]
    gate0 = CB_MERGE * LANES // tn
    gstep = d // tn
    act = lambda: pl.BlockSpec((tm, kc), lambda i, j: (i, 0))
    gate = lambda n: pl.BlockSpec((tm, tn), lambda i, j: (i, gate0 + n * gstep + j))
    wsp = lambda: pl.BlockSpec((None, kc, tn), lambda i, j: (wl, 0, j))
    return pl.pallas_call(
        _merge_kernel,
        grid=(m // tm, d // tn),
        in_specs=[act(), act(), act(), gate(0), gate(1), gate(2), wsp(), wsp(), wsp()],
        out_specs=pl.BlockSpec((tm, tn), lambda i, j: (i, j)),
        out_shape=jax.ShapeDtypeStruct((m, d), BF16),
        compiler_params=_cparams(2),
        name="merge",
    )(u_act, sb, nsa, proj, proj, proj, wc, ws, wn)


def _mm_res_kernel(a_ref, w_ref, x_ref, mod_ref, o_ref, acc_ref, *, row_ga, nk):
    k = pl.program_id(2)

    @pl.when(k == 0)
    def _():
        acc_ref[...] = jnp.zeros_like(acc_ref)

    acc_ref[...] += _dot(a_ref[...], w_ref[...])

    @pl.when(k == nk - 1)
    def _():
        o_ref[...] = x_ref[...] + mod_ref[row_ga:row_ga + 1, :] * acc_ref[...]


def _mm_res(a, w, wl, x, mod, row_ga, seq, tm=1024, tn=1024, tk=None):
    m, ka = a.shape
    n = x.shape[1]
    tpb = seq // tm
    tk = tk or ka
    nk = ka // tk
    w_spec = pl.BlockSpec((None, tk, tn), lambda i, j, k: (wl, k, j))
    kern = functools.partial(_mm_res_kernel, row_ga=row_ga, nk=nk)
    return pl.pallas_call(
        kern,
        grid=(m // tm, n // tn, nk),
        in_specs=[
            pl.BlockSpec((tm, tk), lambda i, j, k: (i, k)),
            w_spec,
            pl.BlockSpec((tm, tn), lambda i, j, k: (i, j)),
            pl.BlockSpec((None, 6, tn), lambda i, j, k: (i // tpb, 0, j)),
        ],
        out_specs=pl.BlockSpec((tm, tn), lambda i, j, k: (i, j)),
        out_shape=jax.ShapeDtypeStruct((m, n), F32),
        scratch_shapes=[pltpu.VMEM((tm, tn), F32)],
        compiler_params=_cparams(3),
        name="matmul_residual",
    )(a, w, x, mod)


def _ffn_up_kernel(x_ref, mod_ref, g_ref, w1_ref, w3_ref, o_ref, h_ref):
    @pl.when(pl.program_id(1) == 0)
    def _():
        h = _norm_mod(x_ref[...], g_ref[...], mod_ref[ROW_SH_FFN:ROW_SH_FFN + 1, :], mod_ref[ROW_SC_FFN:ROW_SC_FFN + 1, :])
        h_ref[...] = h.astype(BF16)

    h = h_ref[...]
    a = _dot(h, w1_ref[...])
    o_ref[...] = (a * jax.nn.sigmoid(a) * _dot(h, w3_ref[...])).astype(o_ref.dtype)


def _ffn_up(x, mod, g, w1, w3, wl, seq, tm=1024, tn=512):
    m, d = x.shape
    f = w1.shape[2]
    tpb = seq // tm
    return pl.pallas_call(
        _ffn_up_kernel,
        grid=(m // tm, f // tn),
        in_specs=[
            pl.BlockSpec((tm, d), lambda i, j: (i, 0)),
            pl.BlockSpec((None, 6, d), lambda i, j: (i // tpb, 0, 0)),
            pl.BlockSpec((1, d), lambda i, j: (0, 0)),
            pl.BlockSpec((None, d, tn), lambda i, j: (wl, 0, j)),
            pl.BlockSpec((None, d, tn), lambda i, j: (wl, 0, j)),
        ],
        out_specs=pl.BlockSpec((tm, tn), lambda i, j: (i, j)),
        out_shape=jax.ShapeDtypeStruct((m, f), BF16),
        scratch_shapes=[pltpu.VMEM((tm, d), BF16)],
        compiler_params=_cparams(2),
        name="ffn_up",
    )(x, mod, g.reshape(1, d), w1, w3)


def _router_kernel(x_ref, mod_ref, g_ref, wr_ref, br_ref, cmbt_ref, pos_ref, post_ref, cnt_ref, h_ref):
    h = _norm_mod(x_ref[...], g_ref[...], mod_ref[ROW_SH_FFN:ROW_SH_FFN + 1, :], mod_ref[ROW_SC_FFN:ROW_SC_FFN + 1, :])
    h_ref[...] = h.astype(BF16)
    logits = jnp.dot(h, wr_ref[...], precision=lax.Precision.HIGHEST, preferred_element_type=F32) + br_ref[...]
    lane = lax.broadcasted_iota(jnp.int32, logits.shape, 1).astype(F32)
    pad = float(LANES)
    m1 = jnp.max(logits, axis=-1, keepdims=True)
    i1 = jnp.min(jnp.where(logits == m1, lane, pad), axis=-1, keepdims=True)
    rest = jnp.where(lane == i1, NEG_INF, logits)
    m2 = jnp.max(rest, axis=-1, keepdims=True)
    i2 = jnp.min(jnp.where(rest == m2, lane, pad), axis=-1, keepdims=True)
    e2 = jnp.exp(m2 - m1)
    w1 = 1.0 / (1.0 + e2)
    w2 = e2 / (1.0 + e2)
    cmb = jnp.where(lane == i1, w1, 0.0) + jnp.where(lane == i2, w2, 0.0)
    cmbt_ref[...] = cmb.T[0:cmbt_ref.shape[0], :]
    sel = jnp.where((lane == i1) | (lane == i2), 1.0, 0.0)
    tm = sel.shape[0]
    earlier = lax.broadcasted_iota(jnp.int32, (tm, tm), 1) < lax.broadcasted_iota(jnp.int32, (tm, tm), 0)
    pos = _dot(jnp.where(earlier, 1.0, 0.0).astype(BF16), sel.astype(BF16))
    pos = jnp.where(sel > 0.0, pos, -1.0)
    pos_ref[...] = pos
    post_ref[...] = pos.T[0:post_ref.shape[0], :]
    cnt_ref[...] = jnp.broadcast_to(jnp.sum(sel, axis=0, keepdims=True), cnt_ref.shape)


def _router(x, mod, g, w_router, b_router, seq):
    m, d = x.shape
    tm = MOE_TM
    n_e = w_router.shape[1]
    tpb = seq // tm
    nt = m // tm
    wr = jnp.zeros((d, LANES), F32).at[:, :n_e].set(w_router)
    br = jnp.full((1, LANES), 2.0 * NEG_INF, F32).at[0, :n_e].set(b_router)
    return pl.pallas_call(
        _router_kernel,
        grid=(nt,),
        in_specs=[
            pl.BlockSpec((tm, d), lambda i: (i, 0)),
            pl.BlockSpec((None, 6, d), lambda i: (i // tpb, 0, 0)),
            pl.BlockSpec((1, d), lambda i: (0, 0)),
            pl.BlockSpec((d, LANES), lambda i: (0, 0)),
            pl.BlockSpec((1, LANES), lambda i: (0, 0)),
        ],
        out_specs=[
            pl.BlockSpec((n_e, tm), lambda i: (0, i)),
            pl.BlockSpec((tm, LANES), lambda i: (i, 0)),
            pl.BlockSpec((n_e, tm), lambda i: (0, i)),
            pl.BlockSpec((None, 8, LANES), lambda i: (i, 0, 0)),
            pl.BlockSpec((tm, d), lambda i: (i, 0)),
        ],
        out_shape=[
            jax.ShapeDtypeStruct((n_e, m), F32),
            jax.ShapeDtypeStruct((m, LANES), F32),
            jax.ShapeDtypeStruct((n_e, m), F32),
            jax.ShapeDtypeStruct((nt, 8, LANES), F32),
            jax.ShapeDtypeStruct((m, d), BF16),
        ],
        compiler_params=_cparams(1),
        name="router",
    )(x, mod, g.reshape(1, d), wr, br)


def _count_le(sorted_vals, queries):
    return jnp.sum(sorted_vals[..., None, :] <= queries[..., :, None], axis=-1).astype(jnp.int32)


def _moe_plan(counts, m):
    t_n, e_n = counts.shape
    per_mt = MOE_MT // MOE_BLK
    g_max = 2 * m // MOE_BLK + t_n * e_n
    pad_max = (per_mt - 1) * e_n
    nblk = (counts + MOE_BLK - 1) // MOE_BLK
    nb_e = jnp.sum(nblk, axis=0)
    cap_e = (nb_e + per_mt - 1) // per_mt * per_mt
    start_e = jnp.cumsum(cap_e) - cap_e
    dst0 = (start_e[None, :] + jnp.cumsum(nblk, axis=0) - nblk).reshape(-1)
    flat = nblk.reshape(-1)
    cum = jnp.cumsum(flat)
    g = jnp.minimum(jnp.arange(g_max, dtype=jnp.int32), cum[-1] - 1)
    p = jnp.minimum(_count_le(cum, g), t_n * e_n - 1)
    rb = g - (cum[p] - flat[p])
    n_blocks = g_max + pad_max
    z_max = n_blocks - 2 * m // MOE_BLK
    used = jnp.zeros((n_blocks,), jnp.int32).at[dst0[p] + rb].set(1)
    free = jnp.argsort(used, stable=True).astype(jnp.int32)
    z = jnp.minimum(jnp.arange(z_max, dtype=jnp.int32), n_blocks - cum[-1] - 1)
    never = jnp.full((z_max,), MOE_NEVER, jnp.int32)
    g_tile = jnp.concatenate([p // e_n, jnp.full((z_max,), t_n - 1, jnp.int32)])
    g_exp = jnp.concatenate([p % e_n, jnp.zeros((z_max,), jnp.int32)])
    g_rb = jnp.concatenate([rb, never])
    g_dst = jnp.concatenate([dst0[p] + rb, free[z]])
    mt_max = (g_max + pad_max) // per_mt
    n_mt = (jnp.sum(cap_e) // per_mt).astype(jnp.int32)
    mt = jnp.minimum(jnp.arange(mt_max, dtype=jnp.int32), n_mt - 1)
    mt_exp = jnp.minimum(_count_le(jnp.cumsum(cap_e) // per_mt, mt), e_n - 1)
    s_max = 2 * MOE_TM // MOE_BLK + e_n
    cum_t = jnp.cumsum(nblk, axis=1)
    n_t = cum_t[:, -1:]
    s_all = jnp.arange(s_max, dtype=jnp.int32)[None, :]
    s = jnp.minimum(s_all, n_t - 1)
    s_exp = jnp.minimum(_count_le(cum_t, s), e_n - 1)
    s_rb = s - (jnp.take_along_axis(cum_t, s_exp, axis=1) - jnp.take_along_axis(nblk, s_exp, axis=1))
    s_src = jnp.take_along_axis(dst0.reshape(t_n, e_n), s_exp, axis=1) + s_rb
    s_rb = jnp.where(s_all < n_t, s_rb, MOE_NEVER)
    i32 = lambda a: a.astype(jnp.int32)
    return dict(g_tile=i32(g_tile), g_exp=i32(g_exp), g_rb=i32(g_rb), g_dst=i32(g_dst), n_mt=i32(n_mt.reshape(1)),
                mt_exp=i32(mt_exp), s_exp=i32(s_exp.reshape(-1)), s_rb=i32(s_rb.reshape(-1)),
                s_src=i32(s_src.reshape(-1)), n_blocks=n_blocks, s_max=s_max)


def _moe_gather_kernel(tile_ref, exp_ref, rb_ref, dst_ref, h_ref, post_ref, cmbt_ref, o_ref, w_ref):
    g = pl.program_id(0)
    fill = rb_ref[g] >= MOE_NEVER

    @pl.when(fill)
    def _():
        o_ref[...] = jnp.zeros_like(o_ref)
        w_ref[...] = jnp.zeros_like(w_ref)

    @pl.when(jnp.logical_not(fill))
    def _():
        row = post_ref[pl.ds(exp_ref[g], 1), :]
        tm = row.shape[1]
        want = (lax.broadcasted_iota(jnp.int32, (MOE_BLK, tm), 0) + rb_ref[g] * MOE_BLK).astype(F32)
        match = row == want
        o_ref[...] = _dot(jnp.where(match, 1.0, 0.0).astype(BF16), h_ref[...]).astype(o_ref.dtype)
        cw = jnp.sum(jnp.where(match, cmbt_ref[pl.ds(exp_ref[g], 1), :], 0.0), axis=-1, keepdims=True)
        w_ref[...] = jnp.broadcast_to(cw, w_ref.shape)


def _moe_gather(h, post, cmbt, plan):
    m, d = h.shape
    n_e = post.shape[0]
    n = plan["g_tile"].shape[0]
    rows = plan["n_blocks"] * MOE_BLK
    return pl.pallas_call(
        _moe_gather_kernel,
        grid_spec=pltpu.PrefetchScalarGridSpec(
            num_scalar_prefetch=4,
            grid=(n,),
            in_specs=[
                pl.BlockSpec((MOE_TM, d), lambda g, t, e, r, ds: (t[g], 0)),
                pl.BlockSpec((n_e, MOE_TM), lambda g, t, e, r, ds: (0, t[g])),
                pl.BlockSpec((n_e, MOE_TM), lambda g, t, e, r, ds: (0, t[g])),
            ],
            out_specs=[
                pl.BlockSpec((MOE_BLK, d), lambda g, t, e, r, ds: (ds[g], 0)),
                pl.BlockSpec((MOE_BLK, LANES), lambda g, t, e, r, ds: (ds[g], 0)),
            ],
        ),
        out_shape=[jax.ShapeDtypeStruct((rows, d), BF16), jax.ShapeDtypeStruct((rows, LANES), F32)],
        compiler_params=_cparams(1),
        name="moe_gather",
    )(plan["g_tile"], plan["g_exp"], plan["g_rb"], plan["g_dst"], h, post, cmbt)


def _moe_ffn_kernel(exp_ref, n_ref, x_ref, rw_ref, w1_ref, w3_ref, w2_ref, o_ref, acc_ref, *, nc):
    c = pl.program_id(1)

    @pl.when(pl.program_id(0) < n_ref[0])
    def _():
        @pl.when(c == 0)
        def _():
            acc_ref[...] = jnp.zeros_like(acc_ref)

        x = x_ref[...]
        fc = w1_ref.shape[1]
        for k0 in range(0, fc, MOE_SUB):
            k1 = min(k0 + MOE_SUB, fc)
            a = _dot(x, w1_ref[:, k0:k1])
            a = (a * jax.nn.sigmoid(a) * _dot(x, w3_ref[:, k0:k1])).astype(BF16)
            acc_ref[...] += _dot(a, w2_ref[k0:k1, :])

        @pl.when(c == nc - 1)
        def _():
            o_ref[...] = (rw_ref[:, 0:1] * acc_ref[...]).astype(o_ref.dtype)

    @pl.when(pl.program_id(0) >= n_ref[0])
    def _():
        o_ref[...] = jnp.zeros_like(o_ref)


def _moe_ffn(xg, row_w, w1, w3, w2, wl, plan, fc):
    d = xg.shape[1]
    fe = w1.shape[3]
    nc = fe // fc
    n_mt = plan["mt_exp"].shape[0]

    def tile(mt, n):
        return jnp.minimum(mt, n[0] - 1)

    def chunk(mt, c, n):
        return jnp.where(mt < n[0], c, nc - 1)

    kern = functools.partial(_moe_ffn_kernel, nc=nc)
    return pl.pallas_call(
        kern,
        grid_spec=pltpu.PrefetchScalarGridSpec(
            num_scalar_prefetch=2,
            grid=(n_mt, nc),
            in_specs=[
                pl.BlockSpec((MOE_MT, d), lambda mt, c, e, n: (tile(mt, n), 0)),
                pl.BlockSpec((MOE_MT, LANES), lambda mt, c, e, n: (tile(mt, n), 0)),
                pl.BlockSpec((None, None, d, fc), lambda mt, c, e, n: (wl, e[mt], 0, chunk(mt, c, n))),
                pl.BlockSpec((None, None, d, fc), lambda mt, c, e, n: (wl, e[mt], 0, chunk(mt, c, n))),
                pl.BlockSpec((None, None, fc, d), lambda mt, c, e, n: (wl, e[mt], chunk(mt, c, n), 0)),
            ],
            out_specs=pl.BlockSpec((MOE_MT, d), lambda mt, c, e, n: (mt, 0)),
            scratch_shapes=[pltpu.VMEM((MOE_MT, d), F32)],
        ),
        out_shape=jax.ShapeDtypeStruct((n_mt * MOE_MT, d), BF16),
        compiler_params=_cparams(2),
        name="moe_ffn",
    )(plan["mt_exp"], plan["n_mt"], xg, row_w, w1, w3, w2)


def _moe_scatter_kernel(src_ref, exp_ref, rb_ref, *refs, ns):
    y_refs, (pos_ref, x_ref, mod_ref, o_ref, acc_ref) = refs[:MOE_SCATTER_SLOTS], refs[MOE_SCATTER_SLOTS:]
    i = pl.program_id(0)
    s = pl.program_id(1)

    @pl.when(s == 0)
    def _():
        acc_ref[...] = jnp.zeros_like(acc_ref)

    lane = lax.broadcasted_iota(jnp.int32, pos_ref.shape, 1)
    lane_f = lane.astype(F32)

    def onehot(slot):
        pos = jnp.sum(jnp.where(lane == exp_ref[slot], pos_ref[...], 0.0), axis=-1, keepdims=True)
        return jnp.where(pos - (rb_ref[slot] * MOE_BLK).astype(F32) == lane_f, 1.0, 0.0).astype(BF16)

    s0 = i * ns + MOE_SCATTER_SLOTS * s
    pt = jnp.concatenate([onehot(s0 + k) for k in range(MOE_SCATTER_SLOTS)], axis=1)
    acc_ref[...] += _dot(pt, jnp.concatenate([y[...] for y in y_refs], axis=0))

    @pl.when(s == ns // MOE_SCATTER_SLOTS - 1)
    def _():
        o_ref[...] = x_ref[...] + mod_ref[ROW_GA_FFN:ROW_GA_FFN + 1, :] * acc_ref[...]


def _moe_scatter(yg, pos, x, mod, plan, seq):
    m, d = x.shape
    ns = plan["s_max"]
    tpb = seq // MOE_TM
    kern = functools.partial(_moe_scatter_kernel, ns=ns)

    def y_spec(k):
        return pl.BlockSpec((MOE_BLK, d), lambda i, s, sr, ex, rb: (sr[i * ns + MOE_SCATTER_SLOTS * s + k], 0))

    return pl.pallas_call(
        kern,
        grid_spec=pltpu.PrefetchScalarGridSpec(
            num_scalar_prefetch=3,
            grid=(m // MOE_TM, ns // MOE_SCATTER_SLOTS),
            in_specs=[y_spec(k) for k in range(MOE_SCATTER_SLOTS)] + [
                pl.BlockSpec((MOE_TM, LANES), lambda i, s, sr, ex, rb: (i, 0)),
                pl.BlockSpec((MOE_TM, d), lambda i, s, sr, ex, rb: (i, 0)),
                pl.BlockSpec((None, 6, d), lambda i, s, sr, ex, rb: (i // tpb, 0, 0)),
            ],
            out_specs=pl.BlockSpec((MOE_TM, d), lambda i, s, sr, ex, rb: (i, 0)),
            scratch_shapes=[pltpu.VMEM((MOE_TM, d), F32)],
        ),
        out_shape=jax.ShapeDtypeStruct((m, d), F32),
        compiler_params=_cparams(2),
        name="moe_scatter",
    )(plan["s_src"], plan["s_exp"], plan["s_rb"], *([yg] * MOE_SCATTER_SLOTS), pos, x, mod)


def _moe_layer(x, mod, g, w_router, b_router, w1, w3, w2, wl, seq, fc):
    m = x.shape[0]
    cmbt, pos, post, counts, h = _router(x, mod, g, w_router, b_router, seq)
    plan = _moe_plan(counts[:, 0, :w_router.shape[1]].astype(jnp.int32), m)
    xg, row_w = _moe_gather(h, post, cmbt, plan)
    yg = _moe_ffn(xg, row_w, w1, w3, w2, wl, plan, fc)
    return _moe_scatter(yg, pos, x, mod, plan, seq)


def _kv_chunks(proj3):
    b, s, _ = proj3.shape
    n = 2 * NSA_KV_HEADS
    t = proj3[:, :, CB_KC * LANES:(CB_KC + n) * LANES].reshape(b, s, n, HEAD_DIM)
    return t.transpose(0, 2, 1, 3).reshape(b, n, s // CMP_STRIDE, CMP_STRIDE * HEAD_DIM)


def kernel(x, c, w_ada, b_ada, g_mix, g_ffn, w_in, conv_w, conv_b, conv_ln_g, conv_ln_b, w_conv_out, w_sb_out,
           nsa_cmp_pos_k, nsa_cmp_pos_v, nsa_cmp_wk, nsa_cmp_wv, nsa_q_g, nsa_kc_g, nsa_ks_g, nsa_kw_g, w_nsa_out, w_o,
           rel_bias, ffn_w1, ffn_w3, ffn_w2, moe_router, moe_router_b, moe_w1, moe_w3, moe_w2):
    b, s, d = x.shape
    m = b * s
    depth = w_ada.shape[0]
    mod_all = _ada_all(c, w_ada, b_ada)
    bias_c, bias_t = _bias_tables(rel_bias, s)
    w_in_t = jnp.swapaxes(w_in, 1, 2)
    w_conv_out, w_sb_out, w_nsa_out, w_o, ffn_w1, ffn_w3, ffn_w2, moe_w1, moe_w3, moe_w2 = (
        w.astype(BF16) for w in (w_conv_out, w_sb_out, w_nsa_out, w_o, ffn_w1, ffn_w3, ffn_w2, moe_w1, moe_w3, moe_w2))
    xf = x.reshape(m, d)
    for l in range(depth):
        mod = mod_all[l].reshape(b, 6, d)
        proj = _in_proj(xf, mod, g_mix[l], w_in_t, l, s)
        proj3 = proj.reshape(b, s, NP)
        u_act = _conv_module(proj, conv_w[l], conv_b[l], conv_ln_g[l], conv_ln_b[l], b, s)
        sb = _sb_attention(proj3)
        kcb, vcb, ksn, kwn = _nsa_prep(proj3, nsa_cmp_pos_k[l], nsa_cmp_pos_v[l], nsa_cmp_wk[l], nsa_cmp_wv[l],
                                       nsa_kc_g[l], nsa_ks_g[l], nsa_kw_g[l])
        nsa = _nsa_attention(proj3, kcb, vcb, ksn, kwn, bias_c, bias_t, nsa_q_g[l])
        merged = _merge(u_act, sb.reshape(m, -1), nsa.reshape(m, -1), proj, w_conv_out, w_sb_out, w_nsa_out, l)
        xf = _mm_res(merged, w_o, l, xf, mod, ROW_GA_MIX, s, tm=512, tn=2048)
        i = l // 2
        if l % 2 == 0:
            act = _ffn_up(xf, mod, g_ffn[l], ffn_w1, ffn_w3, i, s)
            xf = _mm_res(act, ffn_w2, i, xf, mod, ROW_GA_FFN, s, tm=1024, tn=512)
        else:
            xf = _moe_layer(xf, mod, g_ffn[l], moe_router[i], moe_router_b[i], moe_w1, moe_w3, moe_w2, i, s, MOE_FC)
    return xf.reshape(b, s, d)
```

```python
import functools
import math

import numpy as np
import jax
import jax.numpy as jnp
from jax import lax
from jax.experimental import pallas as pl
from jax.experimental.pallas import tpu as pltpu

F32 = jnp.float32
BF16 = jnp.bfloat16

D_MODEL = 2048
DEPTH = 4
D_CONV = 1024
CONV_WIDTH = 31
SB_HEADS = 8
HEAD_DIM = 128
NSA_HEADS = 8
NSA_KV_HEADS = 2
NSA_REP = NSA_HEADS // NSA_KV_HEADS
CMP_LEN = 32
CMP_STRIDE = 16
SLC_LEN = 64
SLC_TOP_N = 16
N_LOCAL_BLOCKS = 2
WINDOW = 512
FORCE_SCORE = 1e4
REL_BUCKETS = 32
REL_MAX_DIST = 128
D_FF = 5632
N_EXPERTS = 8
D_FF_EXPERT = 2816
EPS = 1e-6
NEG_INF = -1e30
TINY = 1e-20

LANES = 128
SUBLANES = 8
V7X_VMEM_LIMIT_BYTES = 56 * 1024 * 1024

NP_BLOCKS = 112
NP = NP_BLOCKS * LANES
CB_GLU_A, CB_GLU_G = 0, 8
CB_SB_Q, CB_SB_K, CB_SB_V = 16, 24, 32
CB_NQ = 40
CB_KC, CB_VC, CB_KS, CB_VS, CB_KW, CB_VW = 48, 50, 52, 54, 56, 58
CB_NGATE = 60
CB_MERGE = 64
RAW_GATE_COL = 7680
RAW_MERGE_COL = 7704
IN_PROJ_TN = 1024
ROW_SH_MIX, ROW_SC_MIX, ROW_GA_MIX, ROW_SH_FFN, ROW_SC_FFN, ROW_GA_FFN = range(6)

SCALE = HEAD_DIM ** -0.5
SB_Q_SCALE = SCALE * math.log2(math.e)
TQ = 128
SB_TILE = 256
SB_GROUP = 8
SEL_CHUNK = 512
NSA_QT = 256
NSA_SUB = NSA_QT // TQ
WIN_SPAN = WINDOW + NSA_QT
BT_DIAG, BT_SUB, BT_FAR, BT_FAR_UPPER, BT_MASKED = range(5)
CONV_HALO = 32
MOE_TM = 1024
MOE_BLK = 128
MOE_MT = 512
MOE_FC = 1408
MOE_SUB = 256
MOE_SCATTER_SLOTS = 4
MOE_NEVER = 1 << 20


def _cparams(n_axes):
    return pltpu.CompilerParams(
        dimension_semantics=("arbitrary",) * n_axes,
        vmem_limit_bytes=V7X_VMEM_LIMIT_BYTES,
    )


def _dot(a, b):
    return jnp.dot(a, b, preferred_element_type=F32)


def _dot_t(a, b):
    return lax.dot_general(a, b, (((1,), (1,)), ((), ())), preferred_element_type=F32)


def _split_dot(x, w01):
    hi = x.astype(BF16)
    lo = (x - hi.astype(F32)).astype(BF16)
    return _dot(hi, w01) + _dot(lo, w01)


def _norm_mod(x, g, sh, sc):
    ms = jnp.mean(x * x, axis=-1, keepdims=True)
    return (x * lax.rsqrt(ms + EPS) * g) * (1.0 + sc) + sh


def _rms(x, g):
    ms = jnp.mean(x * x, axis=-1, keepdims=True)
    return x * lax.rsqrt(ms + EPS) * g


def _ada_kernel(c_ref, w_ref, b_ref, o_ref):
    c = c_ref[...]
    ca = (c * jax.nn.sigmoid(c)).astype(BF16)
    o_ref[...] = _dot(ca, w_ref[...].astype(BF16)) + b_ref[...]


def _ada_all(c, w_ada, b_ada):
    depth, d, n = w_ada.shape
    nb = c.shape[0]
    b = 16
    c = jnp.zeros((b, d), c.dtype).at[:nb].set(c)
    tn = 1024
    out = pl.pallas_call(
        _ada_kernel,
        grid=(depth, n // tn),
        in_specs=[
            pl.BlockSpec((b, d), lambda l, j: (0, 0)),
            pl.BlockSpec((None, d, tn), lambda l, j: (l, 0, j)),
            pl.BlockSpec((None, 1, tn), lambda l, j: (l, 0, j)),
        ],
        out_specs=pl.BlockSpec((None, b, tn), lambda l, j: (l, 0, j)),
        out_shape=jax.ShapeDtypeStruct((depth, b, n), F32),
        compiler_params=_cparams(2),
        name="ada",
    )(c, w_ada, b_ada.reshape(depth, 1, n))
    return out[:, :nb]


def _bias_kernel(tab_ref, bk_ref, o_ref):
    h = pl.program_id(0)
    bk = bk_ref[...]
    acc = jnp.zeros(bk.shape, F32)
    for b in range(REL_BUCKETS):
        acc = jnp.where(bk == b, tab_ref[b, h], acc)
    o_ref[...] = acc


def _bias_expand(rel_bias, buckets):
    r = buckets.shape[0]
    tr = r
    return pl.pallas_call(
        _bias_kernel,
        grid=(NSA_HEADS, r // tr),
        in_specs=[
            pl.BlockSpec(memory_space=pltpu.SMEM),
            pl.BlockSpec((tr, LANES), lambda h, i: (i, 0)),
        ],
        out_specs=pl.BlockSpec((None, tr, LANES), lambda h, i: (h, i, 0)),
        out_shape=jax.ShapeDtypeStruct((NSA_HEADS, r, LANES), F32),
        compiler_params=_cparams(2),
        name="bias_expand",
    )(rel_bias, buckets)


def _rel_bucket(dist):
    n = jnp.maximum(dist, 0)
    max_exact = REL_BUCKETS // 2
    nf = jnp.maximum(n, 1).astype(F32)
    large = max_exact + (jnp.log(nf / max_exact) / math.log(REL_MAX_DIST / max_exact) * (REL_BUCKETS - max_exact)).astype(jnp.int32)
    large = jnp.minimum(large, REL_BUCKETS - 1)
    return jnp.where(n < max_exact, n, large)


def _bias_tables(rel_bias, s):
    t = jnp.arange(s, dtype=jnp.int32)[:, None]
    cend = jnp.arange(LANES, dtype=jnp.int32)[None, :] * CMP_STRIDE + (CMP_LEN - 1)
    q = jnp.arange(TQ, dtype=jnp.int32)[:, None]
    k = jnp.arange(TQ, dtype=jnp.int32)[None, :]
    bk_t = jnp.concatenate([_rel_bucket(q - k), _rel_bucket(TQ + q - k), _rel_bucket(2 * TQ + q - k)], axis=0)
    out = _bias_expand(rel_bias, jnp.concatenate([_rel_bucket(t - cend), bk_t], axis=0))
    bias_c = jnp.where(t >= cend, out[:, :s], NEG_INF).reshape(NSA_KV_HEADS, NSA_REP, s, LANES)
    diag, sub, far = out[:, s:s + TQ], out[:, s + TQ:s + 2 * TQ], out[:, s + 2 * TQ:]
    tiles = jnp.stack([jnp.where(q >= k, diag, NEG_INF), sub, far, jnp.where(k > q, far, NEG_INF),
                       jnp.full_like(far, NEG_INF)], axis=1)
    tiles = tiles.reshape(NSA_KV_HEADS, NSA_REP, 5, TQ, TQ).transpose(0, 2, 1, 3, 4)
    return bias_c, tiles.reshape(NSA_KV_HEADS, 5, NSA_REP * TQ, TQ)


def _in_proj_kernel(x_ref, mod_ref, g_ref, w_ref, o_ref, h_ref):
    j = pl.program_id(1)

    @pl.when(j == 0)
    def _():
        h = _norm_mod(x_ref[...], g_ref[...], mod_ref[ROW_SH_MIX:ROW_SH_MIX + 1, :], mod_ref[ROW_SC_MIX:ROW_SC_MIX + 1, :])
        h_ref[...] = h.astype(BF16)

    scale = jnp.where(j == CB_SB_Q * LANES // IN_PROJ_TN, SB_Q_SCALE, 1.0)
    o_ref[...] = _dot_t(h_ref[...], (w_ref[...] * scale).astype(BF16)).astype(o_ref.dtype)


def _in_proj(x, mod, g, wt, wl, seq, tm=1024):
    m, d = x.shape
    tn = IN_PROJ_TN
    tpb = seq // tm
    assert (CB_SB_K - CB_SB_Q) * LANES == tn and CB_SB_Q * LANES % tn == 0 and CB_MERGE * LANES % tn == 0
    shift = CB_MERGE * LANES - RAW_MERGE_COL

    assert tn % SUBLANES == 0 and shift % SUBLANES == 0

    def w_rows(i, j):
        start = jnp.where(j * tn < CB_MERGE * LANES, j * (tn // SUBLANES), j * (tn // SUBLANES) - shift // SUBLANES)
        return wl, start * SUBLANES, 0

    return pl.pallas_call(
        _in_proj_kernel,
        grid=(m // tm, NP // tn),
        in_specs=[
            pl.BlockSpec((tm, d), lambda i, j: (i, 0)),
            pl.BlockSpec((None, 6, d), lambda i, j: (i // tpb, 0, 0)),
            pl.BlockSpec((1, d), lambda i, j: (0, 0)),
            pl.BlockSpec((None, pl.Element(tn), pl.Element(d)), w_rows),
        ],
        out_specs=pl.BlockSpec((tm, tn), lambda i, j: (i, j)),
        out_shape=jax.ShapeDtypeStruct((m, NP), BF16),
        scratch_shapes=[pltpu.VMEM((tm, d), BF16)],
        compiler_params=_cparams(2),
        name="in_proj",
    )(x, mod, g.reshape(1, d), wt)


def _conv_kernel(a_ref, g_ref, ap_ref, gp_ref, cw_ref, cb_ref, lng_ref, lnb_ref, o_ref, ubuf, vbuf, shifted, *, ts):
    i = pl.program_id(1)
    ubuf[CONV_HALO:, :] = a_ref[...].astype(F32) * jax.nn.sigmoid(g_ref[...].astype(F32))
    up = ap_ref[...].astype(F32) * jax.nn.sigmoid(gp_ref[...].astype(F32))
    ubuf[:CONV_HALO, :] = jnp.where(i > 0, up, 0.0)

    first = CONV_HALO - (CONV_WIDTH - 1)
    span = ts + CONV_HALO - SUBLANES

    def chunk(c, carry):
        c0 = pl.multiple_of(c * LANES, LANES)
        for s in range(1, SUBLANES):
            shifted[s - 1] = ubuf[pl.ds(s, span), pl.ds(c0, LANES)]
        acc = jnp.zeros((ts, LANES), F32) + cb_ref[:, pl.ds(c0, LANES)]
        for k in range(CONV_WIDTH):
            a, s = divmod(first + k, SUBLANES)
            rows = ubuf[pl.ds(a * SUBLANES, ts), pl.ds(c0, LANES)] if s == 0 else shifted[s - 1, pl.ds(a * SUBLANES, ts), :]
            acc = acc + cw_ref[k:k + 1, pl.ds(c0, LANES)] * rows
        vbuf[:, pl.ds(c0, LANES)] = acc
        return carry

    lax.fori_loop(0, D_CONV // LANES, chunk, 0)
    v = vbuf[...]
    mu = jnp.mean(v, axis=-1, keepdims=True)
    vc = v - mu
    var = jnp.mean(vc * vc, axis=-1, keepdims=True)
    y = vc * lax.rsqrt(var + EPS) * lng_ref[...] + lnb_ref[...]
    o_ref[...] = (y * jax.nn.sigmoid(y)).astype(o_ref.dtype)


def _conv_module(proj, conv_w, conv_b, ln_g, ln_b, batch, seq, ts=256):
    m = proj.shape[0]
    nt = seq // ts
    hb = ts // CONV_HALO
    kern = functools.partial(_conv_kernel, ts=ts)

    def prev_idx(col):
        return lambda b, i: (jnp.maximum((b * nt + i) * hb - 1, 0), col)

    return pl.pallas_call(
        kern,
        grid=(batch, nt),
        in_specs=[
            pl.BlockSpec((ts, D_CONV), lambda b, i: (b * nt + i, 0)),
            pl.BlockSpec((ts, D_CONV), lambda b, i: (b * nt + i, 1)),
            pl.BlockSpec((CONV_HALO, D_CONV), prev_idx(0)),
            pl.BlockSpec((CONV_HALO, D_CONV), prev_idx(1)),
            pl.BlockSpec((CONV_WIDTH, D_CONV), lambda b, i: (0, 0)),
            pl.BlockSpec((1, D_CONV), lambda b, i: (0, 0)),
            pl.BlockSpec((1, D_CONV), lambda b, i: (0, 0)),
            pl.BlockSpec((1, D_CONV), lambda b, i: (0, 0)),
        ],
        out_specs=pl.BlockSpec((ts, D_CONV), lambda b, i: (b * nt + i, 0)),
        out_shape=jax.ShapeDtypeStruct((m, D_CONV), BF16),
        scratch_shapes=[pltpu.VMEM((CONV_HALO + ts, D_CONV), F32), pltpu.VMEM((ts, D_CONV), F32),
                        pltpu.VMEM((SUBLANES - 1, ts + CONV_HALO - SUBLANES, LANES), F32)],
        compiler_params=_cparams(2),
        name="conv_module",
    )(proj, proj, proj, proj, conv_w, conv_b.reshape(1, -1), ln_g.reshape(1, -1), ln_b.reshape(1, -1))


def _sb_kernel(q_ref, k_ref, v_ref, o_ref, lb_s, lk_s, sums_s, acc_s, rsum_s):
    i = pl.program_id(2)
    t = SB_TILE
    rows = SB_GROUP * t
    row = lax.broadcasted_iota(jnp.int32, (t, t), 0)
    col = lax.broadcasted_iota(jnp.int32, (t, t), 1)
    later = jnp.where(row > col, 1.0, 0.0).astype(BF16)
    later = jnp.concatenate([later, jnp.ones((t, LANES), BF16)], axis=1)
    heads = [slice(h * HEAD_DIM, (h + 1) * HEAD_DIM) for h in range(SB_GROUP)]

    before = col < row

    def tile(j, diag):
        k0 = pl.multiple_of(j * t, t)
        for h, hs in enumerate(heads):
            r = slice(h * t, (h + 1) * t)
            z = _dot_t(q_ref[:, hs], k_ref[pl.ds(k0, t), hs])
            if diag:
                z = jnp.where(before, z, NEG_INF)
            sp = jnp.log2(1.0 + jnp.exp2(-jnp.abs(z)))
            log_beta = jnp.minimum(z, 0.0) - sp
            lb_s[r, :] = log_beta
            lk_s[r, :] = (log_beta - z).astype(BF16)
        sums_s[...] = _dot(lk_s[...], later)
        for h, hs in enumerate(heads):
            r = slice(h * t, (h + 1) * t)
            survive = sums_s[r, 0:t] + jnp.concatenate([rsum_s[r, :]] * (t // LANES), axis=1)
            a = jnp.exp2(lb_s[r, :] + survive).astype(BF16)
            acc_s[r, :] += _dot(a, v_ref[pl.ds(k0, t), hs])
            rsum_s[r, :] += sums_s[r, t:]

    acc_s[...] = jnp.zeros_like(acc_s)
    rsum_s[...] = jnp.zeros_like(rsum_s)
    tile(i, True)

    def body(jj, carry):
        tile(i - 1 - jj, False)
        return carry

    lax.fori_loop(0, i, body, 0)
    for h in range(SB_GROUP):
        o_ref[:, h * HEAD_DIM:(h + 1) * HEAD_DIM] = acc_s[h * t:(h + 1) * t, :].astype(o_ref.dtype)


def _sb_attention(proj3):
    b, s, _ = proj3.shape
    gw = SB_GROUP * HEAD_DIM
    rows = SB_GROUP * SB_TILE
    return pl.pallas_call(
        _sb_kernel,
        grid=(b, SB_HEADS // SB_GROUP, s // SB_TILE),
        in_specs=[
            pl.BlockSpec((None, SB_TILE, gw), lambda bi, hg, i: (bi, i, CB_SB_Q // SB_GROUP + hg)),
            pl.BlockSpec((None, s, gw), lambda bi, hg, i: (bi, 0, CB_SB_K // SB_GROUP + hg)),
            pl.BlockSpec((None, s, gw), lambda bi, hg, i: (bi, 0, CB_SB_V // SB_GROUP + hg)),
        ],
        out_specs=pl.BlockSpec((None, SB_TILE, gw), lambda bi, hg, i: (bi, i, hg)),
        out_shape=jax.ShapeDtypeStruct((b, s, SB_HEADS * HEAD_DIM), BF16),
        scratch_shapes=[pltpu.VMEM((rows, SB_TILE), F32), pltpu.VMEM((rows, SB_TILE), BF16),
                        pltpu.VMEM((rows, SB_TILE + LANES), F32), pltpu.VMEM((rows, HEAD_DIM), F32),
                        pltpu.VMEM((rows, LANES), F32)],
        compiler_params=_cparams(3),
        name="sb_attention",
    )(proj3, proj3, proj3)


def _nsa_prep_kernel(kc_ref, vc_ref, ks_ref, kw_ref, pk_ref, pv_ref, wk_ref, wv_ref, kcg_ref, ksg_ref, kwg_ref,
                     kcb_ref, vcb_ref, ksn_ref, kwn_ref):
    half = CMP_STRIDE * HEAD_DIM

    def compress(a_ref, p_ref, w_ref):
        a = a_ref[...].astype(F32)
        top = _dot((a + p_ref[0:1, :]).astype(BF16), w_ref[0:half, :])
        bot = _dot((a + p_ref[1:2, :]).astype(BF16), w_ref[half:2 * half, :])
        return top + pltpu.roll(bot, bot.shape[0] - 1, axis=0)

    kcb_ref[...] = _rms(compress(kc_ref, pk_ref, wk_ref), kcg_ref[...]).astype(BF16)
    vcb_ref[...] = compress(vc_ref, pv_ref, wv_ref).astype(BF16)
    kwn_ref[...] = _rms(kw_ref[...].astype(F32), kwg_ref[...]).astype(BF16)
    ksn_ref[:, 0:HEAD_DIM] = _rms(ks_ref[...].astype(F32), ksg_ref[...]).astype(BF16)
    s = ks_ref.shape[0]
    blk = jnp.right_shift(lax.broadcasted_iota(jnp.int32, (s, LANES), 0), int(math.log2(SLC_LEN)))
    onehot = blk == lax.broadcasted_iota(jnp.int32, (s, LANES), 1)
    ksn_ref[:, HEAD_DIM:HEAD_DIM + LANES] = jnp.where(onehot, 1.0, 0.0).astype(BF16)


def _nsa_prep(proj3, pos_k, pos_v, wk, wv, kc_g, ks_g, kw_g):
    b, s, _ = proj3.shape
    g_n = NSA_KV_HEADS
    kv_chunks = _kv_chunks(proj3)
    nch = s // CMP_STRIDE
    half = CMP_STRIDE * HEAD_DIM
    vec = lambda: pl.BlockSpec((1, HEAD_DIM), lambda bi, g: (0, 0))
    small = jax.ShapeDtypeStruct((b, g_n, nch, HEAD_DIM), BF16)
    full = jax.ShapeDtypeStruct((b, g_n, s, HEAD_DIM), BF16)
    aug = jax.ShapeDtypeStruct((b, g_n, s, HEAD_DIM + LANES), BF16)
    return pl.pallas_call(
        _nsa_prep_kernel,
        grid=(b, g_n),
        in_specs=[
            pl.BlockSpec((None, None, nch, half), lambda bi, g: (bi, g, 0, 0)),
            pl.BlockSpec((None, None, nch, half), lambda bi, g: (bi, g_n + g, 0, 0)),
            pl.BlockSpec((None, s, HEAD_DIM), lambda bi, g: (bi, 0, CB_KS + g)),
            pl.BlockSpec((None, s, HEAD_DIM), lambda bi, g: (bi, 0, CB_KW + g)),
            pl.BlockSpec((2, half), lambda bi, g: (0, 0)),
            pl.BlockSpec((2, half), lambda bi, g: (0, 0)),
            pl.BlockSpec((2 * half, HEAD_DIM), lambda bi, g: (0, 0)),
            pl.BlockSpec((2 * half, HEAD_DIM), lambda bi, g: (0, 0)),
            vec(), vec(), vec(),
        ],
        out_specs=[
            pl.BlockSpec((None, None, nch, HEAD_DIM), lambda bi, g: (bi, g, 0, 0)),
            pl.BlockSpec((None, None, nch, HEAD_DIM), lambda bi, g: (bi, g, 0, 0)),
            pl.BlockSpec((None, None, s, HEAD_DIM + LANES), lambda bi, g: (bi, g, 0, 0)),
            pl.BlockSpec((None, None, s, HEAD_DIM), lambda bi, g: (bi, g, 0, 0)),
        ],
        out_shape=[small, small, aug, full],
        compiler_params=_cparams(2),
        name="nsa_prep",
    )(kv_chunks, kv_chunks, proj3, proj3, pos_k.reshape(2, half), pos_v.reshape(2, half),
      wk.astype(BF16), wv.astype(BF16), kc_g.reshape(1, -1), ks_g.reshape(1, -1), kw_g.reshape(1, -1))


def _nsa_kernel(q_ref, kcb_ref, vcb_ref, ksa_ref, vs_ref, kwn_ref, vw_ref, gate_ref, bc_ref, bt_ref, qg_ref, ovt_ref,
                o_ref):
    g = pl.program_id(1)
    i = pl.program_id(2)
    r_n = NSA_REP
    groups = [(a, r) for a in range(NSA_SUB) for r in range(r_n)]
    sub = lambda a: slice(a * TQ, (a + 1) * TQ)
    head = lambda r: slice(r * HEAD_DIM, (r + 1) * HEAD_DIM)

    q = jnp.concatenate([_rms(q_ref[sub(a), head(r)].astype(F32), qg_ref[...]).astype(BF16) for a, r in groups], axis=0)

    logit_c = _dot_t(q, kcb_ref[...]) + jnp.concatenate([bc_ref[r, sub(a), :] for a, r in groups], axis=0)
    m_c = jnp.max(logit_c, axis=-1, keepdims=True)
    p_c = jnp.where(logit_c > 0.5 * NEG_INF, jnp.exp(logit_c - m_c), 0.0)
    p_c = p_c / jnp.maximum(jnp.sum(p_c, axis=-1, keepdims=True), TINY)
    o_c = _dot(p_c.astype(BF16), vcb_ref[...])

    p_sum = jnp.concatenate(
        [sum(p_c[(a * r_n + r) * TQ:(a * r_n + r + 1) * TQ] for r in range(r_n)) for a in range(NSA_SUB)], axis=0)
    hi = p_sum.astype(BF16)
    lo = (p_sum - hi.astype(F32)).astype(BF16)
    n_sel = ovt_ref.shape[0]
    imp = _dot_t(ovt_ref[...], hi) + _dot_t(ovt_ref[...], lo)
    blk = lax.broadcasted_iota(jnp.int32, (n_sel, NSA_QT), 0)
    tb = jnp.right_shift(i * NSA_QT + lax.broadcasted_iota(jnp.int32, (n_sel, NSA_QT), 1), int(math.log2(SLC_LEN)))
    causal_blk = blk <= tb
    forced = (blk == 0) | (causal_blk & (blk > tb - N_LOCAL_BLOCKS))
    score = jnp.where(forced, FORCE_SCORE, jnp.where(causal_blk, imp, -FORCE_SCORE))
    rank = jnp.zeros((n_sel, NSA_QT), F32)
    for b in range(n_sel):
        sb = score[b:b + 1, :]
        beats = (sb > score) | ((sb == score) & (blk > b))
        rank = rank + jnp.where(beats, 1.0, 0.0)
    chosen = (rank < float(min(SLC_TOP_N, n_sel))) & causal_blk
    sel_neg = jnp.where(chosen, 0.0, NEG_INF)
    sel_neg = jnp.concatenate([sel_neg, jnp.zeros((LANES - n_sel, NSA_QT), F32)], axis=0).T.astype(BF16)
    q_aug = jnp.concatenate([q, jnp.concatenate([sel_neg[sub(a)] for a, _ in groups], axis=0)], axis=1)

    def biased(s, first_tile, idx_fn):
        return jnp.concatenate(
            [s[:, c * TQ:(c + 1) * TQ]
             + jnp.concatenate([bt_ref[idx_fn(i * NSA_SUB + a - (first_tile + c))] for a in range(NSA_SUB)], axis=0)
             for c in range(s.shape[1] // TQ)], axis=1)

    w_tile = jnp.maximum(i * NSA_SUB - WINDOW // TQ, 0)
    w0 = pl.multiple_of(w_tile * TQ, TQ)

    def win_idx(off):
        near = jnp.where(off == WINDOW // TQ, BT_FAR_UPPER, jnp.minimum(off, BT_FAR))
        return jnp.where((off < 0) | (off > WINDOW // TQ), BT_MASKED, near)

    s_w = biased(_dot_t(q, kwn_ref[pl.ds(w0, WIN_SPAN), :]), w_tile, win_idx)
    p_w = jnp.exp(s_w - jnp.max(s_w, axis=-1, keepdims=True))
    o_w = _dot(p_w.astype(BF16), vw_ref[pl.ds(w0, WIN_SPAN), :]) / jnp.sum(p_w, axis=-1, keepdims=True)

    def sel_chunk(kc, carry, diag):
        c0 = pl.multiple_of(kc * SEL_CHUNK, SEL_CHUNK)
        idx_fn = (lambda off: jnp.where(off < 0, BT_MASKED, jnp.minimum(off, BT_FAR))) if diag else (
            lambda off: jnp.minimum(off, BT_FAR))
        s = biased(_dot_t(q_aug, ksa_ref[pl.ds(c0, SEL_CHUNK), :]), kc * (SEL_CHUNK // TQ), idx_fn)
        m_blk = jnp.max(s, axis=-1, keepdims=True)
        if diag:
            p = jnp.exp(s - m_blk)
            return m_blk, jnp.sum(p, axis=-1, keepdims=True), _dot(p.astype(BF16), vs_ref[pl.ds(c0, SEL_CHUNK), :])
        m, l, acc = carry
        m_new = jnp.maximum(m, m_blk)
        p = jnp.exp(s - m_new)
        alpha = jnp.exp(m - m_new)
        l = alpha * l + jnp.sum(p, axis=-1, keepdims=True)
        acc = alpha * acc + _dot(p.astype(BF16), vs_ref[pl.ds(c0, SEL_CHUNK), :])
        return m_new, l, acc

    kc_diag = lax.shift_right_logical(i * NSA_SUB + (NSA_SUB - 1), jnp.int32(int(math.log2(SEL_CHUNK // TQ))))
    carry = sel_chunk(kc_diag, None, True)
    _, l_s, acc_s = lax.fori_loop(0, kc_diag, lambda jj, c: sel_chunk(kc_diag - 1 - jj, c, False), carry)
    o_s = acc_s / l_s

    gates = jax.nn.sigmoid(gate_ref[...].astype(F32))
    lane = lax.broadcasted_iota(jnp.int32, gates.shape, 1)

    def gate_col(idx):
        return jnp.sum(jnp.where(lane == idx, gates, 0.0), axis=-1, keepdims=True)

    for r in range(r_n):
        h = g * r_n + r
        g_c, g_s, g_w = gate_col(h), gate_col(NSA_HEADS + h), gate_col(2 * NSA_HEADS + h)
        for a in range(NSA_SUB):
            k = slice((a * r_n + r) * TQ, (a * r_n + r + 1) * TQ)
            o = g_c[sub(a)] * o_c[k] + g_s[sub(a)] * o_s[k] + g_w[sub(a)] * o_w[k]
            o_ref[sub(a), head(r)] = o.astype(o_ref.dtype)


def _overlap_matrix(s):
    nch = s // CMP_STRIDE
    nsel = s // SLC_LEN
    ci = np.arange(nch)[:, None]
    sj = np.arange(nsel)[None, :]
    ov = (ci * CMP_STRIDE <= sj * SLC_LEN + SLC_LEN - 1) & (ci * CMP_STRIDE + CMP_LEN - 1 >= sj * SLC_LEN)
    ov = ov & (ci < nch - 1)
    return jnp.asarray(ov.T, BF16)


def _nsa_attention(proj3, kcb, vcb, ksa, kwn, bias_c, bias_t, q_g):
    b, s, _ = proj3.shape
    g_n, r_n = NSA_KV_HEADS, NSA_REP
    nch = s // CMP_STRIDE
    nsel = s // SLC_LEN
    gw = r_n * HEAD_DIM
    return pl.pallas_call(
        _nsa_kernel,
        grid=(b, g_n, s // NSA_QT),
        in_specs=[
            pl.BlockSpec((None, NSA_QT, gw), lambda bi, g, i: (bi, i, CB_NQ // r_n + g)),
            pl.BlockSpec((None, None, nch, HEAD_DIM), lambda bi, g, i: (bi, g, 0, 0)),
            pl.BlockSpec((None, None, nch, HEAD_DIM), lambda bi, g, i: (bi, g, 0, 0)),
            pl.BlockSpec((None, None, s, HEAD_DIM + LANES), lambda bi, g, i: (bi, g, 0, 0)),
            pl.BlockSpec((None, s, HEAD_DIM), lambda bi, g, i: (bi, 0, CB_VS + g)),
            pl.BlockSpec((None, None, s, HEAD_DIM), lambda bi, g, i: (bi, g, 0, 0)),
            pl.BlockSpec((None, s, HEAD_DIM), lambda bi, g, i: (bi, 0, CB_VW + g)),
            pl.BlockSpec((None, NSA_QT, LANES), lambda bi, g, i: (bi, i, CB_NGATE)),
            pl.BlockSpec((None, r_n, NSA_QT, LANES), lambda bi, g, i: (g, 0, i, 0)),
            pl.BlockSpec((None, 5, r_n * TQ, TQ), lambda bi, g, i: (g, 0, 0, 0)),
            pl.BlockSpec((1, HEAD_DIM), lambda bi, g, i: (0, 0)),
            pl.BlockSpec((nsel, nch), lambda bi, g, i: (0, 0)),
        ],
        out_specs=pl.BlockSpec((None, NSA_QT, gw), lambda bi, g, i: (bi, i, g)),
        out_shape=jax.ShapeDtypeStruct((b, s, NSA_HEADS * HEAD_DIM), BF16),
        compiler_params=_cparams(3),
        name="nsa_attention",
    )(proj3, kcb, vcb, ksa, proj3, kwn, proj3, proj3, bias_c, bias_t, q_g.reshape(1, -1) * SCALE, _overlap_matrix(s))


def _merge_kernel(u_ref, sb_ref, ns_ref, gc_ref, gs_ref, gn_ref, wc_ref, ws_ref, wn_ref, o_ref):
    m = jax.nn.sigmoid(gc_ref[...].astype(F32)) * _dot(u_ref[...], wc_ref[...])
    m = m + jax.nn.sigmoid(gs_ref[...].astype(F32)) * _dot(sb_ref[...], ws_ref[...])
    m = m + jax.nn.sigmoid(gn_ref[...].astype(F32)) * _dot(ns_ref[...], wn_ref[...])
    o_ref[...] = m.astype(o_ref.dtype)


def _merge(u_act, sb, nsa, proj, wc, ws, wn, wl, tm=512, tn=2048):
    m = u_act.shape[0]
    d = wc.shape[2]
    kc = wc.shape[1]
    gate0 = CB_MERGE * LANES // tn
    gstep = d // tn
    act = lambda: pl.BlockSpec((tm, kc), lambda i, j: (i, 0))
    gate = lambda n: pl.BlockSpec((tm, tn), lambda i, j: (i, gate0 + n * gstep + j))
    wsp = lambda: pl.BlockSpec((None, kc, tn), lambda i, j: (wl, 0, j))
    return pl.pallas_call(
        _merge_kernel,
        grid=(m // tm, d // tn),
        in_specs=[act(), act(), act(), gate(0), gate(1), gate(2), wsp(), wsp(), wsp()],
        out_specs=pl.BlockSpec((tm, tn), lambda i, j: (i, j)),
        out_shape=jax.ShapeDtypeStruct((m, d), BF16),
        compiler_params=_cparams(2),
        name="merge",
    )(u_act, sb, nsa, proj, proj, proj, wc, ws, wn)


def _mm_res_kernel(a_ref, w_ref, x_ref, mod_ref, o_ref, acc_ref, *, row_ga, nk):
    k = pl.program_id(2)

    @pl.when(k == 0)
    def _():
        acc_ref[...] = jnp.zeros_like(acc_ref)

    acc_ref[...] += _dot(a_ref[...], w_ref[...])

    @pl.when(k == nk - 1)
    def _():
        o_ref[...] = x_ref[...] + mod_ref[row_ga:row_ga + 1, :] * acc_ref[...]


def _mm_res(a, w, wl, x, mod, row_ga, seq, tm=1024, tn=1024, tk=None):
    m, ka = a.shape
    n = x.shape[1]
    tpb = seq // tm
    tk = tk or ka
    nk = ka // tk
    w_spec = pl.BlockSpec((None, tk, tn), lambda i, j, k: (wl, k, j))
    kern = functools.partial(_mm_res_kernel, row_ga=row_ga, nk=nk)
    return pl.pallas_call(
        kern,
        grid=(m // tm, n // tn, nk),
        in_specs=[
            pl.BlockSpec((tm, tk), lambda i, j, k: (i, k)),
            w_spec,
            pl.BlockSpec((tm, tn), lambda i, j, k: (i, j)),
            pl.BlockSpec((None, 6, tn), lambda i, j, k: (i // tpb, 0, j)),
        ],
        out_specs=pl.BlockSpec((tm, tn), lambda i, j, k: (i, j)),
        out_shape=jax.ShapeDtypeStruct((m, n), F32),
        scratch_shapes=[pltpu.VMEM((tm, tn), F32)],
        compiler_params=_cparams(3),
        name="matmul_residual",
    )(a, w, x, mod)


def _ffn_up_kernel(x_ref, mod_ref, g_ref, w1_ref, w3_ref, o_ref, h_ref):
    @pl.when(pl.program_id(1) == 0)
    def _():
        h = _norm_mod(x_ref[...], g_ref[...], mod_ref[ROW_SH_FFN:ROW_SH_FFN + 1, :], mod_ref[ROW_SC_FFN:ROW_SC_FFN + 1, :])
        h_ref[...] = h.astype(BF16)

    h = h_ref[...]
    a = _dot(h, w1_ref[...].astype(BF16))
    o_ref[...] = (a * jax.nn.sigmoid(a) * _dot(h, w3_ref[...].astype(BF16))).astype(o_ref.dtype)


def _ffn_up(x, mod, g, w1, w3, wl, seq, tm=1024, tn=512):
    m, d = x.shape
    f = w1.shape[2]
    tpb = seq // tm
    return pl.pallas_call(
        _ffn_up_kernel,
        grid=(m // tm, f // tn),
        in_specs=[
            pl.BlockSpec((tm, d), lambda i, j: (i, 0)),
            pl.BlockSpec((None, 6, d), lambda i, j: (i // tpb, 0, 0)),
            pl.BlockSpec((1, d), lambda i, j: (0, 0)),
            pl.BlockSpec((None, d, tn), lambda i, j: (wl, 0, j)),
            pl.BlockSpec((None, d, tn), lambda i, j: (wl, 0, j)),
        ],
        out_specs=pl.BlockSpec((tm, tn), lambda i, j: (i, j)),
        out_shape=jax.ShapeDtypeStruct((m, f), BF16),
        scratch_shapes=[pltpu.VMEM((tm, d), BF16)],
        compiler_params=_cparams(2),
        name="ffn_up",
    )(x, mod, g.reshape(1, d), w1, w3)


def _router_kernel(x_ref, mod_ref, g_ref, wr_ref, br_ref, cmbt_ref, pos_ref, post_ref, cnt_ref, h_ref):
    h = _norm_mod(x_ref[...], g_ref[...], mod_ref[ROW_SH_FFN:ROW_SH_FFN + 1, :], mod_ref[ROW_SC_FFN:ROW_SC_FFN + 1, :])
    h_ref[...] = h.astype(BF16)
    logits = jnp.dot(h, wr_ref[...], precision=lax.Precision.HIGHEST, preferred_element_type=F32) + br_ref[...]
    lane = lax.broadcasted_iota(jnp.int32, logits.shape, 1).astype(F32)
    pad = float(LANES)
    m1 = jnp.max(logits, axis=-1, keepdims=True)
    i1 = jnp.min(jnp.where(logits == m1, lane, pad), axis=-1, keepdims=True)
    rest = jnp.where(lane == i1, NEG_INF, logits)
    m2 = jnp.max(rest, axis=-1, keepdims=True)
    i2 = jnp.min(jnp.where(rest == m2, lane, pad), axis=-1, keepdims=True)
    e2 = jnp.exp(m2 - m1)
    w1 = 1.0 / (1.0 + e2)
    w2 = e2 / (1.0 + e2)
    cmb = jnp.where(lane == i1, w1, 0.0) + jnp.where(lane == i2, w2, 0.0)
    cmbt_ref[...] = cmb.T[0:cmbt_ref.shape[0], :]
    sel = jnp.where((lane == i1) | (lane == i2), 1.0, 0.0)
    tm = sel.shape[0]
    earlier = lax.broadcasted_iota(jnp.int32, (tm, tm), 1) < lax.broadcasted_iota(jnp.int32, (tm, tm), 0)
    pos = _dot(jnp.where(earlier, 1.0, 0.0).astype(BF16), sel.astype(BF16))
    pos = jnp.where(sel > 0.0, pos, -1.0)
    pos_ref[...] = pos
    post_ref[...] = pos.T[0:post_ref.shape[0], :]
    cnt_ref[...] = jnp.broadcast_to(jnp.sum(sel, axis=0, keepdims=True), cnt_ref.shape)


def _router(x, mod, g, w_router, b_router, seq):
    m, d = x.shape
    tm = MOE_TM
    n_e = w_router.shape[1]
    tpb = seq // tm
    nt = m // tm
    wr = jnp.zeros((d, LANES), F32).at[:, :n_e].set(w_router)
    br = jnp.full((1, LANES), 2.0 * NEG_INF, F32).at[0, :n_e].set(b_router)
    return pl.pallas_call(
        _router_kernel,
        grid=(nt,),
        in_specs=[
            pl.BlockSpec((tm, d), lambda i: (i, 0)),
            pl.BlockSpec((None, 6, d), lambda i: (i // tpb, 0, 0)),
            pl.BlockSpec((1, d), lambda i: (0, 0)),
            pl.BlockSpec((d, LANES), lambda i: (0, 0)),
            pl.BlockSpec((1, LANES), lambda i: (0, 0)),
        ],
        out_specs=[
            pl.BlockSpec((n_e, tm), lambda i: (0, i)),
            pl.BlockSpec((tm, LANES), lambda i: (i, 0)),
            pl.BlockSpec((n_e, tm), lambda i: (0, i)),
            pl.BlockSpec((None, 8, LANES), lambda i: (i, 0, 0)),
            pl.BlockSpec((tm, d), lambda i: (i, 0)),
        ],
        out_shape=[
            jax.ShapeDtypeStruct((n_e, m), F32),
            jax.ShapeDtypeStruct((m, LANES), F32),
            jax.ShapeDtypeStruct((n_e, m), F32),
            jax.ShapeDtypeStruct((nt, 8, LANES), F32),
            jax.ShapeDtypeStruct((m, d), BF16),
        ],
        compiler_params=_cparams(1),
        name="router",
    )(x, mod, g.reshape(1, d), wr, br)


def _count_le(sorted_vals, queries):
    return jnp.sum(sorted_vals[..., None, :] <= queries[..., :, None], axis=-1).astype(jnp.int32)


def _moe_plan(counts, m):
    t_n, e_n = counts.shape
    per_mt = MOE_MT // MOE_BLK
    g_max = 2 * m // MOE_BLK + t_n * e_n
    pad_max = (per_mt - 1) * e_n
    nblk = (counts + MOE_BLK - 1) // MOE_BLK
    nb_e = jnp.sum(nblk, axis=0)
    cap_e = (nb_e + per_mt - 1) // per_mt * per_mt
    start_e = jnp.cumsum(cap_e) - cap_e
    dst0 = (start_e[None, :] + jnp.cumsum(nblk, axis=0) - nblk).reshape(-1)
    flat = nblk.reshape(-1)
    cum = jnp.cumsum(flat)
    g = jnp.minimum(jnp.arange(g_max, dtype=jnp.int32), cum[-1] - 1)
    p = jnp.minimum(_count_le(cum, g), t_n * e_n - 1)
    rb = g - (cum[p] - flat[p])
    n_blocks = g_max + pad_max
    z_max = n_blocks - 2 * m // MOE_BLK
    used = jnp.zeros((n_blocks,), jnp.int32).at[dst0[p] + rb].set(1)
    free = jnp.argsort(used, stable=True).astype(jnp.int32)
    z = jnp.minimum(jnp.arange(z_max, dtype=jnp.int32), n_blocks - cum[-1] - 1)
    never = jnp.full((z_max,), MOE_NEVER, jnp.int32)
    g_tile = jnp.concatenate([p // e_n, jnp.full((z_max,), t_n - 1, jnp.int32)])
    g_exp = jnp.concatenate([p % e_n, jnp.zeros((z_max,), jnp.int32)])
    g_rb = jnp.concatenate([rb, never])
    g_dst = jnp.concatenate([dst0[p] + rb, free[z]])
    mt_max = (g_max + pad_max) // per_mt
    n_mt = (jnp.sum(cap_e) // per_mt).astype(jnp.int32)
    mt = jnp.minimum(jnp.arange(mt_max, dtype=jnp.int32), n_mt - 1)
    mt_exp = jnp.minimum(_count_le(jnp.cumsum(cap_e) // per_mt, mt), e_n - 1)
    s_max = 2 * MOE_TM // MOE_BLK + e_n
    cum_t = jnp.cumsum(nblk, axis=1)
    n_t = cum_t[:, -1:]
    s_all = jnp.arange(s_max, dtype=jnp.int32)[None, :]
    s = jnp.minimum(s_all, n_t - 1)
    s_exp = jnp.minimum(_count_le(cum_t, s), e_n - 1)
    s_rb = s - (jnp.take_along_axis(cum_t, s_exp, axis=1) - jnp.take_along_axis(nblk, s_exp, axis=1))
    s_src = jnp.take_along_axis(dst0.reshape(t_n, e_n), s_exp, axis=1) + s_rb
    s_rb = jnp.where(s_all < n_t, s_rb, MOE_NEVER)
    i32 = lambda a: a.astype(jnp.int32)
    return dict(g_tile=i32(g_tile), g_exp=i32(g_exp), g_rb=i32(g_rb), g_dst=i32(g_dst), n_mt=i32(n_mt.reshape(1)),
                mt_exp=i32(mt_exp), s_exp=i32(s_exp.reshape(-1)), s_rb=i32(s_rb.reshape(-1)),
                s_src=i32(s_src.reshape(-1)), n_blocks=n_blocks, s_max=s_max)


def _moe_gather_kernel(tile_ref, exp_ref, rb_ref, dst_ref, h_ref, post_ref, cmbt_ref, o_ref, w_ref):
    g = pl.program_id(0)
    fill = rb_ref[g] >= MOE_NEVER

    @pl.when(fill)
    def _():
        o_ref[...] = jnp.zeros_like(o_ref)
        w_ref[...] = jnp.zeros_like(w_ref)

    @pl.when(jnp.logical_not(fill))
    def _():
        row = post_ref[pl.ds(exp_ref[g], 1), :]
        tm = row.shape[1]
        want = (lax.broadcasted_iota(jnp.int32, (MOE_BLK, tm), 0) + rb_ref[g] * MOE_BLK).astype(F32)
        match = row == want
        o_ref[...] = _dot(jnp.where(match, 1.0, 0.0).astype(BF16), h_ref[...]).astype(o_ref.dtype)
        cw = jnp.sum(jnp.where(match, cmbt_ref[pl.ds(exp_ref[g], 1), :], 0.0), axis=-1, keepdims=True)
        w_ref[...] = jnp.broadcast_to(cw, w_ref.shape)


def _moe_gather(h, post, cmbt, plan):
    m, d = h.shape
    n_e = post.shape[0]
    n = plan["g_tile"].shape[0]
    rows = plan["n_blocks"] * MOE_BLK
    return pl.pallas_call(
        _moe_gather_kernel,
        grid_spec=pltpu.PrefetchScalarGridSpec(
            num_scalar_prefetch=4,
            grid=(n,),
            in_specs=[
                pl.BlockSpec((MOE_TM, d), lambda g, t, e, r, ds: (t[g], 0)),
                pl.BlockSpec((n_e, MOE_TM), lambda g, t, e, r, ds: (0, t[g])),
                pl.BlockSpec((n_e, MOE_TM), lambda g, t, e, r, ds: (0, t[g])),
            ],
            out_specs=[
                pl.BlockSpec((MOE_BLK, d), lambda g, t, e, r, ds: (ds[g], 0)),
                pl.BlockSpec((MOE_BLK, LANES), lambda g, t, e, r, ds: (ds[g], 0)),
            ],
        ),
        out_shape=[jax.ShapeDtypeStruct((rows, d), BF16), jax.ShapeDtypeStruct((rows, LANES), F32)],
        compiler_params=_cparams(1),
        name="moe_gather",
    )(plan["g_tile"], plan["g_exp"], plan["g_rb"], plan["g_dst"], h, post, cmbt)


def _moe_ffn_kernel(exp_ref, n_ref, x_ref, rw_ref, w1_ref, w3_ref, w2_ref, o_ref, acc_ref, *, nc):
    c = pl.program_id(1)

    @pl.when(pl.program_id(0) < n_ref[0])
    def _():
        @pl.when(c == 0)
        def _():
            acc_ref[...] = jnp.zeros_like(acc_ref)

        x = x_ref[...]
        fc = w1_ref.shape[1]
        for k0 in range(0, fc, MOE_SUB):
            k1 = min(k0 + MOE_SUB, fc)
            a = _dot(x, w1_ref[:, k0:k1])
            a = (a * jax.nn.sigmoid(a) * _dot(x, w3_ref[:, k0:k1])).astype(BF16)
            acc_ref[...] += _dot(a, w2_ref[k0:k1, :])

        @pl.when(c == nc - 1)
        def _():
            o_ref[...] = (rw_ref[:, 0:1] * acc_ref[...]).astype(o_ref.dtype)

    @pl.when(pl.program_id(0) >= n_ref[0])
    def _():
        o_ref[...] = jnp.zeros_like(o_ref)


def _moe_ffn(xg, row_w, w1, w3, w2, wl, plan, fc):
    d = xg.shape[1]
    fe = w1.shape[3]
    nc = fe // fc
    n_mt = plan["mt_exp"].shape[0]

    def tile(mt, n):
        return jnp.minimum(mt, n[0] - 1)

    def chunk(mt, c, n):
        return jnp.where(mt < n[0], c, nc - 1)

    kern = functools.partial(_moe_ffn_kernel, nc=nc)
    return pl.pallas_call(
        kern,
        grid_spec=pltpu.PrefetchScalarGridSpec(
            num_scalar_prefetch=2,
            grid=(n_mt, nc),
            in_specs=[
                pl.BlockSpec((MOE_MT, d), lambda mt, c, e, n: (tile(mt, n), 0)),
                pl.BlockSpec((MOE_MT, LANES), lambda mt, c, e, n: (tile(mt, n), 0)),
                pl.BlockSpec((None, None, d, fc), lambda mt, c, e, n: (wl, e[mt], 0, chunk(mt, c, n))),
                pl.BlockSpec((None, None, d, fc), lambda mt, c, e, n: (wl, e[mt], 0, chunk(mt, c, n))),
                pl.BlockSpec((None, None, fc, d), lambda mt, c, e, n: (wl, e[mt], chunk(mt, c, n), 0)),
            ],
            out_specs=pl.BlockSpec((MOE_MT, d), lambda mt, c, e, n: (mt, 0)),
            scratch_shapes=[pltpu.VMEM((MOE_MT, d), F32)],
        ),
        out_shape=jax.ShapeDtypeStruct((n_mt * MOE_MT, d), BF16),
        compiler_params=_cparams(2),
        name="moe_ffn",
    )(plan["mt_exp"], plan["n_mt"], xg, row_w, w1, w3, w2)


def _moe_scatter_kernel(src_ref, exp_ref, rb_ref, *refs, ns):
    y_refs, (pos_ref, x_ref, mod_ref, o_ref, acc_ref) = refs[:MOE_SCATTER_SLOTS], refs[MOE_SCATTER_SLOTS:]
    i = pl.program_id(0)
    s = pl.program_id(1)

    @pl.when(s == 0)
    def _():
        acc_ref[...] = jnp.zeros_like(acc_ref)

    lane = lax.broadcasted_iota(jnp.int32, pos_ref.shape, 1)
    lane_f = lane.astype(F32)

    def onehot(slot):
        pos = jnp.sum(jnp.where(lane == exp_ref[slot], pos_ref[...], 0.0), axis=-1, keepdims=True)
        return jnp.where(pos - (rb_ref[slot] * MOE_BLK).astype(F32) == lane_f, 1.0, 0.0).astype(BF16)

    s0 = i * ns + MOE_SCATTER_SLOTS * s
    pt = jnp.concatenate([onehot(s0 + k) for k in range(MOE_SCATTER_SLOTS)], axis=1)
    acc_ref[...] += _dot(pt, jnp.concatenate([y[...] for y in y_refs], axis=0))

    @pl.when(s == ns // MOE_SCATTER_SLOTS - 1)
    def _():
        o_ref[...] = x_ref[...] + mod_ref[ROW_GA_FFN:ROW_GA_FFN + 1, :] * acc_ref[...]


def _moe_scatter(yg, pos, x, mod, plan, seq):
    m, d = x.shape
    ns = plan["s_max"]
    tpb = seq // MOE_TM
    kern = functools.partial(_moe_scatter_kernel, ns=ns)

    def y_spec(k):
        return pl.BlockSpec((MOE_BLK, d), lambda i, s, sr, ex, rb: (sr[i * ns + MOE_SCATTER_SLOTS * s + k], 0))

    return pl.pallas_call(
        kern,
        grid_spec=pltpu.PrefetchScalarGridSpec(
            num_scalar_prefetch=3,
            grid=(m // MOE_TM, ns // MOE_SCATTER_SLOTS),
            in_specs=[y_spec(k) for k in range(MOE_SCATTER_SLOTS)] + [
                pl.BlockSpec((MOE_TM, LANES), lambda i, s, sr, ex, rb: (i, 0)),
                pl.BlockSpec((MOE_TM, d), lambda i, s, sr, ex, rb: (i, 0)),
                pl.BlockSpec((None, 6, d), lambda i, s, sr, ex, rb: (i // tpb, 0, 0)),
            ],
            out_specs=pl.BlockSpec((MOE_TM, d), lambda i, s, sr, ex, rb: (i, 0)),
            scratch_shapes=[pltpu.VMEM((MOE_TM, d), F32)],
        ),
        out_shape=jax.ShapeDtypeStruct((m, d), F32),
        compiler_params=_cparams(2),
        name="moe_scatter",
    )(plan["s_src"], plan["s_exp"], plan["s_rb"], *([yg] * MOE_SCATTER_SLOTS), pos, x, mod)


def _moe_layer(x, mod, g, w_router, b_router, w1, w3, w2, wl, seq, fc):
    m = x.shape[0]
    cmbt, pos, post, counts, h = _router(x, mod, g, w_router, b_router, seq)
    plan = _moe_plan(counts[:, 0, :w_router.shape[1]].astype(jnp.int32), m)
    xg, row_w = _moe_gather(h, post, cmbt, plan)
    yg = _moe_ffn(xg, row_w, w1, w3, w2, wl, plan, fc)
    return _moe_scatter(yg, pos, x, mod, plan, seq)


def _kv_chunks(proj3):
    b, s, _ = proj3.shape
    n = 2 * NSA_KV_HEADS
    t = proj3[:, :, CB_KC * LANES:(CB_KC + n) * LANES].reshape(b, s, n, HEAD_DIM)
    return t.transpose(0, 2, 1, 3).reshape(b, n, s // CMP_STRIDE, CMP_STRIDE * HEAD_DIM)


def kernel(x, c, w_ada, b_ada, g_mix, g_ffn, w_in, conv_w, conv_b, conv_ln_g, conv_ln_b, w_conv_out, w_sb_out,
           nsa_cmp_pos_k, nsa_cmp_pos_v, nsa_cmp_wk, nsa_cmp_wv, nsa_q_g, nsa_kc_g, nsa_ks_g, nsa_kw_g, w_nsa_out, w_o,
           rel_bias, ffn_w1, ffn_w3, ffn_w2, moe_router, moe_router_b, moe_w1, moe_w3, moe_w2):
    b, s, d = x.shape
    m = b * s
    depth = w_ada.shape[0]
    mod_all = _ada_all(c, w_ada, b_ada)
    bias_c, bias_t = _bias_tables(rel_bias, s)
    w_in_t = jnp.swapaxes(w_in, 1, 2)
    w_conv_out, w_sb_out, w_nsa_out, w_o, ffn_w2, moe_w1, moe_w3, moe_w2 = (
        w.astype(BF16) for w in (w_conv_out, w_sb_out, w_nsa_out, w_o, ffn_w2, moe_w1, moe_w3, moe_w2))
    xf = x.reshape(m, d)
    for l in range(depth):
        mod = mod_all[l].reshape(b, 6, d)
        proj = _in_proj(xf, mod, g_mix[l], w_in_t, l, s)
        proj3 = proj.reshape(b, s, NP)
        u_act = _conv_module(proj, conv_w[l], conv_b[l], conv_ln_g[l], conv_ln_b[l], b, s)
        sb = _sb_attention(proj3)
        kcb, vcb, ksn, kwn = _nsa_prep(proj3, nsa_cmp_pos_k[l], nsa_cmp_pos_v[l], nsa_cmp_wk[l], nsa_cmp_wv[l],
                                       nsa_kc_g[l], nsa_ks_g[l], nsa_kw_g[l])
        nsa = _nsa_attention(proj3, kcb, vcb, ksn, kwn, bias_c, bias_t, nsa_q_g[l])
        merged = _merge(u_act, sb.reshape(m, -1), nsa.reshape(m, -1), proj, w_conv_out, w_sb_out, w_nsa_out, l)
        xf = _mm_res(merged, w_o, l, xf, mod, ROW_GA_MIX, s, tm=512, tn=2048)
        i = l // 2
        if l % 2 == 0:
            act = _ffn_up(xf, mod, g_ffn[l], ffn_w1, ffn_w3, i, s)
            xf = _mm_res(act, ffn_w2, i, xf, mod, ROW_GA_FFN, s, tm=1024, tn=512)
        else:
            xf = _moe_layer(xf, mod, g_ffn[l], moe_router[i], moe_router_b[i], moe_w1, moe_w3, moe_w2, i, s, MOE_FC)
    return xf.reshape(b, s, d)
```

```python
import functools
import math

import numpy as np
import jax
import jax.numpy as jnp
from jax import lax
from jax.experimental import pallas as pl
from jax.experimental.pallas import tpu as pltpu

F32 = jnp.float32
BF16 = jnp.bfloat16

D_MODEL = 2048
DEPTH = 4
D_CONV = 1024
CONV_WIDTH = 31
SB_HEADS = 8
HEAD_DIM = 128
NSA_HEADS = 8
NSA_KV_HEADS = 2
NSA_REP = NSA_HEADS // NSA_KV_HEADS
CMP_LEN = 32
CMP_STRIDE = 16
SLC_LEN = 64
SLC_TOP_N = 16
N_LOCAL_BLOCKS = 2
WINDOW = 512
FORCE_SCORE = 1e4
REL_BUCKETS = 32
REL_MAX_DIST = 128
D_FF = 5632
N_EXPERTS = 8
D_FF_EXPERT = 2816
EPS = 1e-6
NEG_INF = -1e30
TINY = 1e-20

LANES = 128
SUBLANES = 8
V7X_VMEM_LIMIT_BYTES = 56 * 1024 * 1024

NP_BLOCKS = 112
NP = NP_BLOCKS * LANES
CB_GLU_A, CB_GLU_G = 0, 8
CB_SB_Q, CB_SB_K, CB_SB_V = 16, 24, 32
CB_NQ = 40
CB_KC, CB_VC, CB_KS, CB_VS, CB_KW, CB_VW = 48, 50, 52, 54, 56, 58
CB_NGATE = 60
CB_MERGE = 64
RAW_GATE_COL = 7680
RAW_MERGE_COL = 7704
IN_PROJ_TN = 1024
ROW_SH_MIX, ROW_SC_MIX, ROW_GA_MIX, ROW_SH_FFN, ROW_SC_FFN, ROW_GA_FFN = range(6)

SCALE = HEAD_DIM ** -0.5
SB_Q_SCALE = SCALE * math.log2(math.e)
TQ = 128
SB_TILE = 256
SB_GROUP = 8
SEL_CHUNK = 512
NSA_QT = 256
NSA_SUB = NSA_QT // TQ
WIN_SPAN = WINDOW + NSA_QT
BT_DIAG, BT_SUB, BT_FAR, BT_FAR_UPPER, BT_MASKED = range(5)
CONV_HALO = 32
MOE_TM = 1024
MOE_BLK = 128
MOE_MT = 512
MOE_FC = 1536
MOE_SUB = 256
MOE_SCATTER_SLOTS = 4
MOE_NEVER = 1 << 20


def _cparams(n_axes):
    return pltpu.CompilerParams(
        dimension_semantics=("arbitrary",) * n_axes,
        vmem_limit_bytes=V7X_VMEM_LIMIT_BYTES,
    )


def _dot(a, b):
    return jnp.dot(a, b, preferred_element_type=F32)


def _dot_t(a, b):
    return lax.dot_general(a, b, (((1,), (1,)), ((), ())), preferred_element_type=F32)


def _split_dot(x, w01):
    hi = x.astype(BF16)
    lo = (x - hi.astype(F32)).astype(BF16)
    return _dot(hi, w01) + _dot(lo, w01)


def _norm_mod(x, g, sh, sc):
    ms = jnp.mean(x * x, axis=-1, keepdims=True)
    return (x * lax.rsqrt(ms + EPS) * g) * (1.0 + sc) + sh


def _rms(x, g):
    ms = jnp.mean(x * x, axis=-1, keepdims=True)
    return x * lax.rsqrt(ms + EPS) * g


def _ada_kernel(c_ref, w_ref, b_ref, o_ref):
    c = c_ref[...]
    ca = (c * jax.nn.sigmoid(c)).astype(BF16)
    o_ref[...] = _dot(ca, w_ref[...].astype(BF16)) + b_ref[...]


def _ada_all(c, w_ada, b_ada):
    depth, d, n = w_ada.shape
    nb = c.shape[0]
    b = 16
    c = jnp.zeros((b, d), c.dtype).at[:nb].set(c)
    tn = 1024
    out = pl.pallas_call(
        _ada_kernel,
        grid=(depth, n // tn),
        in_specs=[
            pl.BlockSpec((b, d), lambda l, j: (0, 0)),
            pl.BlockSpec((None, d, tn), lambda l, j: (l, 0, j)),
            pl.BlockSpec((None, 1, tn), lambda l, j: (l, 0, j)),
        ],
        out_specs=pl.BlockSpec((None, b, tn), lambda l, j: (l, 0, j)),
        out_shape=jax.ShapeDtypeStruct((depth, b, n), F32),
        compiler_params=_cparams(2),
        name="ada",
    )(c, w_ada, b_ada.reshape(depth, 1, n))
    return out[:, :nb]


def _bias_kernel(tab_ref, bk_ref, o_ref):
    h = pl.program_id(0)
    bk = bk_ref[...]
    acc = jnp.zeros(bk.shape, F32)
    for b in range(REL_BUCKETS):
        acc = jnp.where(bk == b, tab_ref[b, h], acc)
    o_ref[...] = acc


def _bias_expand(rel_bias, buckets):
    r = buckets.shape[0]
    tr = r
    return pl.pallas_call(
        _bias_kernel,
        grid=(NSA_HEADS, r // tr),
        in_specs=[
            pl.BlockSpec(memory_space=pltpu.SMEM),
            pl.BlockSpec((tr, LANES), lambda h, i: (i, 0)),
        ],
        out_specs=pl.BlockSpec((None, tr, LANES), lambda h, i: (h, i, 0)),
        out_shape=jax.ShapeDtypeStruct((NSA_HEADS, r, LANES), F32),
        compiler_params=_cparams(2),
        name="bias_expand",
    )(rel_bias, buckets)


def _rel_bucket(dist):
    n = jnp.maximum(dist, 0)
    max_exact = REL_BUCKETS // 2
    nf = jnp.maximum(n, 1).astype(F32)
    large = max_exact + (jnp.log(nf / max_exact) / math.log(REL_MAX_DIST / max_exact) * (REL_BUCKETS - max_exact)).astype(jnp.int32)
    large = jnp.minimum(large, REL_BUCKETS - 1)
    return jnp.where(n < max_exact, n, large)


def _bias_tables(rel_bias, s):
    t = jnp.arange(s, dtype=jnp.int32)[:, None]
    cend = jnp.arange(LANES, dtype=jnp.int32)[None, :] * CMP_STRIDE + (CMP_LEN - 1)
    q = jnp.arange(TQ, dtype=jnp.int32)[:, None]
    k = jnp.arange(TQ, dtype=jnp.int32)[None, :]
    bk_t = jnp.concatenate([_rel_bucket(q - k), _rel_bucket(TQ + q - k), _rel_bucket(2 * TQ + q - k)], axis=0)
    out = _bias_expand(rel_bias, jnp.concatenate([_rel_bucket(t - cend), bk_t], axis=0))
    bias_c = jnp.where(t >= cend, out[:, :s], NEG_INF).reshape(NSA_KV_HEADS, NSA_REP, s, LANES)
    diag, sub, far = out[:, s:s + TQ], out[:, s + TQ:s + 2 * TQ], out[:, s + 2 * TQ:]
    tiles = jnp.stack([jnp.where(q >= k, diag, NEG_INF), sub, far, jnp.where(k > q, far, NEG_INF),
                       jnp.full_like(far, NEG_INF)], axis=1)
    tiles = tiles.reshape(NSA_KV_HEADS, NSA_REP, 5, TQ, TQ).transpose(0, 2, 1, 3, 4)
    return bias_c, tiles.reshape(NSA_KV_HEADS, 5, NSA_REP * TQ, TQ)


def _in_proj_kernel(x_ref, mod_ref, g_ref, w_ref, o_ref, h_ref):
    j = pl.program_id(1)

    @pl.when(j == 0)
    def _():
        h = _norm_mod(x_ref[...], g_ref[...], mod_ref[ROW_SH_MIX:ROW_SH_MIX + 1, :], mod_ref[ROW_SC_MIX:ROW_SC_MIX + 1, :])
        h_ref[...] = h.astype(BF16)

    scale = jnp.where(j == CB_SB_Q * LANES // IN_PROJ_TN, SB_Q_SCALE, 1.0)
    o_ref[...] = _dot_t(h_ref[...], (w_ref[...] * scale).astype(BF16)).astype(o_ref.dtype)


def _in_proj(x, mod, g, wt, wl, seq, tm=1024):
    m, d = x.shape
    tn = IN_PROJ_TN
    tpb = seq // tm
    assert (CB_SB_K - CB_SB_Q) * LANES == tn and CB_SB_Q * LANES % tn == 0 and CB_MERGE * LANES % tn == 0
    shift = CB_MERGE * LANES - RAW_MERGE_COL

    assert tn % SUBLANES == 0 and shift % SUBLANES == 0

    def w_rows(i, j):
        start = jnp.where(j * tn < CB_MERGE * LANES, j * (tn // SUBLANES), j * (tn // SUBLANES) - shift // SUBLANES)
        return wl, start * SUBLANES, 0

    return pl.pallas_call(
        _in_proj_kernel,
        grid=(m // tm, NP // tn),
        in_specs=[
            pl.BlockSpec((tm, d), lambda i, j: (i, 0)),
            pl.BlockSpec((None, 6, d), lambda i, j: (i // tpb, 0, 0)),
            pl.BlockSpec((1, d), lambda i, j: (0, 0)),
            pl.BlockSpec((None, pl.Element(tn), pl.Element(d)), w_rows),
        ],
        out_specs=pl.BlockSpec((tm, tn), lambda i, j: (i, j)),
        out_shape=jax.ShapeDtypeStruct((m, NP), BF16),
        scratch_shapes=[pltpu.VMEM((tm, d), BF16)],
        compiler_params=_cparams(2),
        name="in_proj",
    )(x, mod, g.reshape(1, d), wt)


def _conv_kernel(a_ref, g_ref, ap_ref, gp_ref, cw_ref, cb_ref, lng_ref, lnb_ref, o_ref, ubuf, vbuf, shifted, *, ts):
    i = pl.program_id(1)
    ubuf[CONV_HALO:, :] = a_ref[...].astype(F32) * jax.nn.sigmoid(g_ref[...].astype(F32))
    up = ap_ref[...].astype(F32) * jax.nn.sigmoid(gp_ref[...].astype(F32))
    ubuf[:CONV_HALO, :] = jnp.where(i > 0, up, 0.0)

    first = CONV_HALO - (CONV_WIDTH - 1)
    span = ts + CONV_HALO - SUBLANES

    def chunk(c, carry):
        c0 = pl.multiple_of(c * LANES, LANES)
        for s in range(1, SUBLANES):
            shifted[s - 1] = ubuf[pl.ds(s, span), pl.ds(c0, LANES)]
        acc = jnp.zeros((ts, LANES), F32) + cb_ref[:, pl.ds(c0, LANES)]
        for k in range(CONV_WIDTH):
            a, s = divmod(first + k, SUBLANES)
            rows = ubuf[pl.ds(a * SUBLANES, ts), pl.ds(c0, LANES)] if s == 0 else shifted[s - 1, pl.ds(a * SUBLANES, ts), :]
            acc = acc + cw_ref[k:k + 1, pl.ds(c0, LANES)] * rows
        vbuf[:, pl.ds(c0, LANES)] = acc
        return carry

    lax.fori_loop(0, D_CONV // LANES, chunk, 0)
    v = vbuf[...]
    mu = jnp.mean(v, axis=-1, keepdims=True)
    vc = v - mu
    var = jnp.mean(vc * vc, axis=-1, keepdims=True)
    y = vc * lax.rsqrt(var + EPS) * lng_ref[...] + lnb_ref[...]
    o_ref[...] = (y * jax.nn.sigmoid(y)).astype(o_ref.dtype)


def _conv_module(proj, conv_w, conv_b, ln_g, ln_b, batch, seq, ts=256):
    m = proj.shape[0]
    nt = seq // ts
    hb = ts // CONV_HALO
    kern = functools.partial(_conv_kernel, ts=ts)

    def prev_idx(col):
        return lambda b, i: (jnp.maximum((b * nt + i) * hb - 1, 0), col)

    return pl.pallas_call(
        kern,
        grid=(batch, nt),
        in_specs=[
            pl.BlockSpec((ts, D_CONV), lambda b, i: (b * nt + i, 0)),
            pl.BlockSpec((ts, D_CONV), lambda b, i: (b * nt + i, 1)),
            pl.BlockSpec((CONV_HALO, D_CONV), prev_idx(0)),
            pl.BlockSpec((CONV_HALO, D_CONV), prev_idx(1)),
            pl.BlockSpec((CONV_WIDTH, D_CONV), lambda b, i: (0, 0)),
            pl.BlockSpec((1, D_CONV), lambda b, i: (0, 0)),
            pl.BlockSpec((1, D_CONV), lambda b, i: (0, 0)),
            pl.BlockSpec((1, D_CONV), lambda b, i: (0, 0)),
        ],
        out_specs=pl.BlockSpec((ts, D_CONV), lambda b, i: (b * nt + i, 0)),
        out_shape=jax.ShapeDtypeStruct((m, D_CONV), BF16),
        scratch_shapes=[pltpu.VMEM((CONV_HALO + ts, D_CONV), F32), pltpu.VMEM((ts, D_CONV), F32),
                        pltpu.VMEM((SUBLANES - 1, ts + CONV_HALO - SUBLANES, LANES), F32)],
        compiler_params=_cparams(2),
        name="conv_module",
    )(proj, proj, proj, proj, conv_w, conv_b.reshape(1, -1), ln_g.reshape(1, -1), ln_b.reshape(1, -1))


def _sb_kernel(q_ref, k_ref, v_ref, o_ref, lb_s, lk_s, sums_s, acc_s, rsum_s):
    i = pl.program_id(2)
    t = SB_TILE
    rows = SB_GROUP * t
    row = lax.broadcasted_iota(jnp.int32, (t, t), 0)
    col = lax.broadcasted_iota(jnp.int32, (t, t), 1)
    later = jnp.where(row > col, 1.0, 0.0).astype(BF16)
    later = jnp.concatenate([later, jnp.ones((t, LANES), BF16)], axis=1)
    heads = [slice(h * HEAD_DIM, (h + 1) * HEAD_DIM) for h in range(SB_GROUP)]

    before = col < row

    def tile(j, diag):
        k0 = pl.multiple_of(j * t, t)
        for h, hs in enumerate(heads):
            r = slice(h * t, (h + 1) * t)
            z = _dot_t(q_ref[:, hs], k_ref[pl.ds(k0, t), hs])
            if diag:
                z = jnp.where(before, z, NEG_INF)
            sp = jnp.log2(1.0 + jnp.exp2(-jnp.abs(z)))
            log_beta = jnp.minimum(z, 0.0) - sp
            lb_s[r, :] = log_beta
            lk_s[r, :] = (log_beta - z).astype(BF16)
        sums_s[...] = _dot(lk_s[...], later)
        for h, hs in enumerate(heads):
            r = slice(h * t, (h + 1) * t)
            survive = sums_s[r, 0:t] + jnp.concatenate([rsum_s[r, :]] * (t // LANES), axis=1)
            a = jnp.exp2(lb_s[r, :] + survive).astype(BF16)
            acc_s[r, :] += _dot(a, v_ref[pl.ds(k0, t), hs])
            rsum_s[r, :] += sums_s[r, t:]

    acc_s[...] = jnp.zeros_like(acc_s)
    rsum_s[...] = jnp.zeros_like(rsum_s)
    tile(i, True)

    def body(jj, carry):
        tile(i - 1 - jj, False)
        return carry

    lax.fori_loop(0, i, body, 0)
    for h in range(SB_GROUP):
        o_ref[:, h * HEAD_DIM:(h + 1) * HEAD_DIM] = acc_s[h * t:(h + 1) * t, :].astype(o_ref.dtype)


def _sb_attention(proj3):
    b, s, _ = proj3.shape
    gw = SB_GROUP * HEAD_DIM
    rows = SB_GROUP * SB_TILE
    return pl.pallas_call(
        _sb_kernel,
        grid=(b, SB_HEADS // SB_GROUP, s // SB_TILE),
        in_specs=[
            pl.BlockSpec((None, SB_TILE, gw), lambda bi, hg, i: (bi, i, CB_SB_Q // SB_GROUP + hg)),
            pl.BlockSpec((None, s, gw), lambda bi, hg, i: (bi, 0, CB_SB_K // SB_GROUP + hg)),
            pl.BlockSpec((None, s, gw), lambda bi, hg, i: (bi, 0, CB_SB_V // SB_GROUP + hg)),
        ],
        out_specs=pl.BlockSpec((None, SB_TILE, gw), lambda bi, hg, i: (bi, i, hg)),
        out_shape=jax.ShapeDtypeStruct((b, s, SB_HEADS * HEAD_DIM), BF16),
        scratch_shapes=[pltpu.VMEM((rows, SB_TILE), F32), pltpu.VMEM((rows, SB_TILE), BF16),
                        pltpu.VMEM((rows, SB_TILE + LANES), F32), pltpu.VMEM((rows, HEAD_DIM), F32),
                        pltpu.VMEM((rows, LANES), F32)],
        compiler_params=_cparams(3),
        name="sb_attention",
    )(proj3, proj3, proj3)


def _nsa_prep_kernel(kc_ref, vc_ref, ks_ref, kw_ref, pk_ref, pv_ref, wk_ref, wv_ref, kcg_ref, ksg_ref, kwg_ref,
                     kcb_ref, vcb_ref, ksn_ref, kwn_ref):
    half = CMP_STRIDE * HEAD_DIM

    def compress(a_ref, p_ref, w_ref):
        a = a_ref[...].astype(F32)
        top = _dot((a + p_ref[0:1, :]).astype(BF16), w_ref[0:half, :])
        bot = _dot((a + p_ref[1:2, :]).astype(BF16), w_ref[half:2 * half, :])
        return top + pltpu.roll(bot, bot.shape[0] - 1, axis=0)

    kcb_ref[...] = _rms(compress(kc_ref, pk_ref, wk_ref), kcg_ref[...]).astype(BF16)
    vcb_ref[...] = compress(vc_ref, pv_ref, wv_ref).astype(BF16)
    kwn_ref[...] = _rms(kw_ref[...].astype(F32), kwg_ref[...]).astype(BF16)
    ksn_ref[:, 0:HEAD_DIM] = _rms(ks_ref[...].astype(F32), ksg_ref[...]).astype(BF16)
    s = ks_ref.shape[0]
    blk = jnp.right_shift(lax.broadcasted_iota(jnp.int32, (s, LANES), 0), int(math.log2(SLC_LEN)))
    onehot = blk == lax.broadcasted_iota(jnp.int32, (s, LANES), 1)
    ksn_ref[:, HEAD_DIM:HEAD_DIM + LANES] = jnp.where(onehot, 1.0, 0.0).astype(BF16)


def _nsa_prep(proj3, pos_k, pos_v, wk, wv, kc_g, ks_g, kw_g):
    b, s, _ = proj3.shape
    g_n = NSA_KV_HEADS
    kv_chunks = _kv_chunks(proj3)
    nch = s // CMP_STRIDE
    half = CMP_STRIDE * HEAD_DIM
    vec = lambda: pl.BlockSpec((1, HEAD_DIM), lambda bi, g: (0, 0))
    small = jax.ShapeDtypeStruct((b, g_n, nch, HEAD_DIM), BF16)
    full = jax.ShapeDtypeStruct((b, g_n, s, HEAD_DIM), BF16)
    aug = jax.ShapeDtypeStruct((b, g_n, s, HEAD_DIM + LANES), BF16)
    return pl.pallas_call(
        _nsa_prep_kernel,
        grid=(b, g_n),
        in_specs=[
            pl.BlockSpec((None, None, nch, half), lambda bi, g: (bi, g, 0, 0)),
            pl.BlockSpec((None, None, nch, half), lambda bi, g: (bi, g_n + g, 0, 0)),
            pl.BlockSpec((None, s, HEAD_DIM), lambda bi, g: (bi, 0, CB_KS + g)),
            pl.BlockSpec((None, s, HEAD_DIM), lambda bi, g: (bi, 0, CB_KW + g)),
            pl.BlockSpec((2, half), lambda bi, g: (0, 0)),
            pl.BlockSpec((2, half), lambda bi, g: (0, 0)),
            pl.BlockSpec((2 * half, HEAD_DIM), lambda bi, g: (0, 0)),
            pl.BlockSpec((2 * half, HEAD_DIM), lambda bi, g: (0, 0)),
            vec(), vec(), vec(),
        ],
        out_specs=[
            pl.BlockSpec((None, None, nch, HEAD_DIM), lambda bi, g: (bi, g, 0, 0)),
            pl.BlockSpec((None, None, nch, HEAD_DIM), lambda bi, g: (bi, g, 0, 0)),
            pl.BlockSpec((None, None, s, HEAD_DIM + LANES), lambda bi, g: (bi, g, 0, 0)),
            pl.BlockSpec((None, None, s, HEAD_DIM), lambda bi, g: (bi, g, 0, 0)),
        ],
        out_shape=[small, small, aug, full],
        compiler_params=_cparams(2),
        name="nsa_prep",
    )(kv_chunks, kv_chunks, proj3, proj3, pos_k.reshape(2, half), pos_v.reshape(2, half),
      wk.astype(BF16), wv.astype(BF16), kc_g.reshape(1, -1), ks_g.reshape(1, -1), kw_g.reshape(1, -1))


def _nsa_kernel(q_ref, kcb_ref, vcb_ref, ksa_ref, vs_ref, kwn_ref, vw_ref, gate_ref, bc_ref, bt_ref, qg_ref, ovt_ref,
                o_ref):
    g = pl.program_id(1)
    i = pl.program_id(2)
    r_n = NSA_REP
    groups = [(a, r) for a in range(NSA_SUB) for r in range(r_n)]
    sub = lambda a: slice(a * TQ, (a + 1) * TQ)
    head = lambda r: slice(r * HEAD_DIM, (r + 1) * HEAD_DIM)

    q = jnp.concatenate([_rms(q_ref[sub(a), head(r)].astype(F32), qg_ref[...]).astype(BF16) for a, r in groups], axis=0)

    logit_c = _dot_t(q, kcb_ref[...]) + jnp.concatenate([bc_ref[r, sub(a), :] for a, r in groups], axis=0)
    m_c = jnp.max(logit_c, axis=-1, keepdims=True)
    p_c = jnp.where(logit_c > 0.5 * NEG_INF, jnp.exp(logit_c - m_c), 0.0)
    p_c = p_c / jnp.maximum(jnp.sum(p_c, axis=-1, keepdims=True), TINY)
    o_c = _dot(p_c.astype(BF16), vcb_ref[...])

    p_sum = jnp.concatenate(
        [sum(p_c[(a * r_n + r) * TQ:(a * r_n + r + 1) * TQ] for r in range(r_n)) for a in range(NSA_SUB)], axis=0)
    hi = p_sum.astype(BF16)
    lo = (p_sum - hi.astype(F32)).astype(BF16)
    n_sel = ovt_ref.shape[0]
    imp = _dot_t(ovt_ref[...], hi) + _dot_t(ovt_ref[...], lo)
    blk = lax.broadcasted_iota(jnp.int32, (n_sel, NSA_QT), 0)
    tb = jnp.right_shift(i * NSA_QT + lax.broadcasted_iota(jnp.int32, (n_sel, NSA_QT), 1), int(math.log2(SLC_LEN)))
    causal_blk = blk <= tb
    forced = (blk == 0) | (causal_blk & (blk > tb - N_LOCAL_BLOCKS))
    score = jnp.where(forced, FORCE_SCORE, jnp.where(causal_blk, imp, -FORCE_SCORE))
    rank = jnp.zeros((n_sel, NSA_QT), F32)
    for b in range(n_sel):
        sb = score[b:b + 1, :]
        beats = (sb > score) | ((sb == score) & (blk > b))
        rank = rank + jnp.where(beats, 1.0, 0.0)
    chosen = (rank < float(min(SLC_TOP_N, n_sel))) & causal_blk
    sel_neg = jnp.where(chosen, 0.0, NEG_INF)
    sel_neg = jnp.concatenate([sel_neg, jnp.zeros((LANES - n_sel, NSA_QT), F32)], axis=0).T.astype(BF16)
    q_aug = jnp.concatenate([q, jnp.concatenate([sel_neg[sub(a)] for a, _ in groups], axis=0)], axis=1)

    def biased(s, first_tile, idx_fn):
        return jnp.concatenate(
            [s[:, c * TQ:(c + 1) * TQ]
             + jnp.concatenate([bt_ref[idx_fn(i * NSA_SUB + a - (first_tile + c))] for a in range(NSA_SUB)], axis=0)
             for c in range(s.shape[1] // TQ)], axis=1)

    w_tile = jnp.maximum(i * NSA_SUB - WINDOW // TQ, 0)
    w0 = pl.multiple_of(w_tile * TQ, TQ)

    def win_idx(off):
        near = jnp.where(off == WINDOW // TQ, BT_FAR_UPPER, jnp.minimum(off, BT_FAR))
        return jnp.where((off < 0) | (off > WINDOW // TQ), BT_MASKED, near)

    s_w = biased(_dot_t(q, kwn_ref[pl.ds(w0, WIN_SPAN), :]), w_tile, win_idx)
    p_w = jnp.exp(s_w - jnp.max(s_w, axis=-1, keepdims=True))
    o_w = _dot(p_w.astype(BF16), vw_ref[pl.ds(w0, WIN_SPAN), :]) / jnp.sum(p_w, axis=-1, keepdims=True)

    def sel_chunk(kc, carry, diag):
        c0 = pl.multiple_of(kc * SEL_CHUNK, SEL_CHUNK)
        idx_fn = (lambda off: jnp.where(off < 0, BT_MASKED, jnp.minimum(off, BT_FAR))) if diag else (
            lambda off: jnp.minimum(off, BT_FAR))
        s = biased(_dot_t(q_aug, ksa_ref[pl.ds(c0, SEL_CHUNK), :]), kc * (SEL_CHUNK // TQ), idx_fn)
        m_blk = jnp.max(s, axis=-1, keepdims=True)
        if diag:
            p = jnp.exp(s - m_blk)
            return m_blk, jnp.sum(p, axis=-1, keepdims=True), _dot(p.astype(BF16), vs_ref[pl.ds(c0, SEL_CHUNK), :])
        m, l, acc = carry
        m_new = jnp.maximum(m, m_blk)
        p = jnp.exp(s - m_new)
        alpha = jnp.exp(m - m_new)
        l = alpha * l + jnp.sum(p, axis=-1, keepdims=True)
        acc = alpha * acc + _dot(p.astype(BF16), vs_ref[pl.ds(c0, SEL_CHUNK), :])
        return m_new, l, acc

    kc_diag = lax.shift_right_logical(i * NSA_SUB + (NSA_SUB - 1), jnp.int32(int(math.log2(SEL_CHUNK // TQ))))
    carry = sel_chunk(kc_diag, None, True)
    _, l_s, acc_s = lax.fori_loop(0, kc_diag, lambda jj, c: sel_chunk(kc_diag - 1 - jj, c, False), carry)
    o_s = acc_s / l_s

    gates = jax.nn.sigmoid(gate_ref[...].astype(F32))
    lane = lax.broadcasted_iota(jnp.int32, gates.shape, 1)

    def gate_col(idx):
        return jnp.sum(jnp.where(lane == idx, gates, 0.0), axis=-1, keepdims=True)

    for r in range(r_n):
        h = g * r_n + r
        g_c, g_s, g_w = gate_col(h), gate_col(NSA_HEADS + h), gate_col(2 * NSA_HEADS + h)
        for a in range(NSA_SUB):
            k = slice((a * r_n + r) * TQ, (a * r_n + r + 1) * TQ)
            o = g_c[sub(a)] * o_c[k] + g_s[sub(a)] * o_s[k] + g_w[sub(a)] * o_w[k]
            o_ref[sub(a), head(r)] = o.astype(o_ref.dtype)


def _overlap_matrix(s):
    nch = s // CMP_STRIDE
    nsel = s // SLC_LEN
    ci = np.arange(nch)[:, None]
    sj = np.arange(nsel)[None, :]
    ov = (ci * CMP_STRIDE <= sj * SLC_LEN + SLC_LEN - 1) & (ci * CMP_STRIDE + CMP_LEN - 1 >= sj * SLC_LEN)
    ov = ov & (ci < nch - 1)
    return jnp.asarray(ov.T, BF16)


def _nsa_attention(proj3, kcb, vcb, ksa, kwn, bias_c, bias_t, q_g):
    b, s, _ = proj3.shape
    g_n, r_n = NSA_KV_HEADS, NSA_REP
    nch = s // CMP_STRIDE
    nsel = s // SLC_LEN
    gw = r_n * HEAD_DIM
    return pl.pallas_call(
        _nsa_kernel,
        grid=(b, g_n, s // NSA_QT),
        in_specs=[
            pl.BlockSpec((None, NSA_QT, gw), lambda bi, g, i: (bi, i, CB_NQ // r_n + g)),
            pl.BlockSpec((None, None, nch, HEAD_DIM), lambda bi, g, i: (bi, g, 0, 0)),
            pl.BlockSpec((None, None, nch, HEAD_DIM), lambda bi, g, i: (bi, g, 0, 0)),
            pl.BlockSpec((None, None, s, HEAD_DIM + LANES), lambda bi, g, i: (bi, g, 0, 0)),
            pl.BlockSpec((None, s, HEAD_DIM), lambda bi, g, i: (bi, 0, CB_VS + g)),
            pl.BlockSpec((None, None, s, HEAD_DIM), lambda bi, g, i: (bi, g, 0, 0)),
            pl.BlockSpec((None, s, HEAD_DIM), lambda bi, g, i: (bi, 0, CB_VW + g)),
            pl.BlockSpec((None, NSA_QT, LANES), lambda bi, g, i: (bi, i, CB_NGATE)),
            pl.BlockSpec((None, r_n, NSA_QT, LANES), lambda bi, g, i: (g, 0, i, 0)),
            pl.BlockSpec((None, 5, r_n * TQ, TQ), lambda bi, g, i: (g, 0, 0, 0)),
            pl.BlockSpec((1, HEAD_DIM), lambda bi, g, i: (0, 0)),
            pl.BlockSpec((nsel, nch), lambda bi, g, i: (0, 0)),
        ],
        out_specs=pl.BlockSpec((None, NSA_QT, gw), lambda bi, g, i: (bi, i, g)),
        out_shape=jax.ShapeDtypeStruct((b, s, NSA_HEADS * HEAD_DIM), BF16),
        compiler_params=_cparams(3),
        name="nsa_attention",
    )(proj3, kcb, vcb, ksa, proj3, kwn, proj3, proj3, bias_c, bias_t, q_g.reshape(1, -1) * SCALE, _overlap_matrix(s))


def _merge_kernel(u_ref, sb_ref, ns_ref, gc_ref, gs_ref, gn_ref, wc_ref, ws_ref, wn_ref, o_ref):
    m = jax.nn.sigmoid(gc_ref[...].astype(F32)) * _dot(u_ref[...], wc_ref[...])
    m = m + jax.nn.sigmoid(gs_ref[...].astype(F32)) * _dot(sb_ref[...], ws_ref[...])
    m = m + jax.nn.sigmoid(gn_ref[...].astype(F32)) * _dot(ns_ref[...], wn_ref[...])
    o_ref[...] = m.astype(o_ref.dtype)


def _merge(u_act, sb, nsa, proj, wc, ws, wn, wl, tm=512, tn=2048):
    m = u_act.shape[0]
    d = wc.shape[2]
    kc = wc.shape[1]
    gate0 = CB_MERGE * LANES // tn
    gstep = d // tn
    act = lambda: pl.BlockSpec((tm, kc), lambda i, j: (i, 0))
    gate = lambda n: pl.BlockSpec((tm, tn), lambda i, j: (i, gate0 + n * gstep + j))
    wsp = lambda: pl.BlockSpec((None, kc, tn), lambda i, j: (wl, 0, j))
    return pl.pallas_call(
        _merge_kernel,
        grid=(m // tm, d // tn),
        in_specs=[act(), act(), act(), gate(0), gate(1), gate(2), wsp(), wsp(), wsp()],
        out_specs=pl.BlockSpec((tm, tn), lambda i, j: (i, j)),
        out_shape=jax.ShapeDtypeStruct((m, d), BF16),
        compiler_params=_cparams(2),
        name="merge",
    )(u_act, sb, nsa, proj, proj, proj, wc, ws, wn)


def _mm_res_kernel(a_ref, w_ref, x_ref, mod_ref, o_ref, acc_ref, *, row_ga, nk):
    k = pl.program_id(2)

    @pl.when(k == 0)
    def _():
        acc_ref[...] = jnp.zeros_like(acc_ref)

    acc_ref[...] += _dot(a_ref[...], w_ref[...])

    @pl.when(k == nk - 1)
    def _():
        o_ref[...] = x_ref[...] + mod_ref[row_ga:row_ga + 1, :] * acc_ref[...]


def _mm_res(a, w, wl, x, mod, row_ga, seq, tm=1024, tn=1024, tk=None):
    m, ka = a.shape
    n = x.shape[1]
    tpb = seq // tm
    tk = tk or ka
    nk = ka // tk
    w_spec = pl.BlockSpec((None, tk, tn), lambda i, j, k: (wl, k, j))
    kern = functools.partial(_mm_res_kernel, row_ga=row_ga, nk=nk)
    return pl.pallas_call(
        kern,
        grid=(m // tm, n // tn, nk),
        in_specs=[
            pl.BlockSpec((tm, tk), lambda i, j, k: (i, k)),
            w_spec,
            pl.BlockSpec((tm, tn), lambda i, j, k: (i, j)),
            pl.BlockSpec((None, 6, tn), lambda i, j, k: (i // tpb, 0, j)),
        ],
        out_specs=pl.BlockSpec((tm, tn), lambda i, j, k: (i, j)),
        out_shape=jax.ShapeDtypeStruct((m, n), F32),
        scratch_shapes=[pltpu.VMEM((tm, tn), F32)],
        compiler_params=_cparams(3),
        name="matmul_residual",
    )(a, w, x, mod)


def _ffn_up_kernel(x_ref, mod_ref, g_ref, w1_ref, w3_ref, o_ref, h_ref):
    @pl.when(pl.program_id(1) == 0)
    def _():
        h = _norm_mod(x_ref[...], g_ref[...], mod_ref[ROW_SH_FFN:ROW_SH_FFN + 1, :], mod_ref[ROW_SC_FFN:ROW_SC_FFN + 1, :])
        h_ref[...] = h.astype(BF16)

    h = h_ref[...]
    a = _dot(h, w1_ref[...].astype(BF16))
    o_ref[...] = (a * jax.nn.sigmoid(a) * _dot(h, w3_ref[...].astype(BF16))).astype(o_ref.dtype)


def _ffn_up(x, mod, g, w1, w3, wl, seq, tm=1024, tn=512):
    m, d = x.shape
    f = w1.shape[2]
    tpb = seq // tm
    return pl.pallas_call(
        _ffn_up_kernel,
        grid=(m // tm, f // tn),
        in_specs=[
            pl.BlockSpec((tm, d), lambda i, j: (i, 0)),
            pl.BlockSpec((None, 6, d), lambda i, j: (i // tpb, 0, 0)),
            pl.BlockSpec((1, d), lambda i, j: (0, 0)),
            pl.BlockSpec((None, d, tn), lambda i, j: (wl, 0, j)),
            pl.BlockSpec((None, d, tn), lambda i, j: (wl, 0, j)),
        ],
        out_specs=pl.BlockSpec((tm, tn), lambda i, j: (i, j)),
        out_shape=jax.ShapeDtypeStruct((m, f), BF16),
        scratch_shapes=[pltpu.VMEM((tm, d), BF16)],
        compiler_params=_cparams(2),
        name="ffn_up",
    )(x, mod, g.reshape(1, d), w1, w3)


def _router_kernel(x_ref, mod_ref, g_ref, wr_ref, br_ref, cmbt_ref, pos_ref, post_ref, cnt_ref, h_ref):
    h = _norm_mod(x_ref[...], g_ref[...], mod_ref[ROW_SH_FFN:ROW_SH_FFN + 1, :], mod_ref[ROW_SC_FFN:ROW_SC_FFN + 1, :])
    h_ref[...] = h.astype(BF16)
    logits = jnp.dot(h, wr_ref[...], precision=lax.Precision.HIGHEST, preferred_element_type=F32) + br_ref[...]
    lane = lax.broadcasted_iota(jnp.int32, logits.shape, 1).astype(F32)
    pad = float(LANES)
    m1 = jnp.max(logits, axis=-1, keepdims=True)
    i1 = jnp.min(jnp.where(logits == m1, lane, pad), axis=-1, keepdims=True)
    rest = jnp.where(lane == i1, NEG_INF, logits)
    m2 = jnp.max(rest, axis=-1, keepdims=True)
    i2 = jnp.min(jnp.where(rest == m2, lane, pad), axis=-1, keepdims=True)
    e2 = jnp.exp(m2 - m1)
    w1 = 1.0 / (1.0 + e2)
    w2 = e2 / (1.0 + e2)
    cmb = jnp.where(lane == i1, w1, 0.0) + jnp.where(lane == i2, w2, 0.0)
    cmbt_ref[...] = cmb.T[0:cmbt_ref.shape[0], :]
    sel = jnp.where((lane == i1) | (lane == i2), 1.0, 0.0)
    tm = sel.shape[0]
    earlier = lax.broadcasted_iota(jnp.int32, (tm, tm), 1) < lax.broadcasted_iota(jnp.int32, (tm, tm), 0)
    pos = _dot(jnp.where(earlier, 1.0, 0.0).astype(BF16), sel.astype(BF16))
    pos = jnp.where(sel > 0.0, pos, -1.0)
    pos_ref[...] = pos
    post_ref[...] = pos.T[0:post_ref.shape[0], :]
    cnt_ref[...] = jnp.broadcast_to(jnp.sum(sel, axis=0, keepdims=True), cnt_ref.shape)


def _router(x, mod, g, w_router, b_router, seq):
    m, d = x.shape
    tm = MOE_TM
    n_e = w_router.shape[1]
    tpb = seq // tm
    nt = m // tm
    wr = jnp.zeros((d, LANES), F32).at[:, :n_e].set(w_router)
    br = jnp.full((1, LANES), 2.0 * NEG_INF, F32).at[0, :n_e].set(b_router)
    return pl.pallas_call(
        _router_kernel,
        grid=(nt,),
        in_specs=[
            pl.BlockSpec((tm, d), lambda i: (i, 0)),
            pl.BlockSpec((None, 6, d), lambda i: (i // tpb, 0, 0)),
            pl.BlockSpec((1, d), lambda i: (0, 0)),
            pl.BlockSpec((d, LANES), lambda i: (0, 0)),
            pl.BlockSpec((1, LANES), lambda i: (0, 0)),
        ],
        out_specs=[
            pl.BlockSpec((n_e, tm), lambda i: (0, i)),
            pl.BlockSpec((tm, LANES), lambda i: (i, 0)),
            pl.BlockSpec((n_e, tm), lambda i: (0, i)),
            pl.BlockSpec((None, 8, LANES), lambda i: (i, 0, 0)),
            pl.BlockSpec((tm, d), lambda i: (i, 0)),
        ],
        out_shape=[
            jax.ShapeDtypeStruct((n_e, m), F32),
            jax.ShapeDtypeStruct((m, LANES), F32),
            jax.ShapeDtypeStruct((n_e, m), F32),
            jax.ShapeDtypeStruct((nt, 8, LANES), F32),
            jax.ShapeDtypeStruct((m, d), BF16),
        ],
        compiler_params=_cparams(1),
        name="router",
    )(x, mod, g.reshape(1, d), wr, br)


def _count_le(sorted_vals, queries):
    return jnp.sum(sorted_vals[..., None, :] <= queries[..., :, None], axis=-1).astype(jnp.int32)


def _moe_plan(counts, m):
    t_n, e_n = counts.shape
    per_mt = MOE_MT // MOE_BLK
    g_max = 2 * m // MOE_BLK + t_n * e_n
    pad_max = (per_mt - 1) * e_n
    nblk = (counts + MOE_BLK - 1) // MOE_BLK
    nb_e = jnp.sum(nblk, axis=0)
    cap_e = (nb_e + per_mt - 1) // per_mt * per_mt
    start_e = jnp.cumsum(cap_e) - cap_e
    dst0 = (start_e[None, :] + jnp.cumsum(nblk, axis=0) - nblk).reshape(-1)
    flat = nblk.reshape(-1)
    cum = jnp.cumsum(flat)
    g = jnp.minimum(jnp.arange(g_max, dtype=jnp.int32), cum[-1] - 1)
    p = jnp.minimum(_count_le(cum, g), t_n * e_n - 1)
    rb = g - (cum[p] - flat[p])
    n_blocks = g_max + pad_max
    z_max = n_blocks - 2 * m // MOE_BLK
    used = jnp.zeros((n_blocks,), jnp.int32).at[dst0[p] + rb].set(1)
    free = jnp.argsort(used, stable=True).astype(jnp.int32)
    z = jnp.minimum(jnp.arange(z_max, dtype=jnp.int32), n_blocks - cum[-1] - 1)
    never = jnp.full((z_max,), MOE_NEVER, jnp.int32)
    g_tile = jnp.concatenate([p // e_n, jnp.full((z_max,), t_n - 1, jnp.int32)])
    g_exp = jnp.concatenate([p % e_n, jnp.zeros((z_max,), jnp.int32)])
    g_rb = jnp.concatenate([rb, never])
    g_dst = jnp.concatenate([dst0[p] + rb, free[z]])
    mt_max = (g_max + pad_max) // per_mt
    n_mt = (jnp.sum(cap_e) // per_mt).astype(jnp.int32)
    mt = jnp.minimum(jnp.arange(mt_max, dtype=jnp.int32), n_mt - 1)
    mt_exp = jnp.minimum(_count_le(jnp.cumsum(cap_e) // per_mt, mt), e_n - 1)
    s_max = 2 * MOE_TM // MOE_BLK + e_n
    cum_t = jnp.cumsum(nblk, axis=1)
    n_t = cum_t[:, -1:]
    s_all = jnp.arange(s_max, dtype=jnp.int32)[None, :]
    s = jnp.minimum(s_all, n_t - 1)
    s_exp = jnp.minimum(_count_le(cum_t, s), e_n - 1)
    s_rb = s - (jnp.take_along_axis(cum_t, s_exp, axis=1) - jnp.take_along_axis(nblk, s_exp, axis=1))
    s_src = jnp.take_along_axis(dst0.reshape(t_n, e_n), s_exp, axis=1) + s_rb
    s_rb = jnp.where(s_all < n_t, s_rb, MOE_NEVER)
    i32 = lambda a: a.astype(jnp.int32)
    return dict(g_tile=i32(g_tile), g_exp=i32(g_exp), g_rb=i32(g_rb), g_dst=i32(g_dst), n_mt=i32(n_mt.reshape(1)),
                mt_exp=i32(mt_exp), s_exp=i32(s_exp.reshape(-1)), s_rb=i32(s_rb.reshape(-1)),
                s_src=i32(s_src.reshape(-1)), n_blocks=n_blocks, s_max=s_max)


def _moe_gather_kernel(tile_ref, exp_ref, rb_ref, dst_ref, h_ref, post_ref, cmbt_ref, o_ref, w_ref):
    g = pl.program_id(0)
    fill = rb_ref[g] >= MOE_NEVER

    @pl.when(fill)
    def _():
        o_ref[...] = jnp.zeros_like(o_ref)
        w_ref[...] = jnp.zeros_like(w_ref)

    @pl.when(jnp.logical_not(fill))
    def _():
        row = post_ref[pl.ds(exp_ref[g], 1), :]
        tm = row.shape[1]
        want = (lax.broadcasted_iota(jnp.int32, (MOE_BLK, tm), 0) + rb_ref[g] * MOE_BLK).astype(F32)
        match = row == want
        o_ref[...] = _dot(jnp.where(match, 1.0, 0.0).astype(BF16), h_ref[...]).astype(o_ref.dtype)
        cw = jnp.sum(jnp.where(match, cmbt_ref[pl.ds(exp_ref[g], 1), :], 0.0), axis=-1, keepdims=True)
        w_ref[...] = jnp.broadcast_to(cw, w_ref.shape)


def _moe_gather(h, post, cmbt, plan):
    m, d = h.shape
    n_e = post.shape[0]
    n = plan["g_tile"].shape[0]
    rows = plan["n_blocks"] * MOE_BLK
    return pl.pallas_call(
        _moe_gather_kernel,
        grid_spec=pltpu.PrefetchScalarGridSpec(
            num_scalar_prefetch=4,
            grid=(n,),
            in_specs=[
                pl.BlockSpec((MOE_TM, d), lambda g, t, e, r, ds: (t[g], 0)),
                pl.BlockSpec((n_e, MOE_TM), lambda g, t, e, r, ds: (0, t[g])),
                pl.BlockSpec((n_e, MOE_TM), lambda g, t, e, r, ds: (0, t[g])),
            ],
            out_specs=[
                pl.BlockSpec((MOE_BLK, d), lambda g, t, e, r, ds: (ds[g], 0)),
                pl.BlockSpec((MOE_BLK, LANES), lambda g, t, e, r, ds: (ds[g], 0)),
            ],
        ),
        out_shape=[jax.ShapeDtypeStruct((rows, d), BF16), jax.ShapeDtypeStruct((rows, LANES), F32)],
        compiler_params=_cparams(1),
        name="moe_gather",
    )(plan["g_tile"], plan["g_exp"], plan["g_rb"], plan["g_dst"], h, post, cmbt)


def _moe_ffn_kernel(exp_ref, n_ref, x_ref, rw_ref, w1_ref, w3_ref, w2_ref, o_ref, acc_ref, *, overlap):
    c = pl.program_id(1)

    @pl.when(pl.program_id(0) < n_ref[0])
    def _():
        @pl.when(c == 0)
        def _():
            acc_ref[...] = jnp.zeros_like(acc_ref)

        x = x_ref[...]
        win = w1_ref.shape[1]

        def slab(k0):
            k1 = min(k0 + MOE_SUB, win)
            a = _dot(x, w1_ref[:, k0:k1])
            a = (a * jax.nn.sigmoid(a) * _dot(x, w3_ref[:, k0:k1])).astype(BF16)
            acc_ref[...] += _dot(a, w2_ref[k0:k1, :])

        @pl.when(c == 0)
        def _():
            for k0 in range(0, overlap, MOE_SUB):
                slab(k0)

        for k0 in range(overlap, win, MOE_SUB):
            slab(k0)

        @pl.when(c == 1)
        def _():
            o_ref[...] = (rw_ref[:, 0:1] * acc_ref[...]).astype(o_ref.dtype)

    @pl.when(pl.program_id(0) >= n_ref[0])
    def _():
        o_ref[...] = jnp.zeros_like(o_ref)


def _moe_ffn(xg, row_w, w1, w3, w2, wl, plan, win):
    d = xg.shape[1]
    fe = w1.shape[3]
    second = fe - win
    overlap = 2 * win - fe
    assert 0 <= overlap < win and overlap % MOE_SUB == 0 and second % LANES == 0
    n_mt = plan["mt_exp"].shape[0]

    def tile(mt, n):
        return jnp.minimum(mt, n[0] - 1)

    def start(mt, c, n):
        return jnp.where(mt < n[0], c, 1) * second

    kern = functools.partial(_moe_ffn_kernel, overlap=overlap)
    return pl.pallas_call(
        kern,
        grid_spec=pltpu.PrefetchScalarGridSpec(
            num_scalar_prefetch=2,
            grid=(n_mt, 2),
            in_specs=[
                pl.BlockSpec((MOE_MT, d), lambda mt, c, e, n: (tile(mt, n), 0)),
                pl.BlockSpec((MOE_MT, LANES), lambda mt, c, e, n: (tile(mt, n), 0)),
                pl.BlockSpec((None, None, pl.Element(d), pl.Element(win)),
                             lambda mt, c, e, n: (wl, e[mt], 0, start(mt, c, n))),
                pl.BlockSpec((None, None, pl.Element(d), pl.Element(win)),
                             lambda mt, c, e, n: (wl, e[mt], 0, start(mt, c, n))),
                pl.BlockSpec((None, None, pl.Element(win), pl.Element(d)),
                             lambda mt, c, e, n: (wl, e[mt], start(mt, c, n), 0)),
            ],
            out_specs=pl.BlockSpec((MOE_MT, d), lambda mt, c, e, n: (mt, 0)),
            scratch_shapes=[pltpu.VMEM((MOE_MT, d), F32)],
        ),
        out_shape=jax.ShapeDtypeStruct((n_mt * MOE_MT, d), BF16),
        compiler_params=_cparams(2),
        name="moe_ffn",
    )(plan["mt_exp"], plan["n_mt"], xg, row_w, w1, w3, w2)


def _moe_scatter_kernel(src_ref, exp_ref, rb_ref, *refs, ns):
    y_refs, (pos_ref, x_ref, mod_ref, o_ref, acc_ref) = refs[:MOE_SCATTER_SLOTS], refs[MOE_SCATTER_SLOTS:]
    i = pl.program_id(0)
    s = pl.program_id(1)

    @pl.when(s == 0)
    def _():
        acc_ref[...] = jnp.zeros_like(acc_ref)

    lane = lax.broadcasted_iota(jnp.int32, pos_ref.shape, 1)
    lane_f = lane.astype(F32)

    def onehot(slot):
        pos = jnp.sum(jnp.where(lane == exp_ref[slot], pos_ref[...], 0.0), axis=-1, keepdims=True)
        return jnp.where(pos - (rb_ref[slot] * MOE_BLK).astype(F32) == lane_f, 1.0, 0.0).astype(BF16)

    s0 = i * ns + MOE_SCATTER_SLOTS * s
    pt = jnp.concatenate([onehot(s0 + k) for k in range(MOE_SCATTER_SLOTS)], axis=1)
    acc_ref[...] += _dot(pt, jnp.concatenate([y[...] for y in y_refs], axis=0))

    @pl.when(s == ns // MOE_SCATTER_SLOTS - 1)
    def _():
        o_ref[...] = x_ref[...] + mod_ref[ROW_GA_FFN:ROW_GA_FFN + 1, :] * acc_ref[...]


def _moe_scatter(yg, pos, x, mod, plan, seq):
    m, d = x.shape
    ns = plan["s_max"]
    tpb = seq // MOE_TM
    kern = functools.partial(_moe_scatter_kernel, ns=ns)

    def y_spec(k):
        return pl.BlockSpec((MOE_BLK, d), lambda i, s, sr, ex, rb: (sr[i * ns + MOE_SCATTER_SLOTS * s + k], 0))

    return pl.pallas_call(
        kern,
        grid_spec=pltpu.PrefetchScalarGridSpec(
            num_scalar_prefetch=3,
            grid=(m // MOE_TM, ns // MOE_SCATTER_SLOTS),
            in_specs=[y_spec(k) for k in range(MOE_SCATTER_SLOTS)] + [
                pl.BlockSpec((MOE_TM, LANES), lambda i, s, sr, ex, rb: (i, 0)),
                pl.BlockSpec((MOE_TM, d), lambda i, s, sr, ex, rb: (i, 0)),
                pl.BlockSpec((None, 6, d), lambda i, s, sr, ex, rb: (i // tpb, 0, 0)),
            ],
            out_specs=pl.BlockSpec((MOE_TM, d), lambda i, s, sr, ex, rb: (i, 0)),
            scratch_shapes=[pltpu.VMEM((MOE_TM, d), F32)],
        ),
        out_shape=jax.ShapeDtypeStruct((m, d), F32),
        compiler_params=_cparams(2),
        name="moe_scatter",
    )(plan["s_src"], plan["s_exp"], plan["s_rb"], *([yg] * MOE_SCATTER_SLOTS), pos, x, mod)


def _moe_layer(x, mod, g, w_router, b_router, w1, w3, w2, wl, seq, fc):
    m = x.shape[0]
    cmbt, pos, post, counts, h = _router(x, mod, g, w_router, b_router, seq)
    plan = _moe_plan(counts[:, 0, :w_router.shape[1]].astype(jnp.int32), m)
    xg, row_w = _moe_gather(h, post, cmbt, plan)
    yg = _moe_ffn(xg, row_w, w1, w3, w2, wl, plan, fc)
    return _moe_scatter(yg, pos, x, mod, plan, seq)


def _kv_chunks(proj3):
    b, s, _ = proj3.shape
    n = 2 * NSA_KV_HEADS
    t = proj3[:, :, CB_KC * LANES:(CB_KC + n) * LANES].reshape(b, s, n, HEAD_DIM)
    return t.transpose(0, 2, 1, 3).reshape(b, n, s // CMP_STRIDE, CMP_STRIDE * HEAD_DIM)


def kernel(x, c, w_ada, b_ada, g_mix, g_ffn, w_in, conv_w, conv_b, conv_ln_g, conv_ln_b, w_conv_out, w_sb_out,
           nsa_cmp_pos_k, nsa_cmp_pos_v, nsa_cmp_wk, nsa_cmp_wv, nsa_q_g, nsa_kc_g, nsa_ks_g, nsa_kw_g, w_nsa_out, w_o,
           rel_bias, ffn_w1, ffn_w3, ffn_w2, moe_router, moe_router_b, moe_w1, moe_w3, moe_w2):
    b, s, d = x.shape
    m = b * s
    depth = w_ada.shape[0]
    mod_all = _ada_all(c, w_ada, b_ada)
    bias_c, bias_t = _bias_tables(rel_bias, s)
    w_in_t = jnp.swapaxes(w_in, 1, 2)
    w_conv_out, w_sb_out, w_nsa_out, w_o, ffn_w2, moe_w1, moe_w3, moe_w2 = (
        w.astype(BF16) for w in (w_conv_out, w_sb_out, w_nsa_out, w_o, ffn_w2, moe_w1, moe_w3, moe_w2))
    xf = x.reshape(m, d)
    for l in range(depth):
        mod = mod_all[l].reshape(b, 6, d)
        proj = _in_proj(xf, mod, g_mix[l], w_in_t, l, s)
        proj3 = proj.reshape(b, s, NP)
        u_act = _conv_module(proj, conv_w[l], conv_b[l], conv_ln_g[l], conv_ln_b[l], b, s)
        sb = _sb_attention(proj3)
        kcb, vcb, ksn, kwn = _nsa_prep(proj3, nsa_cmp_pos_k[l], nsa_cmp_pos_v[l], nsa_cmp_wk[l], nsa_cmp_wv[l],
                                       nsa_kc_g[l], nsa_ks_g[l], nsa_kw_g[l])
        nsa = _nsa_attention(proj3, kcb, vcb, ksn, kwn, bias_c, bias_t, nsa_q_g[l])
        merged = _merge(u_act, sb.reshape(m, -1), nsa.reshape(m, -1), proj, w_conv_out, w_sb_out, w_nsa_out, l)
        xf = _mm_res(merged, w_o, l, xf, mod, ROW_GA_MIX, s, tm=512, tn=2048)
        i = l // 2
        if l % 2 == 0:
            act = _ffn_up(xf, mod, g_ffn[l], ffn_w1, ffn_w3, i, s)
            xf = _mm_res(act, ffn_w2, i, xf, mod, ROW_GA_FFN, s, tm=1024, tn=512)
        else:
            xf = _moe_layer(xf, mod, g_ffn[l], moe_router[i], moe_router_b[i], moe_w1, moe_w3, moe_w2, i, s, MOE_FC)
    return xf.reshape(b, s, d)
```

```python
import functools
import math

import numpy as np
import jax
import jax.numpy as jnp
from jax import lax
from jax.experimental import pallas as pl
from jax.experimental.pallas import tpu as pltpu

F32 = jnp.float32
BF16 = jnp.bfloat16

D_CONV = 1024
CONV_WIDTH = 31
SB_HEADS = 8
HEAD_DIM = 128
NSA_HEADS = 8
NSA_KV_HEADS = 2
NSA_REP = NSA_HEADS // NSA_KV_HEADS
CMP_LEN = 32
CMP_STRIDE = 16
SLC_LEN = 64
SLC_TOP_N = 16
N_LOCAL_BLOCKS = 2
WINDOW = 512
FORCE_SCORE = 1e4
REL_BUCKETS = 32
REL_MAX_DIST = 128
EPS = 1e-6
NEG_INF = -1e30
TINY = 1e-20

LANES = 128
SUBLANES = 8
V7X_VMEM_LIMIT_BYTES = 56 * 1024 * 1024

NP_BLOCKS = 112
NP = NP_BLOCKS * LANES
CB_GLU_A, CB_GLU_G = 0, 8
CB_SB_Q, CB_SB_K, CB_SB_V = 16, 24, 32
CB_NQ = 40
CB_KC, CB_VC, CB_KS, CB_VS, CB_KW, CB_VW = 48, 50, 52, 54, 56, 58
CB_NGATE = 60
CB_MERGE = 64
RAW_MERGE_COL = 7704
IN_PROJ_TN = 1024
ROW_SH_MIX, ROW_SC_MIX, ROW_GA_MIX, ROW_SH_FFN, ROW_SC_FFN, ROW_GA_FFN = range(6)

SCALE = HEAD_DIM ** -0.5
SB_Q_SCALE = SCALE * math.log2(math.e)
TQ = 128
SB_TILE = 256
SB_GROUP = 8
SEL_CHUNK = 512
NSA_QT = 256
NSA_SUB = NSA_QT // TQ
WIN_SPAN = WINDOW + NSA_QT
BT_DIAG, BT_SUB, BT_FAR, BT_FAR_UPPER, BT_MASKED = range(5)
CONV_HALO = 32
MOE_TM = 1024
MOE_BLK = 128
MOE_MT = 512
MOE_FC = 1536
MOE_SUB = 256
MOE_SCATTER_SLOTS = 8
MOE_NEVER = 1 << 20


def _cparams(n_axes):
    return pltpu.CompilerParams(
        dimension_semantics=("arbitrary",) * n_axes,
        vmem_limit_bytes=V7X_VMEM_LIMIT_BYTES,
    )


def _dot(a, b):
    return jnp.dot(a, b, preferred_element_type=F32)


def _dot_t(a, b):
    return lax.dot_general(a, b, (((1,), (1,)), ((), ())), preferred_element_type=F32)


def _norm_mod(x, g, sh, sc):
    ms = jnp.mean(x * x, axis=-1, keepdims=True)
    return (x * lax.rsqrt(ms + EPS) * g) * (1.0 + sc) + sh


def _rms(x, g):
    ms = jnp.mean(x * x, axis=-1, keepdims=True)
    return x * lax.rsqrt(ms + EPS) * g


def _ada_kernel(c_ref, w_ref, b_ref, o_ref):
    c = c_ref[...]
    ca = (c * jax.nn.sigmoid(c)).astype(BF16)
    o_ref[...] = _dot(ca, w_ref[...].astype(BF16)) + b_ref[...]


def _ada_all(c, w_ada, b_ada):
    depth, d, n = w_ada.shape
    nb = c.shape[0]
    b = 16
    c = jnp.zeros((b, d), c.dtype).at[:nb].set(c)
    tn = 1024
    out = pl.pallas_call(
        _ada_kernel,
        grid=(depth, n // tn),
        in_specs=[
            pl.BlockSpec((b, d), lambda l, j: (0, 0)),
            pl.BlockSpec((None, d, tn), lambda l, j: (l, 0, j)),
            pl.BlockSpec((None, 1, tn), lambda l, j: (l, 0, j)),
        ],
        out_specs=pl.BlockSpec((None, b, tn), lambda l, j: (l, 0, j)),
        out_shape=jax.ShapeDtypeStruct((depth, b, n), F32),
        compiler_params=_cparams(2),
        name="ada",
    )(c, w_ada, b_ada.reshape(depth, 1, n))
    return out[:, :nb]


def _bias_kernel(tab_ref, bk_ref, o_ref):
    h = pl.program_id(0)
    bk = bk_ref[...]
    acc = jnp.zeros(bk.shape, F32)
    for b in range(REL_BUCKETS):
        acc = jnp.where(bk == b, tab_ref[b, h], acc)
    o_ref[...] = acc


def _bias_expand(rel_bias, buckets):
    r = buckets.shape[0]
    tr = r
    return pl.pallas_call(
        _bias_kernel,
        grid=(NSA_HEADS, r // tr),
        in_specs=[
            pl.BlockSpec(memory_space=pltpu.SMEM),
            pl.BlockSpec((tr, LANES), lambda h, i: (i, 0)),
        ],
        out_specs=pl.BlockSpec((None, tr, LANES), lambda h, i: (h, i, 0)),
        out_shape=jax.ShapeDtypeStruct((NSA_HEADS, r, LANES), F32),
        compiler_params=_cparams(2),
        name="bias_expand",
    )(rel_bias, buckets)


def _rel_bucket(dist):
    n = jnp.maximum(dist, 0)
    max_exact = REL_BUCKETS // 2
    nf = jnp.maximum(n, 1).astype(F32)
    large = max_exact + (jnp.log(nf / max_exact) / math.log(REL_MAX_DIST / max_exact) * (REL_BUCKETS - max_exact)).astype(jnp.int32)
    large = jnp.minimum(large, REL_BUCKETS - 1)
    return jnp.where(n < max_exact, n, large)


def _bias_tables(rel_bias, s):
    t = jnp.arange(s, dtype=jnp.int32)[:, None]
    cend = jnp.arange(LANES, dtype=jnp.int32)[None, :] * CMP_STRIDE + (CMP_LEN - 1)
    q = jnp.arange(TQ, dtype=jnp.int32)[:, None]
    k = jnp.arange(TQ, dtype=jnp.int32)[None, :]
    bk_t = jnp.concatenate([_rel_bucket(q - k), _rel_bucket(TQ + q - k), _rel_bucket(2 * TQ + q - k)], axis=0)
    out = _bias_expand(rel_bias, jnp.concatenate([_rel_bucket(t - cend), bk_t], axis=0))
    bias_c = jnp.where(t >= cend, out[:, :s], NEG_INF).reshape(NSA_KV_HEADS, NSA_REP, s, LANES)
    diag, sub, far = out[:, s:s + TQ], out[:, s + TQ:s + 2 * TQ], out[:, s + 2 * TQ:]
    tiles = jnp.stack([jnp.where(q >= k, diag, NEG_INF), sub, far, jnp.where(k > q, far, NEG_INF),
                       jnp.full_like(far, NEG_INF)], axis=1)
    tiles = tiles.reshape(NSA_KV_HEADS, NSA_REP, 5, TQ, TQ).transpose(0, 2, 1, 3, 4)
    return bias_c, tiles.reshape(NSA_KV_HEADS, 5, NSA_REP * TQ, TQ)


def _in_proj_kernel(x_ref, mod_ref, g_ref, w_ref, o_ref, h_ref):
    j = pl.program_id(1)

    @pl.when(j == 0)
    def _():
        h = _norm_mod(x_ref[...], g_ref[...], mod_ref[ROW_SH_MIX:ROW_SH_MIX + 1, :], mod_ref[ROW_SC_MIX:ROW_SC_MIX + 1, :])
        h_ref[...] = h.astype(BF16)

    scale = jnp.where(j == CB_SB_Q * LANES // IN_PROJ_TN, SB_Q_SCALE, 1.0)
    o_ref[...] = _dot_t(h_ref[...], (w_ref[...] * scale).astype(BF16)).astype(o_ref.dtype)


def _in_proj(x, mod, g, wt, wl, seq, tm=1024):
    m, d = x.shape
    tn = IN_PROJ_TN
    tpb = seq // tm
    assert (CB_SB_K - CB_SB_Q) * LANES == tn and CB_SB_Q * LANES % tn == 0 and CB_MERGE * LANES % tn == 0
    shift = CB_MERGE * LANES - RAW_MERGE_COL

    assert tn % SUBLANES == 0 and shift % SUBLANES == 0

    def w_rows(i, j):
        start = jnp.where(j * tn < CB_MERGE * LANES, j * (tn // SUBLANES), j * (tn // SUBLANES) - shift // SUBLANES)
        return wl, start * SUBLANES, 0

    return pl.pallas_call(
        _in_proj_kernel,
        grid=(m // tm, NP // tn),
        in_specs=[
            pl.BlockSpec((tm, d), lambda i, j: (i, 0)),
            pl.BlockSpec((None, 6, d), lambda i, j: (i // tpb, 0, 0)),
            pl.BlockSpec((1, d), lambda i, j: (0, 0)),
            pl.BlockSpec((None, pl.Element(tn), pl.Element(d)), w_rows),
        ],
        out_specs=pl.BlockSpec((tm, tn), lambda i, j: (i, j)),
        out_shape=jax.ShapeDtypeStruct((m, NP), BF16),
        scratch_shapes=[pltpu.VMEM((tm, d), BF16)],
        compiler_params=_cparams(2),
        name="in_proj",
    )(x, mod, g.reshape(1, d), wt)


def _conv_kernel(a_ref, g_ref, ap_ref, gp_ref, cw_ref, cb_ref, lng_ref, lnb_ref, o_ref, ubuf, vbuf, shifted, *, ts):
    i = pl.program_id(1)
    ubuf[CONV_HALO:, :] = a_ref[...].astype(F32) * jax.nn.sigmoid(g_ref[...].astype(F32))
    up = ap_ref[...].astype(F32) * jax.nn.sigmoid(gp_ref[...].astype(F32))
    ubuf[:CONV_HALO, :] = jnp.where(i > 0, up, 0.0)

    first = CONV_HALO - (CONV_WIDTH - 1)
    span = ts + CONV_HALO - SUBLANES

    def chunk(c, carry):
        c0 = pl.multiple_of(c * LANES, LANES)
        for s in range(1, SUBLANES):
            shifted[s - 1] = ubuf[pl.ds(s, span), pl.ds(c0, LANES)]
        acc = jnp.zeros((ts, LANES), F32) + cb_ref[:, pl.ds(c0, LANES)]
        for k in range(CONV_WIDTH):
            a, s = divmod(first + k, SUBLANES)
            rows = ubuf[pl.ds(a * SUBLANES, ts), pl.ds(c0, LANES)] if s == 0 else shifted[s - 1, pl.ds(a * SUBLANES, ts), :]
            acc = acc + cw_ref[k:k + 1, pl.ds(c0, LANES)] * rows
        vbuf[:, pl.ds(c0, LANES)] = acc
        return carry

    lax.fori_loop(0, D_CONV // LANES, chunk, 0)
    v = vbuf[...]
    mu = jnp.mean(v, axis=-1, keepdims=True)
    vc = v - mu
    var = jnp.mean(vc * vc, axis=-1, keepdims=True)
    y = vc * lax.rsqrt(var + EPS) * lng_ref[...] + lnb_ref[...]
    o_ref[...] = (y * jax.nn.sigmoid(y)).astype(o_ref.dtype)


def _conv_module(proj, conv_w, conv_b, ln_g, ln_b, batch, seq, ts=256):
    m = proj.shape[0]
    nt = seq // ts
    hb = ts // CONV_HALO
    kern = functools.partial(_conv_kernel, ts=ts)

    col_a, col_g = CB_GLU_A * LANES // D_CONV, CB_GLU_G * LANES // D_CONV

    def prev_idx(col):
        return lambda b, i: (jnp.maximum((b * nt + i) * hb - 1, 0), col)

    return pl.pallas_call(
        kern,
        grid=(batch, nt),
        in_specs=[
            pl.BlockSpec((ts, D_CONV), lambda b, i: (b * nt + i, col_a)),
            pl.BlockSpec((ts, D_CONV), lambda b, i: (b * nt + i, col_g)),
            pl.BlockSpec((CONV_HALO, D_CONV), prev_idx(col_a)),
            pl.BlockSpec((CONV_HALO, D_CONV), prev_idx(col_g)),
            pl.BlockSpec((CONV_WIDTH, D_CONV), lambda b, i: (0, 0)),
            pl.BlockSpec((1, D_CONV), lambda b, i: (0, 0)),
            pl.BlockSpec((1, D_CONV), lambda b, i: (0, 0)),
            pl.BlockSpec((1, D_CONV), lambda b, i: (0, 0)),
        ],
        out_specs=pl.BlockSpec((ts, D_CONV), lambda b, i: (b * nt + i, 0)),
        out_shape=jax.ShapeDtypeStruct((m, D_CONV), BF16),
        scratch_shapes=[pltpu.VMEM((CONV_HALO + ts, D_CONV), F32), pltpu.VMEM((ts, D_CONV), F32),
                        pltpu.VMEM((SUBLANES - 1, ts + CONV_HALO - SUBLANES, LANES), F32)],
        compiler_params=_cparams(2),
        name="conv_module",
    )(proj, proj, proj, proj, conv_w, conv_b.reshape(1, -1), ln_g.reshape(1, -1), ln_b.reshape(1, -1))


def _sb_kernel(q_ref, k_ref, v_ref, o_ref, lb_s, lk_s, sums_s, acc_s, rsum_s):
    i = pl.program_id(2)
    t = SB_TILE
    rows = SB_GROUP * t
    row = lax.broadcasted_iota(jnp.int32, (t, t), 0)
    col = lax.broadcasted_iota(jnp.int32, (t, t), 1)
    later = jnp.where(row > col, 1.0, 0.0).astype(BF16)
    later = jnp.concatenate([later, jnp.ones((t, LANES), BF16)], axis=1)
    heads = [slice(h * HEAD_DIM, (h + 1) * HEAD_DIM) for h in range(SB_GROUP)]

    before = col < row

    def tile(j, diag):
        k0 = pl.multiple_of(j * t, t)
        for h, hs in enumerate(heads):
            r = slice(h * t, (h + 1) * t)
            z = _dot_t(q_ref[:, hs], k_ref[pl.ds(k0, t), hs])
            if diag:
                z = jnp.where(before, z, NEG_INF)
            sp = jnp.log2(1.0 + jnp.exp2(-jnp.abs(z)))
            log_beta = jnp.minimum(z, 0.0) - sp
            lb_s[r, :] = log_beta
            lk_s[r, :] = (log_beta - z).astype(BF16)
        sums_s[...] = _dot(lk_s[...], later)
        for h, hs in enumerate(heads):
            r = slice(h * t, (h + 1) * t)
            survive = sums_s[r, 0:t] + jnp.concatenate([rsum_s[r, :]] * (t // LANES), axis=1)
            a = jnp.exp2(lb_s[r, :] + survive).astype(BF16)
            acc_s[r, :] += _dot(a, v_ref[pl.ds(k0, t), hs])
            rsum_s[r, :] += sums_s[r, t:]

    acc_s[...] = jnp.zeros_like(acc_s)
    rsum_s[...] = jnp.zeros_like(rsum_s)
    tile(i, True)

    def body(jj, carry):
        tile(i - 1 - jj, False)
        return carry

    lax.fori_loop(0, i, body, 0)
    for h in range(SB_GROUP):
        o_ref[:, h * HEAD_DIM:(h + 1) * HEAD_DIM] = acc_s[h * t:(h + 1) * t, :].astype(o_ref.dtype)


def _sb_attention(proj3):
    b, s, _ = proj3.shape
    gw = SB_GROUP * HEAD_DIM
    rows = SB_GROUP * SB_TILE
    return pl.pallas_call(
        _sb_kernel,
        grid=(b, SB_HEADS // SB_GROUP, s // SB_TILE),
        in_specs=[
            pl.BlockSpec((None, SB_TILE, gw), lambda bi, hg, i: (bi, i, CB_SB_Q // SB_GROUP + hg)),
            pl.BlockSpec((None, s, gw), lambda bi, hg, i: (bi, 0, CB_SB_K // SB_GROUP + hg)),
            pl.BlockSpec((None, s, gw), lambda bi, hg, i: (bi, 0, CB_SB_V // SB_GROUP + hg)),
        ],
        out_specs=pl.BlockSpec((None, SB_TILE, gw), lambda bi, hg, i: (bi, i, hg)),
        out_shape=jax.ShapeDtypeStruct((b, s, SB_HEADS * HEAD_DIM), BF16),
        scratch_shapes=[pltpu.VMEM((rows, SB_TILE), F32), pltpu.VMEM((rows, SB_TILE), BF16),
                        pltpu.VMEM((rows, SB_TILE + LANES), F32), pltpu.VMEM((rows, HEAD_DIM), F32),
                        pltpu.VMEM((rows, LANES), F32)],
        compiler_params=_cparams(3),
        name="sb_attention",
    )(proj3, proj3, proj3)


def _nsa_prep_kernel(kc_ref, vc_ref, ks_ref, kw_ref, pk_ref, pv_ref, wk_ref, wv_ref, kcg_ref, ksg_ref, kwg_ref,
                     kcb_ref, vcb_ref, ksn_ref, kwn_ref):
    half = CMP_STRIDE * HEAD_DIM

    def compress(a_ref, p_ref, w_ref):
        a = a_ref[...].astype(F32)
        top = _dot((a + p_ref[0:1, :]).astype(BF16), w_ref[0:half, :])
        bot = _dot((a + p_ref[1:2, :]).astype(BF16), w_ref[half:2 * half, :])
        return top + pltpu.roll(bot, bot.shape[0] - 1, axis=0)

    kcb_ref[...] = _rms(compress(kc_ref, pk_ref, wk_ref), kcg_ref[...]).astype(BF16)
    vcb_ref[...] = compress(vc_ref, pv_ref, wv_ref).astype(BF16)
    kwn_ref[...] = _rms(kw_ref[...].astype(F32), kwg_ref[...]).astype(BF16)
    ksn_ref[:, 0:HEAD_DIM] = _rms(ks_ref[...].astype(F32), ksg_ref[...]).astype(BF16)
    s = ks_ref.shape[0]
    blk = jnp.right_shift(lax.broadcasted_iota(jnp.int32, (s, LANES), 0), int(math.log2(SLC_LEN)))
    onehot = blk == lax.broadcasted_iota(jnp.int32, (s, LANES), 1)
    ksn_ref[:, HEAD_DIM:HEAD_DIM + LANES] = jnp.where(onehot, 1.0, 0.0).astype(BF16)


def _nsa_prep(proj3, pos_k, pos_v, wk, wv, kc_g, ks_g, kw_g):
    b, s, _ = proj3.shape
    g_n = NSA_KV_HEADS
    kv_chunks = _kv_chunks(proj3)
    nch = s // CMP_STRIDE
    half = CMP_STRIDE * HEAD_DIM
    vec = lambda: pl.BlockSpec((1, HEAD_DIM), lambda bi, g: (0, 0))
    small = jax.ShapeDtypeStruct((b, g_n, nch, HEAD_DIM), BF16)
    full = jax.ShapeDtypeStruct((b, g_n, s, HEAD_DIM), BF16)
    aug = jax.ShapeDtypeStruct((b, g_n, s, HEAD_DIM + LANES), BF16)
    return pl.pallas_call(
        _nsa_prep_kernel,
        grid=(b, g_n),
        in_specs=[
            pl.BlockSpec((None, None, nch, half), lambda bi, g: (bi, g, 0, 0)),
            pl.BlockSpec((None, None, nch, half), lambda bi, g: (bi, g_n + g, 0, 0)),
            pl.BlockSpec((None, s, HEAD_DIM), lambda bi, g: (bi, 0, CB_KS + g)),
            pl.BlockSpec((None, s, HEAD_DIM), lambda bi, g: (bi, 0, CB_KW + g)),
            pl.BlockSpec((2, half), lambda bi, g: (0, 0)),
            pl.BlockSpec((2, half), lambda bi, g: (0, 0)),
            pl.BlockSpec((2 * half, HEAD_DIM), lambda bi, g: (0, 0)),
            pl.BlockSpec((2 * half, HEAD_DIM), lambda bi, g: (0, 0)),
            vec(), vec(), vec(),
        ],
        out_specs=[
            pl.BlockSpec((None, None, nch, HEAD_DIM), lambda bi, g: (bi, g, 0, 0)),
            pl.BlockSpec((None, None, nch, HEAD_DIM), lambda bi, g: (bi, g, 0, 0)),
            pl.BlockSpec((None, None, s, HEAD_DIM + LANES), lambda bi, g: (bi, g, 0, 0)),
            pl.BlockSpec((None, None, s, HEAD_DIM), lambda bi, g: (bi, g, 0, 0)),
        ],
        out_shape=[small, small, aug, full],
        compiler_params=_cparams(2),
        name="nsa_prep",
    )(kv_chunks, kv_chunks, proj3, proj3, pos_k.reshape(2, half), pos_v.reshape(2, half),
      wk.astype(BF16), wv.astype(BF16), kc_g.reshape(1, -1), ks_g.reshape(1, -1), kw_g.reshape(1, -1))


def _nsa_kernel(q_ref, kcb_ref, vcb_ref, ksa_ref, vs_ref, kwn_ref, vw_ref, gate_ref, bc_ref, bt_ref, qg_ref, ovt_ref,
                o_ref):
    g = pl.program_id(1)
    i = pl.program_id(2)
    r_n = NSA_REP
    groups = [(a, r) for a in range(NSA_SUB) for r in range(r_n)]
    sub = lambda a: slice(a * TQ, (a + 1) * TQ)
    head = lambda r: slice(r * HEAD_DIM, (r + 1) * HEAD_DIM)

    q = jnp.concatenate([_rms(q_ref[sub(a), head(r)].astype(F32), qg_ref[...]).astype(BF16) for a, r in groups], axis=0)

    logit_c = _dot_t(q, kcb_ref[...]) + jnp.concatenate([bc_ref[r, sub(a), :] for a, r in groups], axis=0)
    m_c = jnp.max(logit_c, axis=-1, keepdims=True)
    p_c = jnp.where(logit_c > 0.5 * NEG_INF, jnp.exp(logit_c - m_c), 0.0)
    p_c = p_c / jnp.maximum(jnp.sum(p_c, axis=-1, keepdims=True), TINY)
    o_c = _dot(p_c.astype(BF16), vcb_ref[...])

    p_sum = jnp.concatenate(
        [sum(p_c[(a * r_n + r) * TQ:(a * r_n + r + 1) * TQ] for r in range(r_n)) for a in range(NSA_SUB)], axis=0)
    hi = p_sum.astype(BF16)
    lo = (p_sum - hi.astype(F32)).astype(BF16)
    n_sel = ovt_ref.shape[0]
    imp = _dot_t(ovt_ref[...], hi) + _dot_t(ovt_ref[...], lo)
    blk = lax.broadcasted_iota(jnp.int32, (n_sel, NSA_QT), 0)
    tb = jnp.right_shift(i * NSA_QT + lax.broadcasted_iota(jnp.int32, (n_sel, NSA_QT), 1), int(math.log2(SLC_LEN)))
    causal_blk = blk <= tb
    forced = (blk == 0) | (causal_blk & (blk > tb - N_LOCAL_BLOCKS))
    score = jnp.where(forced, FORCE_SCORE, jnp.where(causal_blk, imp, -FORCE_SCORE))
    rank = jnp.zeros((n_sel, NSA_QT), F32)
    for b in range(n_sel):
        sb = score[b:b + 1, :]
        beats = (sb > score) | ((sb == score) & (blk > b))
        rank = rank + jnp.where(beats, 1.0, 0.0)
    chosen = (rank < float(min(SLC_TOP_N, n_sel))) & causal_blk
    sel_neg = jnp.where(chosen, 0.0, NEG_INF)
    sel_neg = jnp.concatenate([sel_neg, jnp.zeros((LANES - n_sel, NSA_QT), F32)], axis=0).T.astype(BF16)
    q_aug = jnp.concatenate([q, jnp.concatenate([sel_neg[sub(a)] for a, _ in groups], axis=0)], axis=1)

    def biased(s, first_tile, idx_fn):
        return jnp.concatenate(
            [s[:, c * TQ:(c + 1) * TQ]
             + jnp.concatenate([bt_ref[idx_fn(i * NSA_SUB + a - (first_tile + c))] for a in range(NSA_SUB)], axis=0)
             for c in range(s.shape[1] // TQ)], axis=1)

    w_tile = jnp.maximum(i * NSA_SUB - WINDOW // TQ, 0)
    w0 = pl.multiple_of(w_tile * TQ, TQ)

    def win_idx(off):
        near = jnp.where(off == WINDOW // TQ, BT_FAR_UPPER, jnp.minimum(off, BT_FAR))
        return jnp.where((off < 0) | (off > WINDOW // TQ), BT_MASKED, near)

    s_w = biased(_dot_t(q, kwn_ref[pl.ds(w0, WIN_SPAN), :]), w_tile, win_idx)
    p_w = jnp.exp(s_w - jnp.max(s_w, axis=-1, keepdims=True))
    o_w = _dot(p_w.astype(BF16), vw_ref[pl.ds(w0, WIN_SPAN), :]) / jnp.sum(p_w, axis=-1, keepdims=True)

    def sel_chunk(kc, carry, diag):
        c0 = pl.multiple_of(kc * SEL_CHUNK, SEL_CHUNK)
        idx_fn = (lambda off: jnp.where(off < 0, BT_MASKED, jnp.minimum(off, BT_FAR))) if diag else (
            lambda off: jnp.minimum(off, BT_FAR))
        s = biased(_dot_t(q_aug, ksa_ref[pl.ds(c0, SEL_CHUNK), :]), kc * (SEL_CHUNK // TQ), idx_fn)
        m_blk = jnp.max(s, axis=-1, keepdims=True)
        if diag:
            p = jnp.exp(s - m_blk)
            return m_blk, jnp.sum(p, axis=-1, keepdims=True), _dot(p.astype(BF16), vs_ref[pl.ds(c0, SEL_CHUNK), :])
        m, l, acc = carry
        m_new = jnp.maximum(m, m_blk)
        p = jnp.exp(s - m_new)
        alpha = jnp.exp(m - m_new)
        l = alpha * l + jnp.sum(p, axis=-1, keepdims=True)
        acc = alpha * acc + _dot(p.astype(BF16), vs_ref[pl.ds(c0, SEL_CHUNK), :])
        return m_new, l, acc

    kc_diag = lax.shift_right_logical(i * NSA_SUB + (NSA_SUB - 1), jnp.int32(int(math.log2(SEL_CHUNK // TQ))))
    carry = sel_chunk(kc_diag, None, True)
    _, l_s, acc_s = lax.fori_loop(0, kc_diag, lambda jj, c: sel_chunk(kc_diag - 1 - jj, c, False), carry)
    o_s = acc_s / l_s

    gates = jax.nn.sigmoid(gate_ref[...].astype(F32))
    lane = lax.broadcasted_iota(jnp.int32, gates.shape, 1)

    def gate_col(idx):
        return jnp.sum(jnp.where(lane == idx, gates, 0.0), axis=-1, keepdims=True)

    for r in range(r_n):
        h = g * r_n + r
        g_c, g_s, g_w = gate_col(h), gate_col(NSA_HEADS + h), gate_col(2 * NSA_HEADS + h)
        for a in range(NSA_SUB):
            k = slice((a * r_n + r) * TQ, (a * r_n + r + 1) * TQ)
            o = g_c[sub(a)] * o_c[k] + g_s[sub(a)] * o_s[k] + g_w[sub(a)] * o_w[k]
            o_ref[sub(a), head(r)] = o.astype(o_ref.dtype)


def _overlap_matrix(s):
    nch = s // CMP_STRIDE
    nsel = s // SLC_LEN
    ci = np.arange(nch)[:, None]
    sj = np.arange(nsel)[None, :]
    ov = (ci * CMP_STRIDE <= sj * SLC_LEN + SLC_LEN - 1) & (ci * CMP_STRIDE + CMP_LEN - 1 >= sj * SLC_LEN)
    ov = ov & (ci < nch - 1)
    return jnp.asarray(ov.T, BF16)


def _nsa_attention(proj3, kcb, vcb, ksa, kwn, bias_c, bias_t, q_g):
    b, s, _ = proj3.shape
    g_n, r_n = NSA_KV_HEADS, NSA_REP
    nch = s // CMP_STRIDE
    nsel = s // SLC_LEN
    gw = r_n * HEAD_DIM
    return pl.pallas_call(
        _nsa_kernel,
        grid=(b, g_n, s // NSA_QT),
        in_specs=[
            pl.BlockSpec((None, NSA_QT, gw), lambda bi, g, i: (bi, i, CB_NQ // r_n + g)),
            pl.BlockSpec((None, None, nch, HEAD_DIM), lambda bi, g, i: (bi, g, 0, 0)),
            pl.BlockSpec((None, None, nch, HEAD_DIM), lambda bi, g, i: (bi, g, 0, 0)),
            pl.BlockSpec((None, None, s, HEAD_DIM + LANES), lambda bi, g, i: (bi, g, 0, 0)),
            pl.BlockSpec((None, s, HEAD_DIM), lambda bi, g, i: (bi, 0, CB_VS + g)),
            pl.BlockSpec((None, None, s, HEAD_DIM), lambda bi, g, i: (bi, g, 0, 0)),
            pl.BlockSpec((None, s, HEAD_DIM), lambda bi, g, i: (bi, 0, CB_VW + g)),
            pl.BlockSpec((None, NSA_QT, LANES), lambda bi, g, i: (bi, i, CB_NGATE)),
            pl.BlockSpec((None, r_n, NSA_QT, LANES), lambda bi, g, i: (g, 0, i, 0)),
            pl.BlockSpec((None, 5, r_n * TQ, TQ), lambda bi, g, i: (g, 0, 0, 0)),
            pl.BlockSpec((1, HEAD_DIM), lambda bi, g, i: (0, 0)),
            pl.BlockSpec((nsel, nch), lambda bi, g, i: (0, 0)),
        ],
        out_specs=pl.BlockSpec((None, NSA_QT, gw), lambda bi, g, i: (bi, i, g)),
        out_shape=jax.ShapeDtypeStruct((b, s, NSA_HEADS * HEAD_DIM), BF16),
        compiler_params=_cparams(3),
        name="nsa_attention",
    )(proj3, kcb, vcb, ksa, proj3, kwn, proj3, proj3, bias_c, bias_t, q_g.reshape(1, -1) * SCALE, _overlap_matrix(s))


def _merge_kernel(u_ref, sb_ref, ns_ref, gc_ref, gs_ref, gn_ref, wc_ref, ws_ref, wn_ref, o_ref):
    m = jax.nn.sigmoid(gc_ref[...].astype(F32)) * _dot(u_ref[...], wc_ref[...])
    m = m + jax.nn.sigmoid(gs_ref[...].astype(F32)) * _dot(sb_ref[...], ws_ref[...])
    m = m + jax.nn.sigmoid(gn_ref[...].astype(F32)) * _dot(ns_ref[...], wn_ref[...])
    o_ref[...] = m.astype(o_ref.dtype)


def _merge(u_act, sb, nsa, proj, wc, ws, wn, wl, tm=512, tn=2048):
    m = u_act.shape[0]
    d = wc.shape[2]
    kc = wc.shape[1]
    gate0 = CB_MERGE * LANES // tn
    gstep = d // tn
    act = lambda: pl.BlockSpec((tm, kc), lambda i, j: (i, 0))
    gate = lambda n: pl.BlockSpec((tm, tn), lambda i, j: (i, gate0 + n * gstep + j))
    wsp = lambda: pl.BlockSpec((None, kc, tn), lambda i, j: (wl, 0, j))
    return pl.pallas_call(
        _merge_kernel,
        grid=(m // tm, d // tn),
        in_specs=[act(), act(), act(), gate(0), gate(1), gate(2), wsp(), wsp(), wsp()],
        out_specs=pl.BlockSpec((tm, tn), lambda i, j: (i, j)),
        out_shape=jax.ShapeDtypeStruct((m, d), BF16),
        compiler_params=_cparams(2),
        name="merge",
    )(u_act, sb, nsa, proj, proj, proj, wc, ws, wn)


def _mm_res_kernel(a_ref, w_ref, x_ref, mod_ref, o_ref, acc_ref, *, row_ga, nk):
    k = pl.program_id(2)

    @pl.when(k == 0)
    def _():
        acc_ref[...] = jnp.zeros_like(acc_ref)

    acc_ref[...] += _dot(a_ref[...], w_ref[...])

    @pl.when(k == nk - 1)
    def _():
        o_ref[...] = x_ref[...] + mod_ref[row_ga:row_ga + 1, :] * acc_ref[...]


def _mm_res(a, w, wl, x, mod, row_ga, seq, tm=1024, tn=1024, tk=None):
    m, ka = a.shape
    n = x.shape[1]
    tpb = seq // tm
    tk = tk or ka
    nk = ka // tk
    w_spec = pl.BlockSpec((None, tk, tn), lambda i, j, k: (wl, k, j))
    kern = functools.partial(_mm_res_kernel, row_ga=row_ga, nk=nk)
    return pl.pallas_call(
        kern,
        grid=(m // tm, n // tn, nk),
        in_specs=[
            pl.BlockSpec((tm, tk), lambda i, j, k: (i, k)),
            w_spec,
            pl.BlockSpec((tm, tn), lambda i, j, k: (i, j)),
            pl.BlockSpec((None, 6, tn), lambda i, j, k: (i // tpb, 0, j)),
        ],
        out_specs=pl.BlockSpec((tm, tn), lambda i, j, k: (i, j)),
        out_shape=jax.ShapeDtypeStruct((m, n), F32),
        scratch_shapes=[pltpu.VMEM((tm, tn), F32)],
        compiler_params=_cparams(3),
        name="matmul_residual",
    )(a, w, x, mod)


def _ffn_up_kernel(x_ref, mod_ref, g_ref, w1_ref, w3_ref, o_ref, h_ref):
    @pl.when(pl.program_id(1) == 0)
    def _():
        h = _norm_mod(x_ref[...], g_ref[...], mod_ref[ROW_SH_FFN:ROW_SH_FFN + 1, :], mod_ref[ROW_SC_FFN:ROW_SC_FFN + 1, :])
        h_ref[...] = h.astype(BF16)

    h = h_ref[...]
    a = _dot(h, w1_ref[...].astype(BF16))
    o_ref[...] = (a * jax.nn.sigmoid(a) * _dot(h, w3_ref[...].astype(BF16))).astype(o_ref.dtype)


def _ffn_up(x, mod, g, w1, w3, wl, seq, tm=1024, tn=512):
    m, d = x.shape
    f = w1.shape[2]
    tpb = seq // tm
    return pl.pallas_call(
        _ffn_up_kernel,
        grid=(m // tm, f // tn),
        in_specs=[
            pl.BlockSpec((tm, d), lambda i, j: (i, 0)),
            pl.BlockSpec((None, 6, d), lambda i, j: (i // tpb, 0, 0)),
            pl.BlockSpec((1, d), lambda i, j: (0, 0)),
            pl.BlockSpec((None, d, tn), lambda i, j: (wl, 0, j)),
            pl.BlockSpec((None, d, tn), lambda i, j: (wl, 0, j)),
        ],
        out_specs=pl.BlockSpec((tm, tn), lambda i, j: (i, j)),
        out_shape=jax.ShapeDtypeStruct((m, f), BF16),
        scratch_shapes=[pltpu.VMEM((tm, d), BF16)],
        compiler_params=_cparams(2),
        name="ffn_up",
    )(x, mod, g.reshape(1, d), w1, w3)


def _router_kernel(x_ref, mod_ref, g_ref, wr_ref, br_ref, cmbt_ref, pos_ref, post_ref, cnt_ref, h_ref):
    h = _norm_mod(x_ref[...], g_ref[...], mod_ref[ROW_SH_FFN:ROW_SH_FFN + 1, :], mod_ref[ROW_SC_FFN:ROW_SC_FFN + 1, :])
    h_ref[...] = h.astype(BF16)
    logits = jnp.dot(h, wr_ref[...], precision=lax.Precision.HIGHEST, preferred_element_type=F32) + br_ref[...]
    lane = lax.broadcasted_iota(jnp.int32, logits.shape, 1).astype(F32)
    pad = float(LANES)
    m1 = jnp.max(logits, axis=-1, keepdims=True)
    i1 = jnp.min(jnp.where(logits == m1, lane, pad), axis=-1, keepdims=True)
    rest = jnp.where(lane == i1, NEG_INF, logits)
    m2 = jnp.max(rest, axis=-1, keepdims=True)
    i2 = jnp.min(jnp.where(rest == m2, lane, pad), axis=-1, keepdims=True)
    e2 = jnp.exp(m2 - m1)
    w1 = 1.0 / (1.0 + e2)
    w2 = e2 / (1.0 + e2)
    cmb = jnp.where(lane == i1, w1, 0.0) + jnp.where(lane == i2, w2, 0.0)
    cmbt_ref[...] = cmb.T[0:cmbt_ref.shape[0], :]
    sel = jnp.where((lane == i1) | (lane == i2), 1.0, 0.0)
    tm = sel.shape[0]
    earlier = lax.broadcasted_iota(jnp.int32, (tm, tm), 1) < lax.broadcasted_iota(jnp.int32, (tm, tm), 0)
    pos = _dot(jnp.where(earlier, 1.0, 0.0).astype(BF16), sel.astype(BF16))
    pos = jnp.where(sel > 0.0, pos, -1.0)
    pos_ref[...] = pos
    post_ref[...] = pos.T[0:post_ref.shape[0], :]
    cnt_ref[...] = jnp.broadcast_to(jnp.sum(sel, axis=0, keepdims=True), cnt_ref.shape)


def _router(x, mod, g, w_router, b_router, seq):
    m, d = x.shape
    tm = MOE_TM
    n_e = w_router.shape[1]
    tpb = seq // tm
    nt = m // tm
    wr = jnp.zeros((d, LANES), F32).at[:, :n_e].set(w_router)
    br = jnp.full((1, LANES), 2.0 * NEG_INF, F32).at[0, :n_e].set(b_router)
    return pl.pallas_call(
        _router_kernel,
        grid=(nt,),
        in_specs=[
            pl.BlockSpec((tm, d), lambda i: (i, 0)),
            pl.BlockSpec((None, 6, d), lambda i: (i // tpb, 0, 0)),
            pl.BlockSpec((1, d), lambda i: (0, 0)),
            pl.BlockSpec((d, LANES), lambda i: (0, 0)),
            pl.BlockSpec((1, LANES), lambda i: (0, 0)),
        ],
        out_specs=[
            pl.BlockSpec((n_e, tm), lambda i: (0, i)),
            pl.BlockSpec((tm, LANES), lambda i: (i, 0)),
            pl.BlockSpec((n_e, tm), lambda i: (0, i)),
            pl.BlockSpec((None, 8, LANES), lambda i: (i, 0, 0)),
            pl.BlockSpec((tm, d), lambda i: (i, 0)),
        ],
        out_shape=[
            jax.ShapeDtypeStruct((n_e, m), F32),
            jax.ShapeDtypeStruct((m, LANES), F32),
            jax.ShapeDtypeStruct((n_e, m), F32),
            jax.ShapeDtypeStruct((nt, 8, LANES), F32),
            jax.ShapeDtypeStruct((m, d), BF16),
        ],
        compiler_params=_cparams(1),
        name="router",
    )(x, mod, g.reshape(1, d), wr, br)


def _count_le(sorted_vals, queries):
    return jnp.sum(sorted_vals[..., None, :] <= queries[..., :, None], axis=-1).astype(jnp.int32)


def _moe_plan(counts, m):
    t_n, e_n = counts.shape
    per_mt = MOE_MT // MOE_BLK
    g_max = 2 * m // MOE_BLK + t_n * e_n
    pad_max = (per_mt - 1) * e_n
    nblk = (counts + MOE_BLK - 1) // MOE_BLK
    nb_e = jnp.sum(nblk, axis=0)
    cap_e = (nb_e + per_mt - 1) // per_mt * per_mt
    start_e = jnp.cumsum(cap_e) - cap_e
    dst0 = (start_e[None, :] + jnp.cumsum(nblk, axis=0) - nblk).reshape(-1)
    flat = nblk.reshape(-1)
    cum = jnp.cumsum(flat)
    g = jnp.minimum(jnp.arange(g_max, dtype=jnp.int32), cum[-1] - 1)
    p = jnp.minimum(_count_le(cum, g), t_n * e_n - 1)
    rb = g - (cum[p] - flat[p])
    n_blocks = g_max + pad_max
    z_max = n_blocks - 2 * m // MOE_BLK
    used = jnp.zeros((n_blocks,), jnp.int32).at[dst0[p] + rb].set(1)
    free = jnp.argsort(used, stable=True).astype(jnp.int32)
    z = jnp.minimum(jnp.arange(z_max, dtype=jnp.int32), n_blocks - cum[-1] - 1)
    never = jnp.full((z_max,), MOE_NEVER, jnp.int32)
    g_tile = jnp.concatenate([p // e_n, jnp.full((z_max,), t_n - 1, jnp.int32)])
    g_exp = jnp.concatenate([p % e_n, jnp.zeros((z_max,), jnp.int32)])
    g_rb = jnp.concatenate([rb, never])
    g_dst = jnp.concatenate([dst0[p] + rb, free[z]])
    mt_max = (g_max + pad_max) // per_mt
    n_mt = (jnp.sum(cap_e) // per_mt).astype(jnp.int32)
    mt = jnp.minimum(jnp.arange(mt_max, dtype=jnp.int32), n_mt - 1)
    mt_exp = jnp.minimum(_count_le(jnp.cumsum(cap_e) // per_mt, mt), e_n - 1)
    s_max = 2 * MOE_TM // MOE_BLK + e_n
    cum_t = jnp.cumsum(nblk, axis=1)
    n_t = cum_t[:, -1:]
    s_all = jnp.arange(s_max, dtype=jnp.int32)[None, :]
    s = jnp.minimum(s_all, n_t - 1)
    s_exp = jnp.minimum(_count_le(cum_t, s), e_n - 1)
    s_rb = s - (jnp.take_along_axis(cum_t, s_exp, axis=1) - jnp.take_along_axis(nblk, s_exp, axis=1))
    s_src = jnp.take_along_axis(dst0.reshape(t_n, e_n), s_exp, axis=1) + s_rb
    s_rb = jnp.where(s_all < n_t, s_rb, MOE_NEVER)
    i32 = lambda a: a.astype(jnp.int32)
    return dict(g_tile=i32(g_tile), g_exp=i32(g_exp), g_rb=i32(g_rb), g_dst=i32(g_dst), n_mt=i32(n_mt.reshape(1)),
                mt_exp=i32(mt_exp), s_exp=i32(s_exp.reshape(-1)), s_rb=i32(s_rb.reshape(-1)),
                s_src=i32(s_src.reshape(-1)), n_blocks=n_blocks, s_max=s_max)


def _moe_gather_kernel(tile_ref, exp_ref, rb_ref, dst_ref, h_ref, post_ref, cmbt_ref, o_ref, w_ref):
    g = pl.program_id(0)
    fill = rb_ref[g] >= MOE_NEVER

    @pl.when(fill)
    def _():
        o_ref[...] = jnp.zeros_like(o_ref)
        w_ref[...] = jnp.zeros_like(w_ref)

    @pl.when(jnp.logical_not(fill))
    def _():
        row = post_ref[pl.ds(exp_ref[g], 1), :]
        tm = row.shape[1]
        want = (lax.broadcasted_iota(jnp.int32, (MOE_BLK, tm), 0) + rb_ref[g] * MOE_BLK).astype(F32)
        match = row == want
        o_ref[...] = _dot(jnp.where(match, 1.0, 0.0).astype(BF16), h_ref[...]).astype(o_ref.dtype)
        cw = jnp.sum(jnp.where(match, cmbt_ref[pl.ds(exp_ref[g], 1), :], 0.0), axis=-1, keepdims=True)
        w_ref[...] = jnp.broadcast_to(cw, w_ref.shape)


def _moe_gather(h, post, cmbt, plan):
    m, d = h.shape
    n_e = post.shape[0]
    n = plan["g_tile"].shape[0]
    rows = plan["n_blocks"] * MOE_BLK
    return pl.pallas_call(
        _moe_gather_kernel,
        grid_spec=pltpu.PrefetchScalarGridSpec(
            num_scalar_prefetch=4,
            grid=(n,),
            in_specs=[
                pl.BlockSpec((MOE_TM, d), lambda g, t, e, r, ds: (t[g], 0)),
                pl.BlockSpec((n_e, MOE_TM), lambda g, t, e, r, ds: (0, t[g])),
                pl.BlockSpec((n_e, MOE_TM), lambda g, t, e, r, ds: (0, t[g])),
            ],
            out_specs=[
                pl.BlockSpec((MOE_BLK, d), lambda g, t, e, r, ds: (ds[g], 0)),
                pl.BlockSpec((MOE_BLK, LANES), lambda g, t, e, r, ds: (ds[g], 0)),
            ],
        ),
        out_shape=[jax.ShapeDtypeStruct((rows, d), BF16), jax.ShapeDtypeStruct((rows, LANES), F32)],
        compiler_params=_cparams(1),
        name="moe_gather",
    )(plan["g_tile"], plan["g_exp"], plan["g_rb"], plan["g_dst"], h, post, cmbt)


def _moe_ffn_kernel(exp_ref, n_ref, x_ref, rw_ref, w1_ref, w3_ref, w2_ref, o_ref, acc_ref, *, overlap):
    c = pl.program_id(1)

    @pl.when(pl.program_id(0) < n_ref[0])
    def _():
        @pl.when(c == 0)
        def _():
            acc_ref[...] = jnp.zeros_like(acc_ref)

        x = x_ref[...]
        win = w1_ref.shape[1]

        def slab(k0):
            k1 = min(k0 + MOE_SUB, win)
            a = _dot(x, w1_ref[:, k0:k1])
            a = (a * jax.nn.sigmoid(a) * _dot(x, w3_ref[:, k0:k1])).astype(BF16)
            acc_ref[...] += _dot(a, w2_ref[k0:k1, :])

        @pl.when(c == 0)
        def _():
            for k0 in range(0, overlap, MOE_SUB):
                slab(k0)

        for k0 in range(overlap, win, MOE_SUB):
            slab(k0)

        @pl.when(c == 1)
        def _():
            o_ref[...] = (rw_ref[:, 0:1] * acc_ref[...]).astype(o_ref.dtype)

    @pl.when(pl.program_id(0) >= n_ref[0])
    def _():
        o_ref[...] = jnp.zeros_like(o_ref)


def _moe_ffn(xg, row_w, w1, w3, w2, wl, plan, win):
    d = xg.shape[1]
    fe = w1.shape[3]
    second = fe - win
    overlap = 2 * win - fe
    assert 0 <= overlap < win and overlap % MOE_SUB == 0 and second % LANES == 0
    n_mt = plan["mt_exp"].shape[0]

    def tile(mt, n):
        return jnp.minimum(mt, n[0] - 1)

    def start(mt, c, n):
        return jnp.where(mt < n[0], c, 1) * second

    kern = functools.partial(_moe_ffn_kernel, overlap=overlap)
    return pl.pallas_call(
        kern,
        grid_spec=pltpu.PrefetchScalarGridSpec(
            num_scalar_prefetch=2,
            grid=(n_mt, 2),
            in_specs=[
                pl.BlockSpec((MOE_MT, d), lambda mt, c, e, n: (tile(mt, n), 0)),
                pl.BlockSpec((MOE_MT, LANES), lambda mt, c, e, n: (tile(mt, n), 0)),
                pl.BlockSpec((None, None, pl.Element(d), pl.Element(win)),
                             lambda mt, c, e, n: (wl, e[mt], 0, start(mt, c, n))),
                pl.BlockSpec((None, None, pl.Element(d), pl.Element(win)),
                             lambda mt, c, e, n: (wl, e[mt], 0, start(mt, c, n))),
                pl.BlockSpec((None, None, pl.Element(win), pl.Element(d)),
                             lambda mt, c, e, n: (wl, e[mt], start(mt, c, n), 0)),
            ],
            out_specs=pl.BlockSpec((MOE_MT, d), lambda mt, c, e, n: (mt, 0)),
            scratch_shapes=[pltpu.VMEM((MOE_MT, d), F32)],
        ),
        out_shape=jax.ShapeDtypeStruct((n_mt * MOE_MT, d), BF16),
        compiler_params=_cparams(2),
        name="moe_ffn",
    )(plan["mt_exp"], plan["n_mt"], xg, row_w, w1, w3, w2)


def _moe_scatter_kernel(src_ref, exp_ref, rb_ref, *refs, ns):
    y_refs, (pos_ref, x_ref, mod_ref, o_ref, acc_ref) = refs[:MOE_SCATTER_SLOTS], refs[MOE_SCATTER_SLOTS:]
    i = pl.program_id(0)
    s = pl.program_id(1)

    @pl.when(s == 0)
    def _():
        acc_ref[...] = jnp.zeros_like(acc_ref)

    lane = lax.broadcasted_iota(jnp.int32, pos_ref.shape, 1)
    lane_f = lane.astype(F32)

    def onehot(slot):
        pos = jnp.sum(jnp.where(lane == exp_ref[slot], pos_ref[...], 0.0), axis=-1, keepdims=True)
        return jnp.where(pos - (rb_ref[slot] * MOE_BLK).astype(F32) == lane_f, 1.0, 0.0).astype(BF16)

    s0 = i * ns + MOE_SCATTER_SLOTS * s
    pt = jnp.concatenate([onehot(s0 + k) for k in range(MOE_SCATTER_SLOTS)], axis=1)
    acc_ref[...] += _dot(pt, jnp.concatenate([y[...] for y in y_refs], axis=0))

    @pl.when(s == ns // MOE_SCATTER_SLOTS - 1)
    def _():
        o_ref[...] = x_ref[...] + mod_ref[ROW_GA_FFN:ROW_GA_FFN + 1, :] * acc_ref[...]


def _moe_scatter(yg, pos, x, mod, plan, seq):
    m, d = x.shape
    ns = plan["s_max"]
    tpb = seq // MOE_TM
    kern = functools.partial(_moe_scatter_kernel, ns=ns)

    def y_spec(k):
        return pl.BlockSpec((MOE_BLK, d), lambda i, s, sr, ex, rb: (sr[i * ns + MOE_SCATTER_SLOTS * s + k], 0))

    return pl.pallas_call(
        kern,
        grid_spec=pltpu.PrefetchScalarGridSpec(
            num_scalar_prefetch=3,
            grid=(m // MOE_TM, ns // MOE_SCATTER_SLOTS),
            in_specs=[y_spec(k) for k in range(MOE_SCATTER_SLOTS)] + [
                pl.BlockSpec((MOE_TM, LANES), lambda i, s, sr, ex, rb: (i, 0)),
                pl.BlockSpec((MOE_TM, d), lambda i, s, sr, ex, rb: (i, 0)),
                pl.BlockSpec((None, 6, d), lambda i, s, sr, ex, rb: (i // tpb, 0, 0)),
            ],
            out_specs=pl.BlockSpec((MOE_TM, d), lambda i, s, sr, ex, rb: (i, 0)),
            scratch_shapes=[pltpu.VMEM((MOE_TM, d), F32)],
        ),
        out_shape=jax.ShapeDtypeStruct((m, d), F32),
        compiler_params=_cparams(2),
        name="moe_scatter",
    )(plan["s_src"], plan["s_exp"], plan["s_rb"], *([yg] * MOE_SCATTER_SLOTS), pos, x, mod)


def _moe_layer(x, mod, g, w_router, b_router, w1, w3, w2, wl, seq, fc):
    m = x.shape[0]
    cmbt, pos, post, counts, h = _router(x, mod, g, w_router, b_router, seq)
    plan = _moe_plan(counts[:, 0, :w_router.shape[1]].astype(jnp.int32), m)
    xg, row_w = _moe_gather(h, post, cmbt, plan)
    yg = _moe_ffn(xg, row_w, w1, w3, w2, wl, plan, fc)
    return _moe_scatter(yg, pos, x, mod, plan, seq)


def _kv_chunks(proj3):
    b, s, _ = proj3.shape
    n = 2 * NSA_KV_HEADS
    t = proj3[:, :, CB_KC * LANES:(CB_KC + n) * LANES].reshape(b, s, n, HEAD_DIM)
    return t.transpose(0, 2, 1, 3).reshape(b, n, s // CMP_STRIDE, CMP_STRIDE * HEAD_DIM)


def kernel(x, c, w_ada, b_ada, g_mix, g_ffn, w_in, conv_w, conv_b, conv_ln_g, conv_ln_b, w_conv_out, w_sb_out,
           nsa_cmp_pos_k, nsa_cmp_pos_v, nsa_cmp_wk, nsa_cmp_wv, nsa_q_g, nsa_kc_g, nsa_ks_g, nsa_kw_g, w_nsa_out, w_o,
           rel_bias, ffn_w1, ffn_w3, ffn_w2, moe_router, moe_router_b, moe_w1, moe_w3, moe_w2):
    b, s, d = x.shape
    m = b * s
    depth = w_ada.shape[0]
    mod_all = _ada_all(c, w_ada, b_ada)
    bias_c, bias_t = _bias_tables(rel_bias, s)
    w_in_t = jnp.swapaxes(w_in, 1, 2)
    w_conv_out, w_sb_out, w_nsa_out, w_o, ffn_w2, moe_w1, moe_w3, moe_w2 = (
        w.astype(BF16) for w in (w_conv_out, w_sb_out, w_nsa_out, w_o, ffn_w2, moe_w1, moe_w3, moe_w2))
    xf = x.reshape(m, d)
    for l in range(depth):
        mod = mod_all[l].reshape(b, 6, d)
        proj = _in_proj(xf, mod, g_mix[l], w_in_t, l, s)
        proj3 = proj.reshape(b, s, NP)
        u_act = _conv_module(proj, conv_w[l], conv_b[l], conv_ln_g[l], conv_ln_b[l], b, s)
        sb = _sb_attention(proj3)
        kcb, vcb, ksn, kwn = _nsa_prep(proj3, nsa_cmp_pos_k[l], nsa_cmp_pos_v[l], nsa_cmp_wk[l], nsa_cmp_wv[l],
                                       nsa_kc_g[l], nsa_ks_g[l], nsa_kw_g[l])
        nsa = _nsa_attention(proj3, kcb, vcb, ksn, kwn, bias_c, bias_t, nsa_q_g[l])
        merged = _merge(u_act, sb.reshape(m, -1), nsa.reshape(m, -1), proj, w_conv_out, w_sb_out, w_nsa_out, l)
        xf = _mm_res(merged, w_o, l, xf, mod, ROW_GA_MIX, s, tm=512, tn=2048)
        i = l // 2
        if l % 2 == 0:
            act = _ffn_up(xf, mod, g_ffn[l], ffn_w1, ffn_w3, i, s)
            xf = _mm_res(act, ffn_w2, i, xf, mod, ROW_GA_FFN, s, tm=1024, tn=512)
        else:
            xf = _moe_layer(xf, mod, g_ffn[l], moe_router[i], moe_router_b[i], moe_w1, moe_w3, moe_w2, i, s, MOE_FC)
    return xf.reshape(b, s, d)
```

```python
import functools
import math

import numpy as np
import jax
import jax.numpy as jnp
from jax import lax
from jax.experimental import pallas as pl
from jax.experimental.pallas import tpu as pltpu

F32 = jnp.float32
BF16 = jnp.bfloat16

D_CONV = 1024
CONV_WIDTH = 31
SB_HEADS = 8
HEAD_DIM = 128
NSA_HEADS = 8
NSA_KV_HEADS = 2
NSA_REP = NSA_HEADS // NSA_KV_HEADS
CMP_LEN = 32
CMP_STRIDE = 16
SLC_LEN = 64
SLC_TOP_N = 16
N_LOCAL_BLOCKS = 2
WINDOW = 512
FORCE_SCORE = 1e4
REL_BUCKETS = 32
REL_MAX_DIST = 128
EPS = 1e-6
NEG_INF = -1e30
TINY = 1e-20

LANES = 128
SUBLANES = 8
V7X_VMEM_LIMIT_BYTES = 56 * 1024 * 1024

NP_BLOCKS = 112
NP = NP_BLOCKS * LANES
CB_GLU_A, CB_GLU_G = 0, 8
CB_SB_Q, CB_SB_K, CB_SB_V = 16, 24, 32
CB_NQ = 40
CB_KC, CB_VC, CB_KS, CB_VS, CB_KW, CB_VW = 48, 50, 52, 54, 56, 58
CB_NGATE = 60
CB_MERGE = 64
RAW_MERGE_COL = 7704
IN_PROJ_TN = 1024
ROW_SH_MIX, ROW_SC_MIX, ROW_GA_MIX, ROW_SH_FFN, ROW_SC_FFN, ROW_GA_FFN = range(6)

SCALE = HEAD_DIM ** -0.5
SB_Q_SCALE = SCALE * math.log2(math.e)
TQ = 128
SB_TILE = 256
SB_GROUP = 8
SEL_CHUNK = 512
NSA_QT = 256
NSA_SUB = NSA_QT // TQ
WIN_SPAN = WINDOW + NSA_QT
BT_DIAG, BT_SUB, BT_FAR, BT_FAR_UPPER, BT_MASKED = range(5)
CONV_HALO = 32
MOE_TM = 1024
MOE_BLK = 128
MOE_MT = 512
MOE_FC = 1536
MOE_SUB = 256
MOE_SCATTER_SLOTS = 8
MOE_CAST_ROWS_UP = 256
MOE_CAST_ROWS_DOWN = 256
MOE_NEVER = 1 << 20


def _cparams(n_axes):
    return pltpu.CompilerParams(
        dimension_semantics=("arbitrary",) * n_axes,
        vmem_limit_bytes=V7X_VMEM_LIMIT_BYTES,
    )


def _dot(a, b):
    return jnp.dot(a, b, preferred_element_type=F32)


def _dot_t(a, b):
    return lax.dot_general(a, b, (((1,), (1,)), ((), ())), preferred_element_type=F32)


def _norm_mod(x, g, sh, sc):
    ms = jnp.mean(x * x, axis=-1, keepdims=True)
    return (x * lax.rsqrt(ms + EPS) * g) * (1.0 + sc) + sh


def _rms(x, g):
    ms = jnp.mean(x * x, axis=-1, keepdims=True)
    return x * lax.rsqrt(ms + EPS) * g


def _ada_kernel(c_ref, w_ref, b_ref, o_ref):
    c = c_ref[...]
    ca = (c * jax.nn.sigmoid(c)).astype(BF16)
    o_ref[...] = _dot(ca, w_ref[...].astype(BF16)) + b_ref[...]


def _ada_all(c, w_ada, b_ada):
    depth, d, n = w_ada.shape
    nb = c.shape[0]
    b = 16
    c = jnp.zeros((b, d), c.dtype).at[:nb].set(c)
    tn = 1024
    out = pl.pallas_call(
        _ada_kernel,
        grid=(depth, n // tn),
        in_specs=[
            pl.BlockSpec((b, d), lambda l, j: (0, 0)),
            pl.BlockSpec((None, d, tn), lambda l, j: (l, 0, j)),
            pl.BlockSpec((None, 1, tn), lambda l, j: (l, 0, j)),
        ],
        out_specs=pl.BlockSpec((None, b, tn), lambda l, j: (l, 0, j)),
        out_shape=jax.ShapeDtypeStruct((depth, b, n), F32),
        compiler_params=_cparams(2),
        name="ada",
    )(c, w_ada, b_ada.reshape(depth, 1, n))
    return out[:, :nb]


def _bias_kernel(tab_ref, bk_ref, o_ref):
    h = pl.program_id(0)
    bk = bk_ref[...]
    acc = jnp.zeros(bk.shape, F32)
    for b in range(REL_BUCKETS):
        acc = jnp.where(bk == b, tab_ref[b, h], acc)
    o_ref[...] = acc


def _bias_expand(rel_bias, buckets):
    r = buckets.shape[0]
    tr = r
    return pl.pallas_call(
        _bias_kernel,
        grid=(NSA_HEADS, r // tr),
        in_specs=[
            pl.BlockSpec(memory_space=pltpu.SMEM),
            pl.BlockSpec((tr, LANES), lambda h, i: (i, 0)),
        ],
        out_specs=pl.BlockSpec((None, tr, LANES), lambda h, i: (h, i, 0)),
        out_shape=jax.ShapeDtypeStruct((NSA_HEADS, r, LANES), F32),
        compiler_params=_cparams(2),
        name="bias_expand",
    )(rel_bias, buckets)


def _rel_bucket(dist):
    n = jnp.maximum(dist, 0)
    max_exact = REL_BUCKETS // 2
    nf = jnp.maximum(n, 1).astype(F32)
    large = max_exact + (jnp.log(nf / max_exact) / math.log(REL_MAX_DIST / max_exact) * (REL_BUCKETS - max_exact)).astype(jnp.int32)
    large = jnp.minimum(large, REL_BUCKETS - 1)
    return jnp.where(n < max_exact, n, large)


def _bias_tables(rel_bias, s):
    t = jnp.arange(s, dtype=jnp.int32)[:, None]
    cend = jnp.arange(LANES, dtype=jnp.int32)[None, :] * CMP_STRIDE + (CMP_LEN - 1)
    q = jnp.arange(TQ, dtype=jnp.int32)[:, None]
    k = jnp.arange(TQ, dtype=jnp.int32)[None, :]
    bk_t = jnp.concatenate([_rel_bucket(q - k), _rel_bucket(TQ + q - k), _rel_bucket(2 * TQ + q - k)], axis=0)
    out = _bias_expand(rel_bias, jnp.concatenate([_rel_bucket(t - cend), bk_t], axis=0))
    bias_c = jnp.where(t >= cend, out[:, :s], NEG_INF).reshape(NSA_KV_HEADS, NSA_REP, s, LANES)
    diag, sub, far = out[:, s:s + TQ], out[:, s + TQ:s + 2 * TQ], out[:, s + 2 * TQ:]
    tiles = jnp.stack([jnp.where(q >= k, diag, NEG_INF), sub, far, jnp.where(k > q, far, NEG_INF),
                       jnp.full_like(far, NEG_INF)], axis=1)
    tiles = tiles.reshape(NSA_KV_HEADS, NSA_REP, 5, TQ, TQ).transpose(0, 2, 1, 3, 4)
    return bias_c, tiles.reshape(NSA_KV_HEADS, 5, NSA_REP * TQ, TQ)


def _in_proj_kernel(x_ref, mod_ref, g_ref, w_ref, o_ref, h_ref):
    j = pl.program_id(1)

    @pl.when(j == 0)
    def _():
        h = _norm_mod(x_ref[...], g_ref[...], mod_ref[ROW_SH_MIX:ROW_SH_MIX + 1, :], mod_ref[ROW_SC_MIX:ROW_SC_MIX + 1, :])
        h_ref[...] = h.astype(BF16)

    scale = jnp.where(j == CB_SB_Q * LANES // IN_PROJ_TN, SB_Q_SCALE, 1.0)
    o_ref[...] = _dot_t(h_ref[...], (w_ref[...] * scale).astype(BF16)).astype(o_ref.dtype)


def _in_proj(x, mod, g, wt, wl, seq, tm=1024):
    m, d = x.shape
    tn = IN_PROJ_TN
    tpb = seq // tm
    assert (CB_SB_K - CB_SB_Q) * LANES == tn and CB_SB_Q * LANES % tn == 0 and CB_MERGE * LANES % tn == 0
    shift = CB_MERGE * LANES - RAW_MERGE_COL

    assert tn % SUBLANES == 0 and shift % SUBLANES == 0

    def w_rows(i, j):
        start = jnp.where(j * tn < CB_MERGE * LANES, j * (tn // SUBLANES), j * (tn // SUBLANES) - shift // SUBLANES)
        return wl, start * SUBLANES, 0

    return pl.pallas_call(
        _in_proj_kernel,
        grid=(m // tm, NP // tn),
        in_specs=[
            pl.BlockSpec((tm, d), lambda i, j: (i, 0)),
            pl.BlockSpec((None, 6, d), lambda i, j: (i // tpb, 0, 0)),
            pl.BlockSpec((1, d), lambda i, j: (0, 0)),
            pl.BlockSpec((None, pl.Element(tn), pl.Element(d)), w_rows),
        ],
        out_specs=pl.BlockSpec((tm, tn), lambda i, j: (i, j)),
        out_shape=jax.ShapeDtypeStruct((m, NP), BF16),
        scratch_shapes=[pltpu.VMEM((tm, d), BF16)],
        compiler_params=_cparams(2),
        name="in_proj",
    )(x, mod, g.reshape(1, d), wt)


def _conv_kernel(a_ref, g_ref, ap_ref, gp_ref, cw_ref, cb_ref, lng_ref, lnb_ref, o_ref, ubuf, vbuf, shifted, *, ts):
    i = pl.program_id(1)
    ubuf[CONV_HALO:, :] = a_ref[...].astype(F32) * jax.nn.sigmoid(g_ref[...].astype(F32))
    up = ap_ref[...].astype(F32) * jax.nn.sigmoid(gp_ref[...].astype(F32))
    ubuf[:CONV_HALO, :] = jnp.where(i > 0, up, 0.0)

    first = CONV_HALO - (CONV_WIDTH - 1)
    span = ts + CONV_HALO - SUBLANES

    def chunk(c, carry):
        c0 = pl.multiple_of(c * LANES, LANES)
        for s in range(1, SUBLANES):
            shifted[s - 1] = ubuf[pl.ds(s, span), pl.ds(c0, LANES)]
        acc = jnp.zeros((ts, LANES), F32) + cb_ref[:, pl.ds(c0, LANES)]
        for k in range(CONV_WIDTH):
            a, s = divmod(first + k, SUBLANES)
            rows = ubuf[pl.ds(a * SUBLANES, ts), pl.ds(c0, LANES)] if s == 0 else shifted[s - 1, pl.ds(a * SUBLANES, ts), :]
            acc = acc + cw_ref[k:k + 1, pl.ds(c0, LANES)] * rows
        vbuf[:, pl.ds(c0, LANES)] = acc
        return carry

    lax.fori_loop(0, D_CONV // LANES, chunk, 0)
    v = vbuf[...]
    mu = jnp.mean(v, axis=-1, keepdims=True)
    vc = v - mu
    var = jnp.mean(vc * vc, axis=-1, keepdims=True)
    y = vc * lax.rsqrt(var + EPS) * lng_ref[...] + lnb_ref[...]
    o_ref[...] = (y * jax.nn.sigmoid(y)).astype(o_ref.dtype)


def _conv_module(proj, conv_w, conv_b, ln_g, ln_b, batch, seq, ts=256):
    m = proj.shape[0]
    nt = seq // ts
    hb = ts // CONV_HALO
    kern = functools.partial(_conv_kernel, ts=ts)

    col_a, col_g = CB_GLU_A * LANES // D_CONV, CB_GLU_G * LANES // D_CONV

    def prev_idx(col):
        return lambda b, i: (jnp.maximum((b * nt + i) * hb - 1, 0), col)

    return pl.pallas_call(
        kern,
        grid=(batch, nt),
        in_specs=[
            pl.BlockSpec((ts, D_CONV), lambda b, i: (b * nt + i, col_a)),
            pl.BlockSpec((ts, D_CONV), lambda b, i: (b * nt + i, col_g)),
            pl.BlockSpec((CONV_HALO, D_CONV), prev_idx(col_a)),
            pl.BlockSpec((CONV_HALO, D_CONV), prev_idx(col_g)),
            pl.BlockSpec((CONV_WIDTH, D_CONV), lambda b, i: (0, 0)),
            pl.BlockSpec((1, D_CONV), lambda b, i: (0, 0)),
            pl.BlockSpec((1, D_CONV), lambda b, i: (0, 0)),
            pl.BlockSpec((1, D_CONV), lambda b, i: (0, 0)),
        ],
        out_specs=pl.BlockSpec((ts, D_CONV), lambda b, i: (b * nt + i, 0)),
        out_shape=jax.ShapeDtypeStruct((m, D_CONV), BF16),
        scratch_shapes=[pltpu.VMEM((CONV_HALO + ts, D_CONV), F32), pltpu.VMEM((ts, D_CONV), F32),
                        pltpu.VMEM((SUBLANES - 1, ts + CONV_HALO - SUBLANES, LANES), F32)],
        compiler_params=_cparams(2),
        name="conv_module",
    )(proj, proj, proj, proj, conv_w, conv_b.reshape(1, -1), ln_g.reshape(1, -1), ln_b.reshape(1, -1))


def _sb_kernel(q_ref, k_ref, v_ref, o_ref, lb_s, lk_s, sums_s, acc_s, rsum_s):
    i = pl.program_id(2)
    t = SB_TILE
    rows = SB_GROUP * t
    row = lax.broadcasted_iota(jnp.int32, (t, t), 0)
    col = lax.broadcasted_iota(jnp.int32, (t, t), 1)
    later = jnp.where(row > col, 1.0, 0.0).astype(BF16)
    later = jnp.concatenate([later, jnp.ones((t, LANES), BF16)], axis=1)
    heads = [slice(h * HEAD_DIM, (h + 1) * HEAD_DIM) for h in range(SB_GROUP)]

    before = col < row

    def tile(j, diag):
        k0 = pl.multiple_of(j * t, t)
        for h, hs in enumerate(heads):
            r = slice(h * t, (h + 1) * t)
            z = _dot_t(q_ref[:, hs], k_ref[pl.ds(k0, t), hs])
            if diag:
                z = jnp.where(before, z, NEG_INF)
            sp = jnp.log2(1.0 + jnp.exp2(-jnp.abs(z)))
            log_beta = jnp.minimum(z, 0.0) - sp
            lb_s[r, :] = log_beta
            lk_s[r, :] = (log_beta - z).astype(BF16)
        sums_s[...] = _dot(lk_s[...], later)
        for h, hs in enumerate(heads):
            r = slice(h * t, (h + 1) * t)
            survive = sums_s[r, 0:t] + jnp.concatenate([rsum_s[r, :]] * (t // LANES), axis=1)
            a = jnp.exp2(lb_s[r, :] + survive).astype(BF16)
            acc_s[r, :] += _dot(a, v_ref[pl.ds(k0, t), hs])
            rsum_s[r, :] += sums_s[r, t:]

    acc_s[...] = jnp.zeros_like(acc_s)
    rsum_s[...] = jnp.zeros_like(rsum_s)
    tile(i, True)

    def body(jj, carry):
        tile(i - 1 - jj, False)
        return carry

    lax.fori_loop(0, i, body, 0)
    for h in range(SB_GROUP):
        o_ref[:, h * HEAD_DIM:(h + 1) * HEAD_DIM] = acc_s[h * t:(h + 1) * t, :].astype(o_ref.dtype)


def _sb_attention(proj3):
    b, s, _ = proj3.shape
    gw = SB_GROUP * HEAD_DIM
    rows = SB_GROUP * SB_TILE
    return pl.pallas_call(
        _sb_kernel,
        grid=(b, SB_HEADS // SB_GROUP, s // SB_TILE),
        in_specs=[
            pl.BlockSpec((None, SB_TILE, gw), lambda bi, hg, i: (bi, i, CB_SB_Q // SB_GROUP + hg)),
            pl.BlockSpec((None, s, gw), lambda bi, hg, i: (bi, 0, CB_SB_K // SB_GROUP + hg)),
            pl.BlockSpec((None, s, gw), lambda bi, hg, i: (bi, 0, CB_SB_V // SB_GROUP + hg)),
        ],
        out_specs=pl.BlockSpec((None, SB_TILE, gw), lambda bi, hg, i: (bi, i, hg)),
        out_shape=jax.ShapeDtypeStruct((b, s, SB_HEADS * HEAD_DIM), BF16),
        scratch_shapes=[pltpu.VMEM((rows, SB_TILE), F32), pltpu.VMEM((rows, SB_TILE), BF16),
                        pltpu.VMEM((rows, SB_TILE + LANES), F32), pltpu.VMEM((rows, HEAD_DIM), F32),
                        pltpu.VMEM((rows, LANES), F32)],
        compiler_params=_cparams(3),
        name="sb_attention",
    )(proj3, proj3, proj3)


def _nsa_prep_kernel(kc_ref, vc_ref, ks_ref, kw_ref, pk_ref, pv_ref, wk_ref, wv_ref, kcg_ref, ksg_ref, kwg_ref,
                     kcb_ref, vcb_ref, ksn_ref, kwn_ref):
    half = CMP_STRIDE * HEAD_DIM

    def compress(a_ref, p_ref, w_ref):
        a = a_ref[...].astype(F32)
        top = _dot((a + p_ref[0:1, :]).astype(BF16), w_ref[0:half, :])
        bot = _dot((a + p_ref[1:2, :]).astype(BF16), w_ref[half:2 * half, :])
        return top + pltpu.roll(bot, bot.shape[0] - 1, axis=0)

    kcb_ref[...] = _rms(compress(kc_ref, pk_ref, wk_ref), kcg_ref[...]).astype(BF16)
    vcb_ref[...] = compress(vc_ref, pv_ref, wv_ref).astype(BF16)
    kwn_ref[...] = _rms(kw_ref[...].astype(F32), kwg_ref[...]).astype(BF16)
    ksn_ref[:, 0:HEAD_DIM] = _rms(ks_ref[...].astype(F32), ksg_ref[...]).astype(BF16)
    s = ks_ref.shape[0]
    blk = jnp.right_shift(lax.broadcasted_iota(jnp.int32, (s, LANES), 0), int(math.log2(SLC_LEN)))
    onehot = blk == lax.broadcasted_iota(jnp.int32, (s, LANES), 1)
    ksn_ref[:, HEAD_DIM:HEAD_DIM + LANES] = jnp.where(onehot, 1.0, 0.0).astype(BF16)


def _nsa_prep(proj3, pos_k, pos_v, wk, wv, kc_g, ks_g, kw_g):
    b, s, _ = proj3.shape
    g_n = NSA_KV_HEADS
    kv_chunks = _kv_chunks(proj3)
    nch = s // CMP_STRIDE
    half = CMP_STRIDE * HEAD_DIM
    vec = lambda: pl.BlockSpec((1, HEAD_DIM), lambda bi, g: (0, 0))
    small = jax.ShapeDtypeStruct((b, g_n, nch, HEAD_DIM), BF16)
    full = jax.ShapeDtypeStruct((b, g_n, s, HEAD_DIM), BF16)
    aug = jax.ShapeDtypeStruct((b, g_n, s, HEAD_DIM + LANES), BF16)
    return pl.pallas_call(
        _nsa_prep_kernel,
        grid=(b, g_n),
        in_specs=[
            pl.BlockSpec((None, None, nch, half), lambda bi, g: (bi, g, 0, 0)),
            pl.BlockSpec((None, None, nch, half), lambda bi, g: (bi, g_n + g, 0, 0)),
            pl.BlockSpec((None, s, HEAD_DIM), lambda bi, g: (bi, 0, CB_KS + g)),
            pl.BlockSpec((None, s, HEAD_DIM), lambda bi, g: (bi, 0, CB_KW + g)),
            pl.BlockSpec((2, half), lambda bi, g: (0, 0)),
            pl.BlockSpec((2, half), lambda bi, g: (0, 0)),
            pl.BlockSpec((2 * half, HEAD_DIM), lambda bi, g: (0, 0)),
            pl.BlockSpec((2 * half, HEAD_DIM), lambda bi, g: (0, 0)),
            vec(), vec(), vec(),
        ],
        out_specs=[
            pl.BlockSpec((None, None, nch, HEAD_DIM), lambda bi, g: (bi, g, 0, 0)),
            pl.BlockSpec((None, None, nch, HEAD_DIM), lambda bi, g: (bi, g, 0, 0)),
            pl.BlockSpec((None, None, s, HEAD_DIM + LANES), lambda bi, g: (bi, g, 0, 0)),
            pl.BlockSpec((None, None, s, HEAD_DIM), lambda bi, g: (bi, g, 0, 0)),
        ],
        out_shape=[small, small, aug, full],
        compiler_params=_cparams(2),
        name="nsa_prep",
    )(kv_chunks, kv_chunks, proj3, proj3, pos_k.reshape(2, half), pos_v.reshape(2, half),
      wk.astype(BF16), wv.astype(BF16), kc_g.reshape(1, -1), ks_g.reshape(1, -1), kw_g.reshape(1, -1))


def _nsa_kernel(q_ref, kcb_ref, vcb_ref, ksa_ref, vs_ref, kwn_ref, vw_ref, gate_ref, bc_ref, bt_ref, qg_ref, ovt_ref,
                o_ref):
    g = pl.program_id(1)
    i = pl.program_id(2)
    r_n = NSA_REP
    groups = [(a, r) for a in range(NSA_SUB) for r in range(r_n)]
    sub = lambda a: slice(a * TQ, (a + 1) * TQ)
    head = lambda r: slice(r * HEAD_DIM, (r + 1) * HEAD_DIM)

    q = jnp.concatenate([_rms(q_ref[sub(a), head(r)].astype(F32), qg_ref[...]).astype(BF16) for a, r in groups], axis=0)

    logit_c = _dot_t(q, kcb_ref[...]) + jnp.concatenate([bc_ref[r, sub(a), :] for a, r in groups], axis=0)
    m_c = jnp.max(logit_c, axis=-1, keepdims=True)
    p_c = jnp.where(logit_c > 0.5 * NEG_INF, jnp.exp(logit_c - m_c), 0.0)
    p_c = p_c / jnp.maximum(jnp.sum(p_c, axis=-1, keepdims=True), TINY)
    o_c = _dot(p_c.astype(BF16), vcb_ref[...])

    p_sum = jnp.concatenate(
        [sum(p_c[(a * r_n + r) * TQ:(a * r_n + r + 1) * TQ] for r in range(r_n)) for a in range(NSA_SUB)], axis=0)
    hi = p_sum.astype(BF16)
    lo = (p_sum - hi.astype(F32)).astype(BF16)
    n_sel = ovt_ref.shape[0]
    imp = _dot_t(ovt_ref[...], hi) + _dot_t(ovt_ref[...], lo)
    blk = lax.broadcasted_iota(jnp.int32, (n_sel, NSA_QT), 0)
    tb = jnp.right_shift(i * NSA_QT + lax.broadcasted_iota(jnp.int32, (n_sel, NSA_QT), 1), int(math.log2(SLC_LEN)))
    causal_blk = blk <= tb
    forced = (blk == 0) | (causal_blk & (blk > tb - N_LOCAL_BLOCKS))
    score = jnp.where(forced, FORCE_SCORE, jnp.where(causal_blk, imp, -FORCE_SCORE))
    rank = jnp.zeros((n_sel, NSA_QT), F32)
    for b in range(n_sel):
        sb = score[b:b + 1, :]
        beats = (sb > score) | ((sb == score) & (blk > b))
        rank = rank + jnp.where(beats, 1.0, 0.0)
    chosen = (rank < float(min(SLC_TOP_N, n_sel))) & causal_blk
    sel_neg = jnp.where(chosen, 0.0, NEG_INF)
    sel_neg = jnp.concatenate([sel_neg, jnp.zeros((LANES - n_sel, NSA_QT), F32)], axis=0).T.astype(BF16)
    q_aug = jnp.concatenate([q, jnp.concatenate([sel_neg[sub(a)] for a, _ in groups], axis=0)], axis=1)

    def biased(s, first_tile, idx_fn):
        return jnp.concatenate(
            [s[:, c * TQ:(c + 1) * TQ]
             + jnp.concatenate([bt_ref[idx_fn(i * NSA_SUB + a - (first_tile + c))] for a in range(NSA_SUB)], axis=0)
             for c in range(s.shape[1] // TQ)], axis=1)

    w_tile = jnp.maximum(i * NSA_SUB - WINDOW // TQ, 0)
    w0 = pl.multiple_of(w_tile * TQ, TQ)

    def win_idx(off):
        near = jnp.where(off == WINDOW // TQ, BT_FAR_UPPER, jnp.minimum(off, BT_FAR))
        return jnp.where((off < 0) | (off > WINDOW // TQ), BT_MASKED, near)

    s_w = biased(_dot_t(q, kwn_ref[pl.ds(w0, WIN_SPAN), :]), w_tile, win_idx)
    p_w = jnp.exp(s_w - jnp.max(s_w, axis=-1, keepdims=True))
    o_w = _dot(p_w.astype(BF16), vw_ref[pl.ds(w0, WIN_SPAN), :]) / jnp.sum(p_w, axis=-1, keepdims=True)

    def sel_chunk(kc, carry, diag):
        c0 = pl.multiple_of(kc * SEL_CHUNK, SEL_CHUNK)
        idx_fn = (lambda off: jnp.where(off < 0, BT_MASKED, jnp.minimum(off, BT_FAR))) if diag else (
            lambda off: jnp.minimum(off, BT_FAR))
        s = biased(_dot_t(q_aug, ksa_ref[pl.ds(c0, SEL_CHUNK), :]), kc * (SEL_CHUNK // TQ), idx_fn)
        m_blk = jnp.max(s, axis=-1, keepdims=True)
        if diag:
            p = jnp.exp(s - m_blk)
            return m_blk, jnp.sum(p, axis=-1, keepdims=True), _dot(p.astype(BF16), vs_ref[pl.ds(c0, SEL_CHUNK), :])
        m, l, acc = carry
        m_new = jnp.maximum(m, m_blk)
        p = jnp.exp(s - m_new)
        alpha = jnp.exp(m - m_new)
        l = alpha * l + jnp.sum(p, axis=-1, keepdims=True)
        acc = alpha * acc + _dot(p.astype(BF16), vs_ref[pl.ds(c0, SEL_CHUNK), :])
        return m_new, l, acc

    kc_diag = lax.shift_right_logical(i * NSA_SUB + (NSA_SUB - 1), jnp.int32(int(math.log2(SEL_CHUNK // TQ))))
    carry = sel_chunk(kc_diag, None, True)
    _, l_s, acc_s = lax.fori_loop(0, kc_diag, lambda jj, c: sel_chunk(kc_diag - 1 - jj, c, False), carry)
    o_s = acc_s / l_s

    gates = jax.nn.sigmoid(gate_ref[...].astype(F32))
    lane = lax.broadcasted_iota(jnp.int32, gates.shape, 1)

    def gate_col(idx):
        return jnp.sum(jnp.where(lane == idx, gates, 0.0), axis=-1, keepdims=True)

    for r in range(r_n):
        h = g * r_n + r
        g_c, g_s, g_w = gate_col(h), gate_col(NSA_HEADS + h), gate_col(2 * NSA_HEADS + h)
        for a in range(NSA_SUB):
            k = slice((a * r_n + r) * TQ, (a * r_n + r + 1) * TQ)
            o = g_c[sub(a)] * o_c[k] + g_s[sub(a)] * o_s[k] + g_w[sub(a)] * o_w[k]
            o_ref[sub(a), head(r)] = o.astype(o_ref.dtype)


def _overlap_matrix(s):
    nch = s // CMP_STRIDE
    nsel = s // SLC_LEN
    ci = np.arange(nch)[:, None]
    sj = np.arange(nsel)[None, :]
    ov = (ci * CMP_STRIDE <= sj * SLC_LEN + SLC_LEN - 1) & (ci * CMP_STRIDE + CMP_LEN - 1 >= sj * SLC_LEN)
    ov = ov & (ci < nch - 1)
    return jnp.asarray(ov.T, BF16)


def _nsa_attention(proj3, kcb, vcb, ksa, kwn, bias_c, bias_t, q_g):
    b, s, _ = proj3.shape
    g_n, r_n = NSA_KV_HEADS, NSA_REP
    nch = s // CMP_STRIDE
    nsel = s // SLC_LEN
    gw = r_n * HEAD_DIM
    return pl.pallas_call(
        _nsa_kernel,
        grid=(b, g_n, s // NSA_QT),
        in_specs=[
            pl.BlockSpec((None, NSA_QT, gw), lambda bi, g, i: (bi, i, CB_NQ // r_n + g)),
            pl.BlockSpec((None, None, nch, HEAD_DIM), lambda bi, g, i: (bi, g, 0, 0)),
            pl.BlockSpec((None, None, nch, HEAD_DIM), lambda bi, g, i: (bi, g, 0, 0)),
            pl.BlockSpec((None, None, s, HEAD_DIM + LANES), lambda bi, g, i: (bi, g, 0, 0)),
            pl.BlockSpec((None, s, HEAD_DIM), lambda bi, g, i: (bi, 0, CB_VS + g)),
            pl.BlockSpec((None, None, s, HEAD_DIM), lambda bi, g, i: (bi, g, 0, 0)),
            pl.BlockSpec((None, s, HEAD_DIM), lambda bi, g, i: (bi, 0, CB_VW + g)),
            pl.BlockSpec((None, NSA_QT, LANES), lambda bi, g, i: (bi, i, CB_NGATE)),
            pl.BlockSpec((None, r_n, NSA_QT, LANES), lambda bi, g, i: (g, 0, i, 0)),
            pl.BlockSpec((None, 5, r_n * TQ, TQ), lambda bi, g, i: (g, 0, 0, 0)),
            pl.BlockSpec((1, HEAD_DIM), lambda bi, g, i: (0, 0)),
            pl.BlockSpec((nsel, nch), lambda bi, g, i: (0, 0)),
        ],
        out_specs=pl.BlockSpec((None, NSA_QT, gw), lambda bi, g, i: (bi, i, g)),
        out_shape=jax.ShapeDtypeStruct((b, s, NSA_HEADS * HEAD_DIM), BF16),
        compiler_params=_cparams(3),
        name="nsa_attention",
    )(proj3, kcb, vcb, ksa, proj3, kwn, proj3, proj3, bias_c, bias_t, q_g.reshape(1, -1) * SCALE, _overlap_matrix(s))


def _merge_kernel(u_ref, sb_ref, ns_ref, gc_ref, gs_ref, gn_ref, wc_ref, ws_ref, wn_ref, o_ref):
    m = jax.nn.sigmoid(gc_ref[...].astype(F32)) * _dot(u_ref[...], wc_ref[...])
    m = m + jax.nn.sigmoid(gs_ref[...].astype(F32)) * _dot(sb_ref[...], ws_ref[...])
    m = m + jax.nn.sigmoid(gn_ref[...].astype(F32)) * _dot(ns_ref[...], wn_ref[...])
    o_ref[...] = m.astype(o_ref.dtype)


def _merge(u_act, sb, nsa, proj, wc, ws, wn, wl, tm=512, tn=2048):
    m = u_act.shape[0]
    d = wc.shape[2]
    kc = wc.shape[1]
    gate0 = CB_MERGE * LANES // tn
    gstep = d // tn
    act = lambda: pl.BlockSpec((tm, kc), lambda i, j: (i, 0))
    gate = lambda n: pl.BlockSpec((tm, tn), lambda i, j: (i, gate0 + n * gstep + j))
    wsp = lambda: pl.BlockSpec((None, kc, tn), lambda i, j: (wl, 0, j))
    return pl.pallas_call(
        _merge_kernel,
        grid=(m // tm, d // tn),
        in_specs=[act(), act(), act(), gate(0), gate(1), gate(2), wsp(), wsp(), wsp()],
        out_specs=pl.BlockSpec((tm, tn), lambda i, j: (i, j)),
        out_shape=jax.ShapeDtypeStruct((m, d), BF16),
        compiler_params=_cparams(2),
        name="merge",
    )(u_act, sb, nsa, proj, proj, proj, wc, ws, wn)


def _mm_res_kernel(a_ref, w_ref, x_ref, mod_ref, o_ref, acc_ref, *, row_ga, nk):
    k = pl.program_id(2)

    @pl.when(k == 0)
    def _():
        acc_ref[...] = jnp.zeros_like(acc_ref)

    acc_ref[...] += _dot(a_ref[...], w_ref[...])

    @pl.when(k == nk - 1)
    def _():
        o_ref[...] = x_ref[...] + mod_ref[row_ga:row_ga + 1, :] * acc_ref[...]


def _mm_res(a, w, wl, x, mod, row_ga, seq, tm=1024, tn=1024, tk=None):
    m, ka = a.shape
    n = x.shape[1]
    tpb = seq // tm
    tk = tk or ka
    nk = ka // tk
    w_spec = pl.BlockSpec((None, tk, tn), lambda i, j, k: (wl, k, j))
    kern = functools.partial(_mm_res_kernel, row_ga=row_ga, nk=nk)
    return pl.pallas_call(
        kern,
        grid=(m // tm, n // tn, nk),
        in_specs=[
            pl.BlockSpec((tm, tk), lambda i, j, k: (i, k)),
            w_spec,
            pl.BlockSpec((tm, tn), lambda i, j, k: (i, j)),
            pl.BlockSpec((None, 6, tn), lambda i, j, k: (i // tpb, 0, j)),
        ],
        out_specs=pl.BlockSpec((tm, tn), lambda i, j, k: (i, j)),
        out_shape=jax.ShapeDtypeStruct((m, n), F32),
        scratch_shapes=[pltpu.VMEM((tm, tn), F32)],
        compiler_params=_cparams(3),
        name="matmul_residual",
    )(a, w, x, mod)


def _ffn_up_kernel(x_ref, mod_ref, g_ref, w1_ref, w3_ref, o_ref, h_ref):
    @pl.when(pl.program_id(1) == 0)
    def _():
        h = _norm_mod(x_ref[...], g_ref[...], mod_ref[ROW_SH_FFN:ROW_SH_FFN + 1, :], mod_ref[ROW_SC_FFN:ROW_SC_FFN + 1, :])
        h_ref[...] = h.astype(BF16)

    h = h_ref[...]
    a = _dot(h, w1_ref[...].astype(BF16))
    o_ref[...] = (a * jax.nn.sigmoid(a) * _dot(h, w3_ref[...].astype(BF16))).astype(o_ref.dtype)


def _ffn_up(x, mod, g, w1, w3, wl, seq, tm=1024, tn=512):
    m, d = x.shape
    f = w1.shape[2]
    tpb = seq // tm
    return pl.pallas_call(
        _ffn_up_kernel,
        grid=(m // tm, f // tn),
        in_specs=[
            pl.BlockSpec((tm, d), lambda i, j: (i, 0)),
            pl.BlockSpec((None, 6, d), lambda i, j: (i // tpb, 0, 0)),
            pl.BlockSpec((1, d), lambda i, j: (0, 0)),
            pl.BlockSpec((None, d, tn), lambda i, j: (wl, 0, j)),
            pl.BlockSpec((None, d, tn), lambda i, j: (wl, 0, j)),
        ],
        out_specs=pl.BlockSpec((tm, tn), lambda i, j: (i, j)),
        out_shape=jax.ShapeDtypeStruct((m, f), BF16),
        scratch_shapes=[pltpu.VMEM((tm, d), BF16)],
        compiler_params=_cparams(2),
        name="ffn_up",
    )(x, mod, g.reshape(1, d), w1, w3)


def _router_kernel(x_ref, mod_ref, g_ref, wr_ref, br_ref, cmbt_ref, pos_ref, post_ref, cnt_ref, h_ref):
    h = _norm_mod(x_ref[...], g_ref[...], mod_ref[ROW_SH_FFN:ROW_SH_FFN + 1, :], mod_ref[ROW_SC_FFN:ROW_SC_FFN + 1, :])
    h_ref[...] = h.astype(BF16)
    logits = jnp.dot(h, wr_ref[...], precision=lax.Precision.HIGHEST, preferred_element_type=F32) + br_ref[...]
    lane = lax.broadcasted_iota(jnp.int32, logits.shape, 1).astype(F32)
    pad = float(LANES)
    m1 = jnp.max(logits, axis=-1, keepdims=True)
    i1 = jnp.min(jnp.where(logits == m1, lane, pad), axis=-1, keepdims=True)
    rest = jnp.where(lane == i1, NEG_INF, logits)
    m2 = jnp.max(rest, axis=-1, keepdims=True)
    i2 = jnp.min(jnp.where(rest == m2, lane, pad), axis=-1, keepdims=True)
    e2 = jnp.exp(m2 - m1)
    w1 = 1.0 / (1.0 + e2)
    w2 = e2 / (1.0 + e2)
    cmb = jnp.where(lane == i1, w1, 0.0) + jnp.where(lane == i2, w2, 0.0)
    cmbt_ref[...] = cmb.T[0:cmbt_ref.shape[0], :]
    sel = jnp.where((lane == i1) | (lane == i2), 1.0, 0.0)
    tm = sel.shape[0]
    earlier = lax.broadcasted_iota(jnp.int32, (tm, tm), 1) < lax.broadcasted_iota(jnp.int32, (tm, tm), 0)
    pos = _dot(jnp.where(earlier, 1.0, 0.0).astype(BF16), sel.astype(BF16))
    pos = jnp.where(sel > 0.0, pos, -1.0)
    pos_ref[...] = pos
    post_ref[...] = pos.T[0:post_ref.shape[0], :]
    cnt_ref[...] = jnp.broadcast_to(jnp.sum(sel, axis=0, keepdims=True), cnt_ref.shape)


def _router(x, mod, g, w_router, b_router, seq):
    m, d = x.shape
    tm = MOE_TM
    n_e = w_router.shape[1]
    tpb = seq // tm
    nt = m // tm
    wr = jnp.zeros((d, LANES), F32).at[:, :n_e].set(w_router)
    br = jnp.full((1, LANES), 2.0 * NEG_INF, F32).at[0, :n_e].set(b_router)
    return pl.pallas_call(
        _router_kernel,
        grid=(nt,),
        in_specs=[
            pl.BlockSpec((tm, d), lambda i: (i, 0)),
            pl.BlockSpec((None, 6, d), lambda i: (i // tpb, 0, 0)),
            pl.BlockSpec((1, d), lambda i: (0, 0)),
            pl.BlockSpec((d, LANES), lambda i: (0, 0)),
            pl.BlockSpec((1, LANES), lambda i: (0, 0)),
        ],
        out_specs=[
            pl.BlockSpec((n_e, tm), lambda i: (0, i)),
            pl.BlockSpec((tm, LANES), lambda i: (i, 0)),
            pl.BlockSpec((n_e, tm), lambda i: (0, i)),
            pl.BlockSpec((None, 8, LANES), lambda i: (i, 0, 0)),
            pl.BlockSpec((tm, d), lambda i: (i, 0)),
        ],
        out_shape=[
            jax.ShapeDtypeStruct((n_e, m), F32),
            jax.ShapeDtypeStruct((m, LANES), F32),
            jax.ShapeDtypeStruct((n_e, m), F32),
            jax.ShapeDtypeStruct((nt, 8, LANES), F32),
            jax.ShapeDtypeStruct((m, d), BF16),
        ],
        compiler_params=_cparams(1),
        name="router",
    )(x, mod, g.reshape(1, d), wr, br)


def _count_le(sorted_vals, queries):
    return jnp.sum(sorted_vals[..., None, :] <= queries[..., :, None], axis=-1).astype(jnp.int32)


def _moe_plan(counts, m):
    t_n, e_n = counts.shape
    per_mt = MOE_MT // MOE_BLK
    g_max = 2 * m // MOE_BLK + t_n * e_n
    pad_max = (per_mt - 1) * e_n
    nblk = (counts + MOE_BLK - 1) // MOE_BLK
    nb_e = jnp.sum(nblk, axis=0)
    cap_e = (nb_e + per_mt - 1) // per_mt * per_mt
    start_e = jnp.cumsum(cap_e) - cap_e
    dst0 = (start_e[None, :] + jnp.cumsum(nblk, axis=0) - nblk).reshape(-1)
    flat = nblk.reshape(-1)
    cum = jnp.cumsum(flat)
    g = jnp.minimum(jnp.arange(g_max, dtype=jnp.int32), cum[-1] - 1)
    p = jnp.minimum(_count_le(cum, g), t_n * e_n - 1)
    rb = g - (cum[p] - flat[p])
    n_blocks = g_max + pad_max
    z_max = n_blocks - 2 * m // MOE_BLK
    used = jnp.zeros((n_blocks,), jnp.int32).at[dst0[p] + rb].set(1)
    free = jnp.argsort(used, stable=True).astype(jnp.int32)
    z = jnp.minimum(jnp.arange(z_max, dtype=jnp.int32), n_blocks - cum[-1] - 1)
    never = jnp.full((z_max,), MOE_NEVER, jnp.int32)
    g_tile = jnp.concatenate([p // e_n, jnp.full((z_max,), t_n - 1, jnp.int32)])
    g_exp = jnp.concatenate([p % e_n, jnp.zeros((z_max,), jnp.int32)])
    g_rb = jnp.concatenate([rb, never])
    g_dst = jnp.concatenate([dst0[p] + rb, free[z]])
    mt_max = (g_max + pad_max) // per_mt
    n_mt = (jnp.sum(cap_e) // per_mt).astype(jnp.int32)
    mt = jnp.minimum(jnp.arange(mt_max, dtype=jnp.int32), n_mt - 1)
    mt_exp = jnp.minimum(_count_le(jnp.cumsum(cap_e) // per_mt, mt), e_n - 1)
    s_max = 2 * MOE_TM // MOE_BLK + e_n
    cum_t = jnp.cumsum(nblk, axis=1)
    n_t = cum_t[:, -1:]
    s_all = jnp.arange(s_max, dtype=jnp.int32)[None, :]
    s = jnp.minimum(s_all, n_t - 1)
    s_exp = jnp.minimum(_count_le(cum_t, s), e_n - 1)
    s_rb = s - (jnp.take_along_axis(cum_t, s_exp, axis=1) - jnp.take_along_axis(nblk, s_exp, axis=1))
    s_src = jnp.take_along_axis(dst0.reshape(t_n, e_n), s_exp, axis=1) + s_rb
    s_rb = jnp.where(s_all < n_t, s_rb, MOE_NEVER)
    i32 = lambda a: a.astype(jnp.int32)
    return dict(g_tile=i32(g_tile), g_exp=i32(g_exp), g_rb=i32(g_rb), g_dst=i32(g_dst), n_mt=i32(n_mt.reshape(1)),
                mt_exp=i32(mt_exp), s_exp=i32(s_exp.reshape(-1)), s_rb=i32(s_rb.reshape(-1)),
                s_src=i32(s_src.reshape(-1)), n_blocks=n_blocks, s_max=s_max)


def _moe_gather_kernel(tile_ref, exp_ref, rb_ref, dst_ref, h_ref, post_ref, cmbt_ref, w1f_ref, w3f_ref, w2f_ref,
                       o_ref, w_ref, w1b_ref, w3b_ref, w2b_ref):
    g = pl.program_id(0)
    w1b_ref[...] = w1f_ref[...].astype(BF16)
    w3b_ref[...] = w3f_ref[...].astype(BF16)
    w2b_ref[...] = w2f_ref[...].astype(BF16)
    fill = rb_ref[g] >= MOE_NEVER

    @pl.when(fill)
    def _():
        o_ref[...] = jnp.zeros_like(o_ref)
        w_ref[...] = jnp.zeros_like(w_ref)

    @pl.when(jnp.logical_not(fill))
    def _():
        row = post_ref[pl.ds(exp_ref[g], 1), :]
        tm = row.shape[1]
        want = (lax.broadcasted_iota(jnp.int32, (MOE_BLK, tm), 0) + rb_ref[g] * MOE_BLK).astype(F32)
        match = row == want
        o_ref[...] = _dot(jnp.where(match, 1.0, 0.0).astype(BF16), h_ref[...]).astype(o_ref.dtype)
        cw = jnp.sum(jnp.where(match, cmbt_ref[pl.ds(exp_ref[g], 1), :], 0.0), axis=-1, keepdims=True)
        w_ref[...] = jnp.broadcast_to(cw, w_ref.shape)


def _moe_gather(h, post, cmbt, w1, w3, w2, wl, plan):
    m, d = h.shape
    n_e, _, fe = w1.shape[1:]
    n = plan["g_tile"].shape[0]
    rows = plan["n_blocks"] * MOE_BLK
    up_rows, down_rows = MOE_CAST_ROWS_UP, MOE_CAST_ROWS_DOWN
    per_up, per_down = d // up_rows, fe // down_rows
    assert d % up_rows == 0 and fe % down_rows == 0 and n >= n_e * max(per_up, per_down)

    def slab(g, per):
        gg = jnp.minimum(g, n_e * per - 1)
        return lax.div(gg, per), lax.rem(gg, per)

    def f32_spec(slab_rows, per, width):
        return pl.BlockSpec((None, None, slab_rows, width), lambda g, t, e, r, ds: (wl, *slab(g, per), 0))

    def bf16_spec(slab_rows, per, width):
        return pl.BlockSpec((None, slab_rows, width), lambda g, t, e, r, ds: (*slab(g, per), 0))

    return pl.pallas_call(
        _moe_gather_kernel,
        grid_spec=pltpu.PrefetchScalarGridSpec(
            num_scalar_prefetch=4,
            grid=(n,),
            in_specs=[
                pl.BlockSpec((MOE_TM, d), lambda g, t, e, r, ds: (t[g], 0)),
                pl.BlockSpec((n_e, MOE_TM), lambda g, t, e, r, ds: (0, t[g])),
                pl.BlockSpec((n_e, MOE_TM), lambda g, t, e, r, ds: (0, t[g])),
                f32_spec(up_rows, per_up, fe), f32_spec(up_rows, per_up, fe), f32_spec(down_rows, per_down, d),
            ],
            out_specs=[
                pl.BlockSpec((MOE_BLK, d), lambda g, t, e, r, ds: (ds[g], 0)),
                pl.BlockSpec((MOE_BLK, LANES), lambda g, t, e, r, ds: (ds[g], 0)),
                bf16_spec(up_rows, per_up, fe), bf16_spec(up_rows, per_up, fe), bf16_spec(down_rows, per_down, d),
            ],
        ),
        out_shape=[jax.ShapeDtypeStruct((rows, d), BF16), jax.ShapeDtypeStruct((rows, LANES), F32),
                   jax.ShapeDtypeStruct((n_e, d, fe), BF16), jax.ShapeDtypeStruct((n_e, d, fe), BF16),
                   jax.ShapeDtypeStruct((n_e, fe, d), BF16)],
        compiler_params=_cparams(1),
        name="moe_gather",
    )(plan["g_tile"], plan["g_exp"], plan["g_rb"], plan["g_dst"], h, post, cmbt, w1, w3, w2)


def _moe_ffn_kernel(exp_ref, n_ref, x_ref, rw_ref, w1_ref, w3_ref, w2_ref, o_ref, acc_ref, *, overlap):
    c = pl.program_id(1)

    @pl.when(pl.program_id(0) < n_ref[0])
    def _():
        @pl.when(c == 0)
        def _():
            acc_ref[...] = jnp.zeros_like(acc_ref)

        x = x_ref[...]
        win = w1_ref.shape[1]

        def slab(k0):
            k1 = min(k0 + MOE_SUB, win)
            a = _dot(x, w1_ref[:, k0:k1])
            a = (a * jax.nn.sigmoid(a) * _dot(x, w3_ref[:, k0:k1])).astype(BF16)
            acc_ref[...] += _dot(a, w2_ref[k0:k1, :])

        @pl.when(c == 0)
        def _():
            for k0 in range(0, overlap, MOE_SUB):
                slab(k0)

        for k0 in range(overlap, win, MOE_SUB):
            slab(k0)

        @pl.when(c == 1)
        def _():
            o_ref[...] = (rw_ref[:, 0:1] * acc_ref[...]).astype(o_ref.dtype)

    @pl.when(pl.program_id(0) >= n_ref[0])
    def _():
        o_ref[...] = jnp.zeros_like(o_ref)


def _moe_ffn(xg, row_w, w1, w3, w2, plan, win):
    d = xg.shape[1]
    fe = w1.shape[2]
    second = fe - win
    overlap = 2 * win - fe
    assert 0 <= overlap < win and overlap % MOE_SUB == 0 and second % LANES == 0
    n_mt = plan["mt_exp"].shape[0]

    def tile(mt, n):
        return jnp.minimum(mt, n[0] - 1)

    def start(mt, c, n):
        return jnp.where(mt < n[0], c, 1) * second

    kern = functools.partial(_moe_ffn_kernel, overlap=overlap)
    return pl.pallas_call(
        kern,
        grid_spec=pltpu.PrefetchScalarGridSpec(
            num_scalar_prefetch=2,
            grid=(n_mt, 2),
            in_specs=[
                pl.BlockSpec((MOE_MT, d), lambda mt, c, e, n: (tile(mt, n), 0)),
                pl.BlockSpec((MOE_MT, LANES), lambda mt, c, e, n: (tile(mt, n), 0)),
                pl.BlockSpec((None, pl.Element(d), pl.Element(win)), lambda mt, c, e, n: (e[mt], 0, start(mt, c, n))),
                pl.BlockSpec((None, pl.Element(d), pl.Element(win)), lambda mt, c, e, n: (e[mt], 0, start(mt, c, n))),
                pl.BlockSpec((None, pl.Element(win), pl.Element(d)), lambda mt, c, e, n: (e[mt], start(mt, c, n), 0)),
            ],
            out_specs=pl.BlockSpec((MOE_MT, d), lambda mt, c, e, n: (mt, 0)),
            scratch_shapes=[pltpu.VMEM((MOE_MT, d), F32)],
        ),
        out_shape=jax.ShapeDtypeStruct((n_mt * MOE_MT, d), BF16),
        compiler_params=_cparams(2),
        name="moe_ffn",
    )(plan["mt_exp"], plan["n_mt"], xg, row_w, w1, w3, w2)


def _moe_scatter_kernel(src_ref, exp_ref, rb_ref, *refs, ns):
    y_refs, (pos_ref, x_ref, mod_ref, o_ref, acc_ref) = refs[:MOE_SCATTER_SLOTS], refs[MOE_SCATTER_SLOTS:]
    i = pl.program_id(0)
    s = pl.program_id(1)

    @pl.when(s == 0)
    def _():
        acc_ref[...] = jnp.zeros_like(acc_ref)

    lane = lax.broadcasted_iota(jnp.int32, pos_ref.shape, 1)
    lane_f = lane.astype(F32)

    def onehot(slot):
        pos = jnp.sum(jnp.where(lane == exp_ref[slot], pos_ref[...], 0.0), axis=-1, keepdims=True)
        return jnp.where(pos - (rb_ref[slot] * MOE_BLK).astype(F32) == lane_f, 1.0, 0.0).astype(BF16)

    s0 = i * ns + MOE_SCATTER_SLOTS * s
    pt = jnp.concatenate([onehot(s0 + k) for k in range(MOE_SCATTER_SLOTS)], axis=1)
    acc_ref[...] += _dot(pt, jnp.concatenate([y[...] for y in y_refs], axis=0))

    @pl.when(s == ns // MOE_SCATTER_SLOTS - 1)
    def _():
        o_ref[...] = x_ref[...] + mod_ref[ROW_GA_FFN:ROW_GA_FFN + 1, :] * acc_ref[...]


def _moe_scatter(yg, pos, x, mod, plan, seq):
    m, d = x.shape
    ns = plan["s_max"]
    tpb = seq // MOE_TM
    kern = functools.partial(_moe_scatter_kernel, ns=ns)

    def y_spec(k):
        return pl.BlockSpec((MOE_BLK, d), lambda i, s, sr, ex, rb: (sr[i * ns + MOE_SCATTER_SLOTS * s + k], 0))

    return pl.pallas_call(
        kern,
        grid_spec=pltpu.PrefetchScalarGridSpec(
            num_scalar_prefetch=3,
            grid=(m // MOE_TM, ns // MOE_SCATTER_SLOTS),
            in_specs=[y_spec(k) for k in range(MOE_SCATTER_SLOTS)] + [
                pl.BlockSpec((MOE_TM, LANES), lambda i, s, sr, ex, rb: (i, 0)),
                pl.BlockSpec((MOE_TM, d), lambda i, s, sr, ex, rb: (i, 0)),
                pl.BlockSpec((None, 6, d), lambda i, s, sr, ex, rb: (i // tpb, 0, 0)),
            ],
            out_specs=pl.BlockSpec((MOE_TM, d), lambda i, s, sr, ex, rb: (i, 0)),
            scratch_shapes=[pltpu.VMEM((MOE_TM, d), F32)],
        ),
        out_shape=jax.ShapeDtypeStruct((m, d), F32),
        compiler_params=_cparams(2),
        name="moe_scatter",
    )(plan["s_src"], plan["s_exp"], plan["s_rb"], *([yg] * MOE_SCATTER_SLOTS), pos, x, mod)


def _moe_layer(x, mod, g, w_router, b_router, w1, w3, w2, wl, seq, fc):
    m = x.shape[0]
    cmbt, pos, post, counts, h = _router(x, mod, g, w_router, b_router, seq)
    plan = _moe_plan(counts[:, 0, :w_router.shape[1]].astype(jnp.int32), m)
    xg, row_w, w1b, w3b, w2b = _moe_gather(h, post, cmbt, w1, w3, w2, wl, plan)
    yg = _moe_ffn(xg, row_w, w1b, w3b, w2b, plan, fc)
    return _moe_scatter(yg, pos, x, mod, plan, seq)


def _kv_chunks(proj3):
    b, s, _ = proj3.shape
    n = 2 * NSA_KV_HEADS
    t = proj3[:, :, CB_KC * LANES:(CB_KC + n) * LANES].reshape(b, s, n, HEAD_DIM)
    return t.transpose(0, 2, 1, 3).reshape(b, n, s // CMP_STRIDE, CMP_STRIDE * HEAD_DIM)


def kernel(x, c, w_ada, b_ada, g_mix, g_ffn, w_in, conv_w, conv_b, conv_ln_g, conv_ln_b, w_conv_out, w_sb_out,
           nsa_cmp_pos_k, nsa_cmp_pos_v, nsa_cmp_wk, nsa_cmp_wv, nsa_q_g, nsa_kc_g, nsa_ks_g, nsa_kw_g, w_nsa_out, w_o,
           rel_bias, ffn_w1, ffn_w3, ffn_w2, moe_router, moe_router_b, moe_w1, moe_w3, moe_w2):
    b, s, d = x.shape
    m = b * s
    depth = w_ada.shape[0]
    mod_all = _ada_all(c, w_ada, b_ada)
    bias_c, bias_t = _bias_tables(rel_bias, s)
    w_in_t = jnp.swapaxes(w_in, 1, 2)
    w_conv_out, w_sb_out, w_nsa_out, w_o, ffn_w2 = (
        w.astype(BF16) for w in (w_conv_out, w_sb_out, w_nsa_out, w_o, ffn_w2))
    xf = x.reshape(m, d)
    for l in range(depth):
        mod = mod_all[l].reshape(b, 6, d)
        proj = _in_proj(xf, mod, g_mix[l], w_in_t, l, s)
        proj3 = proj.reshape(b, s, NP)
        u_act = _conv_module(proj, conv_w[l], conv_b[l], conv_ln_g[l], conv_ln_b[l], b, s)
        sb = _sb_attention(proj3)
        kcb, vcb, ksn, kwn = _nsa_prep(proj3, nsa_cmp_pos_k[l], nsa_cmp_pos_v[l], nsa_cmp_wk[l], nsa_cmp_wv[l],
                                       nsa_kc_g[l], nsa_ks_g[l], nsa_kw_g[l])
        nsa = _nsa_attention(proj3, kcb, vcb, ksn, kwn, bias_c, bias_t, nsa_q_g[l])
        merged = _merge(u_act, sb.reshape(m, -1), nsa.reshape(m, -1), proj, w_conv_out, w_sb_out, w_nsa_out, l)
        xf = _mm_res(merged, w_o, l, xf, mod, ROW_GA_MIX, s, tm=512, tn=2048)
        i = l // 2
        if l % 2 == 0:
            act = _ffn_up(xf, mod, g_ffn[l], ffn_w1, ffn_w3, i, s)
            xf = _mm_res(act, ffn_w2, i, xf, mod, ROW_GA_FFN, s, tm=1024, tn=512)
        else:
            xf = _moe_layer(xf, mod, g_ffn[l], moe_router[i], moe_router_b[i], moe_w1, moe_w3, moe_w2, i, s, MOE_FC)
    return xf.reshape(b, s, d)
```
